```python
import jax, jax.numpy as jnp
from jax import lax
import numpy as np

D_MODEL = 1024
BATCH = 1
SEQ = 16384
DEPTH = 4
DEC_BATCH = 16
DEC_SEQ = 16
PAST_LEN = 4096

CHUNK = 64
N_EVEN = (DEPTH + 1) // 2
N_ODD = DEPTH // 2
MIX_WIDTH = D_MODEL
H_RET = 8
DK_RET = 64
DV_RET = 64
D_RET = H_RET * DK_RET
ROPE_BASE = 10000.0
D_SCONV = MIX_WIDTH - D_RET
SCONV_WIDTH = 3
POOL_WINDOWS = (2, 4, 8, 16)
N_POOL_GROUPS = 4
D_POOL = MIX_WIDTH // 2
POOL_GROUP = D_POOL // N_POOL_GROUPS
POOL_BUF = max(POOL_WINDOWS) - 1
H_ATT = 8
DH_ATT = 64
D_ATT = H_ATT * DH_ATT
N_PREV_CHUNKS = 8
REL_CLIP = 256
REL_SIZE = CHUNK + REL_CLIP
D_FF = 2816
FFN_CONV_WIDTH = 3
EPS = 1e-6
D_IN_EVEN = 4 * D_RET + 3 * D_SCONV
D_IN_ODD = D_POOL + 3 * D_ATT
NEG_INF = -1e30

kernel_name = 'hybrid_stream_encoder_step'


def rmsnorm(x, g):
    xf = x.astype(jnp.float32)
    y = xf * lax.rsqrt(jnp.mean(xf * xf, axis=-1, keepdims=True) + EPS)
    return (y * g.astype(jnp.float32)).astype(x.dtype)


def modulate(x, shift, scale):
    return x * (1 + scale[:, None, :]) + shift[:, None, :]


def causal_dwconv(u, buf, w):
    W = w.shape[0]
    T = u.shape[1]
    xc = jnp.concatenate([buf.astype(u.dtype), u], axis=1)
    y = sum(w[j] * xc[:, j:j + T] for j in range(W))
    return y, xc[:, -(W - 1):]


def rotary(x, pos):
    half = x.shape[-1] // 2
    inv = ROPE_BASE ** (-jnp.arange(half, dtype=jnp.float32) / half)
    ang = pos.astype(jnp.float32)[:, None] * inv[None, :]
    cos = jnp.cos(ang)[None, :, None, :]
    sin = jnp.sin(ang)[None, :, None, :]
    x1 = x[..., :half].astype(jnp.float32)
    x2 = x[..., half:].astype(jnp.float32)
    return jnp.concatenate([x1 * cos - x2 * sin, x1 * sin + x2 * cos], axis=-1).astype(x.dtype)


def retention_log_decay():
    return jnp.log1p(-(2.0 ** (-5.0 - jnp.arange(H_RET, dtype=jnp.float32))))


def retention_intra(q, k, v, log_g):
    T = q.shape[2]
    idx = jnp.arange(T, dtype=jnp.float32)
    diff = idx[:, None] - idx[None, :]
    dec = jnp.where(diff[None] >= 0, jnp.exp(log_g[:, None, None] * jnp.maximum(diff, 0.0)[None]), 0.0)
    s = jnp.einsum('bnthd,bnshd->bnhts', q, k) * dec[None, None]
    return jnp.einsum('bnhts,bnshe->bnthe', s, v.astype(jnp.float32))


def retention_prompt(q, k, v, log_g):
    B, T, H, dk = q.shape
    dv = v.shape[-1]
    n_chunks = T // CHUNK
    qc = q.reshape(B, n_chunks, CHUNK, H, dk)
    kc = k.reshape(B, n_chunks, CHUNK, H, dk)
    vc = v.reshape(B, n_chunks, CHUNK, H, dv)
    inner = retention_intra(qc, kc, vc, log_g)
    idx = jnp.arange(CHUNK, dtype=jnp.float32)
    zeta = jnp.exp(log_g[:, None] * (CHUNK - 1 - idx)[None, :])
    xi = jnp.exp(log_g[:, None] * (idx + 1)[None, :])
    kv = jnp.einsum('bnchd,hc,bnche->bnhde', kc.astype(jnp.float32), zeta, vc.astype(jnp.float32))
    g_chunk = jnp.exp(log_g * CHUNK)[None, :, None, None]

    def step(R, kv_n):
        return R * g_chunk + kv_n, R

    R_fin, R_prev = lax.scan(step, jnp.zeros((B, H, dk, dv), jnp.float32), jnp.moveaxis(kv, 1, 0))
    R_prev = jnp.moveaxis(R_prev, 0, 1)
    cross = jnp.einsum('bnchd,bnhde->bnche', qc.astype(jnp.float32), R_prev) * xi.T[None, None, :, :, None]
    return (inner + cross).reshape(B, T, H, dv), R_fin


def retention_step(q, k, v, S, log_g):
    T = q.shape[1]
    S = S.astype(jnp.float32)
    inner = retention_intra(q[:, None], k[:, None], v[:, None], log_g)[:, 0]
    idx = jnp.arange(T, dtype=jnp.float32)
    xi = jnp.exp(log_g[:, None] * (idx + 1)[None, :])
    zeta = jnp.exp(log_g[:, None] * (T - 1 - idx)[None, :])
    cross = jnp.einsum('bthd,bhde->bthe', q.astype(jnp.float32), S) * xi.T[None, :, :, None]
    S_new = S * jnp.exp(log_g * T)[None, :, None, None] + jnp.einsum('bthd,ht,bthe->bhde', k.astype(jnp.float32), zeta, v.astype(jnp.float32))
    return inner + cross, S_new


def retention_output(o, gain, g, dtype):
    B, T = o.shape[:2]
    mu = jnp.mean(o, axis=-1, keepdims=True)
    var = jnp.mean(jnp.square(o - mu), axis=-1, keepdims=True)
    on = ((o - mu) * lax.rsqrt(var + EPS)).reshape(B, T, D_RET) * gain.astype(jnp.float32)
    return (jax.nn.silu(g.astype(jnp.float32)) * on).astype(dtype)


def multiscale_pool(p, buf, pos, pool_w, pool_scale):
    B, T, C = p.shape
    xc = jnp.concatenate([buf.astype(p.dtype), p], axis=1).astype(jnp.float32)
    cs = jnp.concatenate([jnp.zeros((B, 1, C), jnp.float32), lax.cumsum(xc, axis=1)], axis=1)
    end = cs[:, POOL_BUF + 1:]
    cur = xc[:, POOL_BUF:]
    groups = []
    for gi, w in enumerate(POOL_WINDOWS):
        sl = slice(gi * POOL_GROUP, (gi + 1) * POOL_GROUP)
        start = cs[:, POOL_BUF + 1 - w:POOL_BUF + 1 - w + T, sl]
        cnt = jnp.minimum(pos + 1, w).astype(jnp.float32)[None, :, None]
        groups.append((end[..., sl] - start) / cnt - cur[..., sl])
    pooled = jnp.stack(groups, axis=2)
    y = jnp.einsum('btgc,gcd->btgd', pooled, pool_w.astype(jnp.float32)).reshape(B, T, C)
    y = (y * pool_scale.astype(jnp.float32)).astype(p.dtype)
    return y, xc[:, -POOL_BUF:].astype(p.dtype)


def rel_bias(table, qpos, kpos):
    rel = jnp.clip(qpos[:, None] - kpos[None, :], -(CHUNK - 1), REL_CLIP) + (CHUNK - 1)
    return table[:, rel].astype(jnp.float32)


def chunk_attention_prompt(q, k, v, table):
    B, T, H, dh = q.shape
    n_chunks = T // CHUNK
    pad = N_PREV_CHUNKS * CHUNK
    kp = jnp.pad(k, ((0, 0), (pad, 0), (0, 0), (0, 0))).reshape(B, n_chunks + N_PREV_CHUNKS, CHUNK, H, dh)
    vp = jnp.pad(v, ((0, 0), (pad, 0), (0, 0), (0, 0))).reshape(B, n_chunks + N_PREV_CHUNKS, CHUNK, H, dh)
    kb = jnp.concatenate([kp[:, j:j + n_chunks] for j in range(N_PREV_CHUNKS + 1)], axis=2)
    vb = jnp.concatenate([vp[:, j:j + n_chunks] for j in range(N_PREV_CHUNKS + 1)], axis=2)
    qc = q.reshape(B, n_chunks, CHUNK, H, dh)
    band = (N_PREV_CHUNKS + 1) * CHUNK
    qi = jnp.arange(CHUNK, dtype=jnp.int32)
    kj = jnp.arange(band, dtype=jnp.int32)
    bias = rel_bias(table, qi + pad, kj)
    valid = (jnp.arange(n_chunks, dtype=jnp.int32)[:, None] * CHUNK + kj[None, :] - pad) >= 0
    s = jnp.einsum('bnqhd,bnkhd->bnhqk', qc, kb).astype(jnp.float32) * (DH_ATT ** -0.5) + bias[None, None]
    s = jnp.where(valid[None, :, None, None, :], s, NEG_INF)
    p = jax.nn.softmax(s, axis=-1)
    o = jnp.einsum('bnhqk,bnkhd->bnqhd', p.astype(v.dtype), vb).reshape(B, T, H * dh)
    keep = min(pad, T)
    return o, k[:, -keep:], v[:, -keep:]


def chunk_attention_step(q, k, v, k_cache, v_cache, table, pos):
    B, T, H, dh = q.shape
    L = k_cache.shape[1]
    k_all = jnp.concatenate([k_cache.astype(k.dtype), k], axis=1)
    v_all = jnp.concatenate([v_cache.astype(v.dtype), v], axis=1)
    kpos = jnp.concatenate([pos[0] - L + jnp.arange(L, dtype=jnp.int32), pos])
    bias = rel_bias(table, pos, kpos)
    s = jnp.einsum('bqhd,bkhd->bhqk', q, k_all).astype(jnp.float32) * (DH_ATT ** -0.5) + bias[None]
    p = jax.nn.softmax(s, axis=-1)
    return jnp.einsum('bhqk,bkhd->bqhd', p.astype(v.dtype), v_all).reshape(B, T, H * dh)


def even_mixer(h, pos, S, sconv_buf, w_in, w_out, gn_gain, conv_w):
    B, T, _ = h.shape
    proj = h @ w_in
    splits = [D_RET, 2 * D_RET, 3 * D_RET, 4 * D_RET, 4 * D_RET + D_SCONV, 4 * D_RET + 2 * D_SCONV]
    q, k, v, g, gate_b, gate_c, hv = jnp.split(proj, splits, axis=-1)
    q = rotary(q.reshape(B, T, H_RET, DK_RET), pos)
    k = rotary(k.reshape(B, T, H_RET, DK_RET), pos) * (DK_RET ** -0.5)
    v = v.reshape(B, T, H_RET, DV_RET)
    log_g = retention_log_decay()
    if S is None:
        o, S_new = retention_prompt(q, k, v, log_g)
    else:
        o, S_new = retention_step(q, k, v, S, log_g)
    ret_out = retention_output(o, gn_gain, g, h.dtype)
    conv_y, sconv_new = causal_dwconv(gate_c * hv, sconv_buf, conv_w)
    out = jnp.concatenate([ret_out, gate_b * conv_y], axis=-1) @ w_out
    return out, S_new, sconv_new


def odd_mixer(h, pos, pool_buf, k_cache, v_cache, w_in, w_out, pool_w, pool_scale, table):
    B, T, _ = h.shape
    proj = h @ w_in
    p, q, k, v = jnp.split(proj, [D_POOL, D_POOL + D_ATT, D_POOL + 2 * D_ATT], axis=-1)
    pool_out, pool_new = multiscale_pool(p, pool_buf, pos, pool_w, pool_scale)
    q = q.reshape(B, T, H_ATT, DH_ATT)
    k = k.reshape(B, T, H_ATT, DH_ATT)
    v = v.reshape(B, T, H_ATT, DH_ATT)
    if k_cache is None:
        att, k_new, v_new = chunk_attention_prompt(q, k, v, table)
    else:
        att = chunk_attention_step(q, k, v, k_cache, v_cache, table, pos)
        k_new, v_new = k, v
    out = jnp.concatenate([pool_out, att], axis=-1) @ w_out
    return out, pool_new, k_new, v_new


def conv_ffn(h, buf, w_up, conv_w, w_down):
    up = h @ w_up
    upc, buf_new = causal_dwconv(up, buf, conv_w)
    a, b = jnp.split(upc, 2, axis=-1)
    return (jax.nn.silu(a) * b) @ w_down, buf_new


def run_trunk(x, c, pos, cache, weights):
    (norm_mix, norm_ffn, norm_final, w_ada, b_ada, w_in_even, w_out_even, ret_gn_gain, sconv_w,
     w_in_odd, w_out_odd, pool_w, pool_scale, rel_bias_table, ffn_w_up, ffn_conv, ffn_w_down) = weights
    B = x.shape[0]
    mod = jnp.einsum('bd,lde->lbe', jax.nn.silu(c), w_ada) + b_ada[:, None, :]
    new_ret, new_sconv, new_pool, new_k, new_v, new_ffn = [], [], [], [], [], []
    for l in range(DEPTH):
        sh_m, sc_m, g_m, sh_f, sc_f, g_f = jnp.split(mod[l], 6, axis=-1)
        hm = modulate(rmsnorm(x, norm_mix[l]), sh_m, sc_m)
        i = l // 2
        if l % 2 == 0:
            if cache is None:
                S, sbuf = None, jnp.zeros((B, SCONV_WIDTH - 1, D_SCONV), x.dtype)
            else:
                S, sbuf = cache[0][i], cache[1][i]
            y, S_new, sbuf_new = even_mixer(hm, pos, S, sbuf, w_in_even[i], w_out_even[i], ret_gn_gain[i], sconv_w[i])
            new_ret.append(S_new)
            new_sconv.append(sbuf_new)
        else:
            if cache is None:
                pbuf, kc, vc = jnp.zeros((B, POOL_BUF, D_POOL), x.dtype), None, None
            else:
                pbuf, kc, vc = cache[2][i], cache[3][i], cache[4][i]
            y, pbuf_new, k_new, v_new = odd_mixer(hm, pos, pbuf, kc, vc, w_in_odd[i], w_out_odd[i], pool_w[i], pool_scale[i], rel_bias_table[i])
            new_pool.append(pbuf_new)
            new_k.append(k_new)
            new_v.append(v_new)
        x = x + g_m[:, None, :] * y
        hf = modulate(rmsnorm(x, norm_ffn[l]), sh_f, sc_f)
        fbuf = jnp.zeros((B, FFN_CONV_WIDTH - 1, 2 * D_FF), x.dtype) if cache is None else cache[5][l]
        y, fbuf_new = conv_ffn(hf, fbuf, ffn_w_up[l], ffn_conv[l], ffn_w_down[l])
        new_ffn.append(fbuf_new)
        x = x + g_f[:, None, :] * y
    return (rmsnorm(x, norm_final), jnp.stack(new_ret), jnp.stack(new_sconv), jnp.stack(new_pool),
            jnp.stack(new_k), jnp.stack(new_v), jnp.stack(new_ffn))


def setup_inputs(seed: int = 0) -> dict:
    key = jax.random.key(seed)
    ks = jax.random.split(key, 27)
    f32 = jnp.float32

    def nrm(i, shape, s):
        return jax.random.normal(ks[i], shape, f32) * s

    att_cache = min(N_PREV_CHUNKS * CHUNK, PAST_LEN)
    return {
        'x_prompt': nrm(0, (BATCH, SEQ, D_MODEL), 1.0),
        'x_sample': nrm(1, (DEC_BATCH, DEC_SEQ, D_MODEL), 1.0),
        'state_ret': nrm(2, (N_EVEN, DEC_BATCH, H_RET, DK_RET, DV_RET), 1.0),
        'state_sconv': nrm(3, (N_EVEN, DEC_BATCH, SCONV_WIDTH - 1, D_SCONV), 1.0),
        'state_pool': nrm(4, (N_ODD, DEC_BATCH, POOL_BUF, D_POOL), 1.0),
        'cache_k': nrm(5, (N_ODD, DEC_BATCH, att_cache, H_ATT, DH_ATT), 1.0),
        'cache_v': nrm(6, (N_ODD, DEC_BATCH, att_cache, H_ATT, DH_ATT), 1.0),
        'state_ffn': nrm(7, (DEPTH, DEC_BATCH, FFN_CONV_WIDTH - 1, 2 * D_FF), 1.0),
        'c_prompt': nrm(8, (BATCH, D_MODEL), 1.0),
        'c_sample': nrm(9, (DEC_BATCH, D_MODEL), 1.0),
        'norm_mix': 1.0 + nrm(10, (DEPTH, D_MODEL), 0.02),
        'norm_ffn': 1.0 + nrm(11, (DEPTH, D_MODEL), 0.02),
        'norm_final': 1.0 + nrm(12, (D_MODEL,), 0.02),
        'w_ada': nrm(13, (DEPTH, D_MODEL, 6 * D_MODEL), 0.5 * D_MODEL ** -0.5),
        'b_ada': nrm(14, (DEPTH, 6 * D_MODEL), 0.01),
        'w_in_even': nrm(15, (N_EVEN, D_MODEL, D_IN_EVEN), D_MODEL ** -0.5),
        'w_out_even': nrm(16, (N_EVEN, MIX_WIDTH, D_MODEL), MIX_WIDTH ** -0.5),
        'ret_gn_gain': 1.0 + nrm(17, (N_EVEN, D_RET), 0.02),
        'sconv_w': nrm(18, (N_EVEN, SCONV_WIDTH, D_SCONV), SCONV_WIDTH ** -0.5),
        'w_in_odd': nrm(19, (N_ODD, D_MODEL, D_IN_ODD), D_MODEL ** -0.5),
        'w_out_odd': nrm(20, (N_ODD, MIX_WIDTH, D_MODEL), MIX_WIDTH ** -0.5),
        'pool_w': nrm(21, (N_ODD, N_POOL_GROUPS, POOL_GROUP, POOL_GROUP), POOL_GROUP ** -0.5),
        'pool_scale': 1.0 + nrm(22, (N_ODD, D_POOL), 0.1),
        'rel_bias_table': nrm(23, (N_ODD, H_ATT, REL_SIZE), 0.5),
        'ffn_w_up': nrm(24, (DEPTH, D_MODEL, 2 * D_FF), D_MODEL ** -0.5),
        'ffn_conv': nrm(25, (DEPTH, FFN_CONV_WIDTH, 2 * D_FF), FFN_CONV_WIDTH ** -0.5),
        'ffn_w_down': nrm(26, (DEPTH, D_FF, D_MODEL), D_FF ** -0.5),
    }


def reference(x_prompt, x_sample, state_ret, state_sconv, state_pool, cache_k, cache_v, state_ffn,
              c_prompt, c_sample, norm_mix, norm_ffn, norm_final, w_ada, b_ada, w_in_even, w_out_even,
              ret_gn_gain, sconv_w, w_in_odd, w_out_odd, pool_w, pool_scale, rel_bias_table,
              ffn_w_up, ffn_conv, ffn_w_down):
    weights = (norm_mix, norm_ffn, norm_final, w_ada, b_ada, w_in_even, w_out_even, ret_gn_gain, sconv_w,
               w_in_odd, w_out_odd, pool_w, pool_scale, rel_bias_table, ffn_w_up, ffn_conv, ffn_w_down)
    pos_p = jnp.arange(x_prompt.shape[1], dtype=jnp.int32)
    y_prompt, ret_p, sconv_p, pool_p, k_p, v_p, ffn_p = run_trunk(x_prompt, c_prompt, pos_p, None, weights)
    pos_s = PAST_LEN + jnp.arange(x_sample.shape[1], dtype=jnp.int32)
    cache = (state_ret, state_sconv, state_pool, cache_k, cache_v, state_ffn)
    y_sample, ret_s, sconv_s, pool_s, k_s, v_s, ffn_s = run_trunk(x_sample, c_sample, pos_s, cache, weights)
    return (y_prompt, y_sample, ret_p, ret_s, sconv_p, sconv_s, pool_p, pool_s, k_p, k_s, v_p, v_s, ffn_p, ffn_s)
```

```python
import functools

import numpy as np
import jax
import jax.numpy as jnp
from jax import lax
from jax.experimental import pallas as pl
from jax.experimental.pallas import tpu as pltpu

F32 = jnp.float32
BF16 = jnp.bfloat16

D_MODEL = 1024
DEPTH = 4
PAST_LEN = 4096
CHUNK = 64
H_RET = 8
DK_RET = 64
D_RET = H_RET * DK_RET
ROPE_BASE = 10000.0
D_SCONV = D_MODEL - D_RET
POOL_WINDOWS = (2, 4, 8, 16)
D_POOL = D_MODEL // 2
POOL_GROUP = D_POOL // len(POOL_WINDOWS)
POOL_BUF = max(POOL_WINDOWS) - 1
H_ATT = 8
DH_ATT = 64
D_ATT = H_ATT * DH_ATT
N_PREV_CHUNKS = 8
REL_CLIP = 256
D_FF = 2816
EPS = 1e-6
NEG_INF = -1e30
D_IN_EVEN = 4 * D_RET + 3 * D_SCONV
D_IN_ODD = D_POOL + 3 * D_ATT

LANES = 128
SUBLANES = 8
HEAD_PAIRS = H_RET // 2
ROW_TILE = 512
RET_BLOCK = 256
FFN_COLS = 256
BAND = (N_PREV_CHUNKS + 1) * CHUNK
HIST = N_PREV_CHUNKS * CHUNK
POOL_BASE = 2 * SUBLANES
VMEM_LIMIT = 56 * 1024 * 1024

LOG_G = np.log1p(-(2.0 ** (-5.0 - np.arange(H_RET, dtype=np.float64))))


def _params(n_axes=1):
    return pltpu.CompilerParams(dimension_semantics=("arbitrary",) * n_axes, vmem_limit_bytes=VMEM_LIMIT)


def _whole(shape):
    nd = len(shape)
    return pl.BlockSpec(shape, lambda i: (0,) * nd, pipeline_mode=pl.Buffered(1))


def _whole_out(shape):
    nd = len(shape)
    return pl.BlockSpec(shape, lambda i: (0,) * nd)


def _rows(block_rows, cols):
    return pl.BlockSpec((block_rows, cols), lambda i: (i, 0))


def _lead(shape):
    nd = len(shape)
    return pl.BlockSpec((1,) + tuple(shape), lambda i: (i,) + (0,) * nd)


def _dot(a, b):
    return jnp.dot(a, b, preferred_element_type=F32)


def _dot_nt(a, b):
    return lax.dot_general(a, b, (((1,), (1,)), ((), ())), preferred_element_type=F32)


def _rmsnorm(x, g):
    return x * lax.rsqrt(jnp.mean(x * x, axis=-1, keepdims=True) + EPS) * g


def _norm_mod(x, g, shift, scale):
    return _rmsnorm(x, g) * (1.0 + scale) + shift


def _silu(x):
    return x * (1.0 / (1.0 + jnp.exp(-x)))


def _low_half(shape):
    return (lax.broadcasted_iota(jnp.int32, shape, len(shape) - 1) % LANES) < DK_RET


def _shift_rows(u, prev8, s):
    rolled = pltpu.roll(u, s, axis=0)
    prolled = pltpu.roll(prev8, s, axis=0)
    row = lax.broadcasted_iota(jnp.int32, prev8.shape, 0)
    first = jnp.where(row < s, prolled, rolled[0:SUBLANES])
    return jnp.concatenate([first, rolled[SUBLANES:]], axis=0)


def _shift_rows_streams(u, head_rows, s, t):
    row_in_stream = lax.broadcasted_iota(jnp.int32, u.shape, 0) % t
    return jnp.where(row_in_stream < s, head_rows, pltpu.roll(u, s, axis=0))


def _ada_kernel(c_ref, w_ref, b_ref, o_ref):
    c = c_ref[...]
    o_ref[0] = _dot(_silu(c).astype(BF16), w_ref[0].astype(BF16)) + b_ref[0]


def _ada(c_all, w_ada, b_ada):
    rows = c_all.shape[0]
    tn = 1536
    return pl.pallas_call(
        _ada_kernel,
        out_shape=jax.ShapeDtypeStruct((DEPTH, rows, 6 * D_MODEL), F32),
        grid=(DEPTH, 6 * D_MODEL // tn),
        in_specs=[
            pl.BlockSpec((rows, D_MODEL), lambda l, j: (0, 0)),
            pl.BlockSpec((1, D_MODEL, tn), lambda l, j: (l, 0, j)),
            pl.BlockSpec((1, 1, tn), lambda l, j: (l, 0, j)),
        ],
        out_specs=pl.BlockSpec((1, rows, tn), lambda l, j: (l, 0, j)),
        compiler_params=_params(2),
        name="ada_mod",
    )(c_all, w_ada, b_ada.reshape(DEPTH, 1, 6 * D_MODEL))


def _rotary_pair(x, cos, sin_signed):
    lane = lax.broadcasted_iota(jnp.int32, x.shape, 1)
    first_half = (lane % DK_RET) < (DK_RET // 2)
    swapped = jnp.where(first_half, pltpu.roll(x, LANES - DK_RET // 2, axis=1),
                        pltpu.roll(x, DK_RET // 2, axis=1))
    return x * cos + swapped * sin_signed


def _retention_block(proj, cos, sin_signed, dec_ref, xi, zeta_t_ref, gain, cross_fn, update_fn):
    tb = proj.shape[0]
    low = _low_half((tb, LANES))
    inv_n = 1.0 / DK_RET
    outs = []
    for p in range(HEAD_PAIRS):
        c0 = p * LANES
        q = _rotary_pair(proj[:, c0:c0 + LANES], cos, sin_signed)
        k = _rotary_pair(proj[:, D_RET + c0:D_RET + c0 + LANES], cos, sin_signed) * (DK_RET ** -0.5)
        v16 = proj[:, 2 * D_RET + c0:2 * D_RET + c0 + LANES].astype(BF16)
        g = proj[:, 3 * D_RET + c0:3 * D_RET + c0 + LANES]
        k_t = k.T
        k_t16 = k_t.astype(BF16)
        kz_t16 = (k_t * zeta_t_ref[c0:c0 + LANES, :]).astype(BF16)
        q16 = q.astype(BF16)
        qe16 = jnp.where(low, q, 0.0).astype(BF16)
        qo16 = jnp.where(low, 0.0, q).astype(BF16)
        s_e = (_dot(qe16, k_t16) * dec_ref[2 * p]).astype(BF16)
        s_o = (_dot(qo16, k_t16) * dec_ref[2 * p + 1]).astype(BF16)
        inner = jnp.where(low, _dot(s_e, v16), _dot(s_o, v16))
        o = inner + cross_fn(p, q16) * xi[:, c0:c0 + LANES]
        s_lo = jnp.sum(jnp.where(low, o, 0.0), axis=-1, keepdims=True)
        s_hi = jnp.sum(jnp.where(low, 0.0, o), axis=-1, keepdims=True)
        d = o - jnp.where(low, s_lo, s_hi) * inv_n
        d2 = d * d
        v_lo = jnp.sum(jnp.where(low, d2, 0.0), axis=-1, keepdims=True)
        v_hi = jnp.sum(jnp.where(low, 0.0, d2), axis=-1, keepdims=True)
        on = d * lax.rsqrt(jnp.where(low, v_lo, v_hi) * inv_n + EPS)
        outs.append(_silu(g) * (on * gain[:, c0:c0 + LANES]))
        update_fn(p, kz_t16, v16)
    return jnp.concatenate(outs, axis=-1)


def _even_tail(x, proj, ret_out, conv_in_shift, cw_ref, gate, w_out_ref):
    gate_b = proj[:, 4 * D_RET:4 * D_RET + D_SCONV]
    u = proj[:, 4 * D_RET + D_SCONV:4 * D_RET + 2 * D_SCONV] * proj[:, 4 * D_RET + 2 * D_SCONV:]
    conv = cw_ref[0:1, :] * conv_in_shift(u, 2) + cw_ref[1:2, :] * conv_in_shift(u, 1) + cw_ref[2:3, :] * u
    mixed = jnp.concatenate([ret_out, gate_b * conv], axis=-1).astype(BF16)
    return x + gate * _dot(mixed, w_out_ref[...]), u


def _even_prompt_kernel(x_ref, g_ref, sh_ref, sc_ref, gate_ref, w_in_ref, w_out_ref, gain_ref, cw_ref,
                        cos_ref, sin_ref, dec_ref, xi_ref, zt_ref, gmat_ref, r0_ref, u0_ref,
                        o_ref, r_out_ref, u_out_ref, r_scr, u_scr):
    i = pl.program_id(0)

    @pl.when(i == 0)
    def _():
        r_scr[...] = r0_ref[...]
        u_scr[...] = u0_ref[...]

    x = x_ref[...]
    h = _norm_mod(x, g_ref[...], sh_ref[...], sc_ref[...]).astype(BF16)
    proj = _dot(h, w_in_ref[...])
    tm = x.shape[0]
    tb = dec_ref.shape[1]
    r_i = lax.broadcasted_iota(jnp.int32, (LANES, LANES), 0) < DK_RET
    c_i = lax.broadcasted_iota(jnp.int32, (LANES, LANES), 1) < DK_RET
    blockdiag = r_i == c_i

    def cross_fn(p, q16):
        return _dot(q16, r_scr[p].astype(BF16))

    def update_fn(p, kz_t16, v16):
        r_scr[p] = r_scr[p] * gmat_ref[p] + jnp.where(blockdiag, _dot(kz_t16, v16), 0.0)

    rets = []
    for r in range(tm // tb):
        rows = slice(r * tb, (r + 1) * tb)
        rets.append(_retention_block(proj[rows, :], cos_ref[rows, :], sin_ref[rows, :], dec_ref, xi_ref[...],
                                     zt_ref, gain_ref[...], cross_fn, update_fn))
    ret_out = jnp.concatenate(rets, axis=0)

    prev8 = u_scr[...]
    out, u = _even_tail(x, proj, ret_out, lambda u, s: _shift_rows(u, prev8, s), cw_ref, gate_ref[...], w_out_ref)
    u_scr[...] = u[tm - SUBLANES:, :]
    o_ref[...] = out
    r_out_ref[...] = r_scr[...]
    u_out_ref[...] = u[tm - SUBLANES:, :]


def _even_prompt(x, g, sh, sc, gate, w_in16, w_out16, gain, cw, tabs, r0, u0):
    t = x.shape[0]
    tm, tb = ROW_TILE, RET_BLOCK
    cos, sin, dec, xi, zt, gmat = tabs
    vec = _whole((1, D_MODEL))
    state = (HEAD_PAIRS, LANES, LANES)
    return pl.pallas_call(
        _even_prompt_kernel,
        out_shape=(jax.ShapeDtypeStruct((t, D_MODEL), F32),
                   jax.ShapeDtypeStruct(state, F32),
                   jax.ShapeDtypeStruct((SUBLANES, D_SCONV), F32)),
        grid=(t // tm,),
        in_specs=[_rows(tm, D_MODEL), vec, vec, vec, vec,
                  _whole((D_MODEL, D_IN_EVEN)), _whole((D_MODEL, D_MODEL)),
                  _whole((1, D_RET)), _whole((3, D_SCONV)),
                  _rows(tm, LANES), _rows(tm, LANES),
                  _whole((H_RET, tb, tb)), _whole((tb, D_RET)), _whole((D_RET, tb)),
                  _whole(state), _whole(state), _whole((SUBLANES, D_SCONV))],
        out_specs=(_rows(tm, D_MODEL), _whole_out(state), _whole_out((SUBLANES, D_SCONV))),
        scratch_shapes=[pltpu.VMEM(state, F32), pltpu.VMEM((SUBLANES, D_SCONV), F32)],
        compiler_params=_params(),
        name="even_prompt",
    )(x, g, sh, sc, gate, w_in16, w_out16, gain, cw, cos, sin, dec, xi, zt, gmat, r0, u0)


def _even_sample_kernel(x_ref, g_ref, sh_ref, sc_ref, gate_ref, w_in_ref, w_out_ref, gain_ref, cw_ref,
                        cos_ref, sin_ref, dec_ref, xi_ref, zt_ref, gwide_ref, s_stack_ref, s_wide_ref,
                        u1_ref, u2_ref, o_ref, s_out_ref, u_out_ref, *, t):
    x = x_ref[...]
    rows = x.shape[0]
    n_streams = rows // t
    wide = n_streams * LANES
    h = _norm_mod(x, g_ref[...], sh_ref[...], sc_ref[...]).astype(BF16)
    proj = _dot(h, w_in_ref[...])
    own = (lax.broadcasted_iota(jnp.int32, (rows, wide), 0) // t
           == lax.broadcasted_iota(jnp.int32, (rows, wide), 1) // LANES)
    r_i = lax.broadcasted_iota(jnp.int32, (LANES, wide), 0) < DK_RET
    blockdiag = r_i == _low_half((LANES, wide))

    def expand(a16):
        tiled = jnp.concatenate([a16.astype(F32)] * n_streams, axis=-1)
        return jnp.where(own, tiled, 0.0).astype(BF16)

    def cross_fn(p, q16):
        return _dot(expand(q16), s_stack_ref[p].astype(BF16))

    def update_fn(p, kz_t16, v16):
        kv = _dot(kz_t16, expand(v16))
        s_out_ref[p] = s_wide_ref[p] * gwide_ref[p] + jnp.where(blockdiag, kv, 0.0)

    ret_out = _retention_block(proj, cos_ref[...], sin_ref[...], dec_ref, xi_ref[...], zt_ref, gain_ref[...],
                               cross_fn, update_fn)
    heads = {1: u1_ref, 2: u2_ref}
    out, u = _even_tail(x, proj, ret_out, lambda u, s: _shift_rows_streams(u, heads[s][...], s, t),
                        cw_ref, gate_ref[...], w_out_ref)
    o_ref[...] = out
    u_out_ref[...] = u


def _even_sample(x, g, sh, sc, gate, w_in16, w_out16, gain, cw, tabs, s_stack, s_wide, u1, u2, t):
    rows = x.shape[0]
    wide = (rows // t) * LANES
    cos, sin, dec, xi, zt, gwide = tabs
    vec = _whole((1, D_MODEL))
    mat = _whole((rows, D_MODEL))
    return pl.pallas_call(
        functools.partial(_even_sample_kernel, t=t),
        out_shape=(jax.ShapeDtypeStruct((rows, D_MODEL), F32),
                   jax.ShapeDtypeStruct((HEAD_PAIRS, LANES, wide), F32),
                   jax.ShapeDtypeStruct((rows, D_SCONV), F32)),
        grid=(1,),
        in_specs=[mat, vec, mat, mat, mat,
                  _whole((D_MODEL, D_IN_EVEN)), _whole((D_MODEL, D_MODEL)),
                  _whole((1, D_RET)), _whole((3, D_SCONV)),
                  _whole((rows, LANES)), _whole((rows, LANES)),
                  _whole((H_RET, rows, rows)), _whole((rows, D_RET)), _whole((D_RET, rows)),
                  _whole((HEAD_PAIRS, LANES, wide)), _whole((HEAD_PAIRS, wide, LANES)),
                  _whole((HEAD_PAIRS, LANES, wide)),
                  _whole((rows, D_SCONV)), _whole((rows, D_SCONV))],
        out_specs=(_whole_out((rows, D_MODEL)), _whole_out((HEAD_PAIRS, LANES, wide)),
                   _whole_out((rows, D_SCONV))),
        compiler_params=_params(),
        name="even_sample",
    )(x, g, sh, sc, gate, w_in16, w_out16, gain, cw, cos, sin, dec, xi, zt, gwide, s_stack, s_wide, u1, u2)


def _pool(hist_ref, p, pos, pool_w_ref, scale):
    t = p.shape[0]
    outs = []
    for gi, w in enumerate(POOL_WINDOWS):
        cols = slice(gi * POOL_GROUP, (gi + 1) * POOL_GROUP)
        win = p[:, cols]
        for d in range(1, w):
            win = win + hist_ref[POOL_BASE - d:POOL_BASE - d + t, cols]
        inv_cnt = 1.0 / jnp.minimum(pos + 1, w).astype(F32)
        pooled = win * inv_cnt - p[:, cols]
        outs.append(_dot(pooled.astype(BF16), pool_w_ref[gi]) * scale[:, cols])
    return jnp.concatenate(outs, axis=-1)


def _odd_prompt_kernel(x_ref, g_ref, sh_ref, sc_ref, gate_ref, w_in_ref, w_out_ref, pw_ref, ps_ref, bias_ref, p0_ref,
                       o_ref, p_out_ref, k_out_ref, v_out_ref, pbuf, kbuf, vbuf, q_scr, att_scr):
    i = pl.program_id(0)
    tm = x_ref.shape[0]

    @pl.when(i == 0)
    def _():
        pbuf[0:POOL_BASE, :] = p0_ref[...]
        kbuf[0:HIST, :] = jnp.zeros((HIST, D_ATT), BF16)
        vbuf[0:HIST, :] = jnp.zeros((HIST, D_ATT), BF16)

    x = x_ref[...]
    h = _norm_mod(x, g_ref[...], sh_ref[...], sc_ref[...]).astype(BF16)
    proj = _dot(h, w_in_ref[...])
    p = proj[:, :D_POOL]
    k = proj[:, D_POOL + D_ATT:D_POOL + 2 * D_ATT]
    v = proj[:, D_POOL + 2 * D_ATT:]
    q_scr[...] = proj[:, D_POOL:D_POOL + D_ATT] * (DH_ATT ** -0.5)
    pbuf[POOL_BASE:POOL_BASE + tm, :] = p
    kbuf[HIST:HIST + tm, :] = k.astype(BF16)
    vbuf[HIST:HIST + tm, :] = v.astype(BF16)
    k_out_ref[...] = k[tm - HIST:, :]
    v_out_ref[...] = v[tm - HIST:, :]

    pos = i * tm + lax.broadcasted_iota(jnp.int32, (tm, 1), 0)
    pool_out = _pool(pbuf, p, pos, pw_ref, ps_ref[...])

    low = _low_half((CHUNK, LANES))
    col = lax.broadcasted_iota(jnp.int32, (CHUNK, BAND), 1)

    def chunk_body(j, carry):
        r0 = pl.multiple_of(j * CHUNK, CHUNK)
        first_valid = (N_PREV_CHUNKS - (i * (tm // CHUNK) + j)) * CHUNK
        valid = col >= first_valid
        outs = []
        for pr in range(HEAD_PAIRS):
            cols = slice(pr * LANES, (pr + 1) * LANES)
            qp = q_scr[pl.ds(r0, CHUNK), cols]
            kb = kbuf[pl.ds(r0, BAND), cols]
            vb = vbuf[pl.ds(r0, BAND), cols]
            halves = []
            for hh, qh in ((2 * pr, jnp.where(low, qp, 0.0)), (2 * pr + 1, jnp.where(low, 0.0, qp))):
                s = _dot_nt(qh.astype(BF16), kb) + bias_ref[hh]
                s = jnp.where(valid, s, NEG_INF)
                e = jnp.exp(s - jnp.max(s, axis=-1, keepdims=True))
                inv_l = 1.0 / jnp.sum(e, axis=-1, keepdims=True)
                halves.append(_dot(e.astype(BF16), vb) * inv_l)
            outs.append(jnp.where(low, halves[0], halves[1]))
        att_scr[pl.ds(r0, CHUNK), :] = jnp.concatenate(outs, axis=-1)
        return carry

    lax.fori_loop(0, tm // CHUNK, chunk_body, 0)

    kbuf[0:HIST, :] = kbuf[tm:tm + HIST, :]
    vbuf[0:HIST, :] = vbuf[tm:tm + HIST, :]
    tail = pbuf[tm:tm + POOL_BASE, :]
    pbuf[0:POOL_BASE, :] = tail
    p_out_ref[...] = tail

    mixed = jnp.concatenate([pool_out, att_scr[...]], axis=-1).astype(BF16)
    o_ref[...] = x + gate_ref[...] * _dot(mixed, w_out_ref[...])


def _odd_prompt(x, g, sh, sc, gate, w_in16, w_out16, pw16, ps, bias, p0):
    t = x.shape[0]
    tm = ROW_TILE
    assert tm == HIST and t % tm == 0
    vec = _whole((1, D_MODEL))
    return pl.pallas_call(
        _odd_prompt_kernel,
        out_shape=(jax.ShapeDtypeStruct((t, D_MODEL), F32),
                   jax.ShapeDtypeStruct((POOL_BASE, D_POOL), F32),
                   jax.ShapeDtypeStruct((HIST, D_ATT), F32),
                   jax.ShapeDtypeStruct((HIST, D_ATT), F32)),
        grid=(t // tm,),
        in_specs=[_rows(tm, D_MODEL), vec, vec, vec, vec,
                  _whole((D_MODEL, D_IN_ODD)), _whole((D_MODEL, D_MODEL)),
                  _whole((len(POOL_WINDOWS), POOL_GROUP, POOL_GROUP)), _whole((1, D_POOL)),
                  _whole((H_ATT, CHUNK, BAND)), _whole((POOL_BASE, D_POOL))],
        out_specs=(_rows(tm, D_MODEL), _whole_out((POOL_BASE, D_POOL)),
                   _whole_out((HIST, D_ATT)), _whole_out((HIST, D_ATT))),
        scratch_shapes=[pltpu.VMEM((POOL_BASE + tm, D_POOL), F32),
                        pltpu.VMEM((HIST + tm, D_ATT), BF16), pltpu.VMEM((HIST + tm, D_ATT), BF16),
                        pltpu.VMEM((tm, D_ATT), F32), pltpu.VMEM((tm, D_ATT), F32)],
        compiler_params=_params(),
        name="odd_prompt",
    )(x, g, sh, sc, gate, w_in16, w_out16, pw16, ps, bias, p0)


def _odd_sample_kernel(x_ref, g_ref, sh_ref, sc_ref, gate_ref, w_in_ref, w_out_ref, pw_ref, ps_ref,
                       bias_c_ref, bias_n_ref, p0_ref, kc_ref, vc_ref,
                       o_ref, p_out_ref, k_out_ref, v_out_ref, proj_scr, mix_scr, pbuf, *, pos0, t):
    b = pl.program_id(0)
    rows = x_ref.shape[0]

    @pl.when(b == 0)
    def _():
        h = _norm_mod(x_ref[...], g_ref[...], sh_ref[...], sc_ref[...]).astype(BF16)
        proj = _dot(h, w_in_ref[...])
        proj_scr[...] = proj
        k_out_ref[...] = proj[:, D_POOL + D_ATT:D_POOL + 2 * D_ATT]
        v_out_ref[...] = proj[:, D_POOL + 2 * D_ATT:]

    r0 = pl.multiple_of(b * t, t)
    proj = proj_scr[pl.ds(r0, t), :]
    p = proj[:, :D_POOL]
    pbuf[0:POOL_BASE, :] = p0_ref[0]
    pbuf[POOL_BASE:POOL_BASE + t, :] = p
    p_out_ref[0] = p
    pos = pos0 + lax.broadcasted_iota(jnp.int32, (t, 1), 0)
    pool_out = _pool(pbuf, p, pos, pw_ref, ps_ref[...])

    low = _low_half((t, LANES))
    own = lax.broadcasted_iota(jnp.int32, (t, rows), 1) // t == b
    outs = []
    for pr in range(HEAD_PAIRS):
        cols = slice(pr * LANES, (pr + 1) * LANES)
        qp = proj[:, D_POOL + pr * LANES:D_POOL + (pr + 1) * LANES] * (DH_ATT ** -0.5)
        kn = proj_scr[:, D_POOL + D_ATT + pr * LANES:D_POOL + D_ATT + (pr + 1) * LANES].astype(BF16)
        vn = proj_scr[:, D_POOL + 2 * D_ATT + pr * LANES:D_POOL + 2 * D_ATT + (pr + 1) * LANES].astype(BF16)
        kc = kc_ref[0, :, cols].astype(BF16)
        vc = vc_ref[0, :, cols].astype(BF16)
        halves = []
        for hh, qh in ((2 * pr, jnp.where(low, qp, 0.0)), (2 * pr + 1, jnp.where(low, 0.0, qp))):
            qh16 = qh.astype(BF16)
            s_c = _dot_nt(qh16, kc) + bias_c_ref[hh]
            s_n = jnp.where(own, _dot_nt(qh16, kn) + bias_n_ref[hh], NEG_INF)
            m = jnp.maximum(jnp.max(s_c, axis=-1, keepdims=True), jnp.max(s_n, axis=-1, keepdims=True))
            e_c = jnp.exp(s_c - m)
            e_n = jnp.exp(s_n - m)
            inv_l = 1.0 / (jnp.sum(e_c, axis=-1, keepdims=True) + jnp.sum(e_n, axis=-1, keepdims=True))
            halves.append((_dot(e_c.astype(BF16), vc) + _dot(e_n.astype(BF16), vn)) * inv_l)
        outs.append(jnp.where(low, halves[0], halves[1]))
    mix_scr[pl.ds(r0, t), :] = jnp.concatenate([pool_out] + outs, axis=-1)

    @pl.when(b == pl.num_programs(0) - 1)
    def _():
        o_ref[...] = x_ref[...] + gate_ref[...] * _dot(mix_scr[...].astype(BF16), w_out_ref[...])


def _odd_sample(x, g, sh, sc, gate, w_in16, w_out16, pw16, ps, bias_c, bias_n, p0, kc, vc, t, pos0):
    rows = x.shape[0]
    n_streams = rows // t
    cache = kc.shape[1]
    vec = _whole((1, D_MODEL))
    mat = _whole((rows, D_MODEL))
    return pl.pallas_call(
        functools.partial(_odd_sample_kernel, pos0=pos0, t=t),
        out_shape=(jax.ShapeDtypeStruct((rows, D_MODEL), F32),
                   jax.ShapeDtypeStruct((n_streams, t, D_POOL), F32),
                   jax.ShapeDtypeStruct((rows, D_ATT), F32),
                   jax.ShapeDtypeStruct((rows, D_ATT), F32)),
        grid=(n_streams,),
        in_specs=[mat, vec, mat, mat, mat,
                  _whole((D_MODEL, D_IN_ODD)), _whole((D_MODEL, D_MODEL)),
                  _whole((len(POOL_WINDOWS), POOL_GROUP, POOL_GROUP)), _whole((1, D_POOL)),
                  _whole((H_ATT, t, cache)), _whole((H_ATT, t, rows)),
                  _lead((POOL_BASE, D_POOL)), _lead((cache, D_ATT)), _lead((cache, D_ATT))],
        out_specs=(_whole_out((rows, D_MODEL)), _lead((t, D_POOL)),
                   _whole_out((rows, D_ATT)), _whole_out((rows, D_ATT))),
        scratch_shapes=[pltpu.VMEM((rows, D_IN_ODD), F32), pltpu.VMEM((rows, D_MODEL), F32),
                        pltpu.VMEM((POOL_BASE + t, D_POOL), F32)],
        compiler_params=_params(),
        name="odd_sample",
    )(x, g, sh, sc, gate, w_in16, w_out16, pw16, ps, bias_c, bias_n, p0, kc, vc)


def _ffn_tile(x, h16, w_up_ref, cw_ref, w_down_ref, shifted, store_up):
    acc = jnp.zeros((x.shape[0], D_MODEL), F32)
    for c in range(D_FF // FFN_COLS):
        halves = []
        for off in (0, D_FF):
            cols = slice(off + c * FFN_COLS, off + (c + 1) * FFN_COLS)
            up = _dot(h16, w_up_ref[:, cols])
            conv = (cw_ref[0:1, cols] * shifted(up, cols, 2) + cw_ref[1:2, cols] * shifted(up, cols, 1)
                    + cw_ref[2:3, cols] * up)
            store_up(up, cols)
            halves.append(conv)
        act = (_silu(halves[0]) * halves[1]).astype(BF16)
        acc = acc + _dot(act, w_down_ref[c * FFN_COLS:(c + 1) * FFN_COLS, :])
    return acc


def _ffn_prompt_kernel(x_ref, g_ref, sh_ref, sc_ref, gate_ref, w_up_ref, cw_ref, w_down_ref, f0_ref, gf_ref,
                       o_ref, f_out_ref, f_scr, *, final_norm):
    i = pl.program_id(0)
    tm = x_ref.shape[0]

    @pl.when(i == 0)
    def _():
        f_scr[...] = f0_ref[...]

    x = x_ref[...]
    h16 = _norm_mod(x, g_ref[...], sh_ref[...], sc_ref[...]).astype(BF16)

    def shifted(up, cols, s):
        return _shift_rows(up, f_scr[:, cols], s)

    def store_up(up, cols):
        f_scr[:, cols] = up[tm - SUBLANES:, :]

    out = x + gate_ref[...] * _ffn_tile(x, h16, w_up_ref, cw_ref, w_down_ref, shifted, store_up)
    if final_norm:
        out = _rmsnorm(out, gf_ref[...])
    o_ref[...] = out
    f_out_ref[...] = f_scr[...]


def _ffn_prompt(x, g, sh, sc, gate, w_up16, cw, w_down16, f0, gf, final_norm):
    t = x.shape[0]
    tm = ROW_TILE
    vec = _whole((1, D_MODEL))
    return pl.pallas_call(
        functools.partial(_ffn_prompt_kernel, final_norm=final_norm),
        out_shape=(jax.ShapeDtypeStruct((t, D_MODEL), F32), jax.ShapeDtypeStruct((SUBLANES, 2 * D_FF), F32)),
        grid=(t // tm,),
        in_specs=[_rows(tm, D_MODEL), vec, vec, vec, vec,
                  _whole((D_MODEL, 2 * D_FF)), _whole((3, 2 * D_FF)), _whole((D_FF, D_MODEL)),
                  _whole((SUBLANES, 2 * D_FF)), vec],
        out_specs=(_rows(tm, D_MODEL), _whole_out((SUBLANES, 2 * D_FF))),
        scratch_shapes=[pltpu.VMEM((SUBLANES, 2 * D_FF), F32)],
        compiler_params=_params(),
        name="ffn_prompt",
    )(x, g, sh, sc, gate, w_up16, cw, w_down16, f0, gf)


def _ffn_sample_kernel(x_ref, g_ref, sh_ref, sc_ref, gate_ref, w_up_ref, cw_ref, w_down_ref, s1_ref, s2_ref, gf_ref,
                       o_ref, up_out_ref, *, final_norm, t):
    x = x_ref[...]
    h16 = _norm_mod(x, g_ref[...], sh_ref[...], sc_ref[...]).astype(BF16)
    heads = {1: s1_ref, 2: s2_ref}

    def shifted(up, cols, s):
        return _shift_rows_streams(up, heads[s][:, cols], s, t)

    def store_up(up, cols):
        up_out_ref[:, cols] = up

    out = x + gate_ref[...] * _ffn_tile(x, h16, w_up_ref, cw_ref, w_down_ref, shifted, store_up)
    if final_norm:
        out = _rmsnorm(out, gf_ref[...])
    o_ref[...] = out


def _ffn_sample(x, g, sh, sc, gate, w_up16, cw, w_down16, s1, s2, gf, final_norm, t):
    rows = x.shape[0]
    vec = _whole((1, D_MODEL))
    mat = _whole((rows, D_MODEL))
    wide = _whole((rows, 2 * D_FF))
    return pl.pallas_call(
        functools.partial(_ffn_sample_kernel, final_norm=final_norm, t=t),
        out_shape=(jax.ShapeDtypeStruct((rows, D_MODEL), F32), jax.ShapeDtypeStruct((rows, 2 * D_FF), F32)),
        grid=(1,),
        in_specs=[mat, vec, mat, mat, mat,
                  _whole((D_MODEL, 2 * D_FF)), _whole((3, 2 * D_FF)), _whole((D_FF, D_MODEL)),
                  wide, wide, vec],
        out_specs=(_whole_out((rows, D_MODEL)), _whole_out((rows, 2 * D_FF))),
        compiler_params=_params(),
        name="ffn_sample",
    )(x, g, sh, sc, gate, w_up16, cw, w_down16, s1, s2, gf)


def _rotary_tables(pos):
    half = DK_RET // 2
    inv = ROPE_BASE ** (-jnp.arange(half, dtype=F32) / half)
    ang = pos.astype(F32)[:, None] * inv[None, :]
    cos, sin = jnp.cos(ang), jnp.sin(ang)
    return jnp.concatenate([cos] * 4, axis=-1), jnp.concatenate([-sin, sin, -sin, sin], axis=-1)


def _retention_tables(tb, n_streams=1):
    idx = np.arange(tb, dtype=np.float64)
    diff = idx[:, None] - idx[None, :]
    dec1 = np.where(diff[None] >= 0, np.exp(LOG_G[:, None, None] * np.maximum(diff, 0.0)[None]), 0.0)
    dec = np.zeros((H_RET, n_streams * tb, n_streams * tb))
    for b in range(n_streams):
        dec[:, b * tb:(b + 1) * tb, b * tb:(b + 1) * tb] = dec1
    xi = np.tile(np.repeat(np.exp(LOG_G[:, None] * (idx + 1)[None, :]).T, DK_RET, axis=1), (n_streams, 1))
    zeta_t = np.tile(np.repeat(np.exp(LOG_G[:, None] * (tb - 1 - idx)[None, :]), DK_RET, axis=0), (1, n_streams))
    gmat = np.zeros((HEAD_PAIRS, LANES, LANES))
    for h in range(H_RET):
        o = (h % 2) * DK_RET
        gmat[h // 2, o:o + DK_RET, o:o + DK_RET] = np.exp(LOG_G[h] * tb)
    gmat = np.tile(gmat, (1, 1, n_streams))
    return tuple(jnp.asarray(a, F32) for a in (dec, xi, zeta_t, gmat))


def _pair_state(s):
    lead = s.shape[:-3]
    s = s.reshape(lead + (HEAD_PAIRS, 2, DK_RET, DK_RET))
    z = jnp.zeros_like(s[..., 0, :, :])
    top = jnp.concatenate([s[..., 0, :, :], z], axis=-1)
    bot = jnp.concatenate([z, s[..., 1, :, :]], axis=-1)
    return jnp.concatenate([top, bot], axis=-2)


def _unpair_state(r):
    a = r[..., :DK_RET, :DK_RET]
    b = r[..., DK_RET:, DK_RET:]
    s = jnp.stack([a, b], axis=-3)
    return s.reshape(r.shape[:-3] + (H_RET, DK_RET, DK_RET))


def _rel_bias(table, qpos, kpos):
    rel = jnp.clip(qpos[:, None] - kpos[None, :], -(CHUNK - 1), REL_CLIP) + (CHUNK - 1)
    return table[:, rel].astype(F32)


def _tail_rows(a, n):
    return a[..., a.shape[-2] - n:, :]


def _stream_heads(st, t, s):
    n, smax, c = st.shape
    zeros = jnp.zeros((n, t - s, c), st.dtype)
    return jnp.concatenate([st[:, smax - s:], zeros], axis=1).reshape(n * t, c)


def kernel(x_prompt, x_sample, state_ret, state_sconv, state_pool, cache_k, cache_v, state_ffn, c_prompt, c_sample,
           norm_mix, norm_ffn, norm_final, w_ada, b_ada, w_in_even, w_out_even, ret_gn_gain, sconv_w, w_in_odd,
           w_out_odd, pool_w, pool_scale, rel_bias_table, ffn_w_up, ffn_conv, ffn_w_down):
    n_prompt, seq, _ = x_prompt.shape
    n_streams, t_s, _ = x_sample.shape
    assert n_prompt == 1
    rows_s = n_streams * t_s

    c_all = jnp.concatenate([c_prompt, c_sample], axis=0)
    pad = (-c_all.shape[0]) % SUBLANES
    mod = _ada(jnp.pad(c_all, ((0, pad), (0, 0))), w_ada, b_ada)
    mod_p = mod[:, 0:1, :].reshape(DEPTH, 1, 6, D_MODEL)
    mod_s = jnp.repeat(mod[:, 1:1 + n_streams, :], t_s, axis=1).reshape(DEPTH, rows_s, 6, D_MODEL)

    bf = lambda w: w.astype(BF16)
    w_in_even16, w_out_even16 = bf(w_in_even), bf(w_out_even)
    w_in_odd16, w_out_odd16, pool_w16 = bf(w_in_odd), bf(w_out_odd), bf(pool_w)
    w_up16, w_down16 = bf(ffn_w_up), bf(ffn_w_down)

    pos_p = jnp.arange(seq, dtype=jnp.int32)
    pos_s = PAST_LEN + jnp.arange(t_s, dtype=jnp.int32)
    tabs_p = _rotary_tables(pos_p) + _retention_tables(RET_BLOCK)
    tabs_s = tuple(jnp.tile(a, (n_streams, 1)) for a in _rotary_tables(pos_s)) + _retention_tables(t_s, n_streams)

    cache_len = cache_k.shape[2]
    band_q = jnp.arange(CHUNK, dtype=jnp.int32) + HIST
    bias_p = [_rel_bias(rel_bias_table[i], band_q, jnp.arange(BAND, dtype=jnp.int32)) for i in range(DEPTH // 2)]
    kpos_s = jnp.concatenate([pos_s[0] - cache_len + jnp.arange(cache_len, dtype=jnp.int32), pos_s])
    bias_s = [_rel_bias(rel_bias_table[i], pos_s, kpos_s) for i in range(DEPTH // 2)]

    xp = x_prompt.reshape(seq, D_MODEL)
    xs = x_sample.reshape(rows_s, D_MODEL)
    row = lambda a: a.reshape(1, -1)

    ret_p, ret_s, sconv_p, sconv_s, pool_p, pool_s = [], [], [], [], [], []
    k_p, k_s, v_p, v_s, ffn_p, ffn_s = [], [], [], [], [], []
    for l in range(DEPTH):
        i = l // 2
        mp = [mod_p[l, :, j, :] for j in range(6)]
        ms = [mod_s[l, :, j, :] for j in range(6)]
        gm = row(norm_mix[l])
        if l % 2 == 0:
            gain, cw = row(ret_gn_gain[i]), sconv_w[i]
            xp, r_new, u_new = _even_prompt(
                xp, gm, mp[0], mp[1], mp[2], w_in_even16[i], w_out_even16[i], gain, cw, tabs_p,
                jnp.zeros((HEAD_PAIRS, LANES, LANES), F32), jnp.zeros((SUBLANES, D_SCONV), F32))
            ret_p.append(_unpair_state(r_new)[None])
            sconv_p.append(_tail_rows(u_new, 2)[None])
            paired = _pair_state(state_ret[i])
            s_stack = jnp.transpose(paired, (1, 0, 2, 3)).reshape(HEAD_PAIRS, n_streams * LANES, LANES)
            s_wide = jnp.transpose(paired, (1, 2, 0, 3)).reshape(HEAD_PAIRS, LANES, n_streams * LANES)
            xs, s_new, u_all = _even_sample(
                xs, gm, ms[0], ms[1], ms[2], w_in_even16[i], w_out_even16[i], gain, cw, tabs_s,
                s_stack, s_wide, _stream_heads(state_sconv[i], t_s, 1), _stream_heads(state_sconv[i], t_s, 2), t_s)
            s_new = jnp.transpose(s_new.reshape(HEAD_PAIRS, LANES, n_streams, LANES), (2, 0, 1, 3))
            ret_s.append(_unpair_state(s_new))
            sconv_s.append(_tail_rows(u_all.reshape(n_streams, t_s, D_SCONV), 2))
        else:
            ps = row(pool_scale[i])
            xp, p_new, k_new, v_new = _odd_prompt(
                xp, gm, mp[0], mp[1], mp[2], w_in_odd16[i], w_out_odd16[i], pool_w16[i], ps, bias_p[i],
                jnp.zeros((POOL_BASE, D_POOL), F32))
            pool_p.append(_tail_rows(p_new, POOL_BUF)[None])
            k_p.append(k_new.reshape(1, HIST, H_ATT, DH_ATT))
            v_p.append(v_new.reshape(1, HIST, H_ATT, DH_ATT))
            p0 = jnp.pad(state_pool[i], ((0, 0), (POOL_BASE - POOL_BUF, 0), (0, 0)))
            xs, p_new, k_new, v_new = _odd_sample(
                xs, gm, ms[0], ms[1], ms[2], w_in_odd16[i], w_out_odd16[i], pool_w16[i], ps,
                bias_s[i][:, :, :cache_len], jnp.tile(bias_s[i][:, :, cache_len:], (1, 1, n_streams)), p0,
                cache_k[i].reshape(n_streams, cache_len, D_ATT), cache_v[i].reshape(n_streams, cache_len, D_ATT),
                t_s, PAST_LEN)
            pool_s.append(_tail_rows(p_new, POOL_BUF))
            k_s.append(k_new.reshape(n_streams, t_s, H_ATT, DH_ATT))
            v_s.append(v_new.reshape(n_streams, t_s, H_ATT, DH_ATT))
        gf = row(norm_ffn[l])
        last = l == DEPTH - 1
        xp, f_new = _ffn_prompt(xp, gf, mp[3], mp[4], mp[5], w_up16[l], ffn_conv[l], w_down16[l],
                                jnp.zeros((SUBLANES, 2 * D_FF), F32), row(norm_final), last)
        ffn_p.append(_tail_rows(f_new, 2)[None])
        xs, up_all = _ffn_sample(xs, gf, ms[3], ms[4], ms[5], w_up16[l], ffn_conv[l], w_down16[l],
                                 _stream_heads(state_ffn[l], t_s, 1), _stream_heads(state_ffn[l], t_s, 2),
                                 row(norm_final), last, t_s)
        ffn_s.append(_tail_rows(up_all.reshape(n_streams, t_s, 2 * D_FF), 2))

    st = jnp.stack
    return (xp.reshape(1, seq, D_MODEL), xs.reshape(n_streams, t_s, D_MODEL),
            st(ret_p), st(ret_s), st(sconv_p), st(sconv_s), st(pool_p), st(pool_s),
            st(k_p), st(k_s), st(v_p), st(v_s), st(ffn_p), st(ffn_s))
```

```python
import functools

import numpy as np
import jax
import jax.numpy as jnp
from jax import lax
from jax.experimental import pallas as pl
from jax.experimental.pallas import tpu as pltpu

F32 = jnp.float32
BF16 = jnp.bfloat16

D_MODEL = 1024
DEPTH = 4
PAST_LEN = 4096
CHUNK = 64
H_RET = 8
DK_RET = 64
D_RET = H_RET * DK_RET
ROPE_BASE = 10000.0
D_SCONV = D_MODEL - D_RET
POOL_WINDOWS = (2, 4, 8, 16)
D_POOL = D_MODEL // 2
POOL_GROUP = D_POOL // len(POOL_WINDOWS)
POOL_BUF = max(POOL_WINDOWS) - 1
H_ATT = 8
DH_ATT = 64
D_ATT = H_ATT * DH_ATT
N_PREV_CHUNKS = 8
REL_CLIP = 256
D_FF = 2816
EPS = 1e-6
NEG_INF = -1e30
D_IN_EVEN = 4 * D_RET + 3 * D_SCONV
D_IN_ODD = D_POOL + 3 * D_ATT

LANES = 128
SUBLANES = 8
HEAD_PAIRS = H_RET // 2
ROW_TILE = 512
RET_BLOCK = 256
FFN_COLS = 256
BAND = (N_PREV_CHUNKS + 1) * CHUNK
BAND2 = BAND + CHUNK
LOG2E = 1.4426950408889634
HIST = N_PREV_CHUNKS * CHUNK
POOL_BASE = 2 * SUBLANES
VMEM_LIMIT = 56 * 1024 * 1024

LOG_G = np.log1p(-(2.0 ** (-5.0 - np.arange(H_RET, dtype=np.float64))))


def _params(n_axes=1):
    return pltpu.CompilerParams(dimension_semantics=("arbitrary",) * n_axes, vmem_limit_bytes=VMEM_LIMIT)


def _whole(shape):
    nd = len(shape)
    return pl.BlockSpec(shape, lambda i: (0,) * nd, pipeline_mode=pl.Buffered(1))


def _whole_out(shape):
    nd = len(shape)
    return pl.BlockSpec(shape, lambda i: (0,) * nd)


def _rows(block_rows, cols):
    return pl.BlockSpec((block_rows, cols), lambda i: (i, 0))


def _lead(shape):
    nd = len(shape)
    return pl.BlockSpec((1,) + tuple(shape), lambda i: (i,) + (0,) * nd)


def _dot(a, b):
    return jnp.dot(a, b, preferred_element_type=F32)


def _dot_nt(a, b):
    return lax.dot_general(a, b, (((1,), (1,)), ((), ())), preferred_element_type=F32)


def _rmsnorm(x, g):
    return x * lax.rsqrt(jnp.mean(x * x, axis=-1, keepdims=True) + EPS) * g


def _norm_mod(x, g, shift, scale):
    return _rmsnorm(x, g) * (1.0 + scale) + shift


def _silu(x):
    return x * (1.0 / (1.0 + jnp.exp(-x)))


def _low_half(shape):
    return (lax.broadcasted_iota(jnp.int32, shape, len(shape) - 1) % LANES) < DK_RET


def _shift_rows(u, prev8, s):
    rolled = pltpu.roll(u, s, axis=0)
    prolled = pltpu.roll(prev8, s, axis=0)
    row = lax.broadcasted_iota(jnp.int32, prev8.shape, 0)
    first = jnp.where(row < s, prolled, rolled[0:SUBLANES])
    return jnp.concatenate([first, rolled[SUBLANES:]], axis=0)


def _shift_rows_streams(u, head_rows, s, t):
    row_in_stream = lax.broadcasted_iota(jnp.int32, u.shape, 0) % t
    return jnp.where(row_in_stream < s, head_rows, pltpu.roll(u, s, axis=0))


def _ada_kernel(c_ref, w_ref, b_ref, o_ref):
    c = c_ref[...]
    o_ref[0] = _dot(_silu(c).astype(BF16), w_ref[0].astype(BF16)) + b_ref[0]


def _ada(c_all, w_ada, b_ada):
    rows = c_all.shape[0]
    tn = 1536
    return pl.pallas_call(
        _ada_kernel,
        out_shape=jax.ShapeDtypeStruct((DEPTH, rows, 6 * D_MODEL), F32),
        grid=(DEPTH, 6 * D_MODEL // tn),
        in_specs=[
            pl.BlockSpec((rows, D_MODEL), lambda l, j: (0, 0)),
            pl.BlockSpec((1, D_MODEL, tn), lambda l, j: (l, 0, j)),
            pl.BlockSpec((1, 1, tn), lambda l, j: (l, 0, j)),
        ],
        out_specs=pl.BlockSpec((1, rows, tn), lambda l, j: (l, 0, j)),
        compiler_params=_params(2),
        name="ada_mod",
    )(c_all, w_ada, b_ada.reshape(DEPTH, 1, 6 * D_MODEL))


def _rotary_pair(x, cos, sin_signed):
    lane = lax.broadcasted_iota(jnp.int32, x.shape, 1)
    first_half = (lane % DK_RET) < (DK_RET // 2)
    swapped = jnp.where(first_half, pltpu.roll(x, LANES - DK_RET // 2, axis=1),
                        pltpu.roll(x, DK_RET // 2, axis=1))
    return x * cos + swapped * sin_signed


def _retention_block(proj, cos, sin_signed, dec_ref, xi, zeta_t_ref, gain, cross_fn, update_fn):
    tb = proj.shape[0]
    low = _low_half((tb, LANES))
    inv_n = 1.0 / DK_RET
    outs = []
    for p in range(HEAD_PAIRS):
        c0 = p * LANES
        q = _rotary_pair(proj[:, c0:c0 + LANES], cos, sin_signed)
        k = _rotary_pair(proj[:, D_RET + c0:D_RET + c0 + LANES], cos, sin_signed) * (DK_RET ** -0.5)
        v16 = proj[:, 2 * D_RET + c0:2 * D_RET + c0 + LANES].astype(BF16)
        g = proj[:, 3 * D_RET + c0:3 * D_RET + c0 + LANES]
        k_t = k.T
        k_t16 = k_t.astype(BF16)
        kz_t16 = (k_t * zeta_t_ref[c0:c0 + LANES, :]).astype(BF16)
        q16 = q.astype(BF16)
        qe16 = jnp.where(low, q, 0.0).astype(BF16)
        qo16 = jnp.where(low, 0.0, q).astype(BF16)
        s_e = (_dot(qe16, k_t16) * dec_ref[2 * p]).astype(BF16)
        s_o = (_dot(qo16, k_t16) * dec_ref[2 * p + 1]).astype(BF16)
        inner = jnp.where(low, _dot(s_e, v16), _dot(s_o, v16))
        o = inner + cross_fn(p, q16) * xi[:, c0:c0 + LANES]
        s_lo = jnp.sum(jnp.where(low, o, 0.0), axis=-1, keepdims=True)
        s_hi = jnp.sum(jnp.where(low, 0.0, o), axis=-1, keepdims=True)
        d = o - jnp.where(low, s_lo, s_hi) * inv_n
        d2 = d * d
        v_lo = jnp.sum(jnp.where(low, d2, 0.0), axis=-1, keepdims=True)
        v_hi = jnp.sum(jnp.where(low, 0.0, d2), axis=-1, keepdims=True)
        on = d * lax.rsqrt(jnp.where(low, v_lo, v_hi) * inv_n + EPS)
        outs.append(_silu(g) * (on * gain[:, c0:c0 + LANES]))
        update_fn(p, kz_t16, v16)
    return jnp.concatenate(outs, axis=-1)


def _even_tail(x, proj, ret_out, conv_in_shift, cw_ref, gate, w_out_ref):
    gate_b = proj[:, 4 * D_RET:4 * D_RET + D_SCONV]
    u = proj[:, 4 * D_RET + D_SCONV:4 * D_RET + 2 * D_SCONV] * proj[:, 4 * D_RET + 2 * D_SCONV:]
    conv = cw_ref[0:1, :] * conv_in_shift(u, 2) + cw_ref[1:2, :] * conv_in_shift(u, 1) + cw_ref[2:3, :] * u
    mixed = jnp.concatenate([ret_out, gate_b * conv], axis=-1).astype(BF16)
    return x + gate * _dot(mixed, w_out_ref[...]), u


def _even_prompt_kernel(x_ref, g_ref, sh_ref, sc_ref, gate_ref, w_in_ref, w_out_ref, gain_ref, cw_ref,
                        cos_ref, sin_ref, dec_ref, xi_ref, zt_ref, gmat_ref, r0_ref, u0_ref,
                        o_ref, r_out_ref, u_out_ref, r_scr, u_scr):
    i = pl.program_id(0)

    @pl.when(i == 0)
    def _():
        r_scr[...] = r0_ref[...]
        u_scr[...] = u0_ref[...]

    x = x_ref[...]
    h = _norm_mod(x, g_ref[...], sh_ref[...], sc_ref[...]).astype(BF16)
    proj = _dot(h, w_in_ref[...])
    tm = x.shape[0]
    tb = dec_ref.shape[1]
    r_i = lax.broadcasted_iota(jnp.int32, (LANES, LANES), 0) < DK_RET
    c_i = lax.broadcasted_iota(jnp.int32, (LANES, LANES), 1) < DK_RET
    blockdiag = r_i == c_i

    def cross_fn(p, q16):
        return _dot(q16, r_scr[p].astype(BF16))

    def update_fn(p, kz_t16, v16):
        r_scr[p] = r_scr[p] * gmat_ref[p] + jnp.where(blockdiag, _dot(kz_t16, v16), 0.0)

    rets = []
    for r in range(tm // tb):
        rows = slice(r * tb, (r + 1) * tb)
        rets.append(_retention_block(proj[rows, :], cos_ref[rows, :], sin_ref[rows, :], dec_ref, xi_ref[...],
                                     zt_ref, gain_ref[...], cross_fn, update_fn))
    ret_out = jnp.concatenate(rets, axis=0)

    prev8 = u_scr[...]
    out, u = _even_tail(x, proj, ret_out, lambda u, s: _shift_rows(u, prev8, s), cw_ref, gate_ref[...], w_out_ref)
    u_scr[...] = u[tm - SUBLANES:, :]
    o_ref[...] = out
    r_out_ref[...] = r_scr[...]
    u_out_ref[...] = u[tm - SUBLANES:, :]


def _even_prompt(x, g, sh, sc, gate, w_in16, w_out16, gain, cw, tabs, r0, u0):
    t = x.shape[0]
    tm, tb = ROW_TILE, RET_BLOCK
    cos, sin, dec, xi, zt, gmat = tabs
    vec = _whole((1, D_MODEL))
    state = (HEAD_PAIRS, LANES, LANES)
    return pl.pallas_call(
        _even_prompt_kernel,
        out_shape=(jax.ShapeDtypeStruct((t, D_MODEL), F32),
                   jax.ShapeDtypeStruct(state, F32),
                   jax.ShapeDtypeStruct((SUBLANES, D_SCONV), F32)),
        grid=(t // tm,),
        in_specs=[_rows(tm, D_MODEL), vec, vec, vec, vec,
                  _whole((D_MODEL, D_IN_EVEN)), _whole((D_MODEL, D_MODEL)),
                  _whole((1, D_RET)), _whole((3, D_SCONV)),
                  _rows(tm, LANES), _rows(tm, LANES),
                  _whole((H_RET, tb, tb)), _whole((tb, D_RET)), _whole((D_RET, tb)),
                  _whole(state), _whole(state), _whole((SUBLANES, D_SCONV))],
        out_specs=(_rows(tm, D_MODEL), _whole_out(state), _whole_out((SUBLANES, D_SCONV))),
        scratch_shapes=[pltpu.VMEM(state, F32), pltpu.VMEM((SUBLANES, D_SCONV), F32)],
        compiler_params=_params(),
        name="even_prompt",
    )(x, g, sh, sc, gate, w_in16, w_out16, gain, cw, cos, sin, dec, xi, zt, gmat, r0, u0)


def _even_sample_kernel(x_ref, g_ref, sh_ref, sc_ref, gate_ref, w_in_ref, w_out_ref, gain_ref, cw_ref,
                        cos_ref, sin_ref, dec_ref, xi_ref, zt_ref, gwide_ref, s_stack_ref, s_wide_ref,
                        u1_ref, u2_ref, o_ref, s_out_ref, u_out_ref, *, t):
    x = x_ref[...]
    rows = x.shape[0]
    n_streams = rows // t
    wide = n_streams * LANES
    h = _norm_mod(x, g_ref[...], sh_ref[...], sc_ref[...]).astype(BF16)
    proj = _dot(h, w_in_ref[...])
    own = (lax.broadcasted_iota(jnp.int32, (rows, wide), 0) // t
           == lax.broadcasted_iota(jnp.int32, (rows, wide), 1) // LANES)
    r_i = lax.broadcasted_iota(jnp.int32, (LANES, wide), 0) < DK_RET
    blockdiag = r_i == _low_half((LANES, wide))

    def expand(a16):
        tiled = jnp.concatenate([a16.astype(F32)] * n_streams, axis=-1)
        return jnp.where(own, tiled, 0.0).astype(BF16)

    def cross_fn(p, q16):
        return _dot(expand(q16), s_stack_ref[p].astype(BF16))

    def update_fn(p, kz_t16, v16):
        kv = _dot(kz_t16, expand(v16))
        s_out_ref[p] = s_wide_ref[p] * gwide_ref[p] + jnp.where(blockdiag, kv, 0.0)

    ret_out = _retention_block(proj, cos_ref[...], sin_ref[...], dec_ref, xi_ref[...], zt_ref, gain_ref[...],
                               cross_fn, update_fn)
    heads = {1: u1_ref, 2: u2_ref}
    out, u = _even_tail(x, proj, ret_out, lambda u, s: _shift_rows_streams(u, heads[s][...], s, t),
                        cw_ref, gate_ref[...], w_out_ref)
    o_ref[...] = out
    u_out_ref[...] = u


def _even_sample(x, g, sh, sc, gate, w_in16, w_out16, gain, cw, tabs, s_stack, s_wide, u1, u2, t):
    rows = x.shape[0]
    wide = (rows // t) * LANES
    cos, sin, dec, xi, zt, gwide = tabs
    vec = _whole((1, D_MODEL))
    mat = _whole((rows, D_MODEL))
    return pl.pallas_call(
        functools.partial(_even_sample_kernel, t=t),
        out_shape=(jax.ShapeDtypeStruct((rows, D_MODEL), F32),
                   jax.ShapeDtypeStruct((HEAD_PAIRS, LANES, wide), F32),
                   jax.ShapeDtypeStruct((rows, D_SCONV), F32)),
        grid=(1,),
        in_specs=[mat, vec, mat, mat, mat,
                  _whole((D_MODEL, D_IN_EVEN)), _whole((D_MODEL, D_MODEL)),
                  _whole((1, D_RET)), _whole((3, D_SCONV)),
                  _whole((rows, LANES)), _whole((rows, LANES)),
                  _whole((H_RET, rows, rows)), _whole((rows, D_RET)), _whole((D_RET, rows)),
                  _whole((HEAD_PAIRS, LANES, wide)), _whole((HEAD_PAIRS, wide, LANES)),
                  _whole((HEAD_PAIRS, LANES, wide)),
                  _whole((rows, D_SCONV)), _whole((rows, D_SCONV))],
        out_specs=(_whole_out((rows, D_MODEL)), _whole_out((HEAD_PAIRS, LANES, wide)),
                   _whole_out((rows, D_SCONV))),
        compiler_params=_params(),
        name="even_sample",
    )(x, g, sh, sc, gate, w_in16, w_out16, gain, cw, cos, sin, dec, xi, zt, gwide, s_stack, s_wide, u1, u2)


def _pool(hist_ref, p, pos, pool_w_ref, scale):
    t = p.shape[0]
    outs = []
    for gi, w in enumerate(POOL_WINDOWS):
        cols = slice(gi * POOL_GROUP, (gi + 1) * POOL_GROUP)
        win = p[:, cols]
        for d in range(1, w):
            win = win + hist_ref[POOL_BASE - d:POOL_BASE - d + t, cols]
        inv_cnt = 1.0 / jnp.minimum(pos + 1, w).astype(F32)
        pooled = win * inv_cnt - p[:, cols]
        outs.append(_dot(pooled.astype(BF16), pool_w_ref[gi]) * scale[:, cols])
    return jnp.concatenate(outs, axis=-1)


def _attend_block(kb16, vt16, q_a, q_b, bias_t):
    return _attend_blocks([kb16], [vt16], [q_a], [q_b], [bias_t])[0]


def _attend_blocks(kbs, vts, q_as, q_bs, biases):
    low = _low_half((CHUNK, LANES))
    n = len(kbs)
    qbd = []
    for q_a, q_b in zip(q_as, q_bs):
        qs = jnp.concatenate([jnp.where(low, q_a, 0.0), jnp.where(low, 0.0, q_a),
                              jnp.where(low, q_b, 0.0), jnp.where(low, 0.0, q_b)], axis=0)
        qbd.append(qs.T.astype(BF16))
    half = [(kb.shape[0] // 2) // (2 * SUBLANES) * (2 * SUBLANES) for kb in kbs]
    s = [jnp.concatenate([_dot(kbs[j][:half[j]], qbd[j]), _dot(kbs[j][half[j]:], qbd[j])], axis=0) + biases[j]
         for j in range(n)]
    e = [jnp.exp2(s[j] - jnp.max(s[j], axis=0, keepdims=True)) for j in range(n)]
    inv_l = [1.0 / jnp.sum(e[j], axis=0, keepdims=True) for j in range(n)]
    e16 = [e[j].astype(BF16) for j in range(n)]
    o_t = [jnp.concatenate([_dot(vts[j][:DK_RET], e16[j]), _dot(vts[j][DK_RET:], e16[j])], axis=0) * inv_l[j]
           for j in range(n)]
    o_t = [o.T for o in o_t]
    return [(jnp.where(low, o[0:CHUNK], o[CHUNK:2 * CHUNK]), jnp.where(low, o[2 * CHUNK:3 * CHUNK], o[3 * CHUNK:]))
            for o in o_t]


def _odd_prompt_kernel(x_ref, g_ref, sh_ref, sc_ref, gate_ref, w_in_ref, w_out_ref, pw_ref, ps_ref, bias_ref, p0_ref,
                       o_ref, p_out_ref, k_out_ref, v_out_ref, pbuf, kbuf, vtbuf, q_scr, att_scr):
    i = pl.program_id(0)
    tm = x_ref.shape[0]

    @pl.when(i == 0)
    def _():
        pbuf[0:POOL_BASE, :] = p0_ref[...]
        kbuf[0:HIST, :] = jnp.zeros((HIST, D_ATT), BF16)
        vtbuf[:, :, 0:HIST] = jnp.zeros((HEAD_PAIRS, LANES, HIST), BF16)

    x = x_ref[...]
    h = _norm_mod(x, g_ref[...], sh_ref[...], sc_ref[...]).astype(BF16)
    proj = _dot(h, w_in_ref[...])
    p = proj[:, :D_POOL]
    q_scr[...] = proj[:, D_POOL:D_POOL + D_ATT] * (DH_ATT ** -0.5 * LOG2E)
    k = proj[:, D_POOL + D_ATT:D_POOL + 2 * D_ATT]
    v = proj[:, D_POOL + 2 * D_ATT:]
    pbuf[POOL_BASE:POOL_BASE + tm, :] = p
    kbuf[HIST:HIST + tm, :] = k.astype(BF16)
    for pr in range(HEAD_PAIRS):
        vtbuf[pr, :, HIST:HIST + tm] = v[:, pr * LANES:(pr + 1) * LANES].T.astype(BF16)
    k_out_ref[...] = k[tm - HIST:, :]
    v_out_ref[...] = v[tm - HIST:, :]

    pos = i * tm + lax.broadcasted_iota(jnp.int32, (tm, 1), 0)
    pool_out = _pool(pbuf, p, pos, pw_ref, ps_ref[...])

    def attend_tile(first_tile):
        for jb in range(tm // (2 * CHUNK)):
            r0 = jb * 2 * CHUNK
            skip = max(HIST - r0, 0) if first_tile else 0
            pairs = range(HEAD_PAIRS)
            lanes = [slice(pr * LANES, (pr + 1) * LANES) for pr in pairs]
            outs = _attend_blocks([kbuf[r0 + skip:r0 + BAND2, lanes[pr]] for pr in pairs],
                                  [vtbuf[pr, :, r0 + skip:r0 + BAND2] for pr in pairs],
                                  [q_scr[r0:r0 + CHUNK, lanes[pr]] for pr in pairs],
                                  [q_scr[r0 + CHUNK:r0 + 2 * CHUNK, lanes[pr]] for pr in pairs],
                                  [bias_ref[pr, skip:, :] for pr in pairs])
            att_scr[r0:r0 + CHUNK, :] = jnp.concatenate([o[0] for o in outs], axis=-1)
            att_scr[r0 + CHUNK:r0 + 2 * CHUNK, :] = jnp.concatenate([o[1] for o in outs], axis=-1)

    pl.when(i == 0)(functools.partial(attend_tile, True))
    pl.when(i > 0)(functools.partial(attend_tile, False))

    kbuf[0:HIST, :] = kbuf[tm:tm + HIST, :]
    vtbuf[:, :, 0:HIST] = vtbuf[:, :, tm:tm + HIST]
    tail = pbuf[tm:tm + POOL_BASE, :]
    pbuf[0:POOL_BASE, :] = tail
    p_out_ref[...] = tail

    mixed = jnp.concatenate([pool_out, att_scr[...]], axis=-1).astype(BF16)
    o_ref[...] = x + gate_ref[...] * _dot(mixed, w_out_ref[...])


def _odd_prompt(x, g, sh, sc, gate, w_in16, w_out16, pw16, ps, bias_t, p0):
    t = x.shape[0]
    tm = ROW_TILE
    assert tm == HIST and t % tm == 0
    vec = _whole((1, D_MODEL))
    return pl.pallas_call(
        _odd_prompt_kernel,
        out_shape=(jax.ShapeDtypeStruct((t, D_MODEL), F32),
                   jax.ShapeDtypeStruct((POOL_BASE, D_POOL), F32),
                   jax.ShapeDtypeStruct((HIST, D_ATT), F32),
                   jax.ShapeDtypeStruct((HIST, D_ATT), F32)),
        grid=(t // tm,),
        in_specs=[_rows(tm, D_MODEL), vec, vec, vec, vec,
                  _whole((D_MODEL, D_IN_ODD)), _whole((D_MODEL, D_MODEL)),
                  _whole((len(POOL_WINDOWS), POOL_GROUP, POOL_GROUP)), _whole((1, D_POOL)),
                  _whole((HEAD_PAIRS, BAND2, 2 * LANES)), _whole((POOL_BASE, D_POOL))],
        out_specs=(_rows(tm, D_MODEL), _whole_out((POOL_BASE, D_POOL)),
                   _whole_out((HIST, D_ATT)), _whole_out((HIST, D_ATT))),
        scratch_shapes=[pltpu.VMEM((POOL_BASE + tm, D_POOL), F32),
                        pltpu.VMEM((HIST + tm, D_ATT), BF16), pltpu.VMEM((HEAD_PAIRS, LANES, HIST + tm), BF16),
                        pltpu.VMEM((tm, D_ATT), F32), pltpu.VMEM((tm, D_ATT), F32)],
        compiler_params=_params(),
        name="odd_prompt",
    )(x, g, sh, sc, gate, w_in16, w_out16, pw16, ps, bias_t, p0)


def _odd_sample_kernel(x_ref, g_ref, sh_ref, sc_ref, gate_ref, w_in_ref, w_out_ref, pw_ref, ps_ref,
                       bias_c_ref, bias_n_ref, p0_ref, kc_ref, vc_ref,
                       o_ref, p_out_ref, k_out_ref, v_out_ref, proj_scr, mix_scr, pbuf, *, pos0, t):
    b = pl.program_id(0)
    rows = x_ref.shape[0]

    @pl.when(b == 0)
    def _():
        h = _norm_mod(x_ref[...], g_ref[...], sh_ref[...], sc_ref[...]).astype(BF16)
        proj = _dot(h, w_in_ref[...])
        proj_scr[...] = proj
        k_out_ref[...] = proj[:, D_POOL + D_ATT:D_POOL + 2 * D_ATT]
        v_out_ref[...] = proj[:, D_POOL + 2 * D_ATT:]

    r0 = pl.multiple_of(b * t, t)
    proj = proj_scr[pl.ds(r0, t), :]
    p = proj[:, :D_POOL]
    pbuf[0:POOL_BASE, :] = p0_ref[0]
    pbuf[POOL_BASE:POOL_BASE + t, :] = p
    p_out_ref[0] = p
    pos = pos0 + lax.broadcasted_iota(jnp.int32, (t, 1), 0)
    pool_out = _pool(pbuf, p, pos, pw_ref, ps_ref[...])

    low = _low_half((t, LANES))
    own = lax.broadcasted_iota(jnp.int32, (t, rows), 1) // t == b
    outs = []
    for pr in range(HEAD_PAIRS):
        cols = slice(pr * LANES, (pr + 1) * LANES)
        qp = proj[:, D_POOL + pr * LANES:D_POOL + (pr + 1) * LANES] * (DH_ATT ** -0.5)
        kn = proj_scr[:, D_POOL + D_ATT + pr * LANES:D_POOL + D_ATT + (pr + 1) * LANES].astype(BF16)
        vn = proj_scr[:, D_POOL + 2 * D_ATT + pr * LANES:D_POOL + 2 * D_ATT + (pr + 1) * LANES].astype(BF16)
        kc = kc_ref[0, :, cols].astype(BF16)
        vc = vc_ref[0, :, cols].astype(BF16)
        halves = []
        for hh, qh in ((2 * pr, jnp.where(low, qp, 0.0)), (2 * pr + 1, jnp.where(low, 0.0, qp))):
            qh16 = qh.astype(BF16)
            s_c = _dot_nt(qh16, kc) + bias_c_ref[hh]
            s_n = jnp.where(own, _dot_nt(qh16, kn) + bias_n_ref[hh], NEG_INF)
            m = jnp.maximum(jnp.max(s_c, axis=-1, keepdims=True), jnp.max(s_n, axis=-1, keepdims=True))
            e_c = jnp.exp(s_c - m)
            e_n = jnp.exp(s_n - m)
            inv_l = 1.0 / (jnp.sum(e_c, axis=-1, keepdims=True) + jnp.sum(e_n, axis=-1, keepdims=True))
            halves.append((_dot(e_c.astype(BF16), vc) + _dot(e_n.astype(BF16), vn)) * inv_l)
        outs.append(jnp.where(low, halves[0], halves[1]))
    mix_scr[pl.ds(r0, t), :] = jnp.concatenate([pool_out] + outs, axis=-1)

    @pl.when(b == pl.num_programs(0) - 1)
    def _():
        o_ref[...] = x_ref[...] + gate_ref[...] * _dot(mix_scr[...].astype(BF16), w_out_ref[...])


def _odd_sample(x, g, sh, sc, gate, w_in16, w_out16, pw16, ps, bias_c, bias_n, p0, kc, vc, t, pos0):
    rows = x.shape[0]
    n_streams = rows // t
    cache = kc.shape[1]
    vec = _whole((1, D_MODEL))
    mat = _whole((rows, D_MODEL))
    return pl.pallas_call(
        functools.partial(_odd_sample_kernel, pos0=pos0, t=t),
        out_shape=(jax.ShapeDtypeStruct((rows, D_MODEL), F32),
                   jax.ShapeDtypeStruct((n_streams, t, D_POOL), F32),
                   jax.ShapeDtypeStruct((rows, D_ATT), F32),
                   jax.ShapeDtypeStruct((rows, D_ATT), F32)),
        grid=(n_streams,),
        in_specs=[mat, vec, mat, mat, mat,
                  _whole((D_MODEL, D_IN_ODD)), _whole((D_MODEL, D_MODEL)),
                  _whole((len(POOL_WINDOWS), POOL_GROUP, POOL_GROUP)), _whole((1, D_POOL)),
                  _whole((H_ATT, t, cache)), _whole((H_ATT, t, rows)),
                  _lead((POOL_BASE, D_POOL)), _lead((cache, D_ATT)), _lead((cache, D_ATT))],
        out_specs=(_whole_out((rows, D_MODEL)), _lead((t, D_POOL)),
                   _whole_out((rows, D_ATT)), _whole_out((rows, D_ATT))),
        scratch_shapes=[pltpu.VMEM((rows, D_IN_ODD), F32), pltpu.VMEM((rows, D_MODEL), F32),
                        pltpu.VMEM((POOL_BASE + t, D_POOL), F32)],
        compiler_params=_params(),
        name="odd_sample",
    )(x, g, sh, sc, gate, w_in16, w_out16, pw16, ps, bias_c, bias_n, p0, kc, vc)


def _ffn_tile(x, h16, w_up_ref, cw_ref, w_down_ref, shifted, store_up):
    acc = jnp.zeros((x.shape[0], D_MODEL), F32)
    for c in range(D_FF // FFN_COLS):
        halves = []
        for off in (0, D_FF):
            cols = slice(off + c * FFN_COLS, off + (c + 1) * FFN_COLS)
            up = _dot(h16, w_up_ref[:, cols])
            conv = (cw_ref[0:1, cols] * shifted(up, cols, 2) + cw_ref[1:2, cols] * shifted(up, cols, 1)
                    + cw_ref[2:3, cols] * up)
            store_up(up, cols)
            halves.append(conv)
        act = (_silu(halves[0]) * halves[1]).astype(BF16)
        acc = acc + _dot(act, w_down_ref[c * FFN_COLS:(c + 1) * FFN_COLS, :])
    return acc


def _ffn_prompt_kernel(x_ref, g_ref, sh_ref, sc_ref, gate_ref, w_up_ref, cw_ref, w_down_ref, f0_ref, gf_ref,
                       o_ref, f_out_ref, f_scr, *, final_norm):
    i = pl.program_id(0)
    tm = x_ref.shape[0]

    @pl.when(i == 0)
    def _():
        f_scr[...] = f0_ref[...]

    x = x_ref[...]
    h16 = _norm_mod(x, g_ref[...], sh_ref[...], sc_ref[...]).astype(BF16)

    def shifted(up, cols, s):
        return _shift_rows(up, f_scr[:, cols], s)

    def store_up(up, cols):
        f_scr[:, cols] = up[tm - SUBLANES:, :]

    out = x + gate_ref[...] * _ffn_tile(x, h16, w_up_ref, cw_ref, w_down_ref, shifted, store_up)
    if final_norm:
        out = _rmsnorm(out, gf_ref[...])
    o_ref[...] = out
    f_out_ref[...] = f_scr[...]


def _ffn_prompt(x, g, sh, sc, gate, w_up16, cw, w_down16, f0, gf, final_norm):
    t = x.shape[0]
    tm = ROW_TILE
    vec = _whole((1, D_MODEL))
    return pl.pallas_call(
        functools.partial(_ffn_prompt_kernel, final_norm=final_norm),
        out_shape=(jax.ShapeDtypeStruct((t, D_MODEL), F32), jax.ShapeDtypeStruct((SUBLANES, 2 * D_FF), F32)),
        grid=(t // tm,),
        in_specs=[_rows(tm, D_MODEL), vec, vec, vec, vec,
                  _whole((D_MODEL, 2 * D_FF)), _whole((3, 2 * D_FF)), _whole((D_FF, D_MODEL)),
                  _whole((SUBLANES, 2 * D_FF)), vec],
        out_specs=(_rows(tm, D_MODEL), _whole_out((SUBLANES, 2 * D_FF))),
        scratch_shapes=[pltpu.VMEM((SUBLANES, 2 * D_FF), F32)],
        compiler_params=_params(),
        name="ffn_prompt",
    )(x, g, sh, sc, gate, w_up16, cw, w_down16, f0, gf)


def _ffn_sample_kernel(x_ref, g_ref, sh_ref, sc_ref, gate_ref, w_up_ref, cw_ref, w_down_ref, s1_ref, s2_ref, gf_ref,
                       o_ref, up_out_ref, *, final_norm, t):
    x = x_ref[...]
    h16 = _norm_mod(x, g_ref[...], sh_ref[...], sc_ref[...]).astype(BF16)
    heads = {1: s1_ref, 2: s2_ref}

    def shifted(up, cols, s):
        return _shift_rows_streams(up, heads[s][:, cols], s, t)

    def store_up(up, cols):
        up_out_ref[:, cols] = up

    out = x + gate_ref[...] * _ffn_tile(x, h16, w_up_ref, cw_ref, w_down_ref, shifted, store_up)
    if final_norm:
        out = _rmsnorm(out, gf_ref[...])
    o_ref[...] = out


def _ffn_sample(x, g, sh, sc, gate, w_up16, cw, w_down16, s1, s2, gf, final_norm, t):
    rows = x.shape[0]
    vec = _whole((1, D_MODEL))
    mat = _whole((rows, D_MODEL))
    wide = _whole((rows, 2 * D_FF))
    return pl.pallas_call(
        functools.partial(_ffn_sample_kernel, final_norm=final_norm, t=t),
        out_shape=(jax.ShapeDtypeStruct((rows, D_MODEL), F32), jax.ShapeDtypeStruct((rows, 2 * D_FF), F32)),
        grid=(1,),
        in_specs=[mat, vec, mat, mat, mat,
                  _whole((D_MODEL, 2 * D_FF)), _whole((3, 2 * D_FF)), _whole((D_FF, D_MODEL)),
                  wide, wide, vec],
        out_specs=(_whole_out((rows, D_MODEL)), _whole_out((rows, 2 * D_FF))),
        compiler_params=_params(),
        name="ffn_sample",
    )(x, g, sh, sc, gate, w_up16, cw, w_down16, s1, s2, gf)


def _rotary_tables(pos):
    half = DK_RET // 2
    inv = ROPE_BASE ** (-jnp.arange(half, dtype=F32) / half)
    ang = pos.astype(F32)[:, None] * inv[None, :]
    cos, sin = jnp.cos(ang), jnp.sin(ang)
    return jnp.concatenate([cos] * 4, axis=-1), jnp.concatenate([-sin, sin, -sin, sin], axis=-1)


def _retention_tables(tb, n_streams=1):
    idx = np.arange(tb, dtype=np.float64)
    diff = idx[:, None] - idx[None, :]
    dec1 = np.where(diff[None] >= 0, np.exp(LOG_G[:, None, None] * np.maximum(diff, 0.0)[None]), 0.0)
    dec = np.zeros((H_RET, n_streams * tb, n_streams * tb))
    for b in range(n_streams):
        dec[:, b * tb:(b + 1) * tb, b * tb:(b + 1) * tb] = dec1
    xi = np.tile(np.repeat(np.exp(LOG_G[:, None] * (idx + 1)[None, :]).T, DK_RET, axis=1), (n_streams, 1))
    zeta_t = np.tile(np.repeat(np.exp(LOG_G[:, None] * (tb - 1 - idx)[None, :]), DK_RET, axis=0), (1, n_streams))
    gmat = np.zeros((HEAD_PAIRS, LANES, LANES))
    for h in range(H_RET):
        o = (h % 2) * DK_RET
        gmat[h // 2, o:o + DK_RET, o:o + DK_RET] = np.exp(LOG_G[h] * tb)
    gmat = np.tile(gmat, (1, 1, n_streams))
    return tuple(jnp.asarray(a, F32) for a in (dec, xi, zeta_t, gmat))


def _pair_state(s):
    lead = s.shape[:-3]
    s = s.reshape(lead + (HEAD_PAIRS, 2, DK_RET, DK_RET))
    z = jnp.zeros_like(s[..., 0, :, :])
    top = jnp.concatenate([s[..., 0, :, :], z], axis=-1)
    bot = jnp.concatenate([z, s[..., 1, :, :]], axis=-1)
    return jnp.concatenate([top, bot], axis=-2)


def _unpair_state(r):
    a = r[..., :DK_RET, :DK_RET]
    b = r[..., DK_RET:, DK_RET:]
    s = jnp.stack([a, b], axis=-3)
    return s.reshape(r.shape[:-3] + (H_RET, DK_RET, DK_RET))


def _band_bias(table):
    nq, nk = 2 * CHUNK, BAND2
    period = nq + nk
    j = np.arange(period)
    j = np.where(j < nk, j, j - period)
    idx = np.clip(HIST - j, -(CHUNK - 1), REL_CLIP) + (CHUNK - 1)
    one_period = table[:, idx].astype(F32)
    flat = jnp.tile(one_period, (1, nq + 1))[:, :nq * (period - 1)]
    return flat.reshape(-1, nq, period - 1)[:, :, :nk]


def _band_bias_t(raw):
    qq = np.arange(2 * CHUNK)[:, None]
    kk = np.arange(BAND2)[None, :]
    valid = np.where(qq < CHUNK, kk < BAND, kk >= CHUNK)
    b = jnp.where(valid, raw * LOG2E, NEG_INF).reshape(HEAD_PAIRS, 2, 2, CHUNK, BAND2)
    return jnp.transpose(b, (0, 4, 2, 1, 3)).reshape(HEAD_PAIRS, BAND2, 2 * LANES)


def _tail_rows(a, n):
    return a[..., a.shape[-2] - n:, :]


def _stream_heads(st, t, s):
    n, smax, c = st.shape
    zeros = jnp.zeros((n, t - s, c), st.dtype)
    return jnp.concatenate([st[:, smax - s:], zeros], axis=1).reshape(n * t, c)


def kernel(x_prompt, x_sample, state_ret, state_sconv, state_pool, cache_k, cache_v, state_ffn, c_prompt, c_sample,
           norm_mix, norm_ffn, norm_final, w_ada, b_ada, w_in_even, w_out_even, ret_gn_gain, sconv_w, w_in_odd,
           w_out_odd, pool_w, pool_scale, rel_bias_table, ffn_w_up, ffn_conv, ffn_w_down):
    n_prompt, seq, _ = x_prompt.shape
    n_streams, t_s, _ = x_sample.shape
    assert n_prompt == 1
    rows_s = n_streams * t_s

    c_all = jnp.concatenate([c_prompt, c_sample], axis=0)
    pad = (-c_all.shape[0]) % SUBLANES
    mod = _ada(jnp.pad(c_all, ((0, pad), (0, 0))), w_ada, b_ada)
    mod_p = mod[:, 0:1, :].reshape(DEPTH, 1, 6, D_MODEL)
    mod_s = jnp.repeat(mod[:, 1:1 + n_streams, :], t_s, axis=1).reshape(DEPTH, rows_s, 6, D_MODEL)

    bf = lambda w: w.astype(BF16)
    w_in_even16, w_out_even16 = bf(w_in_even), bf(w_out_even)
    w_in_odd16, w_out_odd16, pool_w16 = bf(w_in_odd), bf(w_out_odd), bf(pool_w)
    w_up16, w_down16 = bf(ffn_w_up), bf(ffn_w_down)

    pos_p = jnp.arange(seq, dtype=jnp.int32)
    pos_s = PAST_LEN + jnp.arange(t_s, dtype=jnp.int32)
    tabs_p = _rotary_tables(pos_p) + _retention_tables(RET_BLOCK)
    tabs_s = tuple(jnp.tile(a, (n_streams, 1)) for a in _rotary_tables(pos_s)) + _retention_tables(t_s, n_streams)

    cache_len = cache_k.shape[2]
    assert cache_len == HIST and t_s <= CHUNK
    bias_raw = [_band_bias(rel_bias_table[i]) for i in range(DEPTH // 2)]
    bias_p = [_band_bias_t(b) for b in bias_raw]
    bias_s = [b[:, :t_s, :cache_len + t_s] for b in bias_raw]

    xp = x_prompt.reshape(seq, D_MODEL)
    xs = x_sample.reshape(rows_s, D_MODEL)
    row = lambda a: a.reshape(1, -1)

    ret_p, ret_s, sconv_p, sconv_s, pool_p, pool_s = [], [], [], [], [], []
    k_p, k_s, v_p, v_s, ffn_p, ffn_s = [], [], [], [], [], []
    for l in range(DEPTH):
        i = l // 2
        mp = [mod_p[l, :, j, :] for j in range(6)]
        ms = [mod_s[l, :, j, :] for j in range(6)]
        gm = row(norm_mix[l])
        if l % 2 == 0:
            gain, cw = row(ret_gn_gain[i]), sconv_w[i]
            xp, r_new, u_new = _even_prompt(
                xp, gm, mp[0], mp[1], mp[2], w_in_even16[i], w_out_even16[i], gain, cw, tabs_p,
                jnp.zeros((HEAD_PAIRS, LANES, LANES), F32), jnp.zeros((SUBLANES, D_SCONV), F32))
            ret_p.append(_unpair_state(r_new)[None])
            sconv_p.append(_tail_rows(u_new, 2)[None])
            paired = _pair_state(state_ret[i])
            s_stack = jnp.transpose(paired, (1, 0, 2, 3)).reshape(HEAD_PAIRS, n_streams * LANES, LANES)
            s_wide = jnp.transpose(paired, (1, 2, 0, 3)).reshape(HEAD_PAIRS, LANES, n_streams * LANES)
            xs, s_new, u_all = _even_sample(
                xs, gm, ms[0], ms[1], ms[2], w_in_even16[i], w_out_even16[i], gain, cw, tabs_s,
                s_stack, s_wide, _stream_heads(state_sconv[i], t_s, 1), _stream_heads(state_sconv[i], t_s, 2), t_s)
            s_new = jnp.transpose(s_new.reshape(HEAD_PAIRS, LANES, n_streams, LANES), (2, 0, 1, 3))
            ret_s.append(_unpair_state(s_new))
            sconv_s.append(_tail_rows(u_all.reshape(n_streams, t_s, D_SCONV), 2))
        else:
            ps = row(pool_scale[i])
            xp, p_new, k_new, v_new = _odd_prompt(
                xp, gm, mp[0], mp[1], mp[2], w_in_odd16[i], w_out_odd16[i], pool_w16[i], ps, bias_p[i],
                jnp.zeros((POOL_BASE, D_POOL), F32))
            pool_p.append(_tail_rows(p_new, POOL_BUF)[None])
            k_p.append(k_new.reshape(1, HIST, H_ATT, DH_ATT))
            v_p.append(v_new.reshape(1, HIST, H_ATT, DH_ATT))
            p0 = jnp.pad(state_pool[i], ((0, 0), (POOL_BASE - POOL_BUF, 0), (0, 0)))
            xs, p_new, k_new, v_new = _odd_sample(
                xs, gm, ms[0], ms[1], ms[2], w_in_odd16[i], w_out_odd16[i], pool_w16[i], ps,
                bias_s[i][:, :, :cache_len], jnp.tile(bias_s[i][:, :, cache_len:], (1, 1, n_streams)), p0,
                cache_k[i].reshape(n_streams, cache_len, D_ATT), cache_v[i].reshape(n_streams, cache_len, D_ATT),
                t_s, PAST_LEN)
            pool_s.append(_tail_rows(p_new, POOL_BUF))
            k_s.append(k_new.reshape(n_streams, t_s, H_ATT, DH_ATT))
            v_s.append(v_new.reshape(n_streams, t_s, H_ATT, DH_ATT))
        gf = row(norm_ffn[l])
        last = l == DEPTH - 1
        xp, f_new = _ffn_prompt(xp, gf, mp[3], mp[4], mp[5], w_up16[l], ffn_conv[l], w_down16[l],
                                jnp.zeros((SUBLANES, 2 * D_FF), F32), row(norm_final), last)
        ffn_p.append(_tail_rows(f_new, 2)[None])
        xs, up_all = _ffn_sample(xs, gf, ms[3], ms[4], ms[5], w_up16[l], ffn_conv[l], w_down16[l],
                                 _stream_heads(state_ffn[l], t_s, 1), _stream_heads(state_ffn[l], t_s, 2),
                                 row(norm_final), last, t_s)
        ffn_s.append(_tail_rows(up_all.reshape(n_streams, t_s, 2 * D_FF), 2))

    st = jnp.stack
    return (xp.reshape(1, seq, D_MODEL), xs.reshape(n_streams, t_s, D_MODEL),
            st(ret_p), st(ret_s), st(sconv_p), st(sconv_s), st(pool_p), st(pool_s),
            st(k_p), st(k_s), st(v_p), st(v_s), st(ffn_p), st(ffn_s))
```

```python
import functools

import numpy as np
import jax
import jax.numpy as jnp
from jax import lax
from jax.experimental import pallas as pl
from jax.experimental.pallas import tpu as pltpu

F32 = jnp.float32
BF16 = jnp.bfloat16

D_MODEL = 1024
DEPTH = 4
PAST_LEN = 4096
CHUNK = 64
H_RET = 8
DK_RET = 64
D_RET = H_RET * DK_RET
ROPE_BASE = 10000.0
D_SCONV = D_MODEL - D_RET
POOL_WINDOWS = (2, 4, 8, 16)
D_POOL = D_MODEL // 2
POOL_GROUP = D_POOL // len(POOL_WINDOWS)
POOL_BUF = max(POOL_WINDOWS) - 1
H_ATT = 8
DH_ATT = 64
D_ATT = H_ATT * DH_ATT
N_PREV_CHUNKS = 8
REL_CLIP = 256
D_FF = 2816
EPS = 1e-6
NEG_INF = -1e30
D_IN_EVEN = 4 * D_RET + 3 * D_SCONV
D_IN_ODD = D_POOL + 3 * D_ATT

LANES = 128
SUBLANES = 8
HEAD_PAIRS = H_RET // 2
ROW_TILE = 512
RET_BLOCK = 256
FFN_COLS = 256
BAND = (N_PREV_CHUNKS + 1) * CHUNK
BAND2 = BAND + CHUNK
LOG2E = 1.4426950408889634
HIST = N_PREV_CHUNKS * CHUNK
POOL_BASE = 2 * SUBLANES
VMEM_LIMIT = 56 * 1024 * 1024

LOG_G = np.log1p(-(2.0 ** (-5.0 - np.arange(H_RET, dtype=np.float64))))


def _params(n_axes=1):
    return pltpu.CompilerParams(dimension_semantics=("arbitrary",) * n_axes, vmem_limit_bytes=VMEM_LIMIT)


def _whole(shape):
    nd = len(shape)
    return pl.BlockSpec(shape, lambda i: (0,) * nd, pipeline_mode=pl.Buffered(1))


def _whole_out(shape):
    nd = len(shape)
    return pl.BlockSpec(shape, lambda i: (0,) * nd)


def _layer(shape, l):
    nd = len(shape)
    return pl.BlockSpec((None,) + tuple(shape), lambda i: (l,) + (0,) * nd, pipeline_mode=pl.Buffered(1))


def _layer_stream(shape, l):
    nd = len(shape)
    return pl.BlockSpec((None, None) + tuple(shape), lambda i: (l, i) + (0,) * nd)


MOD_ROWS_PROMPT = SUBLANES


def _mod_prompt(l, j, n_streams):
    return pl.BlockSpec((None, MOD_ROWS_PROMPT, D_MODEL), lambda i: (l, n_streams // MOD_ROWS_PROMPT, j),
                        pipeline_mode=pl.Buffered(1))


def _mod_streams(l, j, n_streams):
    return pl.BlockSpec((None, n_streams, D_MODEL), lambda i: (l, 0, j), pipeline_mode=pl.Buffered(1))


def _stream_rows(m, t):
    return jnp.concatenate([jnp.broadcast_to(m[b:b + 1, :], (t, m.shape[1])) for b in range(m.shape[0])], axis=0)


def _rows(block_rows, cols):
    return pl.BlockSpec((block_rows, cols), lambda i: (i, 0))


def _lead(shape):
    nd = len(shape)
    return pl.BlockSpec((1,) + tuple(shape), lambda i: (i,) + (0,) * nd)


def _dot(a, b):
    return jnp.dot(a, b, preferred_element_type=F32)


def _dot_nt(a, b):
    return lax.dot_general(a, b, (((1,), (1,)), ((), ())), preferred_element_type=F32)


def _rmsnorm(x, g):
    return x * lax.rsqrt(jnp.mean(x * x, axis=-1, keepdims=True) + EPS) * g


def _norm_mod(x, g, shift, scale):
    return _rmsnorm(x, g) * (1.0 + scale) + shift


def _silu(x):
    return x * (1.0 / (1.0 + jnp.exp(-x)))


def _low_half(shape):
    return (lax.broadcasted_iota(jnp.int32, shape, len(shape) - 1) % LANES) < DK_RET


def _shift_rows(u, prev8, s):
    rolled = pltpu.roll(u, s, axis=0)
    prolled = pltpu.roll(prev8, s, axis=0)
    row = lax.broadcasted_iota(jnp.int32, prev8.shape, 0)
    first = jnp.where(row < s, prolled, rolled[0:SUBLANES])
    return jnp.concatenate([first, rolled[SUBLANES:]], axis=0)


def _shift_rows_streams(u, head_rows, s, t):
    row_in_stream = lax.broadcasted_iota(jnp.int32, u.shape, 0) % t
    return jnp.where(row_in_stream < s, head_rows, pltpu.roll(u, s, axis=0))


def _ada_kernel(c_ref, w_ref, b_ref, o_ref):
    c = c_ref[...]
    o_ref[0] = _dot(_silu(c).astype(BF16), w_ref[0].astype(BF16)) + b_ref[0]


def _ada(c_all, w_ada, b_ada):
    rows = c_all.shape[0]
    tn = 1536
    return pl.pallas_call(
        _ada_kernel,
        out_shape=jax.ShapeDtypeStruct((DEPTH, rows, 6 * D_MODEL), F32),
        grid=(DEPTH, 6 * D_MODEL // tn),
        in_specs=[
            pl.BlockSpec((rows, D_MODEL), lambda l, j: (0, 0)),
            pl.BlockSpec((1, D_MODEL, tn), lambda l, j: (l, 0, j)),
            pl.BlockSpec((1, 1, tn), lambda l, j: (l, 0, j)),
        ],
        out_specs=pl.BlockSpec((1, rows, tn), lambda l, j: (l, 0, j)),
        compiler_params=_params(2),
        name="ada_mod",
    )(c_all, w_ada, b_ada.reshape(DEPTH, 1, 6 * D_MODEL))


def _rotary_pair(x, cos, sin_signed):
    lane = lax.broadcasted_iota(jnp.int32, x.shape, 1)
    first_half = (lane % DK_RET) < (DK_RET // 2)
    swapped = jnp.where(first_half, pltpu.roll(x, LANES - DK_RET // 2, axis=1),
                        pltpu.roll(x, DK_RET // 2, axis=1))
    return x * cos + swapped * sin_signed


def _retention_block(proj, cos, sin_signed, dec_ref, xi, zeta_t_ref, gain, cross_fn, update_fn):
    tb = proj.shape[0]
    low = _low_half((tb, LANES))
    inv_n = 1.0 / DK_RET
    outs = []
    for p in range(HEAD_PAIRS):
        c0 = p * LANES
        q = _rotary_pair(proj[:, c0:c0 + LANES], cos, sin_signed)
        k = _rotary_pair(proj[:, D_RET + c0:D_RET + c0 + LANES], cos, sin_signed) * (DK_RET ** -0.5)
        v16 = proj[:, 2 * D_RET + c0:2 * D_RET + c0 + LANES].astype(BF16)
        g = proj[:, 3 * D_RET + c0:3 * D_RET + c0 + LANES]
        k_t = k.T
        k_t16 = k_t.astype(BF16)
        kz_t16 = (k_t * zeta_t_ref[c0:c0 + LANES, :]).astype(BF16)
        q16 = q.astype(BF16)
        qe16 = jnp.where(low, q, 0.0).astype(BF16)
        qo16 = jnp.where(low, 0.0, q).astype(BF16)
        s_e = (_dot(qe16, k_t16) * dec_ref[2 * p]).astype(BF16)
        s_o = (_dot(qo16, k_t16) * dec_ref[2 * p + 1]).astype(BF16)
        inner = jnp.where(low, _dot(s_e, v16), _dot(s_o, v16))
        o = inner + cross_fn(p, q16) * xi[:, c0:c0 + LANES]
        s_lo = jnp.sum(jnp.where(low, o, 0.0), axis=-1, keepdims=True)
        s_hi = jnp.sum(jnp.where(low, 0.0, o), axis=-1, keepdims=True)
        d = o - jnp.where(low, s_lo, s_hi) * inv_n
        d2 = d * d
        v_lo = jnp.sum(jnp.where(low, d2, 0.0), axis=-1, keepdims=True)
        v_hi = jnp.sum(jnp.where(low, 0.0, d2), axis=-1, keepdims=True)
        on = d * lax.rsqrt(jnp.where(low, v_lo, v_hi) * inv_n + EPS)
        outs.append(_silu(g) * (on * gain[:, c0:c0 + LANES]))
        update_fn(p, kz_t16, v16)
    return jnp.concatenate(outs, axis=-1)


def _even_tail(x, proj, ret_out, conv_in_shift, cw_ref, gate, w_out_ref):
    gate_b = proj[:, 4 * D_RET:4 * D_RET + D_SCONV]
    u = proj[:, 4 * D_RET + D_SCONV:4 * D_RET + 2 * D_SCONV] * proj[:, 4 * D_RET + 2 * D_SCONV:]
    conv = cw_ref[0:1, :] * conv_in_shift(u, 2) + cw_ref[1:2, :] * conv_in_shift(u, 1) + cw_ref[2:3, :] * u
    mixed = jnp.concatenate([ret_out, gate_b * conv], axis=-1).astype(BF16)
    return x + gate * _dot(mixed, w_out_ref[...]), u


def _even_prompt_kernel(x_ref, g_ref, sh_ref, sc_ref, gate_ref, w_in_ref, w_out_ref, gain_ref, cw_ref,
                        rot_tile_ref, rot_row_ref, dec_ref, xi_ref, zt_ref, gmat_ref, r0_ref, u0_ref,
                        o_ref, r_out_ref, u_out_ref, r_scr, u_scr):
    i = pl.program_id(0)
    cos_0, sin_0, ssin_0 = (rot_tile_ref[j, pl.ds(i, 1), :] for j in range(3))
    cos = cos_0 * rot_row_ref[0] - sin_0 * rot_row_ref[1]
    sin_signed = ssin_0 * rot_row_ref[0] + cos_0 * rot_row_ref[2]

    @pl.when(i == 0)
    def _():
        r_scr[...] = r0_ref[...]
        u_scr[...] = u0_ref[...]

    x = x_ref[...]
    h = _norm_mod(x, g_ref[...], sh_ref[0:1, :], sc_ref[0:1, :]).astype(BF16)
    proj = _dot(h, w_in_ref[...])
    tm = x.shape[0]
    tb = dec_ref.shape[1]
    r_i = lax.broadcasted_iota(jnp.int32, (LANES, LANES), 0) < DK_RET
    c_i = lax.broadcasted_iota(jnp.int32, (LANES, LANES), 1) < DK_RET
    blockdiag = r_i == c_i

    def cross_fn(p, q16):
        return _dot(q16, r_scr[p].astype(BF16))

    def update_fn(p, kz_t16, v16):
        r_scr[p] = r_scr[p] * gmat_ref[p] + jnp.where(blockdiag, _dot(kz_t16, v16), 0.0)

    rets = []
    for r in range(tm // tb):
        rows = slice(r * tb, (r + 1) * tb)
        rets.append(_retention_block(proj[rows, :], cos[rows, :], sin_signed[rows, :], dec_ref, xi_ref[...],
                                     zt_ref, gain_ref[...], cross_fn, update_fn))
    ret_out = jnp.concatenate(rets, axis=0)

    prev8 = u_scr[...]
    out, u = _even_tail(x, proj, ret_out, lambda u, s: _shift_rows(u, prev8, s), cw_ref, gate_ref[0:1, :], w_out_ref)
    u_scr[...] = u[tm - SUBLANES:, :]
    o_ref[...] = out
    r_out_ref[...] = r_scr[...]
    u_out_ref[...] = u[tm - SUBLANES:, :]


def _even_prompt(x, mod, l, n_streams, norm_g, w_in16, w_out16, gain, cw, tabs, r0, u0):
    t = x.shape[0]
    tm, tb = ROW_TILE, RET_BLOCK
    rot_tile, rot_row, dec, xi, zt, gmat = tabs
    li = l // 2
    state = (HEAD_PAIRS, LANES, LANES)
    return pl.pallas_call(
        _even_prompt_kernel,
        out_shape=(jax.ShapeDtypeStruct((t, D_MODEL), F32),
                   jax.ShapeDtypeStruct(state, F32),
                   jax.ShapeDtypeStruct((SUBLANES, D_SCONV), F32)),
        grid=(t // tm,),
        in_specs=[_rows(tm, D_MODEL), _layer((1, D_MODEL), l)] + [_mod_prompt(l, j, n_streams) for j in range(3)]
        + [_layer((D_MODEL, D_IN_EVEN), li), _layer((D_MODEL, D_MODEL), li),
           _layer((1, D_RET), li), _layer((3, D_SCONV), li),
           _whole((3, t // tm, LANES)), _whole((3, tm, LANES)),
           _whole((H_RET, tb, tb)), _whole((tb, D_RET)), _whole((D_RET, tb)),
           _whole(state), _whole(state), _whole((SUBLANES, D_SCONV))],
        out_specs=(_rows(tm, D_MODEL), _whole_out(state), _whole_out((SUBLANES, D_SCONV))),
        scratch_shapes=[pltpu.VMEM(state, F32), pltpu.VMEM((SUBLANES, D_SCONV), F32)],
        compiler_params=_params(),
        name="even_prompt",
    )(x, norm_g, mod, mod, mod, w_in16, w_out16, gain, cw, rot_tile, rot_row, dec, xi, zt, gmat, r0, u0)


def _even_sample_kernel(x_ref, g_ref, sh_ref, sc_ref, gate_ref, w_in_ref, w_out_ref, gain_ref, cw_ref,
                        cos_ref, sin_ref, dec_ref, xi_ref, zt_ref, gwide_ref, s_stack_ref, s_wide_ref,
                        u1_ref, u2_ref, o_ref, s_out_ref, u_out_ref, *, t):
    x = x_ref[...]
    rows = x.shape[0]
    n_streams = rows // t
    wide = n_streams * LANES
    h = _norm_mod(x, g_ref[...], _stream_rows(sh_ref[...], t), _stream_rows(sc_ref[...], t)).astype(BF16)
    proj = _dot(h, w_in_ref[...])
    own = (lax.broadcasted_iota(jnp.int32, (rows, wide), 0) // t
           == lax.broadcasted_iota(jnp.int32, (rows, wide), 1) // LANES)
    r_i = lax.broadcasted_iota(jnp.int32, (LANES, wide), 0) < DK_RET
    blockdiag = r_i == _low_half((LANES, wide))

    def expand(a16):
        tiled = jnp.concatenate([a16.astype(F32)] * n_streams, axis=-1)
        return jnp.where(own, tiled, 0.0).astype(BF16)

    def cross_fn(p, q16):
        return _dot(expand(q16), s_stack_ref[p].astype(BF16))

    def update_fn(p, kz_t16, v16):
        kv = _dot(kz_t16, expand(v16))
        s_out_ref[p] = s_wide_ref[p] * gwide_ref[p] + jnp.where(blockdiag, kv, 0.0)

    ret_out = _retention_block(proj, cos_ref[...], sin_ref[...], dec_ref, xi_ref[...], zt_ref, gain_ref[...],
                               cross_fn, update_fn)
    heads = {1: u1_ref, 2: u2_ref}
    out, u = _even_tail(x, proj, ret_out, lambda u, s: _shift_rows_streams(u, heads[s][...], s, t),
                        cw_ref, _stream_rows(gate_ref[...], t), w_out_ref)
    o_ref[...] = out
    u_out_ref[...] = u


def _even_sample(x, mod, l, norm_g, w_in16, w_out16, gain, cw, tabs, s_stack, s_wide, u1, u2, t):
    rows = x.shape[0]
    n_streams = rows // t
    wide = n_streams * LANES
    cos, sin, dec, xi, zt, gwide = tabs
    li = l // 2
    return pl.pallas_call(
        functools.partial(_even_sample_kernel, t=t),
        out_shape=(jax.ShapeDtypeStruct((rows, D_MODEL), F32),
                   jax.ShapeDtypeStruct((HEAD_PAIRS, LANES, wide), F32),
                   jax.ShapeDtypeStruct((rows, D_SCONV), F32)),
        grid=(1,),
        in_specs=[_whole((rows, D_MODEL)), _layer((1, D_MODEL), l)] + [_mod_streams(l, j, n_streams) for j in range(3)]
        + [_layer((D_MODEL, D_IN_EVEN), li), _layer((D_MODEL, D_MODEL), li),
           _layer((1, D_RET), li), _layer((3, D_SCONV), li),
           _whole((rows, LANES)), _whole((rows, LANES)),
           _whole((H_RET, rows, rows)), _whole((rows, D_RET)), _whole((D_RET, rows)),
           _whole((HEAD_PAIRS, LANES, wide)), _layer((HEAD_PAIRS, wide, LANES), li),
           _layer((HEAD_PAIRS, LANES, wide), li),
           _layer((rows, D_SCONV), li), _layer((rows, D_SCONV), li)],
        out_specs=(_whole_out((rows, D_MODEL)), _whole_out((HEAD_PAIRS, LANES, wide)),
                   _whole_out((rows, D_SCONV))),
        compiler_params=_params(),
        name="even_sample",
    )(x, norm_g, mod, mod, mod, w_in16, w_out16, gain, cw, cos, sin, dec, xi, zt, gwide, s_stack, s_wide, u1, u2)


def _pool(hist_ref, p, pos, pool_w_ref, scale):
    t = p.shape[0]
    outs = []
    for gi, w in enumerate(POOL_WINDOWS):
        cols = slice(gi * POOL_GROUP, (gi + 1) * POOL_GROUP)
        win = p[:, cols]
        for d in range(1, w):
            win = win + hist_ref[POOL_BASE - d:POOL_BASE - d + t, cols]
        inv_cnt = 1.0 / jnp.minimum(pos + 1, w).astype(F32)
        pooled = win * inv_cnt - p[:, cols]
        outs.append(_dot(pooled.astype(BF16), pool_w_ref[gi]) * scale[:, cols])
    return jnp.concatenate(outs, axis=-1)


def _attend_block(kb16, vt16, q_a, q_b, bias_t):
    return _attend_blocks([kb16], [vt16], [q_a], [q_b], [bias_t])[0]


def _attend_blocks(kbs, vts, q_as, q_bs, biases):
    low = _low_half((CHUNK, LANES))
    n = len(kbs)
    qbd = []
    for q_a, q_b in zip(q_as, q_bs):
        qs = jnp.concatenate([jnp.where(low, q_a, 0.0), jnp.where(low, 0.0, q_a),
                              jnp.where(low, q_b, 0.0), jnp.where(low, 0.0, q_b)], axis=0)
        qbd.append(qs.T.astype(BF16))
    half = [(kb.shape[0] // 2) // (2 * SUBLANES) * (2 * SUBLANES) for kb in kbs]
    s = [jnp.concatenate([_dot(kbs[j][:half[j]], qbd[j]), _dot(kbs[j][half[j]:], qbd[j])], axis=0) + biases[j]
         for j in range(n)]
    e = [jnp.exp2(s[j] - jnp.max(s[j], axis=0, keepdims=True)) for j in range(n)]
    inv_l = [1.0 / jnp.sum(e[j], axis=0, keepdims=True) for j in range(n)]
    e16 = [e[j].astype(BF16) for j in range(n)]
    o_t = [jnp.concatenate([_dot(vts[j][:DK_RET], e16[j]), _dot(vts[j][DK_RET:], e16[j])], axis=0) * inv_l[j]
           for j in range(n)]
    o_t = [o.T for o in o_t]
    return [(jnp.where(low, o[0:CHUNK], o[CHUNK:2 * CHUNK]), jnp.where(low, o[2 * CHUNK:3 * CHUNK], o[3 * CHUNK:]))
            for o in o_t]


def _odd_prompt_kernel(x_ref, g_ref, sh_ref, sc_ref, gate_ref, w_in_ref, w_out_ref, pw_ref, ps_ref, bias_ref, p0_ref,
                       o_ref, p_out_ref, k_out_ref, v_out_ref, pbuf, kbuf, vtbuf, q_scr, att_scr):
    i = pl.program_id(0)
    tm = x_ref.shape[0]

    @pl.when(i == 0)
    def _():
        pbuf[0:POOL_BASE, :] = p0_ref[...]
        kbuf[0:HIST, :] = jnp.zeros((HIST, D_ATT), BF16)
        vtbuf[:, :, 0:HIST] = jnp.zeros((HEAD_PAIRS, LANES, HIST), BF16)

    x = x_ref[...]
    h = _norm_mod(x, g_ref[...], sh_ref[0:1, :], sc_ref[0:1, :]).astype(BF16)
    proj = _dot(h, w_in_ref[...])
    p = proj[:, :D_POOL]
    q_scr[...] = proj[:, D_POOL:D_POOL + D_ATT] * (DH_ATT ** -0.5 * LOG2E)
    k = proj[:, D_POOL + D_ATT:D_POOL + 2 * D_ATT]
    v = proj[:, D_POOL + 2 * D_ATT:]
    pbuf[POOL_BASE:POOL_BASE + tm, :] = p
    kbuf[HIST:HIST + tm, :] = k.astype(BF16)
    for pr in range(HEAD_PAIRS):
        vtbuf[pr, :, HIST:HIST + tm] = v[:, pr * LANES:(pr + 1) * LANES].T.astype(BF16)
    k_out_ref[...] = k[tm - HIST:, :]
    v_out_ref[...] = v[tm - HIST:, :]

    pos = i * tm + lax.broadcasted_iota(jnp.int32, (tm, 1), 0)
    pool_out = _pool(pbuf, p, pos, pw_ref, ps_ref[...])

    def attend_tile(first_tile):
        for jb in range(tm // (2 * CHUNK)):
            r0 = jb * 2 * CHUNK
            skip = max(HIST - r0, 0) if first_tile else 0
            pairs = range(HEAD_PAIRS)
            lanes = [slice(pr * LANES, (pr + 1) * LANES) for pr in pairs]
            outs = _attend_blocks([kbuf[r0 + skip:r0 + BAND2, lanes[pr]] for pr in pairs],
                                  [vtbuf[pr, :, r0 + skip:r0 + BAND2] for pr in pairs],
                                  [q_scr[r0:r0 + CHUNK, lanes[pr]] for pr in pairs],
                                  [q_scr[r0 + CHUNK:r0 + 2 * CHUNK, lanes[pr]] for pr in pairs],
                                  [bias_ref[pr, skip:, :] for pr in pairs])
            att_scr[r0:r0 + CHUNK, :] = jnp.concatenate([o[0] for o in outs], axis=-1)
            att_scr[r0 + CHUNK:r0 + 2 * CHUNK, :] = jnp.concatenate([o[1] for o in outs], axis=-1)

    pl.when(i == 0)(functools.partial(attend_tile, True))
    pl.when(i > 0)(functools.partial(attend_tile, False))

    kbuf[0:HIST, :] = kbuf[tm:tm + HIST, :]
    vtbuf[:, :, 0:HIST] = vtbuf[:, :, tm:tm + HIST]
    tail = pbuf[tm:tm + POOL_BASE, :]
    pbuf[0:POOL_BASE, :] = tail
    p_out_ref[...] = tail

    mixed = jnp.concatenate([pool_out, att_scr[...]], axis=-1).astype(BF16)
    o_ref[...] = x + gate_ref[0:1, :] * _dot(mixed, w_out_ref[...])


def _odd_prompt(x, mod, l, n_streams, norm_g, w_in16, w_out16, pw16, ps, bias_t, p0):
    t = x.shape[0]
    tm = ROW_TILE
    assert tm == HIST and t % tm == 0
    li = l // 2
    return pl.pallas_call(
        _odd_prompt_kernel,
        out_shape=(jax.ShapeDtypeStruct((t, D_MODEL), F32),
                   jax.ShapeDtypeStruct((POOL_BASE, D_POOL), F32),
                   jax.ShapeDtypeStruct((HIST, D_ATT), F32),
                   jax.ShapeDtypeStruct((HIST, D_ATT), F32)),
        grid=(t // tm,),
        in_specs=[_rows(tm, D_MODEL), _layer((1, D_MODEL), l)] + [_mod_prompt(l, j, n_streams) for j in range(3)]
        + [_layer((D_MODEL, D_IN_ODD), li), _layer((D_MODEL, D_MODEL), li),
           _layer((len(POOL_WINDOWS), POOL_GROUP, POOL_GROUP), li), _layer((1, D_POOL), li),
           _whole((HEAD_PAIRS, BAND2, 2 * LANES)), _whole((POOL_BASE, D_POOL))],
        out_specs=(_rows(tm, D_MODEL), _whole_out((POOL_BASE, D_POOL)),
                   _whole_out((HIST, D_ATT)), _whole_out((HIST, D_ATT))),
        scratch_shapes=[pltpu.VMEM((POOL_BASE + tm, D_POOL), F32),
                        pltpu.VMEM((HIST + tm, D_ATT), BF16), pltpu.VMEM((HEAD_PAIRS, LANES, HIST + tm), BF16),
                        pltpu.VMEM((tm, D_ATT), F32), pltpu.VMEM((tm, D_ATT), F32)],
        compiler_params=_params(),
        name="odd_prompt",
    )(x, norm_g, mod, mod, mod, w_in16, w_out16, pw16, ps, bias_t, p0)


def _odd_sample_kernel(x_ref, g_ref, sh_ref, sc_ref, gate_ref, w_in_ref, w_out_ref, pw_ref, ps_ref,
                       bias_c_ref, bias_n_ref, p0_ref, kc_ref, vc_ref,
                       o_ref, p_out_ref, k_out_ref, v_out_ref, proj_scr, mix_scr, pbuf, *, pos0, t):
    b = pl.program_id(0)
    rows = x_ref.shape[0]

    @pl.when(b == 0)
    def _():
        h = _norm_mod(x_ref[...], g_ref[...], _stream_rows(sh_ref[...], t), _stream_rows(sc_ref[...], t)).astype(BF16)
        proj = _dot(h, w_in_ref[...])
        proj_scr[...] = proj
        k_out_ref[...] = proj[:, D_POOL + D_ATT:D_POOL + 2 * D_ATT]
        v_out_ref[...] = proj[:, D_POOL + 2 * D_ATT:]

    r0 = pl.multiple_of(b * t, t)
    proj = proj_scr[pl.ds(r0, t), :]
    p = proj[:, :D_POOL]
    pbuf[0:POOL_BASE, :] = p0_ref[...]
    pbuf[POOL_BASE:POOL_BASE + t, :] = p
    p_out_ref[0] = p
    pos = pos0 + lax.broadcasted_iota(jnp.int32, (t, 1), 0)
    pool_out = _pool(pbuf, p, pos, pw_ref, ps_ref[...])

    low = _low_half((t, LANES))
    own = lax.broadcasted_iota(jnp.int32, (t, rows), 1) // t == b
    outs = []
    for pr in range(HEAD_PAIRS):
        cols = slice(pr * LANES, (pr + 1) * LANES)
        qp = proj[:, D_POOL + pr * LANES:D_POOL + (pr + 1) * LANES] * (DH_ATT ** -0.5)
        kn = proj_scr[:, D_POOL + D_ATT + pr * LANES:D_POOL + D_ATT + (pr + 1) * LANES].astype(BF16)
        vn = proj_scr[:, D_POOL + 2 * D_ATT + pr * LANES:D_POOL + 2 * D_ATT + (pr + 1) * LANES].astype(BF16)
        kc = kc_ref[:, cols]
        vc = vc_ref[:, cols]
        halves = []
        for hh, qh in ((2 * pr, jnp.where(low, qp, 0.0)), (2 * pr + 1, jnp.where(low, 0.0, qp))):
            qh16 = qh.astype(BF16)
            s_c = _dot_nt(qh16, kc) + bias_c_ref[hh]
            s_n = jnp.where(own, _dot_nt(qh16, kn) + bias_n_ref[hh], NEG_INF)
            m = jnp.maximum(jnp.max(s_c, axis=-1, keepdims=True), jnp.max(s_n, axis=-1, keepdims=True))
            e_c = jnp.exp(s_c - m)
            e_n = jnp.exp(s_n - m)
            inv_l = 1.0 / (jnp.sum(e_c, axis=-1, keepdims=True) + jnp.sum(e_n, axis=-1, keepdims=True))
            halves.append((_dot(e_c.astype(BF16), vc) + _dot(e_n.astype(BF16), vn)) * inv_l)
        outs.append(jnp.where(low, halves[0], halves[1]))
    mix_scr[pl.ds(r0, t), :] = jnp.concatenate([pool_out] + outs, axis=-1)

    @pl.when(b == pl.num_programs(0) - 1)
    def _():
        o_ref[...] = x_ref[...] + _stream_rows(gate_ref[...], t) * _dot(mix_scr[...].astype(BF16), w_out_ref[...])


def _odd_sample(x, mod, l, norm_g, w_in16, w_out16, pw16, ps, bias_c, bias_n, p0, kc16, vc16, t, pos0):
    rows = x.shape[0]
    n_streams = rows // t
    cache = kc16.shape[2]
    li = l // 2
    return pl.pallas_call(
        functools.partial(_odd_sample_kernel, pos0=pos0, t=t),
        out_shape=(jax.ShapeDtypeStruct((rows, D_MODEL), F32),
                   jax.ShapeDtypeStruct((n_streams, t, D_POOL), F32),
                   jax.ShapeDtypeStruct((rows, D_ATT), F32),
                   jax.ShapeDtypeStruct((rows, D_ATT), F32)),
        grid=(n_streams,),
        in_specs=[_whole((rows, D_MODEL)), _layer((1, D_MODEL), l)] + [_mod_streams(l, j, n_streams) for j in range(3)]
        + [_layer((D_MODEL, D_IN_ODD), li), _layer((D_MODEL, D_MODEL), li),
           _layer((len(POOL_WINDOWS), POOL_GROUP, POOL_GROUP), li), _layer((1, D_POOL), li),
           _whole((H_ATT, t, cache)), _whole((H_ATT, t, rows)),
           _layer_stream((POOL_BASE, D_POOL), li), _layer_stream((cache, D_ATT), li),
           _layer_stream((cache, D_ATT), li)],
        out_specs=(_whole_out((rows, D_MODEL)), _lead((t, D_POOL)),
                   _whole_out((rows, D_ATT)), _whole_out((rows, D_ATT))),
        scratch_shapes=[pltpu.VMEM((rows, D_IN_ODD), F32), pltpu.VMEM((rows, D_MODEL), F32),
                        pltpu.VMEM((POOL_BASE + t, D_POOL), F32)],
        compiler_params=_params(),
        name="odd_sample",
    )(x, norm_g, mod, mod, mod, w_in16, w_out16, pw16, ps, bias_c, bias_n, p0, kc16, vc16)


def _ffn_tile(x, h16, w_up_ref, cw_ref, w_down_ref, shifted, store_up):
    acc = jnp.zeros((x.shape[0], D_MODEL), F32)
    for c in range(D_FF // FFN_COLS):
        halves = []
        for off in (0, D_FF):
            cols = slice(off + c * FFN_COLS, off + (c + 1) * FFN_COLS)
            up = _dot(h16, w_up_ref[:, cols])
            conv = (cw_ref[0:1, cols] * shifted(up, cols, 2) + cw_ref[1:2, cols] * shifted(up, cols, 1)
                    + cw_ref[2:3, cols] * up)
            store_up(up, cols)
            halves.append(conv)
        act = (_silu(halves[0]) * halves[1]).astype(BF16)
        acc = acc + _dot(act, w_down_ref[c * FFN_COLS:(c + 1) * FFN_COLS, :])
    return acc


def _ffn_prompt_kernel(x_ref, g_ref, sh_ref, sc_ref, gate_ref, w_up_ref, cw_ref, w_down_ref, f0_ref, gf_ref,
                       o_ref, f_out_ref, up_scr, act_scr, *, final_norm):
    i = pl.program_id(0)
    tm = x_ref.shape[0]
    n_chunks = D_FF // FFN_COLS
    slabs_per_chunk = FFN_COLS // LANES

    @pl.when(i == 0)
    def _():
        for j in range(2 * D_FF // LANES):
            up_scr[j, 0:SUBLANES, :] = f0_ref[:, j * LANES:(j + 1) * LANES]

    x = x_ref[...]
    h16 = _norm_mod(x, g_ref[...], sh_ref[0:1, :], sc_ref[0:1, :]).astype(BF16)

    def project(c):
        for off in (0, D_FF):
            c0 = off + c * FFN_COLS
            up = _dot(h16, w_up_ref[:, c0:c0 + FFN_COLS])
            for j in range(slabs_per_chunk):
                up_scr[c0 // LANES + j, SUBLANES:SUBLANES + tm, :] = up[:, j * LANES:(j + 1) * LANES]
            f_out_ref[:, c0:c0 + FFN_COLS] = up[tm - SUBLANES:, :]

    def conv_slab(j):
        cols = slice(j * LANES, (j + 1) * LANES)
        y = (cw_ref[0:1, cols] * up_scr[j, SUBLANES - 2:SUBLANES - 2 + tm, :]
             + cw_ref[1:2, cols] * up_scr[j, SUBLANES - 1:SUBLANES - 1 + tm, :]
             + cw_ref[2:3, cols] * up_scr[j, SUBLANES:SUBLANES + tm, :])
        up_scr[j, 0:SUBLANES, :] = up_scr[j, tm:tm + SUBLANES, :]
        return y

    def activate(c):
        for j in range(slabs_per_chunk):
            ja = c * slabs_per_chunk + j
            a = conv_slab(ja)
            b = conv_slab(D_FF // LANES + ja)
            act_scr[:, ja * LANES:(ja + 1) * LANES] = (_silu(a) * b).astype(BF16)

    project(0)
    for c in range(n_chunks):
        if c + 1 < n_chunks:
            project(c + 1)
        activate(c)

    out = x + gate_ref[0:1, :] * _dot(act_scr[...], w_down_ref[...])
    if final_norm:
        out = _rmsnorm(out, gf_ref[...])
    o_ref[...] = out


def _ffn_prompt(x, mod, l, n_streams, norm_g, w_up16, cw, w_down16, f0, gf, final_norm):
    t = x.shape[0]
    tm = ROW_TILE
    return pl.pallas_call(
        functools.partial(_ffn_prompt_kernel, final_norm=final_norm),
        out_shape=(jax.ShapeDtypeStruct((t, D_MODEL), F32), jax.ShapeDtypeStruct((SUBLANES, 2 * D_FF), F32)),
        grid=(t // tm,),
        in_specs=[_rows(tm, D_MODEL), _layer((1, D_MODEL), l)] + [_mod_prompt(l, 3 + j, n_streams) for j in range(3)]
        + [_layer((D_MODEL, 2 * D_FF), l), _layer((3, 2 * D_FF), l), _layer((D_FF, D_MODEL), l),
           _whole((SUBLANES, 2 * D_FF)), _whole((1, D_MODEL))],
        out_specs=(_rows(tm, D_MODEL), _whole_out((SUBLANES, 2 * D_FF))),
        scratch_shapes=[pltpu.VMEM((2 * D_FF // LANES, SUBLANES + tm, LANES), F32),
                        pltpu.VMEM((tm, D_FF), BF16)],
        compiler_params=_params(),
        name="ffn_prompt",
    )(x, norm_g, mod, mod, mod, w_up16, cw, w_down16, f0, gf)


def _ffn_sample_kernel(x_ref, g_ref, sh_ref, sc_ref, gate_ref, w_up_ref, cw_ref, w_down_ref, s1_ref, s2_ref, gf_ref,
                       o_ref, up_out_ref, *, final_norm, t):
    x = x_ref[...]
    h16 = _norm_mod(x, g_ref[...], _stream_rows(sh_ref[...], t), _stream_rows(sc_ref[...], t)).astype(BF16)
    heads = {1: s1_ref, 2: s2_ref}

    def shifted(up, cols, s):
        return _shift_rows_streams(up, heads[s][:, cols], s, t)

    def store_up(up, cols):
        up_out_ref[:, cols] = up

    out = x + _stream_rows(gate_ref[...], t) * _ffn_tile(x, h16, w_up_ref, cw_ref, w_down_ref, shifted, store_up)
    if final_norm:
        out = _rmsnorm(out, gf_ref[...])
    o_ref[...] = out


def _ffn_sample(x, mod, l, norm_g, w_up16, cw, w_down16, s1, s2, gf, final_norm, t):
    rows = x.shape[0]
    n_streams = rows // t
    return pl.pallas_call(
        functools.partial(_ffn_sample_kernel, final_norm=final_norm, t=t),
        out_shape=(jax.ShapeDtypeStruct((rows, D_MODEL), F32), jax.ShapeDtypeStruct((rows, 2 * D_FF), F32)),
        grid=(1,),
        in_specs=[_whole((rows, D_MODEL)), _layer((1, D_MODEL), l)]
        + [_mod_streams(l, 3 + j, n_streams) for j in range(3)]
        + [_layer((D_MODEL, 2 * D_FF), l), _layer((3, 2 * D_FF), l), _layer((D_FF, D_MODEL), l),
           _layer((rows, 2 * D_FF), l), _layer((rows, 2 * D_FF), l), _whole((1, D_MODEL))],
        out_specs=(_whole_out((rows, D_MODEL)), _whole_out((rows, 2 * D_FF))),
        compiler_params=_params(),
        name="ffn_sample",
    )(x, norm_g, mod, mod, mod, w_up16, cw, w_down16, s1, s2, gf)


def _rotary_triplet(pos):
    half = DK_RET // 2
    inv = ROPE_BASE ** (-jnp.arange(half, dtype=F32) / half)
    ang = pos.astype(F32)[:, None] * inv[None, :]
    cos, sin = jnp.cos(ang), jnp.sin(ang)
    return jnp.stack([jnp.concatenate([cos] * 4, axis=-1), jnp.concatenate([sin] * 4, axis=-1),
                      jnp.concatenate([-sin, sin, -sin, sin], axis=-1)])


def _retention_tables(tb, n_streams=1):
    idx = np.arange(tb, dtype=np.float64)
    diff = idx[:, None] - idx[None, :]
    dec1 = np.where(diff[None] >= 0, np.exp(LOG_G[:, None, None] * np.maximum(diff, 0.0)[None]), 0.0)
    dec = np.zeros((H_RET, n_streams * tb, n_streams * tb))
    for b in range(n_streams):
        dec[:, b * tb:(b + 1) * tb, b * tb:(b + 1) * tb] = dec1
    xi = np.tile(np.repeat(np.exp(LOG_G[:, None] * (idx + 1)[None, :]).T, DK_RET, axis=1), (n_streams, 1))
    zeta_t = np.tile(np.repeat(np.exp(LOG_G[:, None] * (tb - 1 - idx)[None, :]), DK_RET, axis=0), (1, n_streams))
    gmat = np.zeros((HEAD_PAIRS, LANES, LANES))
    for h in range(H_RET):
        o = (h % 2) * DK_RET
        gmat[h // 2, o:o + DK_RET, o:o + DK_RET] = np.exp(LOG_G[h] * tb)
    gmat = np.tile(gmat, (1, 1, n_streams))
    return tuple(jnp.asarray(a, F32) for a in (dec, xi, zeta_t, gmat))


def _pair_state(s):
    lead = s.shape[:-3]
    s = s.reshape(lead + (HEAD_PAIRS, 2, DK_RET, DK_RET))
    z = jnp.zeros_like(s[..., 0, :, :])
    top = jnp.concatenate([s[..., 0, :, :], z], axis=-1)
    bot = jnp.concatenate([z, s[..., 1, :, :]], axis=-1)
    return jnp.concatenate([top, bot], axis=-2)


def _unpair_state(r):
    a = r[..., :DK_RET, :DK_RET]
    b = r[..., DK_RET:, DK_RET:]
    s = jnp.stack([a, b], axis=-3)
    return s.reshape(r.shape[:-3] + (H_RET, DK_RET, DK_RET))


def _band_bias(table):
    nq, nk = 2 * CHUNK, BAND2
    period = nq + nk
    j = np.arange(period)
    j = np.where(j < nk, j, j - period)
    idx = np.clip(HIST - j, -(CHUNK - 1), REL_CLIP) + (CHUNK - 1)
    one_period = table[:, idx].astype(F32)
    flat = jnp.tile(one_period, (1, nq + 1))[:, :nq * (period - 1)]
    return flat.reshape(-1, nq, period - 1)[:, :, :nk]


def _band_bias_t(raw):
    qq = np.arange(2 * CHUNK)[:, None]
    kk = np.arange(BAND2)[None, :]
    valid = np.where(qq < CHUNK, kk < BAND, kk >= CHUNK)
    b = jnp.where(valid, raw * LOG2E, NEG_INF).reshape(HEAD_PAIRS, 2, 2, CHUNK, BAND2)
    return jnp.transpose(b, (0, 4, 2, 1, 3)).reshape(HEAD_PAIRS, BAND2, 2 * LANES)


def _tail_rows(a, n):
    return a[..., a.shape[-2] - n:, :]


def _stream_heads(st, t, s):
    lead, (n, smax, c) = st.shape[:-3], st.shape[-3:]
    zeros = jnp.zeros(lead + (n, t - s, c), st.dtype)
    return jnp.concatenate([st[..., smax - s:, :], zeros], axis=-2).reshape(lead + (n * t, c))


def kernel(x_prompt, x_sample, state_ret, state_sconv, state_pool, cache_k, cache_v, state_ffn, c_prompt, c_sample,
           norm_mix, norm_ffn, norm_final, w_ada, b_ada, w_in_even, w_out_even, ret_gn_gain, sconv_w, w_in_odd,
           w_out_odd, pool_w, pool_scale, rel_bias_table, ffn_w_up, ffn_conv, ffn_w_down):
    n_prompt, seq, _ = x_prompt.shape
    n_streams, t_s, _ = x_sample.shape
    assert n_prompt == 1 and n_streams % MOD_ROWS_PROMPT == 0
    rows_s = n_streams * t_s
    n_even, n_odd = (DEPTH + 1) // 2, DEPTH // 2

    c_all = jnp.concatenate([c_sample, c_prompt], axis=0)
    mod = _ada(jnp.pad(c_all, ((0, MOD_ROWS_PROMPT - 1), (0, 0))), w_ada, b_ada)

    bf = lambda w: w.astype(BF16)
    w_in_even16, w_out_even16 = bf(w_in_even), bf(w_out_even)
    w_in_odd16, w_out_odd16, pool_w16 = bf(w_in_odd), bf(w_out_odd), bf(pool_w)
    w_up16, w_down16 = bf(ffn_w_up), bf(ffn_w_down)
    norm_mix3, norm_ffn3 = norm_mix.reshape(DEPTH, 1, D_MODEL), norm_ffn.reshape(DEPTH, 1, D_MODEL)
    gain3, pool_scale3 = ret_gn_gain.reshape(n_even, 1, D_RET), pool_scale.reshape(n_odd, 1, D_POOL)
    norm_final2 = norm_final.reshape(1, D_MODEL)

    n_tiles = seq // ROW_TILE
    tabs_p = (_rotary_triplet(jnp.arange(n_tiles, dtype=jnp.int32) * ROW_TILE),
              _rotary_triplet(jnp.arange(ROW_TILE, dtype=jnp.int32))) + _retention_tables(RET_BLOCK)
    rot_s = _rotary_triplet(PAST_LEN + jnp.arange(t_s, dtype=jnp.int32))
    tabs_s = (jnp.tile(rot_s[0], (n_streams, 1)), jnp.tile(rot_s[2], (n_streams, 1))) \
        + _retention_tables(t_s, n_streams)

    cache_len = cache_k.shape[2]
    assert cache_len == HIST and t_s <= CHUNK
    bias_raw = [_band_bias(rel_bias_table[i]) for i in range(n_odd)]
    bias_p = [_band_bias_t(b) for b in bias_raw]
    bias_c = [b[:, :t_s, :cache_len] for b in bias_raw]
    bias_n = [jnp.tile(b[:, :t_s, cache_len:cache_len + t_s], (1, 1, n_streams)) for b in bias_raw]

    paired = _pair_state(state_ret)
    s_stack = jnp.transpose(paired, (0, 2, 1, 3, 4)).reshape(n_even, HEAD_PAIRS, n_streams * LANES, LANES)
    s_wide = jnp.transpose(paired, (0, 2, 3, 1, 4)).reshape(n_even, HEAD_PAIRS, LANES, n_streams * LANES)
    u1, u2 = _stream_heads(state_sconv, t_s, 1), _stream_heads(state_sconv, t_s, 2)
    f1, f2 = _stream_heads(state_ffn, t_s, 1), _stream_heads(state_ffn, t_s, 2)
    p0_s = jnp.pad(state_pool, ((0, 0), (0, 0), (POOL_BASE - POOL_BUF, 0), (0, 0)))
    kc16 = cache_k.reshape(n_odd, n_streams, cache_len, D_ATT).astype(BF16)
    vc16 = cache_v.reshape(n_odd, n_streams, cache_len, D_ATT).astype(BF16)

    xp = x_prompt.reshape(seq, D_MODEL)
    xs = x_sample.reshape(rows_s, D_MODEL)

    ret_p, ret_s, sconv_p, sconv_s, pool_p, pool_s = [], [], [], [], [], []
    k_p, k_s, v_p, v_s, ffn_p, ffn_s = [], [], [], [], [], []
    for l in range(DEPTH):
        i = l // 2
        if l % 2 == 0:
            xp, r_new, u_new = _even_prompt(
                xp, mod, l, n_streams, norm_mix3, w_in_even16, w_out_even16, gain3, sconv_w, tabs_p,
                jnp.zeros((HEAD_PAIRS, LANES, LANES), F32), jnp.zeros((SUBLANES, D_SCONV), F32))
            ret_p.append(_unpair_state(r_new)[None])
            sconv_p.append(_tail_rows(u_new, 2)[None])
            xs, s_new, u_all = _even_sample(
                xs, mod, l, norm_mix3, w_in_even16, w_out_even16, gain3, sconv_w, tabs_s, s_stack, s_wide, u1, u2, t_s)
            s_new = jnp.transpose(s_new.reshape(HEAD_PAIRS, LANES, n_streams, LANES), (2, 0, 1, 3))
            ret_s.append(_unpair_state(s_new))
            sconv_s.append(_tail_rows(u_all.reshape(n_streams, t_s, D_SCONV), 2))
        else:
            xp, p_new, k_new, v_new = _odd_prompt(
                xp, mod, l, n_streams, norm_mix3, w_in_odd16, w_out_odd16, pool_w16, pool_scale3, bias_p[i],
                jnp.zeros((POOL_BASE, D_POOL), F32))
            pool_p.append(_tail_rows(p_new, POOL_BUF)[None])
            k_p.append(k_new.reshape(1, HIST, H_ATT, DH_ATT))
            v_p.append(v_new.reshape(1, HIST, H_ATT, DH_ATT))
            xs, p_new, k_new, v_new = _odd_sample(
                xs, mod, l, norm_mix3, w_in_odd16, w_out_odd16, pool_w16, pool_scale3, bias_c[i], bias_n[i],
                p0_s, kc16, vc16, t_s, PAST_LEN)
            pool_s.append(_tail_rows(p_new, POOL_BUF))
            k_s.append(k_new.reshape(n_streams, t_s, H_ATT, DH_ATT))
            v_s.append(v_new.reshape(n_streams, t_s, H_ATT, DH_ATT))
        last = l == DEPTH - 1
        xp, f_new = _ffn_prompt(xp, mod, l, n_streams, norm_ffn3, w_up16, ffn_conv, w_down16,
                                jnp.zeros((SUBLANES, 2 * D_FF), F32), norm_final2, last)
        ffn_p.append(_tail_rows(f_new, 2)[None])
        xs, up_all = _ffn_sample(xs, mod, l, norm_ffn3, w_up16, ffn_conv, w_down16, f1, f2, norm_final2, last, t_s)
        ffn_s.append(_tail_rows(up_all.reshape(n_streams, t_s, 2 * D_FF), 2))

    st = jnp.stack
    return (xp.reshape(1, seq, D_MODEL), xs.reshape(n_streams, t_s, D_MODEL),
            st(ret_p), st(ret_s), st(sconv_p), st(sconv_s), st(pool_p), st(pool_s),
            st(k_p), st(k_s), st(v_p), st(v_s), st(ffn_p), st(ffn_s))
```

```python
import functools

import numpy as np
import jax
import jax.numpy as jnp
from jax import lax
from jax.experimental import pallas as pl
from jax.experimental.pallas import tpu as pltpu

F32 = jnp.float32
BF16 = jnp.bfloat16

D_MODEL = 1024
DEPTH = 4
PAST_LEN = 4096
CHUNK = 64
H_RET = 8
DK_RET = 64
D_RET = H_RET * DK_RET
ROPE_BASE = 10000.0
D_SCONV = D_MODEL - D_RET
POOL_WINDOWS = (2, 4, 8, 16)
D_POOL = D_MODEL // 2
POOL_GROUP = D_POOL // len(POOL_WINDOWS)
POOL_BUF = max(POOL_WINDOWS) - 1
H_ATT = 8
DH_ATT = 64
D_ATT = H_ATT * DH_ATT
N_PREV_CHUNKS = 8
REL_CLIP = 256
D_FF = 2816
EPS = 1e-6
NEG_INF = -1e30
D_IN_EVEN = 4 * D_RET + 3 * D_SCONV
D_IN_ODD = D_POOL + 3 * D_ATT

LANES = 128
SUBLANES = 8
HEAD_PAIRS = H_RET // 2
ROW_TILE = 512
RET_BLOCK = 256
FFN_COLS = 256
BAND = (N_PREV_CHUNKS + 1) * CHUNK
BAND2 = BAND + CHUNK
LOG2E = 1.4426950408889634
HIST = N_PREV_CHUNKS * CHUNK
POOL_BASE = 2 * SUBLANES
VMEM_LIMIT = 56 * 1024 * 1024

LOG_G = np.log1p(-(2.0 ** (-5.0 - np.arange(H_RET, dtype=np.float64))))


def _params(n_axes=1):
    return pltpu.CompilerParams(dimension_semantics=("arbitrary",) * n_axes, vmem_limit_bytes=VMEM_LIMIT)


def _whole(shape):
    nd = len(shape)
    return pl.BlockSpec(shape, lambda i: (0,) * nd, pipeline_mode=pl.Buffered(1))


def _whole_out(shape):
    nd = len(shape)
    return pl.BlockSpec(shape, lambda i: (0,) * nd)


def _layer(shape, l):
    nd = len(shape)
    return pl.BlockSpec((None,) + tuple(shape), lambda i: (l,) + (0,) * nd, pipeline_mode=pl.Buffered(1))


MOD_ROWS_PROMPT = SUBLANES


def _mod_prompt(l, j, n_streams):
    return pl.BlockSpec((None, MOD_ROWS_PROMPT, D_MODEL), lambda i: (l, n_streams // MOD_ROWS_PROMPT, j),
                        pipeline_mode=pl.Buffered(1))


def _mod_streams(l, j, n_streams):
    return pl.BlockSpec((None, n_streams, D_MODEL), lambda i: (l, 0, j), pipeline_mode=pl.Buffered(1))


def _stream_rows(m, t):
    return jnp.concatenate([jnp.broadcast_to(m[b:b + 1, :], (t, m.shape[1])) for b in range(m.shape[0])], axis=0)


def _rows(block_rows, cols):
    return pl.BlockSpec((block_rows, cols), lambda i: (i, 0))


def _lead(shape):
    nd = len(shape)
    return pl.BlockSpec((1,) + tuple(shape), lambda i: (i,) + (0,) * nd)


def _dot(a, b):
    return jnp.dot(a, b, preferred_element_type=F32)


def _dot_nt(a, b):
    return lax.dot_general(a, b, (((1,), (1,)), ((), ())), preferred_element_type=F32)


def _rmsnorm(x, g):
    return x * lax.rsqrt(jnp.mean(x * x, axis=-1, keepdims=True) + EPS) * g


def _norm_mod(x, g, shift, scale):
    return _rmsnorm(x, g) * (1.0 + scale) + shift


def _silu(x):
    return x * (1.0 / (1.0 + jnp.exp(-x)))


def _low_half(shape):
    return (lax.broadcasted_iota(jnp.int32, shape, len(shape) - 1) % LANES) < DK_RET


def _shift_rows(u, prev8, s):
    rolled = pltpu.roll(u, s, axis=0)
    prolled = pltpu.roll(prev8, s, axis=0)
    row = lax.broadcasted_iota(jnp.int32, prev8.shape, 0)
    first = jnp.where(row < s, prolled, rolled[0:SUBLANES])
    return jnp.concatenate([first, rolled[SUBLANES:]], axis=0)


def _shift_rows_streams(u, older, newer, s, t):
    row_in_stream = lax.broadcasted_iota(jnp.int32, u.shape, 0) % t
    rolled = pltpu.roll(u, s, axis=0)
    if s == 1:
        return jnp.where(row_in_stream == 0, _stream_rows(newer, t), rolled)
    return jnp.where(row_in_stream == 0, _stream_rows(older, t),
                     jnp.where(row_in_stream == 1, _stream_rows(newer, t), rolled))


def _ada_kernel(c_ref, w_ref, b_ref, o_ref):
    c = c_ref[...]
    o_ref[0] = _dot(_silu(c).astype(BF16), w_ref[0].astype(BF16)) + b_ref[0]


def _ada(c_all, w_ada, b_ada):
    rows = c_all.shape[0]
    tn = 1536
    return pl.pallas_call(
        _ada_kernel,
        out_shape=jax.ShapeDtypeStruct((DEPTH, rows, 6 * D_MODEL), F32),
        grid=(DEPTH, 6 * D_MODEL // tn),
        in_specs=[
            pl.BlockSpec((rows, D_MODEL), lambda l, j: (0, 0)),
            pl.BlockSpec((1, D_MODEL, tn), lambda l, j: (l, 0, j)),
            pl.BlockSpec((1, 1, tn), lambda l, j: (l, 0, j)),
        ],
        out_specs=pl.BlockSpec((1, rows, tn), lambda l, j: (l, 0, j)),
        compiler_params=_params(2),
        name="ada_mod",
    )(c_all, w_ada, b_ada.reshape(DEPTH, 1, 6 * D_MODEL))


def _rotary_pair(x, cos, sin_signed):
    lane = lax.broadcasted_iota(jnp.int32, x.shape, 1)
    first_half = (lane % DK_RET) < (DK_RET // 2)
    swapped = jnp.where(first_half, pltpu.roll(x, LANES - DK_RET // 2, axis=1),
                        pltpu.roll(x, DK_RET // 2, axis=1))
    return x * cos + swapped * sin_signed


def _retention_block(proj, cos, sin_signed, dec_ref, xi, zeta_t_ref, gain, cross_fn, update_fn):
    tb = proj.shape[0]
    low = _low_half((tb, LANES))
    inv_n = 1.0 / DK_RET
    outs = []
    for p in range(HEAD_PAIRS):
        c0 = p * LANES
        q = _rotary_pair(proj[:, c0:c0 + LANES], cos, sin_signed)
        k = _rotary_pair(proj[:, D_RET + c0:D_RET + c0 + LANES], cos, sin_signed) * (DK_RET ** -0.5)
        v16 = proj[:, 2 * D_RET + c0:2 * D_RET + c0 + LANES].astype(BF16)
        g = proj[:, 3 * D_RET + c0:3 * D_RET + c0 + LANES]
        k_t = k.T
        k_t16 = k_t.astype(BF16)
        kz_t16 = (k_t * zeta_t_ref[c0:c0 + LANES, :]).astype(BF16)
        q16 = q.astype(BF16)
        qe16 = jnp.where(low, q, 0.0).astype(BF16)
        qo16 = jnp.where(low, 0.0, q).astype(BF16)
        s_e = (_dot(qe16, k_t16) * dec_ref[2 * p]).astype(BF16)
        s_o = (_dot(qo16, k_t16) * dec_ref[2 * p + 1]).astype(BF16)
        inner = jnp.where(low, _dot(s_e, v16), _dot(s_o, v16))
        o = inner + cross_fn(p, q16) * xi[:, c0:c0 + LANES]
        s_lo = jnp.sum(jnp.where(low, o, 0.0), axis=-1, keepdims=True)
        s_hi = jnp.sum(jnp.where(low, 0.0, o), axis=-1, keepdims=True)
        d = o - jnp.where(low, s_lo, s_hi) * inv_n
        d2 = d * d
        v_lo = jnp.sum(jnp.where(low, d2, 0.0), axis=-1, keepdims=True)
        v_hi = jnp.sum(jnp.where(low, 0.0, d2), axis=-1, keepdims=True)
        on = d * lax.rsqrt(jnp.where(low, v_lo, v_hi) * inv_n + EPS)
        outs.append(_silu(g) * (on * gain[:, c0:c0 + LANES]))
        update_fn(p, kz_t16, v16)
    return jnp.concatenate(outs, axis=-1)


def _even_tail(x, proj, ret_out, conv_in_shift, cw_ref, gate, w_out_ref):
    gate_b = proj[:, 4 * D_RET:4 * D_RET + D_SCONV]
    u = proj[:, 4 * D_RET + D_SCONV:4 * D_RET + 2 * D_SCONV] * proj[:, 4 * D_RET + 2 * D_SCONV:]
    conv = cw_ref[0:1, :] * conv_in_shift(u, 2) + cw_ref[1:2, :] * conv_in_shift(u, 1) + cw_ref[2:3, :] * u
    mixed = jnp.concatenate([ret_out, gate_b * conv], axis=-1).astype(BF16)
    return x + gate * _dot(mixed, w_out_ref[...]), u


def _even_prompt_kernel(x_ref, g_ref, sh_ref, sc_ref, gate_ref, w_in_ref, w_out_ref, gain_ref, cw_ref,
                        rot_tile_ref, rot_row_ref, dec_ref, xi_ref, zt_ref, gmat_ref, r0_ref, u0_ref,
                        o_ref, r_out_ref, u_out_ref, r_scr, u_scr):
    i = pl.program_id(0)
    cos_0, sin_0, ssin_0 = (rot_tile_ref[j, pl.ds(i, 1), :] for j in range(3))
    cos = cos_0 * rot_row_ref[0] - sin_0 * rot_row_ref[1]
    sin_signed = ssin_0 * rot_row_ref[0] + cos_0 * rot_row_ref[2]

    @pl.when(i == 0)
    def _():
        r_scr[...] = r0_ref[...]
        u_scr[...] = u0_ref[...]

    x = x_ref[...]
    h = _norm_mod(x, g_ref[...], sh_ref[0:1, :], sc_ref[0:1, :]).astype(BF16)
    proj = _dot(h, w_in_ref[...])
    tm = x.shape[0]
    tb = dec_ref.shape[1]
    r_i = lax.broadcasted_iota(jnp.int32, (LANES, LANES), 0) < DK_RET
    c_i = lax.broadcasted_iota(jnp.int32, (LANES, LANES), 1) < DK_RET
    blockdiag = r_i == c_i

    def cross_fn(p, q16):
        return _dot(q16, r_scr[p].astype(BF16))

    def update_fn(p, kz_t16, v16):
        r_scr[p] = r_scr[p] * gmat_ref[p] + jnp.where(blockdiag, _dot(kz_t16, v16), 0.0)

    rets = []
    for r in range(tm // tb):
        rows = slice(r * tb, (r + 1) * tb)
        rets.append(_retention_block(proj[rows, :], cos[rows, :], sin_signed[rows, :], dec_ref, xi_ref[...],
                                     zt_ref, gain_ref[...], cross_fn, update_fn))
    ret_out = jnp.concatenate(rets, axis=0)

    prev8 = u_scr[...]
    out, u = _even_tail(x, proj, ret_out, lambda u, s: _shift_rows(u, prev8, s), cw_ref, gate_ref[0:1, :], w_out_ref)
    u_scr[...] = u[tm - SUBLANES:, :]
    o_ref[...] = out
    r_out_ref[...] = r_scr[...]
    u_out_ref[...] = u[tm - SUBLANES:, :]


def _even_prompt(x, mod, l, n_streams, norm_g, w_in16, w_out16, gain, cw, tabs, r0, u0):
    t = x.shape[0]
    tm, tb = ROW_TILE, RET_BLOCK
    rot_tile, rot_row, dec, xi, zt, gmat = tabs
    li = l // 2
    state = (HEAD_PAIRS, LANES, LANES)
    return pl.pallas_call(
        _even_prompt_kernel,
        out_shape=(jax.ShapeDtypeStruct((t, D_MODEL), F32),
                   jax.ShapeDtypeStruct(state, F32),
                   jax.ShapeDtypeStruct((SUBLANES, D_SCONV), F32)),
        grid=(t // tm,),
        in_specs=[_rows(tm, D_MODEL), _layer((1, D_MODEL), l)] + [_mod_prompt(l, j, n_streams) for j in range(3)]
        + [_layer((D_MODEL, D_IN_EVEN), li), _layer((D_MODEL, D_MODEL), li),
           _layer((1, D_RET), li), _layer((3, D_SCONV), li),
           _whole((3, t // tm, LANES)), _whole((3, tm, LANES)),
           _whole((H_RET, tb, tb)), _whole((tb, D_RET)), _whole((D_RET, tb)),
           _whole(state), _whole(state), _whole((SUBLANES, D_SCONV))],
        out_specs=(_rows(tm, D_MODEL), _whole_out(state), _whole_out((SUBLANES, D_SCONV))),
        scratch_shapes=[pltpu.VMEM(state, F32), pltpu.VMEM((SUBLANES, D_SCONV), F32)],
        compiler_params=_params(),
        name="even_prompt",
    )(x, norm_g, mod, mod, mod, w_in16, w_out16, gain, cw, rot_tile, rot_row, dec, xi, zt, gmat, r0, u0)


def _even_sample_kernel(x_ref, g_ref, sh_ref, sc_ref, gate_ref, w_in_ref, w_out_ref, gain_ref, cw_ref,
                        cos_ref, sin_ref, dec_ref, xi_ref, zt_ref, gwide_ref, s_stack_ref, s_wide_ref,
                        u1_ref, u2_ref, o_ref, s_out_ref, u_out_ref, *, t):
    x = x_ref[...]
    rows = x.shape[0]
    n_streams = rows // t
    wide = n_streams * LANES
    h = _norm_mod(x, g_ref[...], _stream_rows(sh_ref[...], t), _stream_rows(sc_ref[...], t)).astype(BF16)
    proj = _dot(h, w_in_ref[...])
    own = (lax.broadcasted_iota(jnp.int32, (rows, wide), 0) // t
           == lax.broadcasted_iota(jnp.int32, (rows, wide), 1) // LANES)
    r_i = lax.broadcasted_iota(jnp.int32, (LANES, wide), 0) < DK_RET
    blockdiag = r_i == _low_half((LANES, wide))

    def expand(a16):
        tiled = jnp.concatenate([a16.astype(F32)] * n_streams, axis=-1)
        return jnp.where(own, tiled, 0.0).astype(BF16)

    def cross_fn(p, q16):
        return _dot(expand(q16), s_stack_ref[p].astype(BF16))

    def update_fn(p, kz_t16, v16):
        kv = _dot(kz_t16, expand(v16))
        s_out_ref[p] = s_wide_ref[p] * gwide_ref[p] + jnp.where(blockdiag, kv, 0.0)

    ret_out = _retention_block(proj, cos_ref[...], sin_ref[...], dec_ref, xi_ref[...], zt_ref, gain_ref[...],
                               cross_fn, update_fn)
    out, u = _even_tail(x, proj, ret_out, lambda u, s: _shift_rows_streams(u, u1_ref[...], u2_ref[...], s, t),
                        cw_ref, _stream_rows(gate_ref[...], t), w_out_ref)
    o_ref[...] = out
    u_out_ref[...] = u


def _even_sample(x, mod, l, norm_g, w_in16, w_out16, gain, cw, tabs, s_stack, s_wide, u1, u2, t):
    rows = x.shape[0]
    n_streams = rows // t
    wide = n_streams * LANES
    cos, sin, dec, xi, zt, gwide = tabs
    li = l // 2
    return pl.pallas_call(
        functools.partial(_even_sample_kernel, t=t),
        out_shape=(jax.ShapeDtypeStruct((rows, D_MODEL), F32),
                   jax.ShapeDtypeStruct((HEAD_PAIRS, LANES, wide), F32),
                   jax.ShapeDtypeStruct((rows, D_SCONV), F32)),
        grid=(1,),
        in_specs=[_whole((rows, D_MODEL)), _layer((1, D_MODEL), l)] + [_mod_streams(l, j, n_streams) for j in range(3)]
        + [_layer((D_MODEL, D_IN_EVEN), li), _layer((D_MODEL, D_MODEL), li),
           _layer((1, D_RET), li), _layer((3, D_SCONV), li),
           _whole((rows, LANES)), _whole((rows, LANES)),
           _whole((H_RET, rows, rows)), _whole((rows, D_RET)), _whole((D_RET, rows)),
           _whole((HEAD_PAIRS, LANES, wide)), _layer((HEAD_PAIRS, wide, LANES), li),
           _layer((HEAD_PAIRS, LANES, wide), li),
           _layer((n_streams, D_SCONV), li), _layer((n_streams, D_SCONV), li)],
        out_specs=(_whole_out((rows, D_MODEL)), _whole_out((HEAD_PAIRS, LANES, wide)),
                   _whole_out((rows, D_SCONV))),
        compiler_params=_params(),
        name="even_sample",
    )(x, norm_g, mod, mod, mod, w_in16, w_out16, gain, cw, cos, sin, dec, xi, zt, gwide, s_stack, s_wide, u1, u2)


def _pool(hist_ref, p, pos, pool_w_ref, scale):
    t = p.shape[0]
    outs = []
    for gi, w in enumerate(POOL_WINDOWS):
        cols = slice(gi * POOL_GROUP, (gi + 1) * POOL_GROUP)
        win = p[:, cols]
        for d in range(1, w):
            win = win + hist_ref[POOL_BASE - d:POOL_BASE - d + t, cols]
        inv_cnt = 1.0 / jnp.minimum(pos + 1, w).astype(F32)
        pooled = win * inv_cnt - p[:, cols]
        outs.append(_dot(pooled.astype(BF16), pool_w_ref[gi]) * scale[:, cols])
    return jnp.concatenate(outs, axis=-1)


def _attend_scores(kbs, q_as, q_bs, biases):
    low = _low_half((CHUNK, LANES))
    out = []
    for kb, q_a, q_b, bias in zip(kbs, q_as, q_bs, biases):
        qs = jnp.concatenate([jnp.where(low, q_a, 0.0), jnp.where(low, 0.0, q_a),
                              jnp.where(low, q_b, 0.0), jnp.where(low, 0.0, q_b)], axis=0)
        qbd = qs.T.astype(BF16)
        half = (kb.shape[0] // 2) // (2 * SUBLANES) * (2 * SUBLANES)
        out.append(jnp.concatenate([_dot(kb[:half], qbd), _dot(kb[half:], qbd)], axis=0) + bias)
    return out


def _attend_values(scores, vts):
    low = _low_half((CHUNK, LANES))
    n = len(scores)
    e = [jnp.exp2(s - jnp.max(s, axis=0, keepdims=True)) for s in scores]
    inv_l = [1.0 / jnp.sum(e[j], axis=0, keepdims=True) for j in range(n)]
    e16 = [e[j].astype(BF16) for j in range(n)]
    o_t = [jnp.concatenate([_dot(vts[j][:DK_RET], e16[j]), _dot(vts[j][DK_RET:], e16[j])], axis=0) * inv_l[j]
           for j in range(n)]
    o_t = [o.T for o in o_t]
    return [(jnp.where(low, o[0:CHUNK], o[CHUNK:2 * CHUNK]), jnp.where(low, o[2 * CHUNK:3 * CHUNK], o[3 * CHUNK:]))
            for o in o_t]


def _odd_prompt_kernel(x_ref, g_ref, sh_ref, sc_ref, gate_ref, w_in_ref, w_out_ref, pw_ref, ps_ref, bias_ref, p0_ref,
                       o_ref, p_out_ref, k_out_ref, v_out_ref, pbuf, kbuf, vtbuf, q_scr, att_scr):
    i = pl.program_id(0)
    tm = x_ref.shape[0]

    @pl.when(i == 0)
    def _():
        pbuf[0:POOL_BASE, :] = p0_ref[...]
        kbuf[0:HIST, :] = jnp.zeros((HIST, D_ATT), BF16)
        vtbuf[:, :, 0:HIST] = jnp.zeros((HEAD_PAIRS, LANES, HIST), BF16)

    x = x_ref[...]
    h = _norm_mod(x, g_ref[...], sh_ref[0:1, :], sc_ref[0:1, :]).astype(BF16)
    proj = _dot(h, w_in_ref[...])
    p = proj[:, :D_POOL]
    q_scr[...] = proj[:, D_POOL:D_POOL + D_ATT] * (DH_ATT ** -0.5 * LOG2E)
    k = proj[:, D_POOL + D_ATT:D_POOL + 2 * D_ATT]
    v = proj[:, D_POOL + 2 * D_ATT:]
    pbuf[POOL_BASE:POOL_BASE + tm, :] = p
    kbuf[HIST:HIST + tm, :] = k.astype(BF16)
    for pr in range(HEAD_PAIRS):
        vtbuf[pr, :, HIST:HIST + tm] = v[:, pr * LANES:(pr + 1) * LANES].T.astype(BF16)
    k_out_ref[...] = k[tm - HIST:, :]
    v_out_ref[...] = v[tm - HIST:, :]

    pos = i * tm + lax.broadcasted_iota(jnp.int32, (tm, 1), 0)
    pool_out = _pool(pbuf, p, pos, pw_ref, ps_ref[...])

    def attend_tile(first_tile):
        pairs = range(HEAD_PAIRS)
        lanes = [slice(pr * LANES, (pr + 1) * LANES) for pr in pairs]
        n_blocks = tm // (2 * CHUNK)
        skip = [max(HIST - jb * 2 * CHUNK, 0) if first_tile else 0 for jb in range(n_blocks)]

        def scores(jb):
            r0 = jb * 2 * CHUNK
            return _attend_scores([kbuf[r0 + skip[jb]:r0 + BAND2, lanes[pr]] for pr in pairs],
                                  [q_scr[r0:r0 + CHUNK, lanes[pr]] for pr in pairs],
                                  [q_scr[r0 + CHUNK:r0 + 2 * CHUNK, lanes[pr]] for pr in pairs],
                                  [bias_ref[pr, skip[jb]:, :] for pr in pairs])

        s_next = scores(0)
        for jb in range(n_blocks):
            r0 = jb * 2 * CHUNK
            s_cur = s_next
            if jb + 1 < n_blocks:
                s_next = scores(jb + 1)
            outs = _attend_values(s_cur, [vtbuf[pr, :, r0 + skip[jb]:r0 + BAND2] for pr in pairs])
            att_scr[r0:r0 + CHUNK, :] = jnp.concatenate([o[0] for o in outs], axis=-1)
            att_scr[r0 + CHUNK:r0 + 2 * CHUNK, :] = jnp.concatenate([o[1] for o in outs], axis=-1)

    pl.when(i == 0)(functools.partial(attend_tile, True))
    pl.when(i > 0)(functools.partial(attend_tile, False))

    kbuf[0:HIST, :] = kbuf[tm:tm + HIST, :]
    vtbuf[:, :, 0:HIST] = vtbuf[:, :, tm:tm + HIST]
    tail = pbuf[tm:tm + POOL_BASE, :]
    pbuf[0:POOL_BASE, :] = tail
    p_out_ref[...] = tail

    mixed = jnp.concatenate([pool_out, att_scr[...]], axis=-1).astype(BF16)
    o_ref[...] = x + gate_ref[0:1, :] * _dot(mixed, w_out_ref[...])


def _odd_prompt(x, mod, l, n_streams, norm_g, w_in16, w_out16, pw16, ps, bias_t, p0):
    t = x.shape[0]
    tm = ROW_TILE
    assert tm == HIST and t % tm == 0
    li = l // 2
    return pl.pallas_call(
        _odd_prompt_kernel,
        out_shape=(jax.ShapeDtypeStruct((t, D_MODEL), F32),
                   jax.ShapeDtypeStruct((POOL_BASE, D_POOL), F32),
                   jax.ShapeDtypeStruct((HIST, D_ATT), F32),
                   jax.ShapeDtypeStruct((HIST, D_ATT), F32)),
        grid=(t // tm,),
        in_specs=[_rows(tm, D_MODEL), _layer((1, D_MODEL), l)] + [_mod_prompt(l, j, n_streams) for j in range(3)]
        + [_layer((D_MODEL, D_IN_ODD), li), _layer((D_MODEL, D_MODEL), li),
           _layer((len(POOL_WINDOWS), POOL_GROUP, POOL_GROUP), li), _layer((1, D_POOL), li),
           _whole((HEAD_PAIRS, BAND2, 2 * LANES)), _whole((POOL_BASE, D_POOL))],
        out_specs=(_rows(tm, D_MODEL), _whole_out((POOL_BASE, D_POOL)),
                   _whole_out((HIST, D_ATT)), _whole_out((HIST, D_ATT))),
        scratch_shapes=[pltpu.VMEM((POOL_BASE + tm, D_POOL), F32),
                        pltpu.VMEM((HIST + tm, D_ATT), BF16), pltpu.VMEM((HEAD_PAIRS, LANES, HIST + tm), BF16),
                        pltpu.VMEM((tm, D_ATT), F32), pltpu.VMEM((tm, D_ATT), F32)],
        compiler_params=_params(),
        name="odd_prompt",
    )(x, norm_g, mod, mod, mod, w_in16, w_out16, pw16, ps, bias_t, p0)


def _odd_sample_kernel(x_ref, g_ref, sh_ref, sc_ref, gate_ref, w_in_ref, w_out_ref, pw_ref, ps_ref,
                       bias_c_ref, bias_n_ref, p0_ref, kc_ref, vc_ref,
                       o_ref, p_out_ref, k_out_ref, v_out_ref, proj_scr, mix_scr, pbuf, *, pos0, t):
    step = pl.program_id(0)
    rows = x_ref.shape[0]
    per_step = kc_ref.shape[0]

    @pl.when(step == 0)
    def _():
        h = _norm_mod(x_ref[...], g_ref[...], _stream_rows(sh_ref[...], t), _stream_rows(sc_ref[...], t)).astype(BF16)
        proj = _dot(h, w_in_ref[...])
        proj_scr[...] = proj
        k_out_ref[...] = proj[:, D_POOL + D_ATT:D_POOL + 2 * D_ATT]
        v_out_ref[...] = proj[:, D_POOL + 2 * D_ATT:]

    kn = proj_scr[:, D_POOL + D_ATT:D_POOL + 2 * D_ATT].astype(BF16)
    vn = proj_scr[:, D_POOL + 2 * D_ATT:].astype(BF16)
    pos = pos0 + lax.broadcasted_iota(jnp.int32, (t, 1), 0)
    head_of_lane = lax.broadcasted_iota(jnp.int32, (t, D_ATT), 1) // DH_ATT
    stream_of_col = lax.broadcasted_iota(jnp.int32, (H_ATT * t, rows), 1) // t
    for j in range(per_step):
        b = step * per_step + j
        r0 = pl.multiple_of(b * t, t)
        proj = proj_scr[pl.ds(r0, t), :]
        p = proj[:, :D_POOL]
        pbuf[j, 0:POOL_BASE, :] = p0_ref[j]
        pbuf[j, POOL_BASE:POOL_BASE + t, :] = p
        p_out_ref[j] = p
        pool_out = _pool(pbuf.at[j], p, pos, pw_ref, ps_ref[...])

        q = proj[:, D_POOL:D_POOL + D_ATT] * (DH_ATT ** -0.5)
        q_heads = jnp.concatenate([jnp.where(head_of_lane == hh, q, 0.0) for hh in range(H_ATT)],
                                  axis=0).astype(BF16)
        s_c = _dot_nt(q_heads, kc_ref[j]) + bias_c_ref[...]
        s_n = jnp.where(stream_of_col == b, _dot_nt(q_heads, kn) + bias_n_ref[...], NEG_INF)
        m = jnp.maximum(jnp.max(s_c, axis=-1, keepdims=True), jnp.max(s_n, axis=-1, keepdims=True))
        e_c = jnp.exp(s_c - m)
        e_n = jnp.exp(s_n - m)
        inv_l = 1.0 / (jnp.sum(e_c, axis=-1, keepdims=True) + jnp.sum(e_n, axis=-1, keepdims=True))
        o_heads = (_dot(e_c.astype(BF16), vc_ref[j]) + _dot(e_n.astype(BF16), vn)) * inv_l
        att = jnp.where(head_of_lane == 0, o_heads[0:t], 0.0)
        for hh in range(1, H_ATT):
            att = jnp.where(head_of_lane == hh, o_heads[hh * t:(hh + 1) * t], att)
        mix_scr[pl.ds(r0, t), :] = jnp.concatenate([pool_out, att], axis=-1)

    @pl.when(step == pl.num_programs(0) - 1)
    def _():
        o_ref[...] = x_ref[...] + _stream_rows(gate_ref[...], t) * _dot(mix_scr[...].astype(BF16), w_out_ref[...])


SAMPLE_STREAMS_PER_STEP = 4


def _odd_sample(x, mod, l, norm_g, w_in16, w_out16, pw16, ps, bias_c, bias_n, p0, kc16, vc16, t, pos0):
    rows = x.shape[0]
    n_streams = rows // t
    cache = kc16.shape[2]
    li = l // 2
    per_step = SAMPLE_STREAMS_PER_STEP
    assert n_streams % per_step == 0

    def streams(shape):
        nd = len(shape)
        return pl.BlockSpec((None, per_step) + tuple(shape), lambda i: (li, i) + (0,) * nd)

    return pl.pallas_call(
        functools.partial(_odd_sample_kernel, pos0=pos0, t=t),
        out_shape=(jax.ShapeDtypeStruct((rows, D_MODEL), F32),
                   jax.ShapeDtypeStruct((n_streams, t, D_POOL), F32),
                   jax.ShapeDtypeStruct((rows, D_ATT), F32),
                   jax.ShapeDtypeStruct((rows, D_ATT), F32)),
        grid=(n_streams // per_step,),
        in_specs=[_whole((rows, D_MODEL)), _layer((1, D_MODEL), l)] + [_mod_streams(l, j, n_streams) for j in range(3)]
        + [_layer((D_MODEL, D_IN_ODD), li), _layer((D_MODEL, D_MODEL), li),
           _layer((len(POOL_WINDOWS), POOL_GROUP, POOL_GROUP), li), _layer((1, D_POOL), li),
           _whole((H_ATT * t, cache)), _whole((H_ATT * t, rows)),
           streams((POOL_BASE, D_POOL)), streams((cache, D_ATT)), streams((cache, D_ATT))],
        out_specs=(_whole_out((rows, D_MODEL)), pl.BlockSpec((per_step, t, D_POOL), lambda i: (i, 0, 0)),
                   _whole_out((rows, D_ATT)), _whole_out((rows, D_ATT))),
        scratch_shapes=[pltpu.VMEM((rows, D_IN_ODD), F32), pltpu.VMEM((rows, D_MODEL), F32),
                        pltpu.VMEM((per_step, POOL_BASE + t, D_POOL), F32)],
        compiler_params=_params(),
        name="odd_sample",
    )(x, norm_g, mod, mod, mod, w_in16, w_out16, pw16, ps, bias_c, bias_n, p0, kc16, vc16)


def _ffn_tile(x, h16, w_up_ref, cw_ref, w_down_ref, shifted, store_up):
    acc = jnp.zeros((x.shape[0], D_MODEL), F32)
    for c in range(D_FF // FFN_COLS):
        halves = []
        for off in (0, D_FF):
            cols = slice(off + c * FFN_COLS, off + (c + 1) * FFN_COLS)
            up = _dot(h16, w_up_ref[:, cols])
            conv = (cw_ref[0:1, cols] * shifted(up, cols, 2) + cw_ref[1:2, cols] * shifted(up, cols, 1)
                    + cw_ref[2:3, cols] * up)
            store_up(up, cols)
            halves.append(conv)
        act = (_silu(halves[0]) * halves[1]).astype(BF16)
        acc = acc + _dot(act, w_down_ref[c * FFN_COLS:(c + 1) * FFN_COLS, :])
    return acc


def _ffn_prompt_kernel(x_ref, g_ref, sh_ref, sc_ref, gate_ref, w_up_ref, cw_ref, w_down_ref, f0_ref, gf_ref,
                       o_ref, f_out_ref, up_scr, act_scr, *, final_norm):
    i = pl.program_id(0)
    tm = x_ref.shape[0]
    n_chunks = D_FF // FFN_COLS
    slabs_per_chunk = FFN_COLS // LANES

    @pl.when(i == 0)
    def _():
        for j in range(2 * D_FF // LANES):
            up_scr[j, 0:SUBLANES, :] = f0_ref[:, j * LANES:(j + 1) * LANES]

    x = x_ref[...]
    h16 = _norm_mod(x, g_ref[...], sh_ref[0:1, :], sc_ref[0:1, :]).astype(BF16)

    def project(c):
        for off in (0, D_FF):
            c0 = off + c * FFN_COLS
            up = _dot(h16, w_up_ref[:, c0:c0 + FFN_COLS])
            for j in range(slabs_per_chunk):
                up_scr[c0 // LANES + j, SUBLANES:SUBLANES + tm, :] = up[:, j * LANES:(j + 1) * LANES]
            f_out_ref[:, c0:c0 + FFN_COLS] = up[tm - SUBLANES:, :]

    def conv_slab(j):
        cols = slice(j * LANES, (j + 1) * LANES)
        y = (cw_ref[0:1, cols] * up_scr[j, SUBLANES - 2:SUBLANES - 2 + tm, :]
             + cw_ref[1:2, cols] * up_scr[j, SUBLANES - 1:SUBLANES - 1 + tm, :]
             + cw_ref[2:3, cols] * up_scr[j, SUBLANES:SUBLANES + tm, :])
        up_scr[j, 0:SUBLANES, :] = up_scr[j, tm:tm + SUBLANES, :]
        return y

    def activate(c):
        for j in range(slabs_per_chunk):
            ja = c * slabs_per_chunk + j
            a = conv_slab(ja)
            b = conv_slab(D_FF // LANES + ja)
            act_scr[:, ja * LANES:(ja + 1) * LANES] = (_silu(a) * b).astype(BF16)

    project(0)
    for c in range(n_chunks):
        if c + 1 < n_chunks:
            project(c + 1)
        activate(c)

    out = x + gate_ref[0:1, :] * _dot(act_scr[...], w_down_ref[...])
    if final_norm:
        out = _rmsnorm(out, gf_ref[...])
    o_ref[...] = out


def _ffn_prompt(x, mod, l, n_streams, norm_g, w_up16, cw, w_down16, f0, gf, final_norm):
    t = x.shape[0]
    tm = ROW_TILE
    return pl.pallas_call(
        functools.partial(_ffn_prompt_kernel, final_norm=final_norm),
        out_shape=(jax.ShapeDtypeStruct((t, D_MODEL), F32), jax.ShapeDtypeStruct((SUBLANES, 2 * D_FF), F32)),
        grid=(t // tm,),
        in_specs=[_rows(tm, D_MODEL), _layer((1, D_MODEL), l)] + [_mod_prompt(l, 3 + j, n_streams) for j in range(3)]
        + [_layer((D_MODEL, 2 * D_FF), l), _layer((3, 2 * D_FF), l), _layer((D_FF, D_MODEL), l),
           _whole((SUBLANES, 2 * D_FF)), _whole((1, D_MODEL))],
        out_specs=(_rows(tm, D_MODEL), _whole_out((SUBLANES, 2 * D_FF))),
        scratch_shapes=[pltpu.VMEM((2 * D_FF // LANES, SUBLANES + tm, LANES), F32),
                        pltpu.VMEM((tm, D_FF), BF16)],
        compiler_params=_params(),
        name="ffn_prompt",
    )(x, norm_g, mod, mod, mod, w_up16, cw, w_down16, f0, gf)


def _ffn_sample_kernel(x_ref, g_ref, sh_ref, sc_ref, gate_ref, w_up_ref, cw_ref, w_down_ref, s1_ref, s2_ref, gf_ref,
                       o_ref, up_out_ref, *, final_norm, t):
    x = x_ref[...]
    h16 = _norm_mod(x, g_ref[...], _stream_rows(sh_ref[...], t), _stream_rows(sc_ref[...], t)).astype(BF16)

    def shifted(up, cols, s):
        return _shift_rows_streams(up, s1_ref[:, cols], s2_ref[:, cols], s, t)

    def store_up(up, cols):
        up_out_ref[:, cols] = up

    out = x + _stream_rows(gate_ref[...], t) * _ffn_tile(x, h16, w_up_ref, cw_ref, w_down_ref, shifted, store_up)
    if final_norm:
        out = _rmsnorm(out, gf_ref[...])
    o_ref[...] = out


def _ffn_sample(x, mod, l, norm_g, w_up16, cw, w_down16, s1, s2, gf, final_norm, t):
    rows = x.shape[0]
    n_streams = rows // t
    return pl.pallas_call(
        functools.partial(_ffn_sample_kernel, final_norm=final_norm, t=t),
        out_shape=(jax.ShapeDtypeStruct((rows, D_MODEL), F32), jax.ShapeDtypeStruct((rows, 2 * D_FF), F32)),
        grid=(1,),
        in_specs=[_whole((rows, D_MODEL)), _layer((1, D_MODEL), l)]
        + [_mod_streams(l, 3 + j, n_streams) for j in range(3)]
        + [_layer((D_MODEL, 2 * D_FF), l), _layer((3, 2 * D_FF), l), _layer((D_FF, D_MODEL), l),
           _layer((n_streams, 2 * D_FF), l), _layer((n_streams, 2 * D_FF), l), _whole((1, D_MODEL))],
        out_specs=(_whole_out((rows, D_MODEL)), _whole_out((rows, 2 * D_FF))),
        compiler_params=_params(),
        name="ffn_sample",
    )(x, norm_g, mod, mod, mod, w_up16, cw, w_down16, s1, s2, gf)


def _rotary_triplet(pos):
    half = DK_RET // 2
    inv = ROPE_BASE ** (-jnp.arange(half, dtype=F32) / half)
    ang = pos.astype(F32)[:, None] * inv[None, :]
    cos, sin = jnp.cos(ang), jnp.sin(ang)
    return jnp.stack([jnp.concatenate([cos] * 4, axis=-1), jnp.concatenate([sin] * 4, axis=-1),
                      jnp.concatenate([-sin, sin, -sin, sin], axis=-1)])


def _retention_tables(tb, n_streams=1):
    idx = np.arange(tb, dtype=np.float64)
    diff = idx[:, None] - idx[None, :]
    dec1 = np.where(diff[None] >= 0, np.exp(LOG_G[:, None, None] * np.maximum(diff, 0.0)[None]), 0.0)
    dec = np.zeros((H_RET, n_streams * tb, n_streams * tb))
    for b in range(n_streams):
        dec[:, b * tb:(b + 1) * tb, b * tb:(b + 1) * tb] = dec1
    xi = np.tile(np.repeat(np.exp(LOG_G[:, None] * (idx + 1)[None, :]).T, DK_RET, axis=1), (n_streams, 1))
    zeta_t = np.tile(np.repeat(np.exp(LOG_G[:, None] * (tb - 1 - idx)[None, :]), DK_RET, axis=0), (1, n_streams))
    gmat = np.zeros((HEAD_PAIRS, LANES, LANES))
    for h in range(H_RET):
        o = (h % 2) * DK_RET
        gmat[h // 2, o:o + DK_RET, o:o + DK_RET] = np.exp(LOG_G[h] * tb)
    gmat = np.tile(gmat, (1, 1, n_streams))
    return tuple(jnp.asarray(a, F32) for a in (dec, xi, zeta_t, gmat))


def _pair_state(s):
    lead = s.shape[:-3]
    s = s.reshape(lead + (HEAD_PAIRS, 2, DK_RET, DK_RET))
    z = jnp.zeros_like(s[..., 0, :, :])
    top = jnp.concatenate([s[..., 0, :, :], z], axis=-1)
    bot = jnp.concatenate([z, s[..., 1, :, :]], axis=-1)
    return jnp.concatenate([top, bot], axis=-2)


def _unpair_state(r):
    a = r[..., :DK_RET, :DK_RET]
    b = r[..., DK_RET:, DK_RET:]
    s = jnp.stack([a, b], axis=-3)
    return s.reshape(r.shape[:-3] + (H_RET, DK_RET, DK_RET))


def _band_bias(table):
    nq, nk = 2 * CHUNK, BAND2
    period = nq + nk
    j = np.arange(period)
    j = np.where(j < nk, j, j - period)
    idx = np.clip(HIST - j, -(CHUNK - 1), REL_CLIP) + (CHUNK - 1)
    one_period = table[:, idx].astype(F32)
    flat = jnp.tile(one_period, (1, nq + 1))[:, :nq * (period - 1)]
    return flat.reshape(-1, nq, period - 1)[:, :, :nk]


def _band_bias_t(raw):
    qq = np.arange(2 * CHUNK)[:, None]
    kk = np.arange(BAND2)[None, :]
    valid = np.where(qq < CHUNK, kk < BAND, kk >= CHUNK)
    b = jnp.where(valid, raw * LOG2E, NEG_INF).reshape(HEAD_PAIRS, 2, 2, CHUNK, BAND2)
    return jnp.transpose(b, (0, 4, 2, 1, 3)).reshape(HEAD_PAIRS, BAND2, 2 * LANES)


def _tail_rows(a, n):
    return a[..., a.shape[-2] - n:, :]


def kernel(x_prompt, x_sample, state_ret, state_sconv, state_pool, cache_k, cache_v, state_ffn, c_prompt, c_sample,
           norm_mix, norm_ffn, norm_final, w_ada, b_ada, w_in_even, w_out_even, ret_gn_gain, sconv_w, w_in_odd,
           w_out_odd, pool_w, pool_scale, rel_bias_table, ffn_w_up, ffn_conv, ffn_w_down):
    n_prompt, seq, _ = x_prompt.shape
    n_streams, t_s, _ = x_sample.shape
    assert n_prompt == 1 and n_streams % MOD_ROWS_PROMPT == 0
    rows_s = n_streams * t_s
    n_even, n_odd = (DEPTH + 1) // 2, DEPTH // 2

    c_all = jnp.concatenate([c_sample, c_prompt], axis=0)
    mod = _ada(jnp.pad(c_all, ((0, MOD_ROWS_PROMPT - 1), (0, 0))), w_ada, b_ada)

    bf = lambda w: w.astype(BF16)
    w_in_even16, w_out_even16 = bf(w_in_even), bf(w_out_even)
    w_in_odd16, w_out_odd16, pool_w16 = bf(w_in_odd), bf(w_out_odd), bf(pool_w)
    w_up16, w_down16 = bf(ffn_w_up), bf(ffn_w_down)
    norm_mix3, norm_ffn3 = norm_mix.reshape(DEPTH, 1, D_MODEL), norm_ffn.reshape(DEPTH, 1, D_MODEL)
    gain3, pool_scale3 = ret_gn_gain.reshape(n_even, 1, D_RET), pool_scale.reshape(n_odd, 1, D_POOL)
    norm_final2 = norm_final.reshape(1, D_MODEL)

    n_tiles = seq // ROW_TILE
    tabs_p = (_rotary_triplet(jnp.arange(n_tiles, dtype=jnp.int32) * ROW_TILE),
              _rotary_triplet(jnp.arange(ROW_TILE, dtype=jnp.int32))) + _retention_tables(RET_BLOCK)
    rot_s = _rotary_triplet(PAST_LEN + jnp.arange(t_s, dtype=jnp.int32))
    tabs_s = (jnp.tile(rot_s[0], (n_streams, 1)), jnp.tile(rot_s[2], (n_streams, 1))) \
        + _retention_tables(t_s, n_streams)

    cache_len = cache_k.shape[2]
    assert cache_len == HIST and t_s <= CHUNK
    bias_raw = [_band_bias(rel_bias_table[i]) for i in range(n_odd)]
    bias_p = [_band_bias_t(b) for b in bias_raw]
    bias_c = [b[:, :t_s, :cache_len].reshape(H_ATT * t_s, cache_len) for b in bias_raw]
    bias_n = [jnp.tile(b[:, :t_s, cache_len:cache_len + t_s], (1, 1, n_streams)).reshape(H_ATT * t_s, rows_s)
              for b in bias_raw]

    paired = _pair_state(state_ret)
    s_stack = jnp.transpose(paired, (0, 2, 1, 3, 4)).reshape(n_even, HEAD_PAIRS, n_streams * LANES, LANES)
    s_wide = jnp.transpose(paired, (0, 2, 3, 1, 4)).reshape(n_even, HEAD_PAIRS, LANES, n_streams * LANES)
    u1, u2 = state_sconv[:, :, 0, :], state_sconv[:, :, 1, :]
    f1, f2 = state_ffn[:, :, 0, :], state_ffn[:, :, 1, :]
    p0_s = jnp.pad(state_pool, ((0, 0), (0, 0), (POOL_BASE - POOL_BUF, 0), (0, 0)))
    kc16 = cache_k.reshape(n_odd, n_streams, cache_len, D_ATT).astype(BF16)
    vc16 = cache_v.reshape(n_odd, n_streams, cache_len, D_ATT).astype(BF16)

    xp = x_prompt.reshape(seq, D_MODEL)
    xs = x_sample.reshape(rows_s, D_MODEL)

    ret_p, ret_s, sconv_p, sconv_s, pool_p, pool_s = [], [], [], [], [], []
    k_p, k_s, v_p, v_s, ffn_p, ffn_s = [], [], [], [], [], []
    for l in range(DEPTH):
        i = l // 2
        if l % 2 == 0:
            xp, r_new, u_new = _even_prompt(
                xp, mod, l, n_streams, norm_mix3, w_in_even16, w_out_even16, gain3, sconv_w, tabs_p,
                jnp.zeros((HEAD_PAIRS, LANES, LANES), F32), jnp.zeros((SUBLANES, D_SCONV), F32))
            ret_p.append(_unpair_state(r_new)[None])
            sconv_p.append(_tail_rows(u_new, 2)[None])
            xs, s_new, u_all = _even_sample(
                xs, mod, l, norm_mix3, w_in_even16, w_out_even16, gain3, sconv_w, tabs_s, s_stack, s_wide, u1, u2, t_s)
            s_new = jnp.transpose(s_new.reshape(HEAD_PAIRS, LANES, n_streams, LANES), (2, 0, 1, 3))
            ret_s.append(_unpair_state(s_new))
            sconv_s.append(_tail_rows(u_all.reshape(n_streams, t_s, D_SCONV), 2))
        else:
            xp, p_new, k_new, v_new = _odd_prompt(
                xp, mod, l, n_streams, norm_mix3, w_in_odd16, w_out_odd16, pool_w16, pool_scale3, bias_p[i],
                jnp.zeros((POOL_BASE, D_POOL), F32))
            pool_p.append(_tail_rows(p_new, POOL_BUF)[None])
            k_p.append(k_new.reshape(1, HIST, H_ATT, DH_ATT))
            v_p.append(v_new.reshape(1, HIST, H_ATT, DH_ATT))
            xs, p_new, k_new, v_new = _odd_sample(
                xs, mod, l, norm_mix3, w_in_odd16, w_out_odd16, pool_w16, pool_scale3, bias_c[i], bias_n[i],
                p0_s, kc16, vc16, t_s, PAST_LEN)
            pool_s.append(_tail_rows(p_new, POOL_BUF))
            k_s.append(k_new.reshape(n_streams, t_s, H_ATT, DH_ATT))
            v_s.append(v_new.reshape(n_streams, t_s, H_ATT, DH_ATT))
        last = l == DEPTH - 1
        xp, f_new = _ffn_prompt(xp, mod, l, n_streams, norm_ffn3, w_up16, ffn_conv, w_down16,
                                jnp.zeros((SUBLANES, 2 * D_FF), F32), norm_final2, last)
        ffn_p.append(_tail_rows(f_new, 2)[None])
        xs, up_all = _ffn_sample(xs, mod, l, norm_ffn3, w_up16, ffn_conv, w_down16, f1, f2, norm_final2, last, t_s)
        ffn_s.append(_tail_rows(up_all.reshape(n_streams, t_s, 2 * D_FF), 2))

    st = jnp.stack
    return (xp.reshape(1, seq, D_MODEL), xs.reshape(n_streams, t_s, D_MODEL),
            st(ret_p), st(ret_s), st(sconv_p), st(sconv_s), st(pool_p), st(pool_s),
            st(k_p), st(k_s), st(v_p), st(v_s), st(ffn_p), st(ffn_s))
```

```python
import functools

import numpy as np
import jax
import jax.numpy as jnp
from jax import lax
from jax.experimental import pallas as pl
from jax.experimental.pallas import tpu as pltpu

F32 = jnp.float32
BF16 = jnp.bfloat16

D_MODEL = 1024
DEPTH = 4
PAST_LEN = 4096
CHUNK = 64
H_RET = 8
DK_RET = 64
D_RET = H_RET * DK_RET
ROPE_BASE = 10000.0
D_SCONV = D_MODEL - D_RET
POOL_WINDOWS = (2, 4, 8, 16)
D_POOL = D_MODEL // 2
POOL_GROUP = D_POOL // len(POOL_WINDOWS)
POOL_BUF = max(POOL_WINDOWS) - 1
H_ATT = 8
DH_ATT = 64
D_ATT = H_ATT * DH_ATT
N_PREV_CHUNKS = 8
REL_CLIP = 256
D_FF = 2816
EPS = 1e-6
NEG_INF = -1e30
D_IN_EVEN = 4 * D_RET + 3 * D_SCONV
D_IN_ODD = D_POOL + 3 * D_ATT

LANES = 128
SUBLANES = 8
HEAD_PAIRS = H_RET // 2
ROW_TILE = 512
RET_BLOCK = 256
FFN_COLS = 256
BAND = (N_PREV_CHUNKS + 1) * CHUNK
BAND2 = BAND + CHUNK
LOG2E = 1.4426950408889634
HIST = N_PREV_CHUNKS * CHUNK
POOL_BASE = 2 * SUBLANES
VMEM_LIMIT = 56 * 1024 * 1024

LOG_G = np.log1p(-(2.0 ** (-5.0 - np.arange(H_RET, dtype=np.float64))))


def _params(n_axes=1):
    return pltpu.CompilerParams(dimension_semantics=("arbitrary",) * n_axes, vmem_limit_bytes=VMEM_LIMIT)


def _whole(shape):
    nd = len(shape)
    return pl.BlockSpec(shape, lambda i: (0,) * nd, pipeline_mode=pl.Buffered(1))


def _whole_out(shape):
    nd = len(shape)
    return pl.BlockSpec(shape, lambda i: (0,) * nd)


def _layer(shape, l):
    nd = len(shape)
    return pl.BlockSpec((None,) + tuple(shape), lambda i: (l,) + (0,) * nd, pipeline_mode=pl.Buffered(1))


MOD_ROWS_PROMPT = SUBLANES


def _mod_prompt(l, j, n_streams):
    return pl.BlockSpec((None, MOD_ROWS_PROMPT, D_MODEL), lambda i: (l, n_streams // MOD_ROWS_PROMPT, j),
                        pipeline_mode=pl.Buffered(1))


def _mod_streams(l, j, n_streams):
    return pl.BlockSpec((None, n_streams, D_MODEL), lambda i: (l, 0, j), pipeline_mode=pl.Buffered(1))


def _stream_rows(m, t):
    return jnp.concatenate([jnp.broadcast_to(m[b:b + 1, :], (t, m.shape[1])) for b in range(m.shape[0])], axis=0)


def _rows(block_rows, cols):
    return pl.BlockSpec((block_rows, cols), lambda i: (i, 0))


def _lead(shape):
    nd = len(shape)
    return pl.BlockSpec((1,) + tuple(shape), lambda i: (i,) + (0,) * nd)


def _dot(a, b):
    return jnp.dot(a, b, preferred_element_type=F32)


def _dot_nt(a, b):
    return lax.dot_general(a, b, (((1,), (1,)), ((), ())), preferred_element_type=F32)


def _rmsnorm(x, g):
    return x * lax.rsqrt(jnp.mean(x * x, axis=-1, keepdims=True) + EPS) * g


def _norm_mod(x, g, shift, scale):
    return _rmsnorm(x, g) * (1.0 + scale) + shift


def _silu(x):
    return x * (1.0 / (1.0 + jnp.exp(-x)))


def _low_half(shape):
    return (lax.broadcasted_iota(jnp.int32, shape, len(shape) - 1) % LANES) < DK_RET


def _shift_rows(u, prev8, s):
    rolled = pltpu.roll(u, s, axis=0)
    prolled = pltpu.roll(prev8, s, axis=0)
    row = lax.broadcasted_iota(jnp.int32, prev8.shape, 0)
    first = jnp.where(row < s, prolled, rolled[0:SUBLANES])
    return jnp.concatenate([first, rolled[SUBLANES:]], axis=0)


def _shift_rows_streams(u, older, newer, s, t):
    row_in_stream = lax.broadcasted_iota(jnp.int32, u.shape, 0) % t
    rolled = pltpu.roll(u, s, axis=0)
    if s == 1:
        return jnp.where(row_in_stream == 0, _stream_rows(newer, t), rolled)
    return jnp.where(row_in_stream == 0, _stream_rows(older, t),
                     jnp.where(row_in_stream == 1, _stream_rows(newer, t), rolled))


def _ada_kernel(c_ref, w_ref, b_ref, o_ref):
    c = c_ref[...]
    o_ref[0] = _dot(_silu(c).astype(BF16), w_ref[0].astype(BF16)) + b_ref[0]


def _ada(c_all, w_ada, b_ada):
    rows = c_all.shape[0]
    tn = 1536
    return pl.pallas_call(
        _ada_kernel,
        out_shape=jax.ShapeDtypeStruct((DEPTH, rows, 6 * D_MODEL), F32),
        grid=(DEPTH, 6 * D_MODEL // tn),
        in_specs=[
            pl.BlockSpec((rows, D_MODEL), lambda l, j: (0, 0)),
            pl.BlockSpec((1, D_MODEL, tn), lambda l, j: (l, 0, j)),
            pl.BlockSpec((1, 1, tn), lambda l, j: (l, 0, j)),
        ],
        out_specs=pl.BlockSpec((1, rows, tn), lambda l, j: (l, 0, j)),
        compiler_params=_params(2),
        name="ada_mod",
    )(c_all, w_ada, b_ada.reshape(DEPTH, 1, 6 * D_MODEL))


def _rotary_pair(x, cos, sin_signed):
    lane = lax.broadcasted_iota(jnp.int32, x.shape, 1)
    first_half = (lane % DK_RET) < (DK_RET // 2)
    swapped = jnp.where(first_half, pltpu.roll(x, LANES - DK_RET // 2, axis=1),
                        pltpu.roll(x, DK_RET // 2, axis=1))
    return x * cos + swapped * sin_signed


def _retention_block(proj, cos, sin_signed, dec_ref, xi, zeta_t_ref, gain, cross_fn, update_fn):
    tb = proj.shape[0]
    low = _low_half((tb, LANES))
    inv_n = 1.0 / DK_RET
    pairs = range(HEAD_PAIRS)
    cols = [slice(p * LANES, (p + 1) * LANES) for p in pairs]
    q = [_rotary_pair(proj[:, cols[p]], cos, sin_signed) for p in pairs]
    k_t = [(_rotary_pair(proj[:, D_RET + p * LANES:D_RET + (p + 1) * LANES], cos, sin_signed)
            * (DK_RET ** -0.5)).T for p in pairs]
    v16 = [proj[:, 2 * D_RET + p * LANES:2 * D_RET + (p + 1) * LANES].astype(BF16) for p in pairs]
    k_t16 = [k_t[p].astype(BF16) for p in pairs]
    kz_t16 = [(k_t[p] * zeta_t_ref[cols[p], :]).astype(BF16) for p in pairs]
    q16 = [q[p].astype(BF16) for p in pairs]
    qe16 = [jnp.where(low, q[p], 0.0).astype(BF16) for p in pairs]
    qo16 = [jnp.where(low, 0.0, q[p]).astype(BF16) for p in pairs]
    s_e = [(_dot(qe16[p], k_t16[p]) * dec_ref[2 * p]).astype(BF16) for p in pairs]
    s_o = [(_dot(qo16[p], k_t16[p]) * dec_ref[2 * p + 1]).astype(BF16) for p in pairs]
    cross = [cross_fn(p, q16[p]) * xi[:, cols[p]] for p in pairs]
    o = [jnp.where(low, _dot(s_e[p], v16[p]), _dot(s_o[p], v16[p])) + cross[p] for p in pairs]
    for p in pairs:
        update_fn(p, kz_t16[p], v16[p])
    outs = []
    for p in pairs:
        s_lo = jnp.sum(jnp.where(low, o[p], 0.0), axis=-1, keepdims=True)
        s_hi = jnp.sum(jnp.where(low, 0.0, o[p]), axis=-1, keepdims=True)
        d = o[p] - jnp.where(low, s_lo, s_hi) * inv_n
        d2 = d * d
        v_lo = jnp.sum(jnp.where(low, d2, 0.0), axis=-1, keepdims=True)
        v_hi = jnp.sum(jnp.where(low, 0.0, d2), axis=-1, keepdims=True)
        on = d * lax.rsqrt(jnp.where(low, v_lo, v_hi) * inv_n + EPS)
        g = proj[:, 3 * D_RET + p * LANES:3 * D_RET + (p + 1) * LANES]
        outs.append(_silu(g) * (on * gain[:, cols[p]]))
    return jnp.concatenate(outs, axis=-1)


def _even_tail(x, proj, ret_out, conv_in_shift, cw_ref, gate, w_out_ref):
    gate_b = proj[:, 4 * D_RET:4 * D_RET + D_SCONV]
    u = proj[:, 4 * D_RET + D_SCONV:4 * D_RET + 2 * D_SCONV] * proj[:, 4 * D_RET + 2 * D_SCONV:]
    conv = cw_ref[0:1, :] * conv_in_shift(u, 2) + cw_ref[1:2, :] * conv_in_shift(u, 1) + cw_ref[2:3, :] * u
    mixed = jnp.concatenate([ret_out, gate_b * conv], axis=-1).astype(BF16)
    return x + gate * _dot(mixed, w_out_ref[...]), u


def _even_prompt_kernel(x_ref, g_ref, sh_ref, sc_ref, gate_ref, w_in_ref, w_out_ref, gain_ref, cw_ref,
                        rot_tile_ref, rot_row_ref, dec_ref, xi_ref, zt_ref, gmat_ref, r0_ref, u0_ref,
                        o_ref, r_out_ref, u_out_ref, r_scr, u_scr):
    i = pl.program_id(0)
    cos_0, sin_0, ssin_0 = (rot_tile_ref[j, pl.ds(i, 1), :] for j in range(3))
    cos = cos_0 * rot_row_ref[0] - sin_0 * rot_row_ref[1]
    sin_signed = ssin_0 * rot_row_ref[0] + cos_0 * rot_row_ref[2]

    @pl.when(i == 0)
    def _():
        r_scr[...] = r0_ref[...]
        u_scr[...] = u0_ref[...]

    x = x_ref[...]
    h = _norm_mod(x, g_ref[...], sh_ref[0:1, :], sc_ref[0:1, :]).astype(BF16)
    proj = _dot(h, w_in_ref[...])
    tm = x.shape[0]
    tb = dec_ref.shape[1]
    r_i = lax.broadcasted_iota(jnp.int32, (LANES, LANES), 0) < DK_RET
    c_i = lax.broadcasted_iota(jnp.int32, (LANES, LANES), 1) < DK_RET
    blockdiag = r_i == c_i

    def cross_fn(p, q16):
        return _dot(q16, r_scr[p].astype(BF16))

    def update_fn(p, kz_t16, v16):
        r_scr[p] = r_scr[p] * gmat_ref[p] + jnp.where(blockdiag, _dot(kz_t16, v16), 0.0)

    rets = []
    for r in range(tm // tb):
        rows = slice(r * tb, (r + 1) * tb)
        rets.append(_retention_block(proj[rows, :], cos[rows, :], sin_signed[rows, :], dec_ref, xi_ref[...],
                                     zt_ref, gain_ref[...], cross_fn, update_fn))
    ret_out = jnp.concatenate(rets, axis=0)

    prev8 = u_scr[...]
    out, u = _even_tail(x, proj, ret_out, lambda u, s: _shift_rows(u, prev8, s), cw_ref, gate_ref[0:1, :], w_out_ref)
    u_scr[...] = u[tm - SUBLANES:, :]
    o_ref[...] = out
    r_out_ref[...] = r_scr[...]
    u_out_ref[...] = u[tm - SUBLANES:, :]


def _even_prompt(x, mod, l, n_streams, norm_g, w_in16, w_out16, gain, cw, tabs, r0, u0):
    t = x.shape[0]
    tm, tb = ROW_TILE, RET_BLOCK
    rot_tile, rot_row, dec, xi, zt, gmat = tabs
    li = l // 2
    state = (HEAD_PAIRS, LANES, LANES)
    return pl.pallas_call(
        _even_prompt_kernel,
        out_shape=(jax.ShapeDtypeStruct((t, D_MODEL), F32),
                   jax.ShapeDtypeStruct(state, F32),
                   jax.ShapeDtypeStruct((SUBLANES, D_SCONV), F32)),
        grid=(t // tm,),
        in_specs=[_rows(tm, D_MODEL), _layer((1, D_MODEL), l)] + [_mod_prompt(l, j, n_streams) for j in range(3)]
        + [_layer((D_MODEL, D_IN_EVEN), li), _layer((D_MODEL, D_MODEL), li),
           _layer((1, D_RET), li), _layer((3, D_SCONV), li),
           _whole((3, t // tm, LANES)), _whole((3, tm, LANES)),
           _whole((H_RET, tb, tb)), _whole((tb, D_RET)), _whole((D_RET, tb)),
           _whole(state), _whole(state), _whole((SUBLANES, D_SCONV))],
        out_specs=(_rows(tm, D_MODEL), _whole_out(state), _whole_out((SUBLANES, D_SCONV))),
        scratch_shapes=[pltpu.VMEM(state, F32), pltpu.VMEM((SUBLANES, D_SCONV), F32)],
        compiler_params=_params(),
        name="even_prompt",
    )(x, norm_g, mod, mod, mod, w_in16, w_out16, gain, cw, rot_tile, rot_row, dec, xi, zt, gmat, r0, u0)


def _even_sample_kernel(x_ref, g_ref, sh_ref, sc_ref, gate_ref, w_in_ref, w_out_ref, gain_ref, cw_ref,
                        cos_ref, sin_ref, dec_ref, xi_ref, zt_ref, gwide_ref, s_stack_ref, s_wide_ref,
                        u1_ref, u2_ref, o_ref, s_out_ref, u_out_ref, *, t):
    x = x_ref[...]
    rows = x.shape[0]
    n_streams = rows // t
    wide = n_streams * LANES
    h = _norm_mod(x, g_ref[...], _stream_rows(sh_ref[...], t), _stream_rows(sc_ref[...], t)).astype(BF16)
    proj = _dot(h, w_in_ref[...])
    own = (lax.broadcasted_iota(jnp.int32, (rows, wide), 0) // t
           == lax.broadcasted_iota(jnp.int32, (rows, wide), 1) // LANES)
    r_i = lax.broadcasted_iota(jnp.int32, (LANES, wide), 0) < DK_RET
    blockdiag = r_i == _low_half((LANES, wide))

    def expand(a16):
        tiled = jnp.concatenate([a16.astype(F32)] * n_streams, axis=-1)
        return jnp.where(own, tiled, 0.0).astype(BF16)

    def cross_fn(p, q16):
        return _dot(expand(q16), s_stack_ref[p].astype(BF16))

    def update_fn(p, kz_t16, v16):
        kv = _dot(kz_t16, expand(v16))
        s_out_ref[p] = s_wide_ref[p] * gwide_ref[p] + jnp.where(blockdiag, kv, 0.0)

    ret_out = _retention_block(proj, cos_ref[...], sin_ref[...], dec_ref, xi_ref[...], zt_ref, gain_ref[...],
                               cross_fn, update_fn)
    out, u = _even_tail(x, proj, ret_out, lambda u, s: _shift_rows_streams(u, u1_ref[...], u2_ref[...], s, t),
                        cw_ref, _stream_rows(gate_ref[...], t), w_out_ref)
    o_ref[...] = out
    u_out_ref[...] = u


def _even_sample(x, mod, l, norm_g, w_in16, w_out16, gain, cw, tabs, s_stack, s_wide, u1, u2, t):
    rows = x.shape[0]
    n_streams = rows // t
    wide = n_streams * LANES
    cos, sin, dec, xi, zt, gwide = tabs
    li = l // 2
    return pl.pallas_call(
        functools.partial(_even_sample_kernel, t=t),
        out_shape=(jax.ShapeDtypeStruct((rows, D_MODEL), F32),
                   jax.ShapeDtypeStruct((HEAD_PAIRS, LANES, wide), F32),
                   jax.ShapeDtypeStruct((rows, D_SCONV), F32)),
        grid=(1,),
        in_specs=[_whole((rows, D_MODEL)), _layer((1, D_MODEL), l)] + [_mod_streams(l, j, n_streams) for j in range(3)]
        + [_layer((D_MODEL, D_IN_EVEN), li), _layer((D_MODEL, D_MODEL), li),
           _layer((1, D_RET), li), _layer((3, D_SCONV), li),
           _whole((rows, LANES)), _whole((rows, LANES)),
           _whole((H_RET, rows, rows)), _whole((rows, D_RET)), _whole((D_RET, rows)),
           _whole((HEAD_PAIRS, LANES, wide)), _layer((HEAD_PAIRS, wide, LANES), li),
           _layer((HEAD_PAIRS, LANES, wide), li),
           _layer((n_streams, D_SCONV), li), _layer((n_streams, D_SCONV), li)],
        out_specs=(_whole_out((rows, D_MODEL)), _whole_out((HEAD_PAIRS, LANES, wide)),
                   _whole_out((rows, D_SCONV))),
        compiler_params=_params(),
        name="even_sample",
    )(x, norm_g, mod, mod, mod, w_in16, w_out16, gain, cw, cos, sin, dec, xi, zt, gwide, s_stack, s_wide, u1, u2)


def _pool(hist_ref, p, pos, pool_w_ref, scale):
    t = p.shape[0]
    outs = []
    for gi, w in enumerate(POOL_WINDOWS):
        cols = slice(gi * POOL_GROUP, (gi + 1) * POOL_GROUP)
        win = p[:, cols]
        for d in range(1, w):
            win = win + hist_ref[POOL_BASE - d:POOL_BASE - d + t, cols]
        inv_cnt = 1.0 / jnp.minimum(pos + 1, w).astype(F32)
        pooled = win * inv_cnt - p[:, cols]
        outs.append(_dot(pooled.astype(BF16), pool_w_ref[gi]) * scale[:, cols])
    return jnp.concatenate(outs, axis=-1)


def _attend_scores(kbs, q_as, q_bs, biases):
    low = _low_half((CHUNK, LANES))
    out = []
    for kb, q_a, q_b, bias in zip(kbs, q_as, q_bs, biases):
        qs = jnp.concatenate([jnp.where(low, q_a, 0.0), jnp.where(low, 0.0, q_a),
                              jnp.where(low, q_b, 0.0), jnp.where(low, 0.0, q_b)], axis=0)
        qbd = qs.T.astype(BF16)
        half = (kb.shape[0] // 2) // (2 * SUBLANES) * (2 * SUBLANES)
        out.append(jnp.concatenate([_dot(kb[:half], qbd), _dot(kb[half:], qbd)], axis=0) + bias)
    return out


def _attend_values(scores, vts):
    low = _low_half((CHUNK, LANES))
    n = len(scores)
    e = [jnp.exp2(s - jnp.max(s, axis=0, keepdims=True)) for s in scores]
    inv_l = [1.0 / jnp.sum(e[j], axis=0, keepdims=True) for j in range(n)]
    e16 = [e[j].astype(BF16) for j in range(n)]
    o_t = [jnp.concatenate([_dot(vts[j][:DK_RET], e16[j]), _dot(vts[j][DK_RET:], e16[j])], axis=0) * inv_l[j]
           for j in range(n)]
    o_t = [o.T for o in o_t]
    return [(jnp.where(low, o[0:CHUNK], o[CHUNK:2 * CHUNK]), jnp.where(low, o[2 * CHUNK:3 * CHUNK], o[3 * CHUNK:]))
            for o in o_t]


def _odd_prompt_kernel(x_ref, g_ref, sh_ref, sc_ref, gate_ref, w_in_ref, w_out_ref, pw_ref, ps_ref, bias_ref, p0_ref,
                       o_ref, p_out_ref, k_out_ref, v_out_ref, pbuf, kbuf, vtbuf, q_scr, att_scr):
    i = pl.program_id(0)
    tm = x_ref.shape[0]

    @pl.when(i == 0)
    def _():
        pbuf[0:POOL_BASE, :] = p0_ref[...]
        kbuf[0:HIST, :] = jnp.zeros((HIST, D_ATT), BF16)
        vtbuf[:, :, 0:HIST] = jnp.zeros((HEAD_PAIRS, LANES, HIST), BF16)

    x = x_ref[...]
    h = _norm_mod(x, g_ref[...], sh_ref[0:1, :], sc_ref[0:1, :]).astype(BF16)
    proj = _dot(h, w_in_ref[...])
    p = proj[:, :D_POOL]
    q_scr[...] = proj[:, D_POOL:D_POOL + D_ATT] * (DH_ATT ** -0.5 * LOG2E)
    k = proj[:, D_POOL + D_ATT:D_POOL + 2 * D_ATT]
    v = proj[:, D_POOL + 2 * D_ATT:]
    pbuf[POOL_BASE:POOL_BASE + tm, :] = p
    kbuf[HIST:HIST + tm, :] = k.astype(BF16)
    for pr in range(HEAD_PAIRS):
        vtbuf[pr, :, HIST:HIST + tm] = v[:, pr * LANES:(pr + 1) * LANES].T.astype(BF16)
    k_out_ref[...] = k[tm - HIST:, :]
    v_out_ref[...] = v[tm - HIST:, :]

    pos = i * tm + lax.broadcasted_iota(jnp.int32, (tm, 1), 0)
    pool_out = _pool(pbuf, p, pos, pw_ref, ps_ref[...])

    def attend_tile(first_tile):
        pairs = range(HEAD_PAIRS)
        lanes = [slice(pr * LANES, (pr + 1) * LANES) for pr in pairs]
        n_blocks = tm // (2 * CHUNK)
        skip = [max(HIST - jb * 2 * CHUNK, 0) if first_tile else 0 for jb in range(n_blocks)]

        def scores(jb):
            r0 = jb * 2 * CHUNK
            return _attend_scores([kbuf[r0 + skip[jb]:r0 + BAND2, lanes[pr]] for pr in pairs],
                                  [q_scr[r0:r0 + CHUNK, lanes[pr]] for pr in pairs],
                                  [q_scr[r0 + CHUNK:r0 + 2 * CHUNK, lanes[pr]] for pr in pairs],
                                  [bias_ref[pr, skip[jb]:, :] for pr in pairs])

        s_next = scores(0)
        for jb in range(n_blocks):
            r0 = jb * 2 * CHUNK
            s_cur = s_next
            if jb + 1 < n_blocks:
                s_next = scores(jb + 1)
            outs = _attend_values(s_cur, [vtbuf[pr, :, r0 + skip[jb]:r0 + BAND2] for pr in pairs])
            att_scr[r0:r0 + CHUNK, :] = jnp.concatenate([o[0] for o in outs], axis=-1)
            att_scr[r0 + CHUNK:r0 + 2 * CHUNK, :] = jnp.concatenate([o[1] for o in outs], axis=-1)

    pl.when(i == 0)(functools.partial(attend_tile, True))
    pl.when(i > 0)(functools.partial(attend_tile, False))

    kbuf[0:HIST, :] = kbuf[tm:tm + HIST, :]
    vtbuf[:, :, 0:HIST] = vtbuf[:, :, tm:tm + HIST]
    tail = pbuf[tm:tm + POOL_BASE, :]
    pbuf[0:POOL_BASE, :] = tail
    p_out_ref[...] = tail

    mixed = jnp.concatenate([pool_out, att_scr[...]], axis=-1).astype(BF16)
    o_ref[...] = x + gate_ref[0:1, :] * _dot(mixed, w_out_ref[...])


def _odd_prompt(x, mod, l, n_streams, norm_g, w_in16, w_out16, pw16, ps, bias_t, p0):
    t = x.shape[0]
    tm = ROW_TILE
    assert tm == HIST and t % tm == 0
    li = l // 2
    return pl.pallas_call(
        _odd_prompt_kernel,
        out_shape=(jax.ShapeDtypeStruct((t, D_MODEL), F32),
                   jax.ShapeDtypeStruct((POOL_BASE, D_POOL), F32),
                   jax.ShapeDtypeStruct((HIST, D_ATT), F32),
                   jax.ShapeDtypeStruct((HIST, D_ATT), F32)),
        grid=(t // tm,),
        in_specs=[_rows(tm, D_MODEL), _layer((1, D_MODEL), l)] + [_mod_prompt(l, j, n_streams) for j in range(3)]
        + [_layer((D_MODEL, D_IN_ODD), li), _layer((D_MODEL, D_MODEL), li),
           _layer((len(POOL_WINDOWS), POOL_GROUP, POOL_GROUP), li), _layer((1, D_POOL), li),
           _whole((HEAD_PAIRS, BAND2, 2 * LANES)), _whole((POOL_BASE, D_POOL))],
        out_specs=(_rows(tm, D_MODEL), _whole_out((POOL_BASE, D_POOL)),
                   _whole_out((HIST, D_ATT)), _whole_out((HIST, D_ATT))),
        scratch_shapes=[pltpu.VMEM((POOL_BASE + tm, D_POOL), F32),
                        pltpu.VMEM((HIST + tm, D_ATT), BF16), pltpu.VMEM((HEAD_PAIRS, LANES, HIST + tm), BF16),
                        pltpu.VMEM((tm, D_ATT), F32), pltpu.VMEM((tm, D_ATT), F32)],
        compiler_params=_params(),
        name="odd_prompt",
    )(x, norm_g, mod, mod, mod, w_in16, w_out16, pw16, ps, bias_t, p0)


def _odd_sample_kernel(x_ref, g_ref, sh_ref, sc_ref, gate_ref, w_in_ref, w_out_ref, pw_ref, ps_ref,
                       bias_c_ref, bias_n_ref, p0_ref, kc_ref, vc_ref,
                       o_ref, p_out_ref, k_out_ref, v_out_ref, proj_scr, mix_scr, pbuf, *, pos0, t):
    step = pl.program_id(0)
    rows = x_ref.shape[0]
    per_step, cache = kc_ref.shape[0], kc_ref.shape[1]

    @pl.when(step == 0)
    def _():
        h = _norm_mod(x_ref[...], g_ref[...], _stream_rows(sh_ref[...], t), _stream_rows(sc_ref[...], t)).astype(BF16)
        proj = _dot(h, w_in_ref[...])
        proj_scr[...] = proj
        k_out_ref[...] = proj[:, D_POOL + D_ATT:D_POOL + 2 * D_ATT]
        v_out_ref[...] = proj[:, D_POOL + 2 * D_ATT:]

    kn = proj_scr[:, D_POOL + D_ATT:D_POOL + 2 * D_ATT].astype(BF16)
    vn = proj_scr[:, D_POOL + 2 * D_ATT:].astype(BF16)
    pos = pos0 + lax.broadcasted_iota(jnp.int32, (t, 1), 0)
    head_of_lane = lax.broadcasted_iota(jnp.int32, (t, D_ATT), 1) // DH_ATT
    stream_of_col = lax.broadcasted_iota(jnp.int32, (H_ATT * t, rows), 1) // t
    for j in range(per_step):
        b = step * per_step + j
        r0 = pl.multiple_of(b * t, t)
        proj = proj_scr[pl.ds(r0, t), :]
        p = proj[:, :D_POOL]
        pbuf[j, 0:POOL_BASE, :] = p0_ref[j]
        pbuf[j, POOL_BASE:POOL_BASE + t, :] = p
        p_out_ref[j] = p
        pool_out = _pool(pbuf.at[j], p, pos, pw_ref, ps_ref[...])

        q = proj[:, D_POOL:D_POOL + D_ATT] * (DH_ATT ** -0.5)
        q_heads = jnp.concatenate([jnp.where(head_of_lane == hh, q, 0.0) for hh in range(H_ATT)],
                                  axis=0).astype(BF16)
        kc = kc_ref[j].reshape(cache, D_ATT).astype(BF16)
        vc = vc_ref[j].reshape(cache, D_ATT).astype(BF16)
        s_c = _dot_nt(q_heads, kc) + bias_c_ref[...]
        s_n = jnp.where(stream_of_col == b, _dot_nt(q_heads, kn) + bias_n_ref[...], NEG_INF)
        m = jnp.maximum(jnp.max(s_c, axis=-1, keepdims=True), jnp.max(s_n, axis=-1, keepdims=True))
        e_c = jnp.exp(s_c - m)
        e_n = jnp.exp(s_n - m)
        inv_l = 1.0 / (jnp.sum(e_c, axis=-1, keepdims=True) + jnp.sum(e_n, axis=-1, keepdims=True))
        o_heads = (_dot(e_c.astype(BF16), vc) + _dot(e_n.astype(BF16), vn)) * inv_l
        att = jnp.where(head_of_lane == 0, o_heads[0:t], 0.0)
        for hh in range(1, H_ATT):
            att = jnp.where(head_of_lane == hh, o_heads[hh * t:(hh + 1) * t], att)
        mix_scr[pl.ds(r0, t), :] = jnp.concatenate([pool_out, att], axis=-1)

    @pl.when(step == pl.num_programs(0) - 1)
    def _():
        o_ref[...] = x_ref[...] + _stream_rows(gate_ref[...], t) * _dot(mix_scr[...].astype(BF16), w_out_ref[...])


SAMPLE_STREAMS_PER_STEP = 2


def _odd_sample(x, mod, l, norm_g, w_in16, w_out16, pw16, ps, bias_c, bias_n, p0, cache_k, cache_v, t, pos0):
    rows = x.shape[0]
    n_streams = rows // t
    cache = cache_k.shape[2]
    li = l // 2
    per_step = SAMPLE_STREAMS_PER_STEP
    assert n_streams % per_step == 0

    def streams(shape):
        nd = len(shape)
        return pl.BlockSpec((None, per_step) + tuple(shape), lambda i: (li, i) + (0,) * nd)

    return pl.pallas_call(
        functools.partial(_odd_sample_kernel, pos0=pos0, t=t),
        out_shape=(jax.ShapeDtypeStruct((rows, D_MODEL), F32),
                   jax.ShapeDtypeStruct((n_streams, t, D_POOL), F32),
                   jax.ShapeDtypeStruct((rows, D_ATT), F32),
                   jax.ShapeDtypeStruct((rows, D_ATT), F32)),
        grid=(n_streams // per_step,),
        in_specs=[_whole((rows, D_MODEL)), _layer((1, D_MODEL), l)] + [_mod_streams(l, j, n_streams) for j in range(3)]
        + [_layer((D_MODEL, D_IN_ODD), li), _layer((D_MODEL, D_MODEL), li),
           _layer((len(POOL_WINDOWS), POOL_GROUP, POOL_GROUP), li), _layer((1, D_POOL), li),
           _whole((H_ATT * t, cache)), _whole((H_ATT * t, rows)),
           streams((POOL_BASE, D_POOL)), streams((cache, H_ATT, DH_ATT)), streams((cache, H_ATT, DH_ATT))],
        out_specs=(_whole_out((rows, D_MODEL)), pl.BlockSpec((per_step, t, D_POOL), lambda i: (i, 0, 0)),
                   _whole_out((rows, D_ATT)), _whole_out((rows, D_ATT))),
        scratch_shapes=[pltpu.VMEM((rows, D_IN_ODD), F32), pltpu.VMEM((rows, D_MODEL), F32),
                        pltpu.VMEM((per_step, POOL_BASE + t, D_POOL), F32)],
        compiler_params=_params(),
        name="odd_sample",
    )(x, norm_g, mod, mod, mod, w_in16, w_out16, pw16, ps, bias_c, bias_n, p0, cache_k, cache_v)


def _ffn_tile(x, h16, w_up_ref, cw_ref, w_down_ref, shifted, store_up):
    acc = jnp.zeros((x.shape[0], D_MODEL), F32)
    for c in range(D_FF // FFN_COLS):
        halves = []
        for off in (0, D_FF):
            cols = slice(off + c * FFN_COLS, off + (c + 1) * FFN_COLS)
            up = _dot(h16, w_up_ref[:, cols])
            conv = (cw_ref[0:1, cols] * shifted(up, cols, 2) + cw_ref[1:2, cols] * shifted(up, cols, 1)
                    + cw_ref[2:3, cols] * up)
            store_up(up, cols)
            halves.append(conv)
        act = (_silu(halves[0]) * halves[1]).astype(BF16)
        acc = acc + _dot(act, w_down_ref[c * FFN_COLS:(c + 1) * FFN_COLS, :])
    return acc


def _ffn_prompt_kernel(x_ref, g_ref, sh_ref, sc_ref, gate_ref, w_up_ref, cw_ref, w_down_ref, f0_ref, gf_ref,
                       o_ref, f_out_ref, up_scr, act_scr, *, final_norm):
    i = pl.program_id(0)
    tm = x_ref.shape[0]
    n_chunks = D_FF // FFN_COLS
    slabs_per_chunk = FFN_COLS // LANES

    @pl.when(i == 0)
    def _():
        for j in range(2 * D_FF // LANES):
            up_scr[j, 0:SUBLANES, :] = f0_ref[:, j * LANES:(j + 1) * LANES]

    x = x_ref[...]
    h16 = _norm_mod(x, g_ref[...], sh_ref[0:1, :], sc_ref[0:1, :]).astype(BF16)

    def project(c):
        for off in (0, D_FF):
            c0 = off + c * FFN_COLS
            up = _dot(h16, w_up_ref[:, c0:c0 + FFN_COLS])
            for j in range(slabs_per_chunk):
                up_scr[c0 // LANES + j, SUBLANES:SUBLANES + tm, :] = up[:, j * LANES:(j + 1) * LANES]
            f_out_ref[:, c0:c0 + FFN_COLS] = up[tm - SUBLANES:, :]

    def conv_slab(j):
        cols = slice(j * LANES, (j + 1) * LANES)
        y = (cw_ref[0:1, cols] * up_scr[j, SUBLANES - 2:SUBLANES - 2 + tm, :]
             + cw_ref[1:2, cols] * up_scr[j, SUBLANES - 1:SUBLANES - 1 + tm, :]
             + cw_ref[2:3, cols] * up_scr[j, SUBLANES:SUBLANES + tm, :])
        up_scr[j, 0:SUBLANES, :] = up_scr[j, tm:tm + SUBLANES, :]
        return y

    def activate(c):
        for j in range(slabs_per_chunk):
            ja = c * slabs_per_chunk + j
            a = conv_slab(ja)
            b = conv_slab(D_FF // LANES + ja)
            act_scr[:, ja * LANES:(ja + 1) * LANES] = (_silu(a) * b).astype(BF16)

    project(0)
    for c in range(n_chunks):
        if c + 1 < n_chunks:
            project(c + 1)
        activate(c)

    out = x + gate_ref[0:1, :] * _dot(act_scr[...], w_down_ref[...])
    if final_norm:
        out = _rmsnorm(out, gf_ref[...])
    o_ref[...] = out


def _ffn_prompt(x, mod, l, n_streams, norm_g, w_up16, cw, w_down16, f0, gf, final_norm):
    t = x.shape[0]
    tm = ROW_TILE
    return pl.pallas_call(
        functools.partial(_ffn_prompt_kernel, final_norm=final_norm),
        out_shape=(jax.ShapeDtypeStruct((t, D_MODEL), F32), jax.ShapeDtypeStruct((SUBLANES, 2 * D_FF), F32)),
        grid=(t // tm,),
        in_specs=[_rows(tm, D_MODEL), _layer((1, D_MODEL), l)] + [_mod_prompt(l, 3 + j, n_streams) for j in range(3)]
        + [_layer((D_MODEL, 2 * D_FF), l), _layer((3, 2 * D_FF), l), _layer((D_FF, D_MODEL), l),
           _whole((SUBLANES, 2 * D_FF)), _whole((1, D_MODEL))],
        out_specs=(_rows(tm, D_MODEL), _whole_out((SUBLANES, 2 * D_FF))),
        scratch_shapes=[pltpu.VMEM((2 * D_FF // LANES, SUBLANES + tm, LANES), F32),
                        pltpu.VMEM((tm, D_FF), BF16)],
        compiler_params=_params(),
        name="ffn_prompt",
    )(x, norm_g, mod, mod, mod, w_up16, cw, w_down16, f0, gf)


def _ffn_sample_kernel(x_ref, g_ref, sh_ref, sc_ref, gate_ref, w_up_ref, cw_ref, w_down_ref, s1_ref, s2_ref, gf_ref,
                       o_ref, up_out_ref, *, final_norm, t):
    x = x_ref[...]
    h16 = _norm_mod(x, g_ref[...], _stream_rows(sh_ref[...], t), _stream_rows(sc_ref[...], t)).astype(BF16)

    def shifted(up, cols, s):
        return _shift_rows_streams(up, s1_ref[:, cols], s2_ref[:, cols], s, t)

    def store_up(up, cols):
        up_out_ref[:, cols] = up

    out = x + _stream_rows(gate_ref[...], t) * _ffn_tile(x, h16, w_up_ref, cw_ref, w_down_ref, shifted, store_up)
    if final_norm:
        out = _rmsnorm(out, gf_ref[...])
    o_ref[...] = out


def _ffn_sample(x, mod, l, norm_g, w_up16, cw, w_down16, s1, s2, gf, final_norm, t):
    rows = x.shape[0]
    n_streams = rows // t
    return pl.pallas_call(
        functools.partial(_ffn_sample_kernel, final_norm=final_norm, t=t),
        out_shape=(jax.ShapeDtypeStruct((rows, D_MODEL), F32), jax.ShapeDtypeStruct((rows, 2 * D_FF), F32)),
        grid=(1,),
        in_specs=[_whole((rows, D_MODEL)), _layer((1, D_MODEL), l)]
        + [_mod_streams(l, 3 + j, n_streams) for j in range(3)]
        + [_layer((D_MODEL, 2 * D_FF), l), _layer((3, 2 * D_FF), l), _layer((D_FF, D_MODEL), l),
           _layer((n_streams, 2 * D_FF), l), _layer((n_streams, 2 * D_FF), l), _whole((1, D_MODEL))],
        out_specs=(_whole_out((rows, D_MODEL)), _whole_out((rows, 2 * D_FF))),
        compiler_params=_params(),
        name="ffn_sample",
    )(x, norm_g, mod, mod, mod, w_up16, cw, w_down16, s1, s2, gf)


def _rotary_triplet(pos):
    half = DK_RET // 2
    inv = ROPE_BASE ** (-jnp.arange(half, dtype=F32) / half)
    ang = pos.astype(F32)[:, None] * inv[None, :]
    cos, sin = jnp.cos(ang), jnp.sin(ang)
    return jnp.stack([jnp.concatenate([cos] * 4, axis=-1), jnp.concatenate([sin] * 4, axis=-1),
                      jnp.concatenate([-sin, sin, -sin, sin], axis=-1)])


def _retention_tables(tb, n_streams=1):
    idx = np.arange(tb, dtype=np.float64)
    diff = idx[:, None] - idx[None, :]
    dec1 = np.where(diff[None] >= 0, np.exp(LOG_G[:, None, None] * np.maximum(diff, 0.0)[None]), 0.0)
    dec = np.zeros((H_RET, n_streams * tb, n_streams * tb))
    for b in range(n_streams):
        dec[:, b * tb:(b + 1) * tb, b * tb:(b + 1) * tb] = dec1
    xi = np.tile(np.repeat(np.exp(LOG_G[:, None] * (idx + 1)[None, :]).T, DK_RET, axis=1), (n_streams, 1))
    zeta_t = np.tile(np.repeat(np.exp(LOG_G[:, None] * (tb - 1 - idx)[None, :]), DK_RET, axis=0), (1, n_streams))
    gmat = np.zeros((HEAD_PAIRS, LANES, LANES))
    for h in range(H_RET):
        o = (h % 2) * DK_RET
        gmat[h // 2, o:o + DK_RET, o:o + DK_RET] = np.exp(LOG_G[h] * tb)
    gmat = np.tile(gmat, (1, 1, n_streams))
    return tuple(jnp.asarray(a, F32) for a in (dec, xi, zeta_t, gmat))


def _pair_state(s):
    lead = s.shape[:-3]
    s = s.reshape(lead + (HEAD_PAIRS, 2, DK_RET, DK_RET))
    z = jnp.zeros_like(s[..., 0, :, :])
    top = jnp.concatenate([s[..., 0, :, :], z], axis=-1)
    bot = jnp.concatenate([z, s[..., 1, :, :]], axis=-1)
    return jnp.concatenate([top, bot], axis=-2)


def _unpair_state(r):
    a = r[..., :DK_RET, :DK_RET]
    b = r[..., DK_RET:, DK_RET:]
    s = jnp.stack([a, b], axis=-3)
    return s.reshape(r.shape[:-3] + (H_RET, DK_RET, DK_RET))


def _band_bias(table):
    nq, nk = 2 * CHUNK, BAND2
    period = nq + nk
    j = np.arange(period)
    j = np.where(j < nk, j, j - period)
    idx = np.clip(HIST - j, -(CHUNK - 1), REL_CLIP) + (CHUNK - 1)
    one_period = table[:, idx].astype(F32)
    flat = jnp.tile(one_period, (1, nq + 1))[:, :nq * (period - 1)]
    return flat.reshape(-1, nq, period - 1)[:, :, :nk]


def _band_bias_t(raw):
    qq = np.arange(2 * CHUNK)[:, None]
    kk = np.arange(BAND2)[None, :]
    valid = np.where(qq < CHUNK, kk < BAND, kk >= CHUNK)
    b = jnp.where(valid, raw * LOG2E, NEG_INF).reshape(HEAD_PAIRS, 2, 2, CHUNK, BAND2)
    return jnp.transpose(b, (0, 4, 2, 1, 3)).reshape(HEAD_PAIRS, BAND2, 2 * LANES)


def _tail_rows(a, n):
    return a[..., a.shape[-2] - n:, :]


def kernel(x_prompt, x_sample, state_ret, state_sconv, state_pool, cache_k, cache_v, state_ffn, c_prompt, c_sample,
           norm_mix, norm_ffn, norm_final, w_ada, b_ada, w_in_even, w_out_even, ret_gn_gain, sconv_w, w_in_odd,
           w_out_odd, pool_w, pool_scale, rel_bias_table, ffn_w_up, ffn_conv, ffn_w_down):
    n_prompt, seq, _ = x_prompt.shape
    n_streams, t_s, _ = x_sample.shape
    assert n_prompt == 1 and n_streams % MOD_ROWS_PROMPT == 0
    rows_s = n_streams * t_s
    n_even, n_odd = (DEPTH + 1) // 2, DEPTH // 2

    c_all = jnp.concatenate([c_sample, c_prompt], axis=0)
    mod = _ada(jnp.pad(c_all, ((0, MOD_ROWS_PROMPT - 1), (0, 0))), w_ada, b_ada)

    bf = lambda w: w.astype(BF16)
    w_in_even16, w_out_even16 = bf(w_in_even), bf(w_out_even)
    w_in_odd16, w_out_odd16, pool_w16 = bf(w_in_odd), bf(w_out_odd), bf(pool_w)
    w_up16, w_down16 = bf(ffn_w_up), bf(ffn_w_down)
    norm_mix3, norm_ffn3 = norm_mix.reshape(DEPTH, 1, D_MODEL), norm_ffn.reshape(DEPTH, 1, D_MODEL)
    gain3, pool_scale3 = ret_gn_gain.reshape(n_even, 1, D_RET), pool_scale.reshape(n_odd, 1, D_POOL)
    norm_final2 = norm_final.reshape(1, D_MODEL)

    n_tiles = seq // ROW_TILE
    tabs_p = (_rotary_triplet(jnp.arange(n_tiles, dtype=jnp.int32) * ROW_TILE),
              _rotary_triplet(jnp.arange(ROW_TILE, dtype=jnp.int32))) + _retention_tables(RET_BLOCK)
    rot_s = _rotary_triplet(PAST_LEN + jnp.arange(t_s, dtype=jnp.int32))
    tabs_s = (jnp.tile(rot_s[0], (n_streams, 1)), jnp.tile(rot_s[2], (n_streams, 1))) \
        + _retention_tables(t_s, n_streams)

    cache_len = cache_k.shape[2]
    assert cache_len == HIST and t_s <= CHUNK
    bias_raw = [_band_bias(rel_bias_table[i]) for i in range(n_odd)]
    bias_p = [_band_bias_t(b) for b in bias_raw]
    bias_c = [b[:, :t_s, :cache_len].reshape(H_ATT * t_s, cache_len) for b in bias_raw]
    bias_n = [jnp.tile(b[:, :t_s, cache_len:cache_len + t_s], (1, 1, n_streams)).reshape(H_ATT * t_s, rows_s)
              for b in bias_raw]

    paired = _pair_state(state_ret)
    s_stack = jnp.transpose(paired, (0, 2, 1, 3, 4)).reshape(n_even, HEAD_PAIRS, n_streams * LANES, LANES)
    s_wide = jnp.transpose(paired, (0, 2, 3, 1, 4)).reshape(n_even, HEAD_PAIRS, LANES, n_streams * LANES)
    u1, u2 = state_sconv[:, :, 0, :], state_sconv[:, :, 1, :]
    f1, f2 = state_ffn[:, :, 0, :], state_ffn[:, :, 1, :]
    p0_s = jnp.pad(state_pool, ((0, 0), (0, 0), (POOL_BASE - POOL_BUF, 0), (0, 0)))

    xp = x_prompt.reshape(seq, D_MODEL)
    xs = x_sample.reshape(rows_s, D_MODEL)

    ret_p, ret_s, sconv_p, sconv_s, pool_p, pool_s = [], [], [], [], [], []
    k_p, k_s, v_p, v_s, ffn_p, ffn_s = [], [], [], [], [], []
    for l in range(DEPTH):
        i = l // 2
        if l % 2 == 0:
            xp, r_new, u_new = _even_prompt(
                xp, mod, l, n_streams, norm_mix3, w_in_even16, w_out_even16, gain3, sconv_w, tabs_p,
                jnp.zeros((HEAD_PAIRS, LANES, LANES), F32), jnp.zeros((SUBLANES, D_SCONV), F32))
            ret_p.append(_unpair_state(r_new)[None])
            sconv_p.append(_tail_rows(u_new, 2)[None])
            xs, s_new, u_all = _even_sample(
                xs, mod, l, norm_mix3, w_in_even16, w_out_even16, gain3, sconv_w, tabs_s, s_stack, s_wide, u1, u2, t_s)
            s_new = jnp.transpose(s_new.reshape(HEAD_PAIRS, LANES, n_streams, LANES), (2, 0, 1, 3))
            ret_s.append(_unpair_state(s_new))
            sconv_s.append(_tail_rows(u_all.reshape(n_streams, t_s, D_SCONV), 2))
        else:
            xp, p_new, k_new, v_new = _odd_prompt(
                xp, mod, l, n_streams, norm_mix3, w_in_odd16, w_out_odd16, pool_w16, pool_scale3, bias_p[i],
                jnp.zeros((POOL_BASE, D_POOL), F32))
            pool_p.append(_tail_rows(p_new, POOL_BUF)[None])
            k_p.append(k_new.reshape(1, HIST, H_ATT, DH_ATT))
            v_p.append(v_new.reshape(1, HIST, H_ATT, DH_ATT))
            xs, p_new, k_new, v_new = _odd_sample(
                xs, mod, l, norm_mix3, w_in_odd16, w_out_odd16, pool_w16, pool_scale3, bias_c[i], bias_n[i],
                p0_s, cache_k, cache_v, t_s, PAST_LEN)
            pool_s.append(_tail_rows(p_new, POOL_BUF))
            k_s.append(k_new.reshape(n_streams, t_s, H_ATT, DH_ATT))
            v_s.append(v_new.reshape(n_streams, t_s, H_ATT, DH_ATT))
        last = l == DEPTH - 1
        xp, f_new = _ffn_prompt(xp, mod, l, n_streams, norm_ffn3, w_up16, ffn_conv, w_down16,
                                jnp.zeros((SUBLANES, 2 * D_FF), F32), norm_final2, last)
        ffn_p.append(_tail_rows(f_new, 2)[None])
        xs, up_all = _ffn_sample(xs, mod, l, norm_ffn3, w_up16, ffn_conv, w_down16, f1, f2, norm_final2, last, t_s)
        ffn_s.append(_tail_rows(up_all.reshape(n_streams, t_s, 2 * D_FF), 2))

    st = jnp.stack
    return (xp.reshape(1, seq, D_MODEL), xs.reshape(n_streams, t_s, D_MODEL),
            st(ret_p), st(ret_s), st(sconv_p), st(sconv_s), st(pool_p), st(pool_s),
            st(k_p), st(k_s), st(v_p), st(v_s), st(ffn_p), st(ffn_s))
```

```python
import functools

import numpy as np
import jax
import jax.numpy as jnp
from jax import lax
from jax.experimental import pallas as pl
from jax.experimental.pallas import tpu as pltpu

F32 = jnp.float32
BF16 = jnp.bfloat16

D_MODEL = 1024
DEPTH = 4
PAST_LEN = 4096
CHUNK = 64
H_RET = 8
DK_RET = 64
D_RET = H_RET * DK_RET
ROPE_BASE = 10000.0
D_SCONV = D_MODEL - D_RET
POOL_WINDOWS = (2, 4, 8, 16)
D_POOL = D_MODEL // 2
POOL_GROUP = D_POOL // len(POOL_WINDOWS)
POOL_BUF = max(POOL_WINDOWS) - 1
H_ATT = 8
DH_ATT = 64
D_ATT = H_ATT * DH_ATT
N_PREV_CHUNKS = 8
REL_CLIP = 256
D_FF = 2816
EPS = 1e-6
NEG_INF = -1e30
D_IN_EVEN = 4 * D_RET + 3 * D_SCONV
D_IN_ODD = D_POOL + 3 * D_ATT

LANES = 128
SUBLANES = 8
HEAD_PAIRS = H_RET // 2
ROW_TILE = 512
RET_BLOCK = 128
FFN_COLS = 256
BAND = (N_PREV_CHUNKS + 1) * CHUNK
BAND2 = BAND + CHUNK
LOG2E = 1.4426950408889634
HIST = N_PREV_CHUNKS * CHUNK
POOL_BASE = 2 * SUBLANES
VMEM_LIMIT = 56 * 1024 * 1024

LOG_G = np.log1p(-(2.0 ** (-5.0 - np.arange(H_RET, dtype=np.float64))))


def _params(n_axes=1):
    return pltpu.CompilerParams(dimension_semantics=("arbitrary",) * n_axes, vmem_limit_bytes=VMEM_LIMIT)


def _whole(shape):
    nd = len(shape)
    return pl.BlockSpec(shape, lambda i: (0,) * nd, pipeline_mode=pl.Buffered(1))


def _whole_out(shape):
    nd = len(shape)
    return pl.BlockSpec(shape, lambda i: (0,) * nd)


def _layer(shape, l):
    nd = len(shape)
    return pl.BlockSpec((None,) + tuple(shape), lambda i: (l,) + (0,) * nd, pipeline_mode=pl.Buffered(1))


MOD_ROWS_PROMPT = SUBLANES


def _mod_prompt(l, j, n_streams):
    return pl.BlockSpec((None, MOD_ROWS_PROMPT, D_MODEL), lambda i: (l, n_streams // MOD_ROWS_PROMPT, j),
                        pipeline_mode=pl.Buffered(1))


def _mod_streams(l, j, n_streams):
    return pl.BlockSpec((None, n_streams, D_MODEL), lambda i: (l, 0, j), pipeline_mode=pl.Buffered(1))


def _stream_rows(m, t):
    return jnp.concatenate([jnp.broadcast_to(m[b:b + 1, :], (t, m.shape[1])) for b in range(m.shape[0])], axis=0)


def _rows(block_rows, cols):
    return pl.BlockSpec((block_rows, cols), lambda i: (i, 0))


def _lead(shape):
    nd = len(shape)
    return pl.BlockSpec((1,) + tuple(shape), lambda i: (i,) + (0,) * nd)


def _dot(a, b):
    return jnp.dot(a, b, preferred_element_type=F32)


def _dot_nt(a, b):
    return lax.dot_general(a, b, (((1,), (1,)), ((), ())), preferred_element_type=F32)


def _rmsnorm(x, g):
    return x * lax.rsqrt(jnp.mean(x * x, axis=-1, keepdims=True) + EPS) * g


def _norm_mod(x, g, shift, scale):
    return _rmsnorm(x, g) * (1.0 + scale) + shift


def _silu(x):
    return x * (1.0 / (1.0 + jnp.exp(-x)))


def _low_half(shape):
    return (lax.broadcasted_iota(jnp.int32, shape, len(shape) - 1) % LANES) < DK_RET


def _shift_rows(u, prev8, s):
    rolled = pltpu.roll(u, s, axis=0)
    prolled = pltpu.roll(prev8, s, axis=0)
    row = lax.broadcasted_iota(jnp.int32, prev8.shape, 0)
    first = jnp.where(row < s, prolled, rolled[0:SUBLANES])
    return jnp.concatenate([first, rolled[SUBLANES:]], axis=0)


def _shift_rows_streams(u, older, newer, s, t):
    row_in_stream = lax.broadcasted_iota(jnp.int32, u.shape, 0) % t
    rolled = pltpu.roll(u, s, axis=0)
    if s == 1:
        return jnp.where(row_in_stream == 0, _stream_rows(newer, t), rolled)
    return jnp.where(row_in_stream == 0, _stream_rows(older, t),
                     jnp.where(row_in_stream == 1, _stream_rows(newer, t), rolled))


def _ada_kernel(c_ref, w_ref, b_ref, o_ref):
    c = c_ref[...]
    o_ref[0] = _dot(_silu(c).astype(BF16), w_ref[0].astype(BF16)) + b_ref[0]


def _ada(c_all, w_ada, b_ada):
    rows = c_all.shape[0]
    tn = 1536
    return pl.pallas_call(
        _ada_kernel,
        out_shape=jax.ShapeDtypeStruct((DEPTH, rows, 6 * D_MODEL), F32),
        grid=(DEPTH, 6 * D_MODEL // tn),
        in_specs=[
            pl.BlockSpec((rows, D_MODEL), lambda l, j: (0, 0)),
            pl.BlockSpec((1, D_MODEL, tn), lambda l, j: (l, 0, j)),
            pl.BlockSpec((1, 1, tn), lambda l, j: (l, 0, j)),
        ],
        out_specs=pl.BlockSpec((1, rows, tn), lambda l, j: (l, 0, j)),
        compiler_params=_params(2),
        name="ada_mod",
    )(c_all, w_ada, b_ada.reshape(DEPTH, 1, 6 * D_MODEL))


def _rotary_pair(x, cos, sin_signed):
    lane = lax.broadcasted_iota(jnp.int32, x.shape, 1)
    first_half = (lane % DK_RET) < (DK_RET // 2)
    swapped = jnp.where(first_half, pltpu.roll(x, LANES - DK_RET // 2, axis=1),
                        pltpu.roll(x, DK_RET // 2, axis=1))
    return x * cos + swapped * sin_signed


def _retention_block(proj, cos, sin_signed, dec_ref, xi, zeta_t_ref, gain, cross_fn, update_fn):
    tb = proj.shape[0]
    low = _low_half((tb, LANES))
    inv_n = 1.0 / DK_RET
    pairs = range(HEAD_PAIRS)
    cols = [slice(p * LANES, (p + 1) * LANES) for p in pairs]
    q = [_rotary_pair(proj[:, cols[p]], cos, sin_signed) for p in pairs]
    k_t = [(_rotary_pair(proj[:, D_RET + p * LANES:D_RET + (p + 1) * LANES], cos, sin_signed)
            * (DK_RET ** -0.5)).T for p in pairs]
    v16 = [proj[:, 2 * D_RET + p * LANES:2 * D_RET + (p + 1) * LANES].astype(BF16) for p in pairs]
    k_t16 = [k_t[p].astype(BF16) for p in pairs]
    kz_t16 = [(k_t[p] * zeta_t_ref[cols[p], :]).astype(BF16) for p in pairs]
    q16 = [q[p].astype(BF16) for p in pairs]
    qe16 = [jnp.where(low, q[p], 0.0).astype(BF16) for p in pairs]
    qo16 = [jnp.where(low, 0.0, q[p]).astype(BF16) for p in pairs]
    s_e = [(_dot(qe16[p], k_t16[p]) * dec_ref[2 * p]).astype(BF16) for p in pairs]
    s_o = [(_dot(qo16[p], k_t16[p]) * dec_ref[2 * p + 1]).astype(BF16) for p in pairs]
    cross = [cross_fn(p, q16[p]) * xi[:, cols[p]] for p in pairs]
    o = [jnp.where(low, _dot(s_e[p], v16[p]), _dot(s_o[p], v16[p])) + cross[p] for p in pairs]
    for p in pairs:
        update_fn(p, kz_t16[p], v16[p])
    outs = []
    for p in pairs:
        s_lo = jnp.sum(jnp.where(low, o[p], 0.0), axis=-1, keepdims=True)
        s_hi = jnp.sum(jnp.where(low, 0.0, o[p]), axis=-1, keepdims=True)
        d = o[p] - jnp.where(low, s_lo, s_hi) * inv_n
        d2 = d * d
        v_lo = jnp.sum(jnp.where(low, d2, 0.0), axis=-1, keepdims=True)
        v_hi = jnp.sum(jnp.where(low, 0.0, d2), axis=-1, keepdims=True)
        on = d * lax.rsqrt(jnp.where(low, v_lo, v_hi) * inv_n + EPS)
        g = proj[:, 3 * D_RET + p * LANES:3 * D_RET + (p + 1) * LANES]
        outs.append(_silu(g) * (on * gain[:, cols[p]]))
    return jnp.concatenate(outs, axis=-1)


def _even_tail(x, proj, ret_out, conv_in_shift, cw_ref, gate, w_out_ref):
    gate_b = proj[:, 4 * D_RET:4 * D_RET + D_SCONV]
    u = proj[:, 4 * D_RET + D_SCONV:4 * D_RET + 2 * D_SCONV] * proj[:, 4 * D_RET + 2 * D_SCONV:]
    conv = cw_ref[0:1, :] * conv_in_shift(u, 2) + cw_ref[1:2, :] * conv_in_shift(u, 1) + cw_ref[2:3, :] * u
    mixed = jnp.concatenate([ret_out, gate_b * conv], axis=-1).astype(BF16)
    return x + gate * _dot(mixed, w_out_ref[...]), u


def _even_prompt_kernel(x_ref, g_ref, sh_ref, sc_ref, gate_ref, w_in_ref, w_out_ref, gain_ref, cw_ref,
                        rot_tile_ref, rot_row_ref, dec_ref, xi_ref, zt_ref, gmat_ref, r0_ref, u0_ref,
                        o_ref, r_out_ref, u_out_ref, r_scr, u_scr):
    i = pl.program_id(0)
    cos_0, sin_0, ssin_0 = (rot_tile_ref[j, pl.ds(i, 1), :] for j in range(3))
    cos = cos_0 * rot_row_ref[0] - sin_0 * rot_row_ref[1]
    sin_signed = ssin_0 * rot_row_ref[0] + cos_0 * rot_row_ref[2]

    @pl.when(i == 0)
    def _():
        r_scr[...] = r0_ref[...]
        u_scr[...] = u0_ref[...]

    x = x_ref[...]
    h = _norm_mod(x, g_ref[...], sh_ref[0:1, :], sc_ref[0:1, :]).astype(BF16)
    proj = _dot(h, w_in_ref[...])
    tm = x.shape[0]
    tb = dec_ref.shape[1]
    r_i = lax.broadcasted_iota(jnp.int32, (LANES, LANES), 0) < DK_RET
    c_i = lax.broadcasted_iota(jnp.int32, (LANES, LANES), 1) < DK_RET
    blockdiag = r_i == c_i

    def cross_fn(p, q16):
        return _dot(q16, r_scr[p].astype(BF16))

    def update_fn(p, kz_t16, v16):
        r_scr[p] = r_scr[p] * gmat_ref[p] + jnp.where(blockdiag, _dot(kz_t16, v16), 0.0)

    rets = []
    for r in range(tm // tb):
        rows = slice(r * tb, (r + 1) * tb)
        rets.append(_retention_block(proj[rows, :], cos[rows, :], sin_signed[rows, :], dec_ref, xi_ref[...],
                                     zt_ref, gain_ref[...], cross_fn, update_fn))
    ret_out = jnp.concatenate(rets, axis=0)

    prev8 = u_scr[...]
    out, u = _even_tail(x, proj, ret_out, lambda u, s: _shift_rows(u, prev8, s), cw_ref, gate_ref[0:1, :], w_out_ref)
    u_scr[...] = u[tm - SUBLANES:, :]
    o_ref[...] = out
    r_out_ref[...] = r_scr[...]
    u_out_ref[...] = u[tm - SUBLANES:, :]


def _even_prompt(x, mod, l, n_streams, norm_g, w_in16, w_out16, gain, cw, tabs, r0, u0):
    t = x.shape[0]
    tm, tb = ROW_TILE, RET_BLOCK
    rot_tile, rot_row, dec, xi, zt, gmat = tabs
    li = l // 2
    state = (HEAD_PAIRS, LANES, LANES)
    return pl.pallas_call(
        _even_prompt_kernel,
        out_shape=(jax.ShapeDtypeStruct((t, D_MODEL), F32),
                   jax.ShapeDtypeStruct(state, F32),
                   jax.ShapeDtypeStruct((SUBLANES, D_SCONV), F32)),
        grid=(t // tm,),
        in_specs=[_rows(tm, D_MODEL), _layer((1, D_MODEL), l)] + [_mod_prompt(l, j, n_streams) for j in range(3)]
        + [_layer((D_MODEL, D_IN_EVEN), li), _layer((D_MODEL, D_MODEL), li),
           _layer((1, D_RET), li), _layer((3, D_SCONV), li),
           _whole((3, t // tm, LANES)), _whole((3, tm, LANES)),
           _whole((H_RET, tb, tb)), _whole((tb, D_RET)), _whole((D_RET, tb)),
           _whole(state), _whole(state), _whole((SUBLANES, D_SCONV))],
        out_specs=(_rows(tm, D_MODEL), _whole_out(state), _whole_out((SUBLANES, D_SCONV))),
        scratch_shapes=[pltpu.VMEM(state, F32), pltpu.VMEM((SUBLANES, D_SCONV), F32)],
        compiler_params=_params(),
        name="even_prompt",
    )(x, norm_g, mod, mod, mod, w_in16, w_out16, gain, cw, rot_tile, rot_row, dec, xi, zt, gmat, r0, u0)


def _even_sample_kernel(x_ref, g_ref, sh_ref, sc_ref, gate_ref, w_in_ref, w_out_ref, gain_ref, cw_ref,
                        cos_ref, sin_ref, dec_ref, xi_ref, zt_ref, gwide_ref, s_stack_ref, s_wide_ref,
                        u1_ref, u2_ref, o_ref, s_out_ref, u_out_ref, *, t):
    x = x_ref[...]
    rows = x.shape[0]
    n_streams = rows // t
    wide = n_streams * LANES
    h = _norm_mod(x, g_ref[...], _stream_rows(sh_ref[...], t), _stream_rows(sc_ref[...], t)).astype(BF16)
    proj = _dot(h, w_in_ref[...])
    own = (lax.broadcasted_iota(jnp.int32, (rows, wide), 0) // t
           == lax.broadcasted_iota(jnp.int32, (rows, wide), 1) // LANES)
    r_i = lax.broadcasted_iota(jnp.int32, (LANES, wide), 0) < DK_RET
    blockdiag = r_i == _low_half((LANES, wide))

    def expand(a16):
        tiled = jnp.concatenate([a16.astype(F32)] * n_streams, axis=-1)
        return jnp.where(own, tiled, 0.0).astype(BF16)

    def cross_fn(p, q16):
        return _dot(expand(q16), s_stack_ref[p].astype(BF16))

    def update_fn(p, kz_t16, v16):
        kv = _dot(kz_t16, expand(v16))
        s_out_ref[p] = s_wide_ref[p] * gwide_ref[p] + jnp.where(blockdiag, kv, 0.0)

    ret_out = _retention_block(proj, cos_ref[...], sin_ref[...], dec_ref, xi_ref[...], zt_ref, gain_ref[...],
                               cross_fn, update_fn)
    out, u = _even_tail(x, proj, ret_out, lambda u, s: _shift_rows_streams(u, u1_ref[...], u2_ref[...], s, t),
                        cw_ref, _stream_rows(gate_ref[...], t), w_out_ref)
    o_ref[...] = out
    u_out_ref[...] = u


def _even_sample(x, mod, l, norm_g, w_in16, w_out16, gain, cw, tabs, s_stack, s_wide, u1, u2, t):
    rows = x.shape[0]
    n_streams = rows // t
    wide = n_streams * LANES
    cos, sin, dec, xi, zt, gwide = tabs
    li = l // 2
    return pl.pallas_call(
        functools.partial(_even_sample_kernel, t=t),
        out_shape=(jax.ShapeDtypeStruct((rows, D_MODEL), F32),
                   jax.ShapeDtypeStruct((HEAD_PAIRS, LANES, wide), F32),
                   jax.ShapeDtypeStruct((rows, D_SCONV), F32)),
        grid=(1,),
        in_specs=[_whole((rows, D_MODEL)), _layer((1, D_MODEL), l)] + [_mod_streams(l, j, n_streams) for j in range(3)]
        + [_layer((D_MODEL, D_IN_EVEN), li), _layer((D_MODEL, D_MODEL), li),
           _layer((1, D_RET), li), _layer((3, D_SCONV), li),
           _whole((rows, LANES)), _whole((rows, LANES)),
           _whole((H_RET, rows, rows)), _whole((rows, D_RET)), _whole((D_RET, rows)),
           _whole((HEAD_PAIRS, LANES, wide)), _layer((HEAD_PAIRS, wide, LANES), li),
           _layer((HEAD_PAIRS, LANES, wide), li),
           _layer((n_streams, D_SCONV), li), _layer((n_streams, D_SCONV), li)],
        out_specs=(_whole_out((rows, D_MODEL)), _whole_out((HEAD_PAIRS, LANES, wide)),
                   _whole_out((rows, D_SCONV))),
        compiler_params=_params(),
        name="even_sample",
    )(x, norm_g, mod, mod, mod, w_in16, w_out16, gain, cw, cos, sin, dec, xi, zt, gwide, s_stack, s_wide, u1, u2)


def _pool_put(hist_ref, row0, block):
    for gi in range(len(POOL_WINDOWS)):
        hist_ref[gi, row0:row0 + block.shape[0], :] = block[:, gi * POOL_GROUP:(gi + 1) * POOL_GROUP]


def _pool_get(hist_ref, row0, rows):
    return jnp.concatenate([hist_ref[gi, row0:row0 + rows, :] for gi in range(len(POOL_WINDOWS))], axis=-1)


def _pool(hist_ref, p, pos, pool_w_ref, scale):
    t = p.shape[0]
    outs = []
    for gi, w in enumerate(POOL_WINDOWS):
        cols = slice(gi * POOL_GROUP, (gi + 1) * POOL_GROUP)
        win = p[:, cols]
        for d in range(1, w):
            win = win + hist_ref[gi, POOL_BASE - d:POOL_BASE - d + t, :]
        inv_cnt = 1.0 / jnp.minimum(pos + 1, w).astype(F32)
        pooled = win * inv_cnt - p[:, cols]
        outs.append(_dot(pooled.astype(BF16), pool_w_ref[gi]) * scale[:, cols])
    return jnp.concatenate(outs, axis=-1)


def _attend_scores(kbs, q_as, q_bs, biases):
    low = _low_half((CHUNK, LANES))
    out = []
    for kb, q_a, q_b, bias in zip(kbs, q_as, q_bs, biases):
        qs = jnp.concatenate([jnp.where(low, q_a, 0.0), jnp.where(low, 0.0, q_a),
                              jnp.where(low, q_b, 0.0), jnp.where(low, 0.0, q_b)], axis=0)
        qbd = qs.T.astype(BF16)
        half = (kb.shape[0] // 2) // (2 * SUBLANES) * (2 * SUBLANES)
        out.append(jnp.concatenate([_dot(kb[:half], qbd), _dot(kb[half:], qbd)], axis=0) + bias)
    return out


def _attend_values(scores, vts):
    low = _low_half((CHUNK, LANES))
    n = len(scores)
    e = [jnp.exp2(s - jnp.max(s, axis=0, keepdims=True)) for s in scores]
    inv_l = [1.0 / jnp.sum(e[j], axis=0, keepdims=True) for j in range(n)]
    e16 = [e[j].astype(BF16) for j in range(n)]
    o_t = [jnp.concatenate([_dot(vts[j][:DK_RET], e16[j]), _dot(vts[j][DK_RET:], e16[j])], axis=0) * inv_l[j]
           for j in range(n)]
    o_t = [o.T for o in o_t]
    return [(jnp.where(low, o[0:CHUNK], o[CHUNK:2 * CHUNK]), jnp.where(low, o[2 * CHUNK:3 * CHUNK], o[3 * CHUNK:]))
            for o in o_t]


def _odd_prompt_kernel(x_ref, g_ref, sh_ref, sc_ref, gate_ref, w_in_ref, w_out_ref, pw_ref, ps_ref, bias_ref, p0_ref,
                       o_ref, p_out_ref, k_out_ref, v_out_ref, pbuf, kbuf, vtbuf, q_scr, att_scr):
    i = pl.program_id(0)
    tm = x_ref.shape[0]

    @pl.when(i == 0)
    def _():
        _pool_put(pbuf, 0, p0_ref[...])
        kbuf[0:HIST, :] = jnp.zeros((HIST, D_ATT), BF16)
        vtbuf[:, :, 0:HIST] = jnp.zeros((HEAD_PAIRS, LANES, HIST), BF16)

    x = x_ref[...]
    h = _norm_mod(x, g_ref[...], sh_ref[0:1, :], sc_ref[0:1, :]).astype(BF16)
    proj = _dot(h, w_in_ref[...])
    p = proj[:, :D_POOL]
    q_scr[...] = proj[:, D_POOL:D_POOL + D_ATT] * (DH_ATT ** -0.5 * LOG2E)
    k = proj[:, D_POOL + D_ATT:D_POOL + 2 * D_ATT]
    v = proj[:, D_POOL + 2 * D_ATT:]
    _pool_put(pbuf, POOL_BASE, p)
    kbuf[HIST:HIST + tm, :] = k.astype(BF16)
    for pr in range(HEAD_PAIRS):
        vtbuf[pr, :, HIST:HIST + tm] = v[:, pr * LANES:(pr + 1) * LANES].T.astype(BF16)
    k_out_ref[...] = k[tm - HIST:, :]
    v_out_ref[...] = v[tm - HIST:, :]

    pos = i * tm + lax.broadcasted_iota(jnp.int32, (tm, 1), 0)
    pool_out = _pool(pbuf, p, pos, pw_ref, ps_ref[...])

    def attend_tile(first_tile):
        pairs = range(HEAD_PAIRS)
        lanes = [slice(pr * LANES, (pr + 1) * LANES) for pr in pairs]
        n_blocks = tm // (2 * CHUNK)
        skip = [max(HIST - jb * 2 * CHUNK, 0) if first_tile else 0 for jb in range(n_blocks)]

        def scores(jb):
            r0 = jb * 2 * CHUNK
            return _attend_scores([kbuf[r0 + skip[jb]:r0 + BAND2, lanes[pr]] for pr in pairs],
                                  [q_scr[r0:r0 + CHUNK, lanes[pr]] for pr in pairs],
                                  [q_scr[r0 + CHUNK:r0 + 2 * CHUNK, lanes[pr]] for pr in pairs],
                                  [bias_ref[pr, skip[jb]:, :] for pr in pairs])

        s_next = scores(0)
        for jb in range(n_blocks):
            r0 = jb * 2 * CHUNK
            s_cur = s_next
            if jb + 1 < n_blocks:
                s_next = scores(jb + 1)
            outs = _attend_values(s_cur, [vtbuf[pr, :, r0 + skip[jb]:r0 + BAND2] for pr in pairs])
            att_scr[r0:r0 + CHUNK, :] = jnp.concatenate([o[0] for o in outs], axis=-1)
            att_scr[r0 + CHUNK:r0 + 2 * CHUNK, :] = jnp.concatenate([o[1] for o in outs], axis=-1)

    pl.when(i == 0)(functools.partial(attend_tile, True))
    pl.when(i > 0)(functools.partial(attend_tile, False))

    kbuf[0:HIST, :] = kbuf[tm:tm + HIST, :]
    vtbuf[:, :, 0:HIST] = vtbuf[:, :, tm:tm + HIST]
    tail = _pool_get(pbuf, tm, POOL_BASE)
    _pool_put(pbuf, 0, tail)
    p_out_ref[...] = tail

    mixed = jnp.concatenate([pool_out, att_scr[...]], axis=-1).astype(BF16)
    o_ref[...] = x + gate_ref[0:1, :] * _dot(mixed, w_out_ref[...])


def _odd_prompt(x, mod, l, n_streams, norm_g, w_in16, w_out16, pw16, ps, bias_t, p0):
    t = x.shape[0]
    tm = ROW_TILE
    assert tm == HIST and t % tm == 0
    li = l // 2
    return pl.pallas_call(
        _odd_prompt_kernel,
        out_shape=(jax.ShapeDtypeStruct((t, D_MODEL), F32),
                   jax.ShapeDtypeStruct((POOL_BASE, D_POOL), F32),
                   jax.ShapeDtypeStruct((HIST, D_ATT), F32),
                   jax.ShapeDtypeStruct((HIST, D_ATT), F32)),
        grid=(t // tm,),
        in_specs=[_rows(tm, D_MODEL), _layer((1, D_MODEL), l)] + [_mod_prompt(l, j, n_streams) for j in range(3)]
        + [_layer((D_MODEL, D_IN_ODD), li), _layer((D_MODEL, D_MODEL), li),
           _layer((len(POOL_WINDOWS), POOL_GROUP, POOL_GROUP), li), _layer((1, D_POOL), li),
           _whole((HEAD_PAIRS, BAND2, 2 * LANES)), _whole((POOL_BASE, D_POOL))],
        out_specs=(_rows(tm, D_MODEL), _whole_out((POOL_BASE, D_POOL)),
                   _whole_out((HIST, D_ATT)), _whole_out((HIST, D_ATT))),
        scratch_shapes=[pltpu.VMEM((len(POOL_WINDOWS), POOL_BASE + tm, POOL_GROUP), F32),
                        pltpu.VMEM((HIST + tm, D_ATT), BF16), pltpu.VMEM((HEAD_PAIRS, LANES, HIST + tm), BF16),
                        pltpu.VMEM((tm, D_ATT), F32), pltpu.VMEM((tm, D_ATT), F32)],
        compiler_params=_params(),
        name="odd_prompt",
    )(x, norm_g, mod, mod, mod, w_in16, w_out16, pw16, ps, bias_t, p0)


def _odd_sample_kernel(x_ref, g_ref, sh_ref, sc_ref, gate_ref, w_in_ref, w_out_ref, pw_ref, ps_ref,
                       bias_c_ref, bias_n_ref, p0_ref, kc_ref, vc_ref,
                       o_ref, p_out_ref, k_out_ref, v_out_ref, proj_scr, mix_scr, pbuf, *, pos0, t):
    step = pl.program_id(0)
    rows = x_ref.shape[0]
    per_step = kc_ref.shape[0]

    @pl.when(step == 0)
    def _():
        h = _norm_mod(x_ref[...], g_ref[...], _stream_rows(sh_ref[...], t), _stream_rows(sc_ref[...], t)).astype(BF16)
        proj = _dot(h, w_in_ref[...])
        proj_scr[...] = proj
        k_out_ref[...] = proj[:, D_POOL + D_ATT:D_POOL + 2 * D_ATT]
        v_out_ref[...] = proj[:, D_POOL + 2 * D_ATT:]

    kn = proj_scr[:, D_POOL + D_ATT:D_POOL + 2 * D_ATT].astype(BF16)
    vn = proj_scr[:, D_POOL + 2 * D_ATT:].astype(BF16)
    pos = pos0 + lax.broadcasted_iota(jnp.int32, (t, 1), 0)
    head_of_lane = lax.broadcasted_iota(jnp.int32, (t, D_ATT), 1) // DH_ATT
    stream_of_col = lax.broadcasted_iota(jnp.int32, (H_ATT * t, rows), 1) // t
    for j in range(per_step):
        b = step * per_step + j
        r0 = pl.multiple_of(b * t, t)
        proj = proj_scr[pl.ds(r0, t), :]
        p = proj[:, :D_POOL]
        _pool_put(pbuf.at[j], 0, p0_ref[j])
        _pool_put(pbuf.at[j], POOL_BASE, p)
        p_out_ref[j] = p
        pool_out = _pool(pbuf.at[j], p, pos, pw_ref, ps_ref[...])

        q = proj[:, D_POOL:D_POOL + D_ATT] * (DH_ATT ** -0.5)
        q_heads = jnp.concatenate([jnp.where(head_of_lane == hh, q, 0.0) for hh in range(H_ATT)],
                                  axis=0).astype(BF16)
        s_c = _dot_nt(q_heads, kc_ref[j]) + bias_c_ref[...]
        s_n = jnp.where(stream_of_col == b, _dot_nt(q_heads, kn) + bias_n_ref[...], NEG_INF)
        m = jnp.maximum(jnp.max(s_c, axis=-1, keepdims=True), jnp.max(s_n, axis=-1, keepdims=True))
        e_c = jnp.exp(s_c - m)
        e_n = jnp.exp(s_n - m)
        inv_l = 1.0 / (jnp.sum(e_c, axis=-1, keepdims=True) + jnp.sum(e_n, axis=-1, keepdims=True))
        o_heads = (_dot(e_c.astype(BF16), vc_ref[j]) + _dot(e_n.astype(BF16), vn)) * inv_l
        att = jnp.where(head_of_lane == 0, o_heads[0:t], 0.0)
        for hh in range(1, H_ATT):
            att = jnp.where(head_of_lane == hh, o_heads[hh * t:(hh + 1) * t], att)
        mix_scr[pl.ds(r0, t), :] = jnp.concatenate([pool_out, att], axis=-1)

    @pl.when(step == pl.num_programs(0) - 1)
    def _():
        o_ref[...] = x_ref[...] + _stream_rows(gate_ref[...], t) * _dot(mix_scr[...].astype(BF16), w_out_ref[...])


SAMPLE_STREAMS_PER_STEP = 4


def _odd_sample(x, mod, l, norm_g, w_in16, w_out16, pw16, ps, bias_c, bias_n, p0, kc16, vc16, t, pos0):
    rows = x.shape[0]
    n_streams = rows // t
    cache = kc16.shape[2]
    li = l // 2
    per_step = SAMPLE_STREAMS_PER_STEP
    assert n_streams % per_step == 0

    def streams(shape):
        nd = len(shape)
        return pl.BlockSpec((None, per_step) + tuple(shape), lambda i: (li, i) + (0,) * nd)

    return pl.pallas_call(
        functools.partial(_odd_sample_kernel, pos0=pos0, t=t),
        out_shape=(jax.ShapeDtypeStruct((rows, D_MODEL), F32),
                   jax.ShapeDtypeStruct((n_streams, t, D_POOL), F32),
                   jax.ShapeDtypeStruct((rows, D_ATT), F32),
                   jax.ShapeDtypeStruct((rows, D_ATT), F32)),
        grid=(n_streams // per_step,),
        in_specs=[_whole((rows, D_MODEL)), _layer((1, D_MODEL), l)] + [_mod_streams(l, j, n_streams) for j in range(3)]
        + [_layer((D_MODEL, D_IN_ODD), li), _layer((D_MODEL, D_MODEL), li),
           _layer((len(POOL_WINDOWS), POOL_GROUP, POOL_GROUP), li), _layer((1, D_POOL), li),
           _whole((H_ATT * t, cache)), _whole((H_ATT * t, rows)),
           streams((POOL_BASE, D_POOL)), streams((cache, D_ATT)), streams((cache, D_ATT))],
        out_specs=(_whole_out((rows, D_MODEL)), pl.BlockSpec((per_step, t, D_POOL), lambda i: (i, 0, 0)),
                   _whole_out((rows, D_ATT)), _whole_out((rows, D_ATT))),
        scratch_shapes=[pltpu.VMEM((rows, D_IN_ODD), F32), pltpu.VMEM((rows, D_MODEL), F32),
                        pltpu.VMEM((per_step, len(POOL_WINDOWS), POOL_BASE + t, POOL_GROUP), F32)],
        compiler_params=_params(),
        name="odd_sample",
    )(x, norm_g, mod, mod, mod, w_in16, w_out16, pw16, ps, bias_c, bias_n, p0, kc16, vc16)


def _ffn_tile(x, h16, w_up_ref, cw_ref, w_down_ref, shifted, store_up):
    acc = jnp.zeros((x.shape[0], D_MODEL), F32)
    for c in range(D_FF // FFN_COLS):
        halves = []
        for off in (0, D_FF):
            cols = slice(off + c * FFN_COLS, off + (c + 1) * FFN_COLS)
            up = _dot(h16, w_up_ref[:, cols])
            conv = (cw_ref[0:1, cols] * shifted(up, cols, 2) + cw_ref[1:2, cols] * shifted(up, cols, 1)
                    + cw_ref[2:3, cols] * up)
            store_up(up, cols)
            halves.append(conv)
        act = (_silu(halves[0]) * halves[1]).astype(BF16)
        acc = acc + _dot(act, w_down_ref[c * FFN_COLS:(c + 1) * FFN_COLS, :])
    return acc


def _ffn_prompt_kernel(x_ref, g_ref, sh_ref, sc_ref, gate_ref, w_up_ref, cw_ref, w_down_ref, f0_ref, gf_ref,
                       o_ref, f_out_ref, up_scr, act_scr, *, final_norm):
    i = pl.program_id(0)
    tm = x_ref.shape[0]
    n_chunks = D_FF // FFN_COLS
    slabs_per_chunk = FFN_COLS // LANES

    @pl.when(i == 0)
    def _():
        for j in range(2 * D_FF // LANES):
            up_scr[j, 0:SUBLANES, :] = f0_ref[:, j * LANES:(j + 1) * LANES]

    x = x_ref[...]
    h16 = _norm_mod(x, g_ref[...], sh_ref[0:1, :], sc_ref[0:1, :]).astype(BF16)

    def project(c):
        for off in (0, D_FF):
            c0 = off + c * FFN_COLS
            up = _dot(h16, w_up_ref[:, c0:c0 + FFN_COLS])
            for j in range(slabs_per_chunk):
                up_scr[c0 // LANES + j, SUBLANES:SUBLANES + tm, :] = up[:, j * LANES:(j + 1) * LANES]
            f_out_ref[:, c0:c0 + FFN_COLS] = up[tm - SUBLANES:, :]

    def conv_slab(j):
        cols = slice(j * LANES, (j + 1) * LANES)
        y = (cw_ref[0:1, cols] * up_scr[j, SUBLANES - 2:SUBLANES - 2 + tm, :]
             + cw_ref[1:2, cols] * up_scr[j, SUBLANES - 1:SUBLANES - 1 + tm, :]
             + cw_ref[2:3, cols] * up_scr[j, SUBLANES:SUBLANES + tm, :])
        up_scr[j, 0:SUBLANES, :] = up_scr[j, tm:tm + SUBLANES, :]
        return y

    def activate(c):
        for j in range(slabs_per_chunk):
            ja = c * slabs_per_chunk + j
            a = conv_slab(ja)
            b = conv_slab(D_FF // LANES + ja)
            act_scr[:, ja * LANES:(ja + 1) * LANES] = (_silu(a) * b).astype(BF16)

    project(0)
    for c in range(n_chunks):
        if c + 1 < n_chunks:
            project(c + 1)
        activate(c)

    out = x + gate_ref[0:1, :] * _dot(act_scr[...], w_down_ref[...])
    if final_norm:
        out = _rmsnorm(out, gf_ref[...])
    o_ref[...] = out


def _ffn_prompt(x, mod, l, n_streams, norm_g, w_up16, cw, w_down16, f0, gf, final_norm):
    t = x.shape[0]
    tm = ROW_TILE
    return pl.pallas_call(
        functools.partial(_ffn_prompt_kernel, final_norm=final_norm),
        out_shape=(jax.ShapeDtypeStruct((t, D_MODEL), F32), jax.ShapeDtypeStruct((SUBLANES, 2 * D_FF), F32)),
        grid=(t // tm,),
        in_specs=[_rows(tm, D_MODEL), _layer((1, D_MODEL), l)] + [_mod_prompt(l, 3 + j, n_streams) for j in range(3)]
        + [_layer((D_MODEL, 2 * D_FF), l), _layer((3, 2 * D_FF), l), _layer((D_FF, D_MODEL), l),
           _whole((SUBLANES, 2 * D_FF)), _whole((1, D_MODEL))],
        out_specs=(_rows(tm, D_MODEL), _whole_out((SUBLANES, 2 * D_FF))),
        scratch_shapes=[pltpu.VMEM((2 * D_FF // LANES, SUBLANES + tm, LANES), F32),
                        pltpu.VMEM((tm, D_FF), BF16)],
        compiler_params=_params(),
        name="ffn_prompt",
    )(x, norm_g, mod, mod, mod, w_up16, cw, w_down16, f0, gf)


def _ffn_sample_kernel(x_ref, g_ref, sh_ref, sc_ref, gate_ref, w_up_ref, cw_ref, w_down_ref, s1_ref, s2_ref, gf_ref,
                       o_ref, up_out_ref, *, final_norm, t):
    x = x_ref[...]
    h16 = _norm_mod(x, g_ref[...], _stream_rows(sh_ref[...], t), _stream_rows(sc_ref[...], t)).astype(BF16)

    def shifted(up, cols, s):
        return _shift_rows_streams(up, s1_ref[:, cols], s2_ref[:, cols], s, t)

    def store_up(up, cols):
        up_out_ref[:, cols] = up

    out = x + _stream_rows(gate_ref[...], t) * _ffn_tile(x, h16, w_up_ref, cw_ref, w_down_ref, shifted, store_up)
    if final_norm:
        out = _rmsnorm(out, gf_ref[...])
    o_ref[...] = out


def _ffn_sample(x, mod, l, norm_g, w_up16, cw, w_down16, s1, s2, gf, final_norm, t):
    rows = x.shape[0]
    n_streams = rows // t
    return pl.pallas_call(
        functools.partial(_ffn_sample_kernel, final_norm=final_norm, t=t),
        out_shape=(jax.ShapeDtypeStruct((rows, D_MODEL), F32), jax.ShapeDtypeStruct((rows, 2 * D_FF), F32)),
        grid=(1,),
        in_specs=[_whole((rows, D_MODEL)), _layer((1, D_MODEL), l)]
        + [_mod_streams(l, 3 + j, n_streams) for j in range(3)]
        + [_layer((D_MODEL, 2 * D_FF), l), _layer((3, 2 * D_FF), l), _layer((D_FF, D_MODEL), l),
           _layer((n_streams, 2 * D_FF), l), _layer((n_streams, 2 * D_FF), l), _whole((1, D_MODEL))],
        out_specs=(_whole_out((rows, D_MODEL)), _whole_out((rows, 2 * D_FF))),
        compiler_params=_params(),
        name="ffn_sample",
    )(x, norm_g, mod, mod, mod, w_up16, cw, w_down16, s1, s2, gf)


def _rotary_triplet(pos):
    half = DK_RET // 2
    inv = ROPE_BASE ** (-jnp.arange(half, dtype=F32) / half)
    ang = pos.astype(F32)[:, None] * inv[None, :]
    cos, sin = jnp.cos(ang), jnp.sin(ang)
    return jnp.stack([jnp.concatenate([cos] * 4, axis=-1), jnp.concatenate([sin] * 4, axis=-1),
                      jnp.concatenate([-sin, sin, -sin, sin], axis=-1)])


def _retention_tables(tb, n_streams=1):
    idx = np.arange(tb, dtype=np.float64)
    diff = idx[:, None] - idx[None, :]
    dec1 = np.where(diff[None] >= 0, np.exp(LOG_G[:, None, None] * np.maximum(diff, 0.0)[None]), 0.0)
    dec = np.zeros((H_RET, n_streams * tb, n_streams * tb))
    for b in range(n_streams):
        dec[:, b * tb:(b + 1) * tb, b * tb:(b + 1) * tb] = dec1
    xi = np.tile(np.repeat(np.exp(LOG_G[:, None] * (idx + 1)[None, :]).T, DK_RET, axis=1), (n_streams, 1))
    zeta_t = np.tile(np.repeat(np.exp(LOG_G[:, None] * (tb - 1 - idx)[None, :]), DK_RET, axis=0), (1, n_streams))
    gmat = np.zeros((HEAD_PAIRS, LANES, LANES))
    for h in range(H_RET):
        o = (h % 2) * DK_RET
        gmat[h // 2, o:o + DK_RET, o:o + DK_RET] = np.exp(LOG_G[h] * tb)
    gmat = np.tile(gmat, (1, 1, n_streams))
    return tuple(jnp.asarray(a, F32) for a in (dec, xi, zeta_t, gmat))


def _pair_state(s):
    lead = s.shape[:-3]
    s = s.reshape(lead + (HEAD_PAIRS, 2, DK_RET, DK_RET))
    z = jnp.zeros_like(s[..., 0, :, :])
    top = jnp.concatenate([s[..., 0, :, :], z], axis=-1)
    bot = jnp.concatenate([z, s[..., 1, :, :]], axis=-1)
    return jnp.concatenate([top, bot], axis=-2)


def _unpair_state(r):
    a = r[..., :DK_RET, :DK_RET]
    b = r[..., DK_RET:, DK_RET:]
    s = jnp.stack([a, b], axis=-3)
    return s.reshape(r.shape[:-3] + (H_RET, DK_RET, DK_RET))


def _band_bias(table):
    nq, nk = 2 * CHUNK, BAND2
    period = nq + nk
    j = np.arange(period)
    j = np.where(j < nk, j, j - period)
    idx = np.clip(HIST - j, -(CHUNK - 1), REL_CLIP) + (CHUNK - 1)
    one_period = table[:, idx].astype(F32)
    flat = jnp.tile(one_period, (1, nq + 1))[:, :nq * (period - 1)]
    return flat.reshape(-1, nq, period - 1)[:, :, :nk]


def _band_bias_t(raw):
    qq = np.arange(2 * CHUNK)[:, None]
    kk = np.arange(BAND2)[None, :]
    valid = np.where(qq < CHUNK, kk < BAND, kk >= CHUNK)
    b = jnp.where(valid, raw * LOG2E, NEG_INF).reshape(HEAD_PAIRS, 2, 2, CHUNK, BAND2)
    return jnp.transpose(b, (0, 4, 2, 1, 3)).reshape(HEAD_PAIRS, BAND2, 2 * LANES)


def _tail_rows(a, n):
    return a[..., a.shape[-2] - n:, :]


def kernel(x_prompt, x_sample, state_ret, state_sconv, state_pool, cache_k, cache_v, state_ffn, c_prompt, c_sample,
           norm_mix, norm_ffn, norm_final, w_ada, b_ada, w_in_even, w_out_even, ret_gn_gain, sconv_w, w_in_odd,
           w_out_odd, pool_w, pool_scale, rel_bias_table, ffn_w_up, ffn_conv, ffn_w_down):
    n_prompt, seq, _ = x_prompt.shape
    n_streams, t_s, _ = x_sample.shape
    assert n_prompt == 1 and n_streams % MOD_ROWS_PROMPT == 0
    rows_s = n_streams * t_s
    n_even, n_odd = (DEPTH + 1) // 2, DEPTH // 2

    c_all = jnp.concatenate([c_sample, c_prompt], axis=0)
    mod = _ada(jnp.pad(c_all, ((0, MOD_ROWS_PROMPT - 1), (0, 0))), w_ada, b_ada)

    bf = lambda w: w.astype(BF16)
    w_in_even16, w_out_even16 = bf(w_in_even), bf(w_out_even)
    w_in_odd16, w_out_odd16, pool_w16 = bf(w_in_odd), bf(w_out_odd), bf(pool_w)
    w_up16, w_down16 = bf(ffn_w_up), bf(ffn_w_down)
    norm_mix3, norm_ffn3 = norm_mix.reshape(DEPTH, 1, D_MODEL), norm_ffn.reshape(DEPTH, 1, D_MODEL)
    gain3, pool_scale3 = ret_gn_gain.reshape(n_even, 1, D_RET), pool_scale.reshape(n_odd, 1, D_POOL)
    norm_final2 = norm_final.reshape(1, D_MODEL)

    n_tiles = seq // ROW_TILE
    tabs_p = (_rotary_triplet(jnp.arange(n_tiles, dtype=jnp.int32) * ROW_TILE),
              _rotary_triplet(jnp.arange(ROW_TILE, dtype=jnp.int32))) + _retention_tables(RET_BLOCK)
    rot_s = _rotary_triplet(PAST_LEN + jnp.arange(t_s, dtype=jnp.int32))
    tabs_s = (jnp.tile(rot_s[0], (n_streams, 1)), jnp.tile(rot_s[2], (n_streams, 1))) \
        + _retention_tables(t_s, n_streams)

    cache_len = cache_k.shape[2]
    assert cache_len == HIST and t_s <= CHUNK
    bias_raw = [_band_bias(rel_bias_table[i]) for i in range(n_odd)]
    bias_p = [_band_bias_t(b) for b in bias_raw]
    bias_c = [b[:, :t_s, :cache_len].reshape(H_ATT * t_s, cache_len) for b in bias_raw]
    bias_n = [jnp.tile(b[:, :t_s, cache_len:cache_len + t_s], (1, 1, n_streams)).reshape(H_ATT * t_s, rows_s)
              for b in bias_raw]

    paired = _pair_state(state_ret)
    s_stack = jnp.transpose(paired, (0, 2, 1, 3, 4)).reshape(n_even, HEAD_PAIRS, n_streams * LANES, LANES)
    s_wide = jnp.transpose(paired, (0, 2, 3, 1, 4)).reshape(n_even, HEAD_PAIRS, LANES, n_streams * LANES)
    u1, u2 = state_sconv[:, :, 0, :], state_sconv[:, :, 1, :]
    f1, f2 = state_ffn[:, :, 0, :], state_ffn[:, :, 1, :]
    p0_s = jnp.pad(state_pool, ((0, 0), (0, 0), (POOL_BASE - POOL_BUF, 0), (0, 0)))
    kc16 = cache_k.reshape(n_odd, n_streams, cache_len, D_ATT).astype(BF16)
    vc16 = cache_v.reshape(n_odd, n_streams, cache_len, D_ATT).astype(BF16)

    xp = x_prompt.reshape(seq, D_MODEL)
    xs = x_sample.reshape(rows_s, D_MODEL)

    ret_p, ret_s, sconv_p, sconv_s, pool_p, pool_s = [], [], [], [], [], []
    k_p, k_s, v_p, v_s, ffn_p, ffn_s = [], [], [], [], [], []
    for l in range(DEPTH):
        i = l // 2
        if l % 2 == 0:
            xp, r_new, u_new = _even_prompt(
                xp, mod, l, n_streams, norm_mix3, w_in_even16, w_out_even16, gain3, sconv_w, tabs_p,
                jnp.zeros((HEAD_PAIRS, LANES, LANES), F32), jnp.zeros((SUBLANES, D_SCONV), F32))
            ret_p.append(_unpair_state(r_new)[None])
            sconv_p.append(_tail_rows(u_new, 2)[None])
            xs, s_new, u_all = _even_sample(
                xs, mod, l, norm_mix3, w_in_even16, w_out_even16, gain3, sconv_w, tabs_s, s_stack, s_wide, u1, u2, t_s)
            s_new = jnp.transpose(s_new.reshape(HEAD_PAIRS, LANES, n_streams, LANES), (2, 0, 1, 3))
            ret_s.append(_unpair_state(s_new))
            sconv_s.append(_tail_rows(u_all.reshape(n_streams, t_s, D_SCONV), 2))
        else:
            xp, p_new, k_new, v_new = _odd_prompt(
                xp, mod, l, n_streams, norm_mix3, w_in_odd16, w_out_odd16, pool_w16, pool_scale3, bias_p[i],
                jnp.zeros((POOL_BASE, D_POOL), F32))
            pool_p.append(_tail_rows(p_new, POOL_BUF)[None])
            k_p.append(k_new.reshape(1, HIST, H_ATT, DH_ATT))
            v_p.append(v_new.reshape(1, HIST, H_ATT, DH_ATT))
            xs, p_new, k_new, v_new = _odd_sample(
                xs, mod, l, norm_mix3, w_in_odd16, w_out_odd16, pool_w16, pool_scale3, bias_c[i], bias_n[i],
                p0_s, kc16, vc16, t_s, PAST_LEN)
            pool_s.append(_tail_rows(p_new, POOL_BUF))
            k_s.append(k_new.reshape(n_streams, t_s, H_ATT, DH_ATT))
            v_s.append(v_new.reshape(n_streams, t_s, H_ATT, DH_ATT))
        last = l == DEPTH - 1
        xp, f_new = _ffn_prompt(xp, mod, l, n_streams, norm_ffn3, w_up16, ffn_conv, w_down16,
                                jnp.zeros((SUBLANES, 2 * D_FF), F32), norm_final2, last)
        ffn_p.append(_tail_rows(f_new, 2)[None])
        xs, up_all = _ffn_sample(xs, mod, l, norm_ffn3, w_up16, ffn_conv, w_down16, f1, f2, norm_final2, last, t_s)
        ffn_s.append(_tail_rows(up_all.reshape(n_streams, t_s, 2 * D_FF), 2))

    st = jnp.stack
    return (xp.reshape(1, seq, D_MODEL), xs.reshape(n_streams, t_s, D_MODEL),
            st(ret_p), st(ret_s), st(sconv_p), st(sconv_s), st(pool_p), st(pool_s),
            st(k_p), st(k_s), st(v_p), st(v_s), st(ffn_p), st(ffn_s))
```

```python
import functools

import numpy as np
import jax
import jax.numpy as jnp
from jax import lax
from jax.experimental import pallas as pl
from jax.experimental.pallas import tpu as pltpu

F32 = jnp.float32
BF16 = jnp.bfloat16

D_MODEL = 1024
DEPTH = 4
PAST_LEN = 4096
CHUNK = 64
H_RET = 8
DK_RET = 64
D_RET = H_RET * DK_RET
ROPE_BASE = 10000.0
D_SCONV = D_MODEL - D_RET
POOL_WINDOWS = (2, 4, 8, 16)
D_POOL = D_MODEL // 2
POOL_GROUP = D_POOL // len(POOL_WINDOWS)
POOL_BUF = max(POOL_WINDOWS) - 1
H_ATT = 8
DH_ATT = 64
D_ATT = H_ATT * DH_ATT
N_PREV_CHUNKS = 8
REL_CLIP = 256
D_FF = 2816
EPS = 1e-6
NEG_INF = -1e30
D_IN_EVEN = 4 * D_RET + 3 * D_SCONV
D_IN_ODD = D_POOL + 3 * D_ATT

LANES = 128
SUBLANES = 8
HEAD_PAIRS = H_RET // 2
ROW_TILE = 512
RET_BLOCK = 128
FFN_COLS = 256
BAND = (N_PREV_CHUNKS + 1) * CHUNK
BAND2 = BAND + CHUNK
LOG2E = 1.4426950408889634
HIST = N_PREV_CHUNKS * CHUNK
POOL_BASE = 2 * SUBLANES
VMEM_LIMIT = 56 * 1024 * 1024

LOG_G = np.log1p(-(2.0 ** (-5.0 - np.arange(H_RET, dtype=np.float64))))


def _params(n_axes=1):
    return pltpu.CompilerParams(dimension_semantics=("arbitrary",) * n_axes, vmem_limit_bytes=VMEM_LIMIT)


def _whole(shape):
    nd = len(shape)
    return pl.BlockSpec(shape, lambda i: (0,) * nd, pipeline_mode=pl.Buffered(1))


def _whole_out(shape):
    nd = len(shape)
    return pl.BlockSpec(shape, lambda i: (0,) * nd)


def _layer(shape, l):
    nd = len(shape)
    return pl.BlockSpec((None,) + tuple(shape), lambda i: (l,) + (0,) * nd, pipeline_mode=pl.Buffered(1))


MOD_ROWS_PROMPT = SUBLANES


def _mod_prompt(l, j, n_streams):
    return pl.BlockSpec((None, MOD_ROWS_PROMPT, D_MODEL), lambda i: (l, n_streams // MOD_ROWS_PROMPT, j),
                        pipeline_mode=pl.Buffered(1))


def _mod_streams(l, j, n_streams):
    return pl.BlockSpec((None, n_streams, D_MODEL), lambda i: (l, 0, j), pipeline_mode=pl.Buffered(1))


def _stream_rows(m, t):
    return jnp.concatenate([jnp.broadcast_to(m[b:b + 1, :], (t, m.shape[1])) for b in range(m.shape[0])], axis=0)


def _rows(block_rows, cols):
    return pl.BlockSpec((block_rows, cols), lambda i: (i, 0))


def _lead(shape):
    nd = len(shape)
    return pl.BlockSpec((1,) + tuple(shape), lambda i: (i,) + (0,) * nd)


def _dot(a, b):
    return jnp.dot(a, b, preferred_element_type=F32)


def _dot_nt(a, b):
    return lax.dot_general(a, b, (((1,), (1,)), ((), ())), preferred_element_type=F32)


def _rmsnorm(x, g):
    return x * lax.rsqrt(jnp.mean(x * x, axis=-1, keepdims=True) + EPS) * g


def _norm_mod(x, g, shift, scale):
    return _rmsnorm(x, g) * (1.0 + scale) + shift


def _silu(x):
    return x * (1.0 / (1.0 + jnp.exp(-x)))


def _low_half(shape):
    return (lax.broadcasted_iota(jnp.int32, shape, len(shape) - 1) % LANES) < DK_RET


def _shift_rows(u, prev8, s):
    rolled = pltpu.roll(u, s, axis=0)
    prolled = pltpu.roll(prev8, s, axis=0)
    row = lax.broadcasted_iota(jnp.int32, prev8.shape, 0)
    first = jnp.where(row < s, prolled, rolled[0:SUBLANES])
    return jnp.concatenate([first, rolled[SUBLANES:]], axis=0)


def _shift_rows_streams(u, older, newer, s, t):
    row_in_stream = lax.broadcasted_iota(jnp.int32, u.shape, 0) % t
    rolled = pltpu.roll(u, s, axis=0)
    if s == 1:
        return jnp.where(row_in_stream == 0, _stream_rows(newer, t), rolled)
    return jnp.where(row_in_stream == 0, _stream_rows(older, t),
                     jnp.where(row_in_stream == 1, _stream_rows(newer, t), rolled))


def _ada_kernel(c_ref, w_ref, b_ref, o_ref):
    c = c_ref[...]
    o_ref[0] = _dot(_silu(c).astype(BF16), w_ref[0].astype(BF16)) + b_ref[0]


def _ada(c_all, w_ada, b_ada):
    rows = c_all.shape[0]
    tn = 1536
    return pl.pallas_call(
        _ada_kernel,
        out_shape=jax.ShapeDtypeStruct((DEPTH, rows, 6 * D_MODEL), F32),
        grid=(DEPTH, 6 * D_MODEL // tn),
        in_specs=[
            pl.BlockSpec((rows, D_MODEL), lambda l, j: (0, 0)),
            pl.BlockSpec((1, D_MODEL, tn), lambda l, j: (l, 0, j)),
            pl.BlockSpec((1, 1, tn), lambda l, j: (l, 0, j)),
        ],
        out_specs=pl.BlockSpec((1, rows, tn), lambda l, j: (l, 0, j)),
        compiler_params=_params(2),
        name="ada_mod",
    )(c_all, w_ada, b_ada.reshape(DEPTH, 1, 6 * D_MODEL))


def _rotary_pair(x, cos, sin_signed):
    lane = lax.broadcasted_iota(jnp.int32, x.shape, 1)
    first_half = (lane % DK_RET) < (DK_RET // 2)
    swapped = jnp.where(first_half, pltpu.roll(x, LANES - DK_RET // 2, axis=1),
                        pltpu.roll(x, DK_RET // 2, axis=1))
    return x * cos + swapped * sin_signed


def _retention_block(proj, cos, sin_signed, dec_ref, xi, zeta_t_ref, gain, cross_fn, update_fn):
    tb = proj.shape[0]
    low = _low_half((tb, LANES))
    inv_n = 1.0 / DK_RET
    pairs = range(HEAD_PAIRS)
    cols = [slice(p * LANES, (p + 1) * LANES) for p in pairs]
    q = [_rotary_pair(proj[:, cols[p]], cos, sin_signed) for p in pairs]
    k_t = [(_rotary_pair(proj[:, D_RET + p * LANES:D_RET + (p + 1) * LANES], cos, sin_signed)
            * (DK_RET ** -0.5)).T for p in pairs]
    v16 = [proj[:, 2 * D_RET + p * LANES:2 * D_RET + (p + 1) * LANES].astype(BF16) for p in pairs]
    k_t16 = [k_t[p].astype(BF16) for p in pairs]
    kz_t16 = [(k_t[p] * zeta_t_ref[cols[p], :]).astype(BF16) for p in pairs]
    q16 = [q[p].astype(BF16) for p in pairs]
    qe16 = [jnp.where(low, q[p], 0.0).astype(BF16) for p in pairs]
    qo16 = [jnp.where(low, 0.0, q[p]).astype(BF16) for p in pairs]
    s_e = [(_dot(qe16[p], k_t16[p]) * dec_ref[2 * p]).astype(BF16) for p in pairs]
    s_o = [(_dot(qo16[p], k_t16[p]) * dec_ref[2 * p + 1]).astype(BF16) for p in pairs]
    cross = [cross_fn(p, q16[p]) * xi[:, cols[p]] for p in pairs]
    o = [jnp.where(low, _dot(s_e[p], v16[p]), _dot(s_o[p], v16[p])) + cross[p] for p in pairs]
    for p in pairs:
        update_fn(p, kz_t16[p], v16[p])
    outs = []
    for p in pairs:
        s_lo = jnp.sum(jnp.where(low, o[p], 0.0), axis=-1, keepdims=True)
        s_hi = jnp.sum(jnp.where(low, 0.0, o[p]), axis=-1, keepdims=True)
        d = o[p] - jnp.where(low, s_lo, s_hi) * inv_n
        d2 = d * d
        v_lo = jnp.sum(jnp.where(low, d2, 0.0), axis=-1, keepdims=True)
        v_hi = jnp.sum(jnp.where(low, 0.0, d2), axis=-1, keepdims=True)
        on = d * lax.rsqrt(jnp.where(low, v_lo, v_hi) * inv_n + EPS)
        g = proj[:, 3 * D_RET + p * LANES:3 * D_RET + (p + 1) * LANES]
        outs.append(_silu(g) * (on * gain[:, cols[p]]))
    return jnp.concatenate(outs, axis=-1)


def _even_tail(x, proj, ret_out, conv_in_shift, cw_ref, gate, w_out_ref):
    gate_b = proj[:, 4 * D_RET:4 * D_RET + D_SCONV]
    u = proj[:, 4 * D_RET + D_SCONV:4 * D_RET + 2 * D_SCONV] * proj[:, 4 * D_RET + 2 * D_SCONV:]
    conv = cw_ref[0:1, :] * conv_in_shift(u, 2) + cw_ref[1:2, :] * conv_in_shift(u, 1) + cw_ref[2:3, :] * u
    mixed = jnp.concatenate([ret_out, gate_b * conv], axis=-1).astype(BF16)
    return x + gate * _dot(mixed, w_out_ref[...]), u


def _even_prompt_kernel(x_ref, g_ref, sh_ref, sc_ref, gate_ref, w_in_ref, w_out_ref, gain_ref, cw_ref,
                        rot_tile_ref, rot_row_ref, dec_ref, xi_ref, zt_ref, gmat_ref, r0_ref, u0_ref,
                        o_ref, r_out_ref, u_out_ref, r_scr, u_scr):
    i = pl.program_id(0)
    cos_0, sin_0, ssin_0 = (rot_tile_ref[j, pl.ds(i, 1), :] for j in range(3))
    cos = cos_0 * rot_row_ref[0] - sin_0 * rot_row_ref[1]
    sin_signed = ssin_0 * rot_row_ref[0] + cos_0 * rot_row_ref[2]

    @pl.when(i == 0)
    def _():
        r_scr[...] = r0_ref[...]
        u_scr[...] = u0_ref[...]

    x = x_ref[...]
    h = _norm_mod(x, g_ref[...], sh_ref[0:1, :], sc_ref[0:1, :]).astype(BF16)
    proj = _dot(h, w_in_ref[...])
    tm = x.shape[0]
    tb = dec_ref.shape[1]
    r_i = lax.broadcasted_iota(jnp.int32, (LANES, LANES), 0) < DK_RET
    c_i = lax.broadcasted_iota(jnp.int32, (LANES, LANES), 1) < DK_RET
    blockdiag = r_i == c_i

    def cross_fn(p, q16):
        return _dot(q16, r_scr[p].astype(BF16))

    def update_fn(p, kz_t16, v16):
        r_scr[p] = r_scr[p] * gmat_ref[p] + jnp.where(blockdiag, _dot(kz_t16, v16), 0.0)

    rets = []
    for r in range(tm // tb):
        rows = slice(r * tb, (r + 1) * tb)
        rets.append(_retention_block(proj[rows, :], cos[rows, :], sin_signed[rows, :], dec_ref, xi_ref[...],
                                     zt_ref, gain_ref[...], cross_fn, update_fn))
    ret_out = jnp.concatenate(rets, axis=0)

    prev8 = u_scr[...]
    out, u = _even_tail(x, proj, ret_out, lambda u, s: _shift_rows(u, prev8, s), cw_ref, gate_ref[0:1, :], w_out_ref)
    u_scr[...] = u[tm - SUBLANES:, :]
    o_ref[...] = out
    r_out_ref[...] = r_scr[...]
    u_out_ref[...] = u[tm - SUBLANES:, :]


def _even_prompt(x, mod, l, n_streams, norm_g, w_in16, w_out16, gain, cw, tabs, r0, u0):
    t = x.shape[0]
    tm, tb = ROW_TILE, RET_BLOCK
    rot_tile, rot_row, dec, xi, zt, gmat = tabs
    li = l // 2
    state = (HEAD_PAIRS, LANES, LANES)
    return pl.pallas_call(
        _even_prompt_kernel,
        out_shape=(jax.ShapeDtypeStruct((t, D_MODEL), F32),
                   jax.ShapeDtypeStruct(state, F32),
                   jax.ShapeDtypeStruct((SUBLANES, D_SCONV), F32)),
        grid=(t // tm,),
        in_specs=[_rows(tm, D_MODEL), _layer((1, D_MODEL), l)] + [_mod_prompt(l, j, n_streams) for j in range(3)]
        + [_layer((D_MODEL, D_IN_EVEN), li), _layer((D_MODEL, D_MODEL), li),
           _layer((1, D_RET), li), _layer((3, D_SCONV), li),
           _whole((3, t // tm, LANES)), _whole((3, tm, LANES)),
           _whole((H_RET, tb, tb)), _whole((tb, D_RET)), _whole((D_RET, tb)),
           _whole(state), _whole(state), _whole((SUBLANES, D_SCONV))],
        out_specs=(_rows(tm, D_MODEL), _whole_out(state), _whole_out((SUBLANES, D_SCONV))),
        scratch_shapes=[pltpu.VMEM(state, F32), pltpu.VMEM((SUBLANES, D_SCONV), F32)],
        compiler_params=_params(),
        name="even_prompt",
    )(x, norm_g, mod, mod, mod, w_in16, w_out16, gain, cw, rot_tile, rot_row, dec, xi, zt, gmat, r0, u0)


def _even_sample_kernel(x_ref, g_ref, sh_ref, sc_ref, gate_ref, w_in_ref, w_out_ref, gain_ref, cw_ref,
                        cos_ref, sin_ref, dec_ref, xi_ref, zt_ref, gwide_ref, s_stack_ref, s_wide_ref,
                        u1_ref, u2_ref, o_ref, s_out_ref, u_out_ref, *, t):
    x = x_ref[...]
    rows = x.shape[0]
    n_streams = rows // t
    wide = n_streams * LANES
    h = _norm_mod(x, g_ref[...], _stream_rows(sh_ref[...], t), _stream_rows(sc_ref[...], t)).astype(BF16)
    proj = _dot(h, w_in_ref[...])
    own = (lax.broadcasted_iota(jnp.int32, (rows, wide), 0) // t
           == lax.broadcasted_iota(jnp.int32, (rows, wide), 1) // LANES)
    r_i = lax.broadcasted_iota(jnp.int32, (LANES, wide), 0) < DK_RET
    blockdiag = r_i == _low_half((LANES, wide))

    def expand(a16):
        tiled = jnp.concatenate([a16.astype(F32)] * n_streams, axis=-1)
        return jnp.where(own, tiled, 0.0).astype(BF16)

    def cross_fn(p, q16):
        return _dot(expand(q16), s_stack_ref[p].astype(BF16))

    def update_fn(p, kz_t16, v16):
        kv = _dot(kz_t16, expand(v16))
        s_out_ref[p] = s_wide_ref[p] * gwide_ref[p] + jnp.where(blockdiag, kv, 0.0)

    ret_out = _retention_block(proj, cos_ref[...], sin_ref[...], dec_ref, xi_ref[...], zt_ref, gain_ref[...],
                               cross_fn, update_fn)
    out, u = _even_tail(x, proj, ret_out, lambda u, s: _shift_rows_streams(u, u1_ref[...], u2_ref[...], s, t),
                        cw_ref, _stream_rows(gate_ref[...], t), w_out_ref)
    o_ref[...] = out
    u_out_ref[...] = u


def _even_sample(x, mod, l, norm_g, w_in16, w_out16, gain, cw, tabs, s_stack, s_wide, u1, u2, t):
    rows = x.shape[0]
    n_streams = rows // t
    wide = n_streams * LANES
    cos, sin, dec, xi, zt, gwide = tabs
    li = l // 2
    return pl.pallas_call(
        functools.partial(_even_sample_kernel, t=t),
        out_shape=(jax.ShapeDtypeStruct((rows, D_MODEL), F32),
                   jax.ShapeDtypeStruct((HEAD_PAIRS, LANES, wide), F32),
                   jax.ShapeDtypeStruct((rows, D_SCONV), F32)),
        grid=(1,),
        in_specs=[_whole((rows, D_MODEL)), _layer((1, D_MODEL), l)] + [_mod_streams(l, j, n_streams) for j in range(3)]
        + [_layer((D_MODEL, D_IN_EVEN), li), _layer((D_MODEL, D_MODEL), li),
           _layer((1, D_RET), li), _layer((3, D_SCONV), li),
           _whole((rows, LANES)), _whole((rows, LANES)),
           _whole((H_RET, rows, rows)), _whole((rows, D_RET)), _whole((D_RET, rows)),
           _whole((HEAD_PAIRS, LANES, wide)), _layer((HEAD_PAIRS, wide, LANES), li),
           _layer((HEAD_PAIRS, LANES, wide), li),
           _layer((n_streams, D_SCONV), li), _layer((n_streams, D_SCONV), li)],
        out_specs=(_whole_out((rows, D_MODEL)), _whole_out((HEAD_PAIRS, LANES, wide)),
                   _whole_out((rows, D_SCONV))),
        compiler_params=_params(),
        name="even_sample",
    )(x, norm_g, mod, mod, mod, w_in16, w_out16, gain, cw, cos, sin, dec, xi, zt, gwide, s_stack, s_wide, u1, u2)


def _pool_put(hist_ref, row0, block):
    for gi in range(len(POOL_WINDOWS)):
        hist_ref[gi, row0:row0 + block.shape[0], :] = block[:, gi * POOL_GROUP:(gi + 1) * POOL_GROUP]


def _pool_get(hist_ref, row0, rows):
    return jnp.concatenate([hist_ref[gi, row0:row0 + rows, :] for gi in range(len(POOL_WINDOWS))], axis=-1)


def _pool(hist_ref, p, pos, pool_w_ref, scale):
    t = p.shape[0]
    outs = []
    for gi, w in enumerate(POOL_WINDOWS):
        cols = slice(gi * POOL_GROUP, (gi + 1) * POOL_GROUP)
        win = p[:, cols]
        for d in range(1, w):
            win = win + hist_ref[gi, POOL_BASE - d:POOL_BASE - d + t, :]
        inv_cnt = 1.0 / jnp.minimum(pos + 1, w).astype(F32)
        pooled = win * inv_cnt - p[:, cols]
        outs.append(_dot(pooled.astype(BF16), pool_w_ref[gi]) * scale[:, cols])
    return jnp.concatenate(outs, axis=-1)


def _attend_scores(kbs, q_as, q_bs, biases):
    low = _low_half((CHUNK, LANES))
    out = []
    for kb, q_a, q_b, bias in zip(kbs, q_as, q_bs, biases):
        qs = jnp.concatenate([jnp.where(low, q_a, 0.0), jnp.where(low, 0.0, q_a),
                              jnp.where(low, q_b, 0.0), jnp.where(low, 0.0, q_b)], axis=0)
        qbd = qs.T.astype(BF16)
        half = (kb.shape[0] // 2) // (2 * SUBLANES) * (2 * SUBLANES)
        out.append(jnp.concatenate([_dot(kb[:half], qbd), _dot(kb[half:], qbd)], axis=0) + bias)
    return out


def _attend_values(scores, vts):
    low = _low_half((CHUNK, LANES))
    n = len(scores)
    e = [jnp.exp2(s - jnp.max(s, axis=0, keepdims=True)) for s in scores]
    inv_l = [1.0 / jnp.sum(e[j], axis=0, keepdims=True) for j in range(n)]
    e16 = [e[j].astype(BF16) for j in range(n)]
    o_t = [jnp.concatenate([_dot(vts[j][:DK_RET], e16[j]), _dot(vts[j][DK_RET:], e16[j])], axis=0) * inv_l[j]
           for j in range(n)]
    o_t = [o.T for o in o_t]
    return [(jnp.where(low, o[0:CHUNK], o[CHUNK:2 * CHUNK]), jnp.where(low, o[2 * CHUNK:3 * CHUNK], o[3 * CHUNK:]))
            for o in o_t]


def _odd_prompt_kernel(x_ref, g_ref, sh_ref, sc_ref, gate_ref, w_in_ref, w_out_ref, pw_ref, ps_ref, bias_ref, p0_ref,
                       o_ref, p_out_ref, k_out_ref, v_out_ref, pbuf, kbuf, vtbuf, q_scr, att_scr):
    i = pl.program_id(0)
    tm = x_ref.shape[0]

    @pl.when(i == 0)
    def _():
        _pool_put(pbuf, 0, p0_ref[...])
        kbuf[0:HIST, :] = jnp.zeros((HIST, D_ATT), BF16)
        vtbuf[:, :, 0:HIST] = jnp.zeros((HEAD_PAIRS, LANES, HIST), BF16)

    x = x_ref[...]
    h = _norm_mod(x, g_ref[...], sh_ref[0:1, :], sc_ref[0:1, :]).astype(BF16)
    proj = _dot(h, w_in_ref[...])
    p = proj[:, :D_POOL]
    q_scr[...] = proj[:, D_POOL:D_POOL + D_ATT] * (DH_ATT ** -0.5 * LOG2E)
    k = proj[:, D_POOL + D_ATT:D_POOL + 2 * D_ATT]
    v = proj[:, D_POOL + 2 * D_ATT:]
    _pool_put(pbuf, POOL_BASE, p)
    kbuf[HIST:HIST + tm, :] = k.astype(BF16)
    for pr in range(HEAD_PAIRS):
        vtbuf[pr, :, HIST:HIST + tm] = v[:, pr * LANES:(pr + 1) * LANES].T.astype(BF16)
    k_out_ref[...] = k[tm - HIST:, :]
    v_out_ref[...] = v[tm - HIST:, :]

    pos = i * tm + lax.broadcasted_iota(jnp.int32, (tm, 1), 0)
    pool_out = _pool(pbuf, p, pos, pw_ref, ps_ref[...])

    def attend_tile(first_tile):
        pairs = range(HEAD_PAIRS)
        lanes = [slice(pr * LANES, (pr + 1) * LANES) for pr in pairs]
        n_blocks = tm // (2 * CHUNK)
        skip = [max(HIST - jb * 2 * CHUNK, 0) if first_tile else 0 for jb in range(n_blocks)]

        def scores(jb):
            r0 = jb * 2 * CHUNK
            return _attend_scores([kbuf[r0 + skip[jb]:r0 + BAND2, lanes[pr]] for pr in pairs],
                                  [q_scr[r0:r0 + CHUNK, lanes[pr]] for pr in pairs],
                                  [q_scr[r0 + CHUNK:r0 + 2 * CHUNK, lanes[pr]] for pr in pairs],
                                  [bias_ref[pr, skip[jb]:, :] for pr in pairs])

        s_next = scores(0)
        for jb in range(n_blocks):
            r0 = jb * 2 * CHUNK
            s_cur = s_next
            if jb + 1 < n_blocks:
                s_next = scores(jb + 1)
            outs = _attend_values(s_cur, [vtbuf[pr, :, r0 + skip[jb]:r0 + BAND2] for pr in pairs])
            att_scr[r0:r0 + CHUNK, :] = jnp.concatenate([o[0] for o in outs], axis=-1)
            att_scr[r0 + CHUNK:r0 + 2 * CHUNK, :] = jnp.concatenate([o[1] for o in outs], axis=-1)

    pl.when(i == 0)(functools.partial(attend_tile, True))
    pl.when(i > 0)(functools.partial(attend_tile, False))

    kbuf[0:HIST, :] = kbuf[tm:tm + HIST, :]
    vtbuf[:, :, 0:HIST] = vtbuf[:, :, tm:tm + HIST]
    tail = _pool_get(pbuf, tm, POOL_BASE)
    _pool_put(pbuf, 0, tail)
    p_out_ref[...] = tail

    mixed = jnp.concatenate([pool_out, att_scr[...]], axis=-1).astype(BF16)
    o_ref[...] = x + gate_ref[0:1, :] * _dot(mixed, w_out_ref[...])


def _odd_prompt(x, mod, l, n_streams, norm_g, w_in16, w_out16, pw16, ps, bias_t, p0):
    t = x.shape[0]
    tm = ROW_TILE
    assert tm == HIST and t % tm == 0
    li = l // 2
    return pl.pallas_call(
        _odd_prompt_kernel,
        out_shape=(jax.ShapeDtypeStruct((t, D_MODEL), F32),
                   jax.ShapeDtypeStruct((POOL_BASE, D_POOL), F32),
                   jax.ShapeDtypeStruct((HIST, D_ATT), F32),
                   jax.ShapeDtypeStruct((HIST, D_ATT), F32)),
        grid=(t // tm,),
        in_specs=[_rows(tm, D_MODEL), _layer((1, D_MODEL), l)] + [_mod_prompt(l, j, n_streams) for j in range(3)]
        + [_layer((D_MODEL, D_IN_ODD), li), _layer((D_MODEL, D_MODEL), li),
           _layer((len(POOL_WINDOWS), POOL_GROUP, POOL_GROUP), li), _layer((1, D_POOL), li),
           _whole((HEAD_PAIRS, BAND2, 2 * LANES)), _whole((POOL_BASE, D_POOL))],
        out_specs=(_rows(tm, D_MODEL), _whole_out((POOL_BASE, D_POOL)),
                   _whole_out((HIST, D_ATT)), _whole_out((HIST, D_ATT))),
        scratch_shapes=[pltpu.VMEM((len(POOL_WINDOWS), POOL_BASE + tm, POOL_GROUP), F32),
                        pltpu.VMEM((HIST + tm, D_ATT), BF16), pltpu.VMEM((HEAD_PAIRS, LANES, HIST + tm), BF16),
                        pltpu.VMEM((tm, D_ATT), F32), pltpu.VMEM((tm, D_ATT), F32)],
        compiler_params=_params(),
        name="odd_prompt",
    )(x, norm_g, mod, mod, mod, w_in16, w_out16, pw16, ps, bias_t, p0)


def _odd_sample_kernel(x_ref, g_ref, sh_ref, sc_ref, gate_ref, w_in_ref, w_out_ref, pw_ref, ps_ref,
                       bias_c_ref, bias_n_ref, p0_ref, kc_ref, vc_ref,
                       o_ref, p_out_ref, k_out_ref, v_out_ref, proj_scr, mix_scr, pbuf, *, pos0, t):
    step = pl.program_id(0)
    rows = x_ref.shape[0]
    per_step = kc_ref.shape[0]

    @pl.when(step == 0)
    def _():
        h = _norm_mod(x_ref[...], g_ref[...], _stream_rows(sh_ref[...], t), _stream_rows(sc_ref[...], t)).astype(BF16)
        proj = _dot(h, w_in_ref[...])
        proj_scr[...] = proj
        k_out_ref[...] = proj[:, D_POOL + D_ATT:D_POOL + 2 * D_ATT]
        v_out_ref[...] = proj[:, D_POOL + 2 * D_ATT:]

    kn = proj_scr[:, D_POOL + D_ATT:D_POOL + 2 * D_ATT].astype(BF16)
    vn = proj_scr[:, D_POOL + 2 * D_ATT:].astype(BF16)
    pos = pos0 + lax.broadcasted_iota(jnp.int32, (t, 1), 0)
    head_of_lane = lax.broadcasted_iota(jnp.int32, (t, D_ATT), 1) // DH_ATT
    stream_of_col = lax.broadcasted_iota(jnp.int32, (H_ATT * t, rows), 1) // t
    for j in range(per_step):
        b = step * per_step + j
        r0 = pl.multiple_of(b * t, t)
        proj = proj_scr[pl.ds(r0, t), :]
        p = proj[:, :D_POOL]
        _pool_put(pbuf.at[j], 0, p0_ref[j])
        _pool_put(pbuf.at[j], POOL_BASE, p)
        p_out_ref[j] = p
        pool_out = _pool(pbuf.at[j], p, pos, pw_ref, ps_ref[...])

        q = proj[:, D_POOL:D_POOL + D_ATT] * (DH_ATT ** -0.5)
        q_heads = jnp.concatenate([jnp.where(head_of_lane == hh, q, 0.0) for hh in range(H_ATT)],
                                  axis=0).astype(BF16)
        s_c = _dot_nt(q_heads, kc_ref[j]) + bias_c_ref[...]
        s_n = jnp.where(stream_of_col == b, _dot_nt(q_heads, kn) + bias_n_ref[...], NEG_INF)
        m = jnp.maximum(jnp.max(s_c, axis=-1, keepdims=True), jnp.max(s_n, axis=-1, keepdims=True))
        e_c = jnp.exp(s_c - m)
        e_n = jnp.exp(s_n - m)
        inv_l = 1.0 / (jnp.sum(e_c, axis=-1, keepdims=True) + jnp.sum(e_n, axis=-1, keepdims=True))
        o_heads = (_dot(e_c.astype(BF16), vc_ref[j]) + _dot(e_n.astype(BF16), vn)) * inv_l
        att = jnp.where(head_of_lane == 0, o_heads[0:t], 0.0)
        for hh in range(1, H_ATT):
            att = jnp.where(head_of_lane == hh, o_heads[hh * t:(hh + 1) * t], att)
        mix_scr[pl.ds(r0, t), :] = jnp.concatenate([pool_out, att], axis=-1)

    @pl.when(step == pl.num_programs(0) - 1)
    def _():
        o_ref[...] = x_ref[...] + _stream_rows(gate_ref[...], t) * _dot(mix_scr[...].astype(BF16), w_out_ref[...])


SAMPLE_STREAMS_PER_STEP = 4


def _odd_sample(x, mod, l, norm_g, w_in16, w_out16, pw16, ps, bias_c, bias_n, p0, kc16, vc16, t, pos0):
    rows = x.shape[0]
    n_streams = rows // t
    cache = kc16.shape[2]
    li = l // 2
    per_step = SAMPLE_STREAMS_PER_STEP
    assert n_streams % per_step == 0

    def streams(shape):
        nd = len(shape)
        return pl.BlockSpec((None, per_step) + tuple(shape), lambda i: (li, i) + (0,) * nd)

    return pl.pallas_call(
        functools.partial(_odd_sample_kernel, pos0=pos0, t=t),
        out_shape=(jax.ShapeDtypeStruct((rows, D_MODEL), F32),
                   jax.ShapeDtypeStruct((n_streams, t, D_POOL), F32),
                   jax.ShapeDtypeStruct((rows, D_ATT), F32),
                   jax.ShapeDtypeStruct((rows, D_ATT), F32)),
        grid=(n_streams // per_step,),
        in_specs=[_whole((rows, D_MODEL)), _layer((1, D_MODEL), l)] + [_mod_streams(l, j, n_streams) for j in range(3)]
        + [_layer((D_MODEL, D_IN_ODD), li), _layer((D_MODEL, D_MODEL), li),
           _layer((len(POOL_WINDOWS), POOL_GROUP, POOL_GROUP), li), _layer((1, D_POOL), li),
           _whole((H_ATT * t, cache)), _whole((H_ATT * t, rows)),
           streams((POOL_BASE, D_POOL)), streams((cache, D_ATT)), streams((cache, D_ATT))],
        out_specs=(_whole_out((rows, D_MODEL)), pl.BlockSpec((per_step, t, D_POOL), lambda i: (i, 0, 0)),
                   _whole_out((rows, D_ATT)), _whole_out((rows, D_ATT))),
        scratch_shapes=[pltpu.VMEM((rows, D_IN_ODD), F32), pltpu.VMEM((rows, D_MODEL), F32),
                        pltpu.VMEM((per_step, len(POOL_WINDOWS), POOL_BASE + t, POOL_GROUP), F32)],
        compiler_params=_params(),
        name="odd_sample",
    )(x, norm_g, mod, mod, mod, w_in16, w_out16, pw16, ps, bias_c, bias_n, p0, kc16, vc16)


FFN_LOAD_COLS = 512


def _ffn_kernel(xp_ref, xs_ref, g_ref, shp_ref, scp_ref, gatep_ref, shs_ref, scs_ref, gates_ref,
                w_up_ref, cw_ref, w_down_ref, f0_ref, f1_ref, f2_ref, gf_ref,
                op_ref, os_ref, fp_out_ref, fs_out_ref, w_up16, w_down16, up_scr, act_scr,
                *, final_norm, n_load, n_tiles, t):
    i = pl.program_id(0)
    tm = xp_ref.shape[0]
    n_chunks = D_FF // FFN_COLS
    slabs_per_chunk = FFN_COLS // LANES
    n_slabs = 2 * D_FF // LANES

    @pl.when(i < n_load)
    def _():
        w_up16[i] = w_up_ref[...].astype(BF16)
        w_down16[i] = w_down_ref[...].astype(BF16)

    def up_weights(c0):
        return w_up16[c0 // FFN_LOAD_COLS, :, c0 % FFN_LOAD_COLS:c0 % FFN_LOAD_COLS + FFN_COLS]

    def conv_slab(j, rows):
        cols = slice(j * LANES, (j + 1) * LANES)
        return (cw_ref[0:1, cols] * up_scr[j, SUBLANES - 2:SUBLANES - 2 + rows, :]
                + cw_ref[1:2, cols] * up_scr[j, SUBLANES - 1:SUBLANES - 1 + rows, :]
                + cw_ref[2:3, cols] * up_scr[j, SUBLANES:SUBLANES + rows, :])

    def run(h16, rows, put_up, after_conv):
        def project(c):
            for off in (0, D_FF):
                c0 = off + c * FFN_COLS
                put_up(c0, _dot(h16, up_weights(c0)))

        def activate(c):
            for j in range(slabs_per_chunk):
                ja = c * slabs_per_chunk + j
                a = conv_slab(ja, rows)
                b = conv_slab(D_FF // LANES + ja, rows)
                after_conv(ja)
                after_conv(D_FF // LANES + ja)
                act_scr[0:rows, ja * LANES:(ja + 1) * LANES] = (_silu(a) * b).astype(BF16)

        project(0)
        for c in range(n_chunks):
            if c + 1 < n_chunks:
                project(c + 1)
            activate(c)
        return _dot(act_scr[0:rows, :], w_down16[...].reshape(D_FF, D_MODEL))

    @pl.when(jnp.logical_and(i >= n_load, i < n_load + n_tiles))
    def _():
        @pl.when(i == n_load)
        def _():
            for j in range(n_slabs):
                up_scr[j, 0:SUBLANES, :] = f0_ref[:, j * LANES:(j + 1) * LANES]

        x = xp_ref[...]
        h16 = _norm_mod(x, g_ref[...], shp_ref[0:1, :], scp_ref[0:1, :]).astype(BF16)

        def put_up(c0, up):
            for j in range(slabs_per_chunk):
                up_scr[c0 // LANES + j, SUBLANES:SUBLANES + tm, :] = up[:, j * LANES:(j + 1) * LANES]
            fp_out_ref[:, c0:c0 + FFN_COLS] = up[tm - SUBLANES:, :]

        def keep_tail(j):
            up_scr[j, 0:SUBLANES, :] = up_scr[j, tm:tm + SUBLANES, :]

        out = x + gatep_ref[0:1, :] * run(h16, tm, put_up, keep_tail)
        if final_norm:
            out = _rmsnorm(out, gf_ref[...])
        op_ref[...] = out

    @pl.when(i == n_load + n_tiles)
    def _():
        x = xs_ref[...]
        n_streams = x.shape[0] // t
        seg = SUBLANES + t
        rows = n_streams * seg
        h16 = _norm_mod(x, g_ref[...], _stream_rows(shs_ref[...], t), _stream_rows(scs_ref[...], t)).astype(BF16)
        for j in range(n_slabs):
            cols = slice(j * LANES, (j + 1) * LANES)
            for b in range(n_streams):
                up_scr[j, b * seg + SUBLANES - 2:b * seg + SUBLANES - 1, :] = f1_ref[b:b + 1, cols]
                up_scr[j, b * seg + SUBLANES - 1:b * seg + SUBLANES, :] = f2_ref[b:b + 1, cols]

        def put_up(c0, up):
            for b in range(n_streams):
                for j in range(slabs_per_chunk):
                    up_scr[c0 // LANES + j, b * seg + SUBLANES:(b + 1) * seg, :] = \
                        up[b * t:(b + 1) * t, j * LANES:(j + 1) * LANES]
                fs_out_ref[b, :, c0:c0 + FFN_COLS] = up[(b + 1) * t - SUBLANES:(b + 1) * t, :]

        y = run(h16, rows, put_up, lambda j: None)
        y = jnp.concatenate([y[b * seg:b * seg + t, :] for b in range(n_streams)], axis=0)
        out = x + _stream_rows(gates_ref[...], t) * y
        if final_norm:
            out = _rmsnorm(out, gf_ref[...])
        os_ref[...] = out


def _ffn(xp, xs, mod, l, norm_g, w_up, cw, w_down, f0, f1, f2, gf, final_norm, t):
    seq, rows = xp.shape[0], xs.shape[0]
    n_streams = rows // t
    tm = ROW_TILE
    n_tiles = seq // tm
    n_load = 2 * D_FF // FFN_LOAD_COLS
    load_rows = D_FF // n_load
    assert n_load * FFN_LOAD_COLS == 2 * D_FF and load_rows % (2 * SUBLANES) == 0 and n_streams * (SUBLANES + t) <= tm

    def tile(i):
        return jnp.clip(i - n_load, 0, n_tiles - 1)

    return pl.pallas_call(
        functools.partial(_ffn_kernel, final_norm=final_norm, n_load=n_load, n_tiles=n_tiles, t=t),
        out_shape=(jax.ShapeDtypeStruct((seq, D_MODEL), F32), jax.ShapeDtypeStruct((rows, D_MODEL), F32),
                   jax.ShapeDtypeStruct((SUBLANES, 2 * D_FF), F32),
                   jax.ShapeDtypeStruct((n_streams, SUBLANES, 2 * D_FF), F32)),
        grid=(n_load + n_tiles + 1,),
        in_specs=[pl.BlockSpec((tm, D_MODEL), lambda i: (tile(i), 0)), _whole((rows, D_MODEL)), _layer((1, D_MODEL), l)]
        + [_mod_prompt(l, 3 + j, n_streams) for j in range(3)] + [_mod_streams(l, 3 + j, n_streams) for j in range(3)]
        + [pl.BlockSpec((None, D_MODEL, FFN_LOAD_COLS), lambda i: (l, 0, jnp.minimum(i, n_load - 1))),
           _layer((3, 2 * D_FF), l),
           pl.BlockSpec((None, load_rows, D_MODEL), lambda i: (l, jnp.minimum(i, n_load - 1), 0)),
           _whole((SUBLANES, 2 * D_FF)), _layer((n_streams, 2 * D_FF), l), _layer((n_streams, 2 * D_FF), l),
           _whole((1, D_MODEL))],
        out_specs=(pl.BlockSpec((tm, D_MODEL), lambda i: (tile(i), 0)), _whole_out((rows, D_MODEL)),
                   _whole_out((SUBLANES, 2 * D_FF)), _whole_out((n_streams, SUBLANES, 2 * D_FF))),
        scratch_shapes=[pltpu.VMEM((n_load, D_MODEL, FFN_LOAD_COLS), BF16), pltpu.VMEM((n_load, load_rows, D_MODEL), BF16),
                        pltpu.VMEM((2 * D_FF // LANES, SUBLANES + tm, LANES), F32), pltpu.VMEM((tm, D_FF), BF16)],
        compiler_params=_params(),
        name="ffn",
    )(xp, xs, norm_g, mod, mod, mod, mod, mod, mod, w_up, cw, w_down, f0, f1, f2, gf)


def _rotary_triplet(pos):
    half = DK_RET // 2
    inv = ROPE_BASE ** (-jnp.arange(half, dtype=F32) / half)
    ang = pos.astype(F32)[:, None] * inv[None, :]
    cos, sin = jnp.cos(ang), jnp.sin(ang)
    return jnp.stack([jnp.concatenate([cos] * 4, axis=-1), jnp.concatenate([sin] * 4, axis=-1),
                      jnp.concatenate([-sin, sin, -sin, sin], axis=-1)])


def _retention_tables(tb, n_streams=1):
    idx = np.arange(tb, dtype=np.float64)
    diff = idx[:, None] - idx[None, :]
    dec1 = np.where(diff[None] >= 0, np.exp(LOG_G[:, None, None] * np.maximum(diff, 0.0)[None]), 0.0)
    dec = np.zeros((H_RET, n_streams * tb, n_streams * tb))
    for b in range(n_streams):
        dec[:, b * tb:(b + 1) * tb, b * tb:(b + 1) * tb] = dec1
    xi = np.tile(np.repeat(np.exp(LOG_G[:, None] * (idx + 1)[None, :]).T, DK_RET, axis=1), (n_streams, 1))
    zeta_t = np.tile(np.repeat(np.exp(LOG_G[:, None] * (tb - 1 - idx)[None, :]), DK_RET, axis=0), (1, n_streams))
    gmat = np.zeros((HEAD_PAIRS, LANES, LANES))
    for h in range(H_RET):
        o = (h % 2) * DK_RET
        gmat[h // 2, o:o + DK_RET, o:o + DK_RET] = np.exp(LOG_G[h] * tb)
    gmat = np.tile(gmat, (1, 1, n_streams))
    return tuple(jnp.asarray(a, F32) for a in (dec, xi, zeta_t, gmat))


def _pair_state(s):
    lead = s.shape[:-3]
    s = s.reshape(lead + (HEAD_PAIRS, 2, DK_RET, DK_RET))
    z = jnp.zeros_like(s[..., 0, :, :])
    top = jnp.concatenate([s[..., 0, :, :], z], axis=-1)
    bot = jnp.concatenate([z, s[..., 1, :, :]], axis=-1)
    return jnp.concatenate([top, bot], axis=-2)


def _unpair_state(r):
    a = r[..., :DK_RET, :DK_RET]
    b = r[..., DK_RET:, DK_RET:]
    s = jnp.stack([a, b], axis=-3)
    return s.reshape(r.shape[:-3] + (H_RET, DK_RET, DK_RET))


def _band_bias(table):
    nq, nk = 2 * CHUNK, BAND2
    period = nq + nk
    j = np.arange(period)
    j = np.where(j < nk, j, j - period)
    idx = np.clip(HIST - j, -(CHUNK - 1), REL_CLIP) + (CHUNK - 1)
    one_period = table[:, idx].astype(F32)
    flat = jnp.tile(one_period, (1, nq + 1))[:, :nq * (period - 1)]
    return flat.reshape(-1, nq, period - 1)[:, :, :nk]


def _band_bias_t(raw):
    qq = np.arange(2 * CHUNK)[:, None]
    kk = np.arange(BAND2)[None, :]
    valid = np.where(qq < CHUNK, kk < BAND, kk >= CHUNK)
    b = jnp.where(valid, raw * LOG2E, NEG_INF).reshape(HEAD_PAIRS, 2, 2, CHUNK, BAND2)
    return jnp.transpose(b, (0, 4, 2, 1, 3)).reshape(HEAD_PAIRS, BAND2, 2 * LANES)


def _tail_rows(a, n):
    return a[..., a.shape[-2] - n:, :]


def kernel(x_prompt, x_sample, state_ret, state_sconv, state_pool, cache_k, cache_v, state_ffn, c_prompt, c_sample,
           norm_mix, norm_ffn, norm_final, w_ada, b_ada, w_in_even, w_out_even, ret_gn_gain, sconv_w, w_in_odd,
           w_out_odd, pool_w, pool_scale, rel_bias_table, ffn_w_up, ffn_conv, ffn_w_down):
    n_prompt, seq, _ = x_prompt.shape
    n_streams, t_s, _ = x_sample.shape
    assert n_prompt == 1 and n_streams % MOD_ROWS_PROMPT == 0
    rows_s = n_streams * t_s
    n_even, n_odd = (DEPTH + 1) // 2, DEPTH // 2

    c_all = jnp.concatenate([c_sample, c_prompt], axis=0)
    mod = _ada(jnp.pad(c_all, ((0, MOD_ROWS_PROMPT - 1), (0, 0))), w_ada, b_ada)

    bf = lambda w: w.astype(BF16)
    w_in_even16, w_out_even16 = bf(w_in_even), bf(w_out_even)
    w_in_odd16, w_out_odd16, pool_w16 = bf(w_in_odd), bf(w_out_odd), bf(pool_w)
    norm_mix3, norm_ffn3 = norm_mix.reshape(DEPTH, 1, D_MODEL), norm_ffn.reshape(DEPTH, 1, D_MODEL)
    gain3, pool_scale3 = ret_gn_gain.reshape(n_even, 1, D_RET), pool_scale.reshape(n_odd, 1, D_POOL)
    norm_final2 = norm_final.reshape(1, D_MODEL)

    n_tiles = seq // ROW_TILE
    tabs_p = (_rotary_triplet(jnp.arange(n_tiles, dtype=jnp.int32) * ROW_TILE),
              _rotary_triplet(jnp.arange(ROW_TILE, dtype=jnp.int32))) + _retention_tables(RET_BLOCK)
    rot_s = _rotary_triplet(PAST_LEN + jnp.arange(t_s, dtype=jnp.int32))
    tabs_s = (jnp.tile(rot_s[0], (n_streams, 1)), jnp.tile(rot_s[2], (n_streams, 1))) \
        + _retention_tables(t_s, n_streams)

    cache_len = cache_k.shape[2]
    assert cache_len == HIST and t_s <= CHUNK
    bias_raw = [_band_bias(rel_bias_table[i]) for i in range(n_odd)]
    bias_p = [_band_bias_t(b) for b in bias_raw]
    bias_c = [b[:, :t_s, :cache_len].reshape(H_ATT * t_s, cache_len) for b in bias_raw]
    bias_n = [jnp.tile(b[:, :t_s, cache_len:cache_len + t_s], (1, 1, n_streams)).reshape(H_ATT * t_s, rows_s)
              for b in bias_raw]

    paired = _pair_state(state_ret)
    s_stack = jnp.transpose(paired, (0, 2, 1, 3, 4)).reshape(n_even, HEAD_PAIRS, n_streams * LANES, LANES)
    s_wide = jnp.transpose(paired, (0, 2, 3, 1, 4)).reshape(n_even, HEAD_PAIRS, LANES, n_streams * LANES)
    u1, u2 = state_sconv[:, :, 0, :], state_sconv[:, :, 1, :]
    f1, f2 = state_ffn[:, :, 0, :], state_ffn[:, :, 1, :]
    p0_s = jnp.pad(state_pool, ((0, 0), (0, 0), (POOL_BASE - POOL_BUF, 0), (0, 0)))
    kc16 = cache_k.reshape(n_odd, n_streams, cache_len, D_ATT).astype(BF16)
    vc16 = cache_v.reshape(n_odd, n_streams, cache_len, D_ATT).astype(BF16)

    xp = x_prompt.reshape(seq, D_MODEL)
    xs = x_sample.reshape(rows_s, D_MODEL)

    ret_p, ret_s, sconv_p, sconv_s, pool_p, pool_s = [], [], [], [], [], []
    k_p, k_s, v_p, v_s, ffn_p, ffn_s = [], [], [], [], [], []
    for l in range(DEPTH):
        i = l // 2
        if l % 2 == 0:
            xp, r_new, u_new = _even_prompt(
                xp, mod, l, n_streams, norm_mix3, w_in_even16, w_out_even16, gain3, sconv_w, tabs_p,
                jnp.zeros((HEAD_PAIRS, LANES, LANES), F32), jnp.zeros((SUBLANES, D_SCONV), F32))
            ret_p.append(_unpair_state(r_new)[None])
            sconv_p.append(_tail_rows(u_new, 2)[None])
            xs, s_new, u_all = _even_sample(
                xs, mod, l, norm_mix3, w_in_even16, w_out_even16, gain3, sconv_w, tabs_s, s_stack, s_wide, u1, u2, t_s)
            s_new = jnp.transpose(s_new.reshape(HEAD_PAIRS, LANES, n_streams, LANES), (2, 0, 1, 3))
            ret_s.append(_unpair_state(s_new))
            sconv_s.append(_tail_rows(u_all.reshape(n_streams, t_s, D_SCONV), 2))
        else:
            xp, p_new, k_new, v_new = _odd_prompt(
                xp, mod, l, n_streams, norm_mix3, w_in_odd16, w_out_odd16, pool_w16, pool_scale3, bias_p[i],
                jnp.zeros((POOL_BASE, D_POOL), F32))
            pool_p.append(_tail_rows(p_new, POOL_BUF)[None])
            k_p.append(k_new.reshape(1, HIST, H_ATT, DH_ATT))
            v_p.append(v_new.reshape(1, HIST, H_ATT, DH_ATT))
            xs, p_new, k_new, v_new = _odd_sample(
                xs, mod, l, norm_mix3, w_in_odd16, w_out_odd16, pool_w16, pool_scale3, bias_c[i], bias_n[i],
                p0_s, kc16, vc16, t_s, PAST_LEN)
            pool_s.append(_tail_rows(p_new, POOL_BUF))
            k_s.append(k_new.reshape(n_streams, t_s, H_ATT, DH_ATT))
            v_s.append(v_new.reshape(n_streams, t_s, H_ATT, DH_ATT))
        last = l == DEPTH - 1
        xp, xs, f_new, fs_new = _ffn(xp, xs, mod, l, norm_ffn3, ffn_w_up, ffn_conv, ffn_w_down,
                                     jnp.zeros((SUBLANES, 2 * D_FF), F32), f1, f2, norm_final2, last, t_s)
        ffn_p.append(_tail_rows(f_new, 2)[None])
        ffn_s.append(_tail_rows(fs_new, 2))

    st = jnp.stack
    return (xp.reshape(1, seq, D_MODEL), xs.reshape(n_streams, t_s, D_MODEL),
            st(ret_p), st(ret_s), st(sconv_p), st(sconv_s), st(pool_p), st(pool_s),
            st(k_p), st(k_s), st(v_p), st(v_s), st(ffn_p), st(ffn_s))
```

```python
import functools

import numpy as np
import jax
import jax.numpy as jnp
from jax import lax
from jax.experimental import pallas as pl
from jax.experimental.pallas import tpu as pltpu

F32 = jnp.float32
BF16 = jnp.bfloat16

D_MODEL = 1024
DEPTH = 4
PAST_LEN = 4096
CHUNK = 64
H_RET = 8
DK_RET = 64
D_RET = H_RET * DK_RET
ROPE_BASE = 10000.0
D_SCONV = D_MODEL - D_RET
POOL_WINDOWS = (2, 4, 8, 16)
D_POOL = D_MODEL // 2
POOL_GROUP = D_POOL // len(POOL_WINDOWS)
POOL_BUF = max(POOL_WINDOWS) - 1
H_ATT = 8
DH_ATT = 64
D_ATT = H_ATT * DH_ATT
N_PREV_CHUNKS = 8
REL_CLIP = 256
D_FF = 2816
EPS = 1e-6
NEG_INF = -1e30
D_IN_EVEN = 4 * D_RET + 3 * D_SCONV
D_IN_ODD = D_POOL + 3 * D_ATT

LANES = 128
SUBLANES = 8
HEAD_PAIRS = H_RET // 2
ROW_TILE = 512
RET_BLOCK = 128
FFN_COLS = 256
BAND = (N_PREV_CHUNKS + 1) * CHUNK
BAND2 = BAND + CHUNK
LOG2E = 1.4426950408889634
HIST = N_PREV_CHUNKS * CHUNK
POOL_BASE = 2 * SUBLANES
VMEM_LIMIT = 56 * 1024 * 1024

LOG_G = np.log1p(-(2.0 ** (-5.0 - np.arange(H_RET, dtype=np.float64))))


def _params(n_axes=1):
    return pltpu.CompilerParams(dimension_semantics=("arbitrary",) * n_axes, vmem_limit_bytes=VMEM_LIMIT)


def _whole(shape):
    nd = len(shape)
    return pl.BlockSpec(shape, lambda i: (0,) * nd, pipeline_mode=pl.Buffered(1))


def _whole_out(shape):
    nd = len(shape)
    return pl.BlockSpec(shape, lambda i: (0,) * nd)


def _layer(shape, l):
    nd = len(shape)
    return pl.BlockSpec((None,) + tuple(shape), lambda i: (l,) + (0,) * nd, pipeline_mode=pl.Buffered(1))


MOD_ROWS_PROMPT = SUBLANES


def _mod_prompt(l, j, n_streams):
    return pl.BlockSpec((None, MOD_ROWS_PROMPT, D_MODEL), lambda i: (l, n_streams // MOD_ROWS_PROMPT, j),
                        pipeline_mode=pl.Buffered(1))


def _mod_streams(l, j, n_streams):
    return pl.BlockSpec((None, n_streams, D_MODEL), lambda i: (l, 0, j), pipeline_mode=pl.Buffered(1))


def _stream_rows(m, t):
    return jnp.concatenate([jnp.broadcast_to(m[b:b + 1, :], (t, m.shape[1])) for b in range(m.shape[0])], axis=0)


def _dot(a, b):
    return jnp.dot(a, b, preferred_element_type=F32)


def _dot_nt(a, b):
    return lax.dot_general(a, b, (((1,), (1,)), ((), ())), preferred_element_type=F32)


def _rmsnorm(x, g):
    return x * lax.rsqrt(jnp.mean(x * x, axis=-1, keepdims=True) + EPS) * g


def _norm_mod(x, g, shift, scale):
    return _rmsnorm(x, g) * (1.0 + scale) + shift


def _silu(x):
    return x * (1.0 / (1.0 + jnp.exp(-x)))


def _low_half(shape):
    return (lax.broadcasted_iota(jnp.int32, shape, len(shape) - 1) % LANES) < DK_RET


def _shift_rows(u, prev8, s):
    rolled = pltpu.roll(u, s, axis=0)
    prolled = pltpu.roll(prev8, s, axis=0)
    row = lax.broadcasted_iota(jnp.int32, prev8.shape, 0)
    first = jnp.where(row < s, prolled, rolled[0:SUBLANES])
    return jnp.concatenate([first, rolled[SUBLANES:]], axis=0)


def _shift_rows_streams(u, older, newer, s, t):
    row_in_stream = lax.broadcasted_iota(jnp.int32, u.shape, 0) % t
    rolled = pltpu.roll(u, s, axis=0)
    if s == 1:
        return jnp.where(row_in_stream == 0, _stream_rows(newer, t), rolled)
    return jnp.where(row_in_stream == 0, _stream_rows(older, t),
                     jnp.where(row_in_stream == 1, _stream_rows(newer, t), rolled))


def _ada_kernel(c_ref, w_ref, b_ref, o_ref):
    c = c_ref[...]
    o_ref[0] = _dot(_silu(c).astype(BF16), w_ref[0].astype(BF16)) + b_ref[0]


def _ada(c_all, w_ada, b_ada):
    rows = c_all.shape[0]
    tn = 1536
    return pl.pallas_call(
        _ada_kernel,
        out_shape=jax.ShapeDtypeStruct((DEPTH, rows, 6 * D_MODEL), F32),
        grid=(DEPTH, 6 * D_MODEL // tn),
        in_specs=[
            pl.BlockSpec((rows, D_MODEL), lambda l, j: (0, 0)),
            pl.BlockSpec((1, D_MODEL, tn), lambda l, j: (l, 0, j)),
            pl.BlockSpec((1, 1, tn), lambda l, j: (l, 0, j)),
        ],
        out_specs=pl.BlockSpec((1, rows, tn), lambda l, j: (l, 0, j)),
        compiler_params=_params(2),
        name="ada_mod",
    )(c_all, w_ada, b_ada.reshape(DEPTH, 1, 6 * D_MODEL))


def _rotary_pair(x, cos, sin_signed):
    lane = lax.broadcasted_iota(jnp.int32, x.shape, 1)
    first_half = (lane % DK_RET) < (DK_RET // 2)
    swapped = jnp.where(first_half, pltpu.roll(x, LANES - DK_RET // 2, axis=1),
                        pltpu.roll(x, DK_RET // 2, axis=1))
    return x * cos + swapped * sin_signed


def _retention_block(proj, cos, sin_signed, dec_ref, xi, zeta_t_ref, gain, cross_fn, update_fn):
    tb = proj.shape[0]
    low = _low_half((tb, LANES))
    inv_n = 1.0 / DK_RET
    pairs = range(HEAD_PAIRS)
    cols = [slice(p * LANES, (p + 1) * LANES) for p in pairs]
    q = [_rotary_pair(proj[:, cols[p]], cos, sin_signed) for p in pairs]
    k_t = [(_rotary_pair(proj[:, D_RET + p * LANES:D_RET + (p + 1) * LANES], cos, sin_signed)
            * (DK_RET ** -0.5)).T for p in pairs]
    v16 = [proj[:, 2 * D_RET + p * LANES:2 * D_RET + (p + 1) * LANES].astype(BF16) for p in pairs]
    k_t16 = [k_t[p].astype(BF16) for p in pairs]
    kz_t16 = [(k_t[p] * zeta_t_ref[cols[p], :]).astype(BF16) for p in pairs]
    q16 = [q[p].astype(BF16) for p in pairs]
    qe16 = [jnp.where(low, q[p], 0.0).astype(BF16) for p in pairs]
    qo16 = [jnp.where(low, 0.0, q[p]).astype(BF16) for p in pairs]
    s_e = [(_dot(qe16[p], k_t16[p]) * dec_ref[2 * p]).astype(BF16) for p in pairs]
    s_o = [(_dot(qo16[p], k_t16[p]) * dec_ref[2 * p + 1]).astype(BF16) for p in pairs]
    cross = [cross_fn(p, q16[p]) * xi[:, cols[p]] for p in pairs]
    o = [jnp.where(low, _dot(s_e[p], v16[p]), _dot(s_o[p], v16[p])) + cross[p] for p in pairs]
    for p in pairs:
        update_fn(p, kz_t16[p], v16[p])
    outs = []
    for p in pairs:
        s_lo = jnp.sum(jnp.where(low, o[p], 0.0), axis=-1, keepdims=True)
        s_hi = jnp.sum(jnp.where(low, 0.0, o[p]), axis=-1, keepdims=True)
        d = o[p] - jnp.where(low, s_lo, s_hi) * inv_n
        d2 = d * d
        v_lo = jnp.sum(jnp.where(low, d2, 0.0), axis=-1, keepdims=True)
        v_hi = jnp.sum(jnp.where(low, 0.0, d2), axis=-1, keepdims=True)
        on = d * lax.rsqrt(jnp.where(low, v_lo, v_hi) * inv_n + EPS)
        g = proj[:, 3 * D_RET + p * LANES:3 * D_RET + (p + 1) * LANES]
        outs.append(_silu(g) * (on * gain[:, cols[p]]))
    return jnp.concatenate(outs, axis=-1)


def _even_tail(x, proj, ret_out, conv_in_shift, cw_ref, gate, w_out16):
    gate_b = proj[:, 4 * D_RET:4 * D_RET + D_SCONV]
    u = proj[:, 4 * D_RET + D_SCONV:4 * D_RET + 2 * D_SCONV] * proj[:, 4 * D_RET + 2 * D_SCONV:]
    conv = cw_ref[0:1, :] * conv_in_shift(u, 2) + cw_ref[1:2, :] * conv_in_shift(u, 1) + cw_ref[2:3, :] * u
    mixed = jnp.concatenate([ret_out, gate_b * conv], axis=-1).astype(BF16)
    return x + gate * _dot(mixed, w_out16), u


EVEN_LOAD_COLS = 512
EVEN_OUT_LOAD_ROWS = 256


def _even_kernel(xp_ref, xs_ref, g_ref, shp_ref, scp_ref, gatep_ref, shs_ref, scs_ref, gates_ref,
                 w_in_ref, w_out_ref, gain_ref, cw_ref,
                 rot_tile_ref, rot_row_ref, dec_ref, xi_ref, zt_ref, gmat_ref, r0_ref, u0_ref,
                 cos_s_ref, sin_s_ref, dec_s_ref, xi_s_ref, zt_s_ref, gmat_s_ref, s_stack_ref, s_wide_ref, u1_ref, u2_ref,
                 op_ref, os_ref, r_out_ref, u_out_ref, s_out_ref, us_out_ref,
                 w_in16, w_out16, r_scr, u_scr, *, n_load, n_tiles, t):
    i = pl.program_id(0)
    n_out_load = w_out16.shape[0]

    @pl.when(i < n_load)
    def _():
        w_in16[i] = w_in_ref[...].astype(BF16)

    @pl.when(i < n_out_load)
    def _():
        w_out16[i] = w_out_ref[...].astype(BF16)

    def in_project(h16):
        return jnp.concatenate([_dot(h16, w_in16[c]) for c in range(n_load)], axis=-1)

    def out_weights():
        return w_out16[...].reshape(D_MODEL, D_MODEL)

    @pl.when(jnp.logical_and(i >= n_load, i < n_load + n_tiles))
    def _():
        tile = i - n_load
        cos_0, sin_0, ssin_0 = (rot_tile_ref[j, pl.ds(tile, 1), :] for j in range(3))
        cos = cos_0 * rot_row_ref[0] - sin_0 * rot_row_ref[1]
        sin_signed = ssin_0 * rot_row_ref[0] + cos_0 * rot_row_ref[2]

        @pl.when(tile == 0)
        def _():
            r_scr[...] = r0_ref[...]
            u_scr[...] = u0_ref[...]

        x = xp_ref[...]
        h = _norm_mod(x, g_ref[...], shp_ref[0:1, :], scp_ref[0:1, :]).astype(BF16)
        proj = in_project(h)
        tm = x.shape[0]
        tb = dec_ref.shape[1]
        r_i = lax.broadcasted_iota(jnp.int32, (LANES, LANES), 0) < DK_RET
        c_i = lax.broadcasted_iota(jnp.int32, (LANES, LANES), 1) < DK_RET
        blockdiag = r_i == c_i

        def cross_fn(p, q16):
            return _dot(q16, r_scr[p].astype(BF16))

        def update_fn(p, kz_t16, v16):
            r_scr[p] = r_scr[p] * gmat_ref[p] + jnp.where(blockdiag, _dot(kz_t16, v16), 0.0)

        rets = []
        for r in range(tm // tb):
            rows = slice(r * tb, (r + 1) * tb)
            rets.append(_retention_block(proj[rows, :], cos[rows, :], sin_signed[rows, :], dec_ref, xi_ref[...],
                                         zt_ref, gain_ref[...], cross_fn, update_fn))
        ret_out = jnp.concatenate(rets, axis=0)

        prev8 = u_scr[...]
        out, u = _even_tail(x, proj, ret_out, lambda u, s: _shift_rows(u, prev8, s), cw_ref, gatep_ref[0:1, :],
                            out_weights())
        u_scr[...] = u[tm - SUBLANES:, :]
        op_ref[...] = out
        r_out_ref[...] = r_scr[...]
        u_out_ref[...] = u[tm - SUBLANES:, :]

    @pl.when(i == n_load + n_tiles)
    def _():
        x = xs_ref[...]
        rows = x.shape[0]
        n_streams = rows // t
        wide = n_streams * LANES
        h = _norm_mod(x, g_ref[...], _stream_rows(shs_ref[...], t), _stream_rows(scs_ref[...], t)).astype(BF16)
        proj = in_project(h)
        own = (lax.broadcasted_iota(jnp.int32, (rows, wide), 0) // t
               == lax.broadcasted_iota(jnp.int32, (rows, wide), 1) // LANES)
        r_i = lax.broadcasted_iota(jnp.int32, (LANES, wide), 0) < DK_RET
        blockdiag = r_i == _low_half((LANES, wide))

        def expand(a16):
            tiled = jnp.concatenate([a16.astype(F32)] * n_streams, axis=-1)
            return jnp.where(own, tiled, 0.0).astype(BF16)

        def cross_fn(p, q16):
            return _dot(expand(q16), s_stack_ref[p].astype(BF16))

        def update_fn(p, kz_t16, v16):
            kv = _dot(kz_t16, expand(v16))
            decay = jnp.concatenate([gmat_s_ref[p]] * n_streams, axis=-1)
            s_out_ref[p] = s_wide_ref[p] * decay + jnp.where(blockdiag, kv, 0.0)

        ret_out = _retention_block(proj, cos_s_ref[...], sin_s_ref[...], dec_s_ref, xi_s_ref[...], zt_s_ref,
                                   gain_ref[...], cross_fn, update_fn)
        out, u = _even_tail(x, proj, ret_out, lambda u, s: _shift_rows_streams(u, u1_ref[...], u2_ref[...], s, t),
                            cw_ref, _stream_rows(gates_ref[...], t), out_weights())
        os_ref[...] = out
        us_out_ref[...] = u


def _even(xp, xs, mod, l, norm_g, w_in, w_out, gain, cw, tabs_p, tabs_s, r0, u0, s_stack, s_wide, u1, u2, t):
    seq, rows = xp.shape[0], xs.shape[0]
    n_streams = rows // t
    wide = n_streams * LANES
    tm, tb = ROW_TILE, RET_BLOCK
    n_tiles = seq // tm
    n_load = D_IN_EVEN // EVEN_LOAD_COLS
    n_out_load = D_MODEL // EVEN_OUT_LOAD_ROWS
    assert n_load * EVEN_LOAD_COLS == D_IN_EVEN and n_out_load <= n_load
    rot_tile, rot_row, dec, xi, zt, gmat = tabs_p
    cos_s, sin_s, dec_s, xi_s, zt_s, gmat_s = tabs_s
    li = l // 2
    state = (HEAD_PAIRS, LANES, LANES)

    def tile(i):
        return jnp.clip(i - n_load, 0, n_tiles - 1)

    return pl.pallas_call(
        functools.partial(_even_kernel, n_load=n_load, n_tiles=n_tiles, t=t),
        out_shape=(jax.ShapeDtypeStruct((seq, D_MODEL), F32), jax.ShapeDtypeStruct((rows, D_MODEL), F32),
                   jax.ShapeDtypeStruct(state, F32), jax.ShapeDtypeStruct((SUBLANES, D_SCONV), F32),
                   jax.ShapeDtypeStruct((HEAD_PAIRS, LANES, wide), F32), jax.ShapeDtypeStruct((rows, D_SCONV), F32)),
        grid=(n_load + n_tiles + 1,),
        in_specs=[pl.BlockSpec((tm, D_MODEL), lambda i: (tile(i), 0)), _whole((rows, D_MODEL)), _layer((1, D_MODEL), l)]
        + [_mod_prompt(l, j, n_streams) for j in range(3)] + [_mod_streams(l, j, n_streams) for j in range(3)]
        + [pl.BlockSpec((None, D_MODEL, EVEN_LOAD_COLS), lambda i: (li, 0, jnp.minimum(i, n_load - 1))),
           pl.BlockSpec((None, EVEN_OUT_LOAD_ROWS, D_MODEL), lambda i: (li, jnp.minimum(i, n_out_load - 1), 0)),
           _layer((1, D_RET), li), _layer((3, D_SCONV), li),
           _whole((3, n_tiles, LANES)), _whole((3, tm, LANES)),
           _whole((H_RET, tb, tb)), _whole((tb, D_RET)), _whole((D_RET, tb)),
           _whole(state), _whole(state), _whole((SUBLANES, D_SCONV)),
           _whole((rows, LANES)), _whole((rows, LANES)),
           _whole((H_RET, rows, rows)), _whole((rows, D_RET)), _whole((D_RET, rows)), _whole(state),
           _layer((HEAD_PAIRS, wide, LANES), li), _layer((HEAD_PAIRS, LANES, wide), li),
           _layer((n_streams, D_SCONV), li), _layer((n_streams, D_SCONV), li)],
        out_specs=(pl.BlockSpec((tm, D_MODEL), lambda i: (tile(i), 0)), _whole_out((rows, D_MODEL)),
                   _whole_out(state), _whole_out((SUBLANES, D_SCONV)),
                   _whole_out((HEAD_PAIRS, LANES, wide)), _whole_out((rows, D_SCONV))),
        scratch_shapes=[pltpu.VMEM((n_load, D_MODEL, EVEN_LOAD_COLS), BF16),
                        pltpu.VMEM((n_out_load, EVEN_OUT_LOAD_ROWS, D_MODEL), BF16),
                        pltpu.VMEM(state, F32), pltpu.VMEM((SUBLANES, D_SCONV), F32)],
        compiler_params=_params(),
        name="even_mixer",
    )(xp, xs, norm_g, mod, mod, mod, mod, mod, mod, w_in, w_out, gain, cw,
      rot_tile, rot_row, dec, xi, zt, gmat, r0, u0, cos_s, sin_s, dec_s, xi_s, zt_s, gmat_s, s_stack, s_wide, u1, u2)


def _pool_put(hist_ref, row0, block):
    for gi in range(len(POOL_WINDOWS)):
        hist_ref[gi, row0:row0 + block.shape[0], :] = block[:, gi * POOL_GROUP:(gi + 1) * POOL_GROUP]


def _pool_get(hist_ref, row0, rows):
    return jnp.concatenate([hist_ref[gi, row0:row0 + rows, :] for gi in range(len(POOL_WINDOWS))], axis=-1)


def _pool(hist_ref, p, pos, pool_w_ref, scale):
    t = p.shape[0]
    outs = []
    for gi, w in enumerate(POOL_WINDOWS):
        cols = slice(gi * POOL_GROUP, (gi + 1) * POOL_GROUP)
        win = p[:, cols]
        for d in range(1, w):
            win = win + hist_ref[gi, POOL_BASE - d:POOL_BASE - d + t, :]
        inv_cnt = 1.0 / jnp.minimum(pos + 1, w).astype(F32)
        pooled = win * inv_cnt - p[:, cols]
        outs.append(_dot(pooled.astype(BF16), pool_w_ref[gi]) * scale[:, cols])
    return jnp.concatenate(outs, axis=-1)


def _attend_scores(kbs, q_as, q_bs, biases):
    low = _low_half((CHUNK, LANES))
    out = []
    for kb, q_a, q_b, bias in zip(kbs, q_as, q_bs, biases):
        qs = jnp.concatenate([jnp.where(low, q_a, 0.0), jnp.where(low, 0.0, q_a),
                              jnp.where(low, q_b, 0.0), jnp.where(low, 0.0, q_b)], axis=0)
        qbd = qs.T.astype(BF16)
        half = (kb.shape[0] // 2) // (2 * SUBLANES) * (2 * SUBLANES)
        out.append(jnp.concatenate([_dot(kb[:half], qbd), _dot(kb[half:], qbd)], axis=0) + bias)
    return out


def _attend_values(scores, vts):
    low = _low_half((CHUNK, LANES))
    n = len(scores)
    e = [jnp.exp2(s - jnp.max(s, axis=0, keepdims=True)) for s in scores]
    inv_l = [1.0 / jnp.sum(e[j], axis=0, keepdims=True) for j in range(n)]
    e16 = [e[j].astype(BF16) for j in range(n)]
    o_t = [jnp.concatenate([_dot(vts[j][:DK_RET], e16[j]), _dot(vts[j][DK_RET:], e16[j])], axis=0) * inv_l[j]
           for j in range(n)]
    o_t = [o.T for o in o_t]
    return [(jnp.where(low, o[0:CHUNK], o[CHUNK:2 * CHUNK]), jnp.where(low, o[2 * CHUNK:3 * CHUNK], o[3 * CHUNK:]))
            for o in o_t]


ODD_LOAD_COLS = 512
SAMPLE_STREAMS_PER_STEP = 4


def _odd_kernel(xp_ref, xs_ref, g_ref, shp_ref, scp_ref, gatep_ref, shs_ref, scs_ref, gates_ref,
                w_in_ref, w_out_ref, pw_ref, ps_ref, bias_ref, p0_ref, bias_c_ref, bias_n_ref, p0s_ref, kc_ref, vc_ref,
                op_ref, os_ref, p_out_ref, k_out_ref, v_out_ref, ps_out_ref, ks_out_ref, vs_out_ref,
                w_in16, w_out16, pbuf, kbuf, vtbuf, q_scr, att_scr, proj_scr, mix_scr, pbuf_s,
                *, n_load, n_tiles, pos0, t):
    i = pl.program_id(0)
    tm = xp_ref.shape[0]
    rows = xs_ref.shape[0]
    per_step = kc_ref.shape[0]

    @pl.when(i < n_load)
    def _():
        w_in16[i] = w_in_ref[...].astype(BF16)
        w_out16[i] = w_out_ref[...].astype(BF16)

    def in_project(h16):
        return jnp.concatenate([_dot(h16, w_in16[c]) for c in range(n_load)], axis=-1)

    def out_weights():
        return w_out16[...].reshape(D_MODEL, D_MODEL)

    @pl.when(jnp.logical_and(i >= n_load, i < n_load + n_tiles))
    def _():
        tile = i - n_load

        @pl.when(tile == 0)
        def _():
            _pool_put(pbuf, 0, p0_ref[...])
            kbuf[0:HIST, :] = jnp.zeros((HIST, D_ATT), BF16)
            vtbuf[:, :, 0:HIST] = jnp.zeros((HEAD_PAIRS, LANES, HIST), BF16)

        x = xp_ref[...]
        h = _norm_mod(x, g_ref[...], shp_ref[0:1, :], scp_ref[0:1, :]).astype(BF16)
        proj = in_project(h)
        p = proj[:, :D_POOL]
        q_scr[...] = proj[:, D_POOL:D_POOL + D_ATT] * (DH_ATT ** -0.5 * LOG2E)
        k = proj[:, D_POOL + D_ATT:D_POOL + 2 * D_ATT]
        v = proj[:, D_POOL + 2 * D_ATT:]
        _pool_put(pbuf, POOL_BASE, p)
        kbuf[HIST:HIST + tm, :] = k.astype(BF16)
        for pr in range(HEAD_PAIRS):
            vtbuf[pr, :, HIST:HIST + tm] = v[:, pr * LANES:(pr + 1) * LANES].T.astype(BF16)
        k_out_ref[...] = k[tm - HIST:, :]
        v_out_ref[...] = v[tm - HIST:, :]

        pos = tile * tm + lax.broadcasted_iota(jnp.int32, (tm, 1), 0)
        pool_out = _pool(pbuf, p, pos, pw_ref, ps_ref[...])

        def attend_tile(first_tile):
            pairs = range(HEAD_PAIRS)
            lanes = [slice(pr * LANES, (pr + 1) * LANES) for pr in pairs]
            n_blocks = tm // (2 * CHUNK)
            skip = [max(HIST - jb * 2 * CHUNK, 0) if first_tile else 0 for jb in range(n_blocks)]

            def scores(jb):
                r0 = jb * 2 * CHUNK
                return _attend_scores([kbuf[r0 + skip[jb]:r0 + BAND2, lanes[pr]] for pr in pairs],
                                      [q_scr[r0:r0 + CHUNK, lanes[pr]] for pr in pairs],
                                      [q_scr[r0 + CHUNK:r0 + 2 * CHUNK, lanes[pr]] for pr in pairs],
                                      [bias_ref[pr, skip[jb]:, :] for pr in pairs])

            s_next = scores(0)
            for jb in range(n_blocks):
                r0 = jb * 2 * CHUNK
                s_cur = s_next
                if jb + 1 < n_blocks:
                    s_next = scores(jb + 1)
                outs = _attend_values(s_cur, [vtbuf[pr, :, r0 + skip[jb]:r0 + BAND2] for pr in pairs])
                att_scr[r0:r0 + CHUNK, :] = jnp.concatenate([o[0] for o in outs], axis=-1)
                att_scr[r0 + CHUNK:r0 + 2 * CHUNK, :] = jnp.concatenate([o[1] for o in outs], axis=-1)

        pl.when(tile == 0)(functools.partial(attend_tile, True))
        pl.when(tile > 0)(functools.partial(attend_tile, False))

        kbuf[0:HIST, :] = kbuf[tm:tm + HIST, :]
        vtbuf[:, :, 0:HIST] = vtbuf[:, :, tm:tm + HIST]
        tail = _pool_get(pbuf, tm, POOL_BASE)
        _pool_put(pbuf, 0, tail)
        p_out_ref[...] = tail

        mixed = jnp.concatenate([pool_out, att_scr[...]], axis=-1).astype(BF16)
        op_ref[...] = x + gatep_ref[0:1, :] * _dot(mixed, out_weights())

    @pl.when(i >= n_load + n_tiles)
    def _():
        step = i - (n_load + n_tiles)

        @pl.when(step == 0)
        def _():
            h = _norm_mod(xs_ref[...], g_ref[...], _stream_rows(shs_ref[...], t),
                          _stream_rows(scs_ref[...], t)).astype(BF16)
            proj = in_project(h)
            proj_scr[...] = proj
            ks_out_ref[...] = proj[:, D_POOL + D_ATT:D_POOL + 2 * D_ATT]
            vs_out_ref[...] = proj[:, D_POOL + 2 * D_ATT:]

        kn = proj_scr[:, D_POOL + D_ATT:D_POOL + 2 * D_ATT].astype(BF16)
        vn = proj_scr[:, D_POOL + 2 * D_ATT:].astype(BF16)
        pos = pos0 + lax.broadcasted_iota(jnp.int32, (t, 1), 0)
        head_of_lane = lax.broadcasted_iota(jnp.int32, (t, D_ATT), 1) // DH_ATT
        stream_of_col = lax.broadcasted_iota(jnp.int32, (H_ATT * t, rows), 1) // t
        for j in range(per_step):
            b = step * per_step + j
            r0 = pl.multiple_of(b * t, t)
            proj = proj_scr[pl.ds(r0, t), :]
            p = proj[:, :D_POOL]
            _pool_put(pbuf_s.at[j], 0, p0s_ref[j])
            _pool_put(pbuf_s.at[j], POOL_BASE, p)
            ps_out_ref[j] = p
            pool_out = _pool(pbuf_s.at[j], p, pos, pw_ref, ps_ref[...])

            q = proj[:, D_POOL:D_POOL + D_ATT] * (DH_ATT ** -0.5)
            q_heads = jnp.concatenate([jnp.where(head_of_lane == hh, q, 0.0) for hh in range(H_ATT)],
                                      axis=0).astype(BF16)
            s_c = _dot_nt(q_heads, kc_ref[j]) + bias_c_ref[...]
            s_n = jnp.where(stream_of_col == b, _dot_nt(q_heads, kn) + bias_n_ref[...], NEG_INF)
            m = jnp.maximum(jnp.max(s_c, axis=-1, keepdims=True), jnp.max(s_n, axis=-1, keepdims=True))
            e_c = jnp.exp(s_c - m)
            e_n = jnp.exp(s_n - m)
            inv_l = 1.0 / (jnp.sum(e_c, axis=-1, keepdims=True) + jnp.sum(e_n, axis=-1, keepdims=True))
            o_heads = (_dot(e_c.astype(BF16), vc_ref[j]) + _dot(e_n.astype(BF16), vn)) * inv_l
            att = jnp.where(head_of_lane == 0, o_heads[0:t], 0.0)
            for hh in range(1, H_ATT):
                att = jnp.where(head_of_lane == hh, o_heads[hh * t:(hh + 1) * t], att)
            mix_scr[pl.ds(r0, t), :] = jnp.concatenate([pool_out, att], axis=-1)

        @pl.when(i == pl.num_programs(0) - 1)
        def _():
            os_ref[...] = xs_ref[...] + _stream_rows(gates_ref[...], t) * _dot(mix_scr[...].astype(BF16), out_weights())


def _odd(xp, xs, mod, l, norm_g, w_in, w_out, pw16, ps, bias_t, p0, bias_c, bias_n, p0s, kc16, vc16, t, pos0):
    seq, rows = xp.shape[0], xs.shape[0]
    n_streams = rows // t
    cache = kc16.shape[2]
    tm = ROW_TILE
    assert tm == HIST and seq % tm == 0
    n_tiles = seq // tm
    n_load = D_IN_ODD // ODD_LOAD_COLS
    out_rows = D_MODEL // n_load
    per_step = SAMPLE_STREAMS_PER_STEP
    n_steps = n_streams // per_step
    assert n_load * ODD_LOAD_COLS == D_IN_ODD and n_streams % per_step == 0
    li = l // 2

    def tile(i):
        return jnp.clip(i - n_load, 0, n_tiles - 1)

    def step(i):
        return jnp.clip(i - n_load - n_tiles, 0, n_steps - 1)

    def streams(shape):
        nd = len(shape)
        return pl.BlockSpec((None, per_step) + tuple(shape), lambda i: (li, step(i)) + (0,) * nd)

    return pl.pallas_call(
        functools.partial(_odd_kernel, n_load=n_load, n_tiles=n_tiles, pos0=pos0, t=t),
        out_shape=(jax.ShapeDtypeStruct((seq, D_MODEL), F32), jax.ShapeDtypeStruct((rows, D_MODEL), F32),
                   jax.ShapeDtypeStruct((POOL_BASE, D_POOL), F32),
                   jax.ShapeDtypeStruct((HIST, D_ATT), F32), jax.ShapeDtypeStruct((HIST, D_ATT), F32),
                   jax.ShapeDtypeStruct((n_streams, t, D_POOL), F32),
                   jax.ShapeDtypeStruct((rows, D_ATT), F32), jax.ShapeDtypeStruct((rows, D_ATT), F32)),
        grid=(n_load + n_tiles + n_steps,),
        in_specs=[pl.BlockSpec((tm, D_MODEL), lambda i: (tile(i), 0)), _whole((rows, D_MODEL)), _layer((1, D_MODEL), l)]
        + [_mod_prompt(l, j, n_streams) for j in range(3)] + [_mod_streams(l, j, n_streams) for j in range(3)]
        + [pl.BlockSpec((None, D_MODEL, ODD_LOAD_COLS), lambda i: (li, 0, jnp.minimum(i, n_load - 1))),
           pl.BlockSpec((None, out_rows, D_MODEL), lambda i: (li, jnp.minimum(i, n_load - 1), 0)),
           _layer((len(POOL_WINDOWS), POOL_GROUP, POOL_GROUP), li), _layer((1, D_POOL), li),
           _whole((HEAD_PAIRS, BAND2, 2 * LANES)), _whole((POOL_BASE, D_POOL)),
           _whole((H_ATT * t, cache)), _whole((H_ATT * t, rows)),
           streams((POOL_BASE, D_POOL)), streams((cache, D_ATT)), streams((cache, D_ATT))],
        out_specs=(pl.BlockSpec((tm, D_MODEL), lambda i: (tile(i), 0)), _whole_out((rows, D_MODEL)),
                   _whole_out((POOL_BASE, D_POOL)), _whole_out((HIST, D_ATT)), _whole_out((HIST, D_ATT)),
                   pl.BlockSpec((per_step, t, D_POOL), lambda i: (step(i), 0, 0)),
                   _whole_out((rows, D_ATT)), _whole_out((rows, D_ATT))),
        scratch_shapes=[pltpu.VMEM((n_load, D_MODEL, ODD_LOAD_COLS), BF16), pltpu.VMEM((n_load, out_rows, D_MODEL), BF16),
                        pltpu.VMEM((len(POOL_WINDOWS), POOL_BASE + tm, POOL_GROUP), F32),
                        pltpu.VMEM((HIST + tm, D_ATT), BF16), pltpu.VMEM((HEAD_PAIRS, LANES, HIST + tm), BF16),
                        pltpu.VMEM((tm, D_ATT), F32), pltpu.VMEM((tm, D_ATT), F32),
                        pltpu.VMEM((rows, D_IN_ODD), F32), pltpu.VMEM((rows, D_MODEL), F32),
                        pltpu.VMEM((per_step, len(POOL_WINDOWS), POOL_BASE + t, POOL_GROUP), F32)],
        compiler_params=_params(),
        name="odd_mixer",
    )(xp, xs, norm_g, mod, mod, mod, mod, mod, mod, w_in, w_out, pw16, ps, bias_t, p0, bias_c, bias_n, p0s, kc16, vc16)


FFN_LOAD_COLS = 512


def _ffn_kernel(xp_ref, xs_ref, g_ref, shp_ref, scp_ref, gatep_ref, shs_ref, scs_ref, gates_ref,
                w_up_ref, cw_ref, w_down_ref, f0_ref, f1_ref, f2_ref, gf_ref,
                op_ref, os_ref, fp_out_ref, fs_out_ref, w_up16, w_down16, up_scr, act_scr,
                *, final_norm, n_load, n_tiles, t):
    i = pl.program_id(0)
    tm = xp_ref.shape[0]
    n_chunks = D_FF // FFN_COLS
    slabs_per_chunk = FFN_COLS // LANES
    n_slabs = 2 * D_FF // LANES

    @pl.when(i < n_load)
    def _():
        w_up16[i] = w_up_ref[...].astype(BF16)
        w_down16[i] = w_down_ref[...].astype(BF16)

    def up_weights(c0):
        return w_up16[c0 // FFN_LOAD_COLS, :, c0 % FFN_LOAD_COLS:c0 % FFN_LOAD_COLS + FFN_COLS]

    def conv_slab(j, rows):
        cols = slice(j * LANES, (j + 1) * LANES)
        return (cw_ref[0:1, cols] * up_scr[j, SUBLANES - 2:SUBLANES - 2 + rows, :]
                + cw_ref[1:2, cols] * up_scr[j, SUBLANES - 1:SUBLANES - 1 + rows, :]
                + cw_ref[2:3, cols] * up_scr[j, SUBLANES:SUBLANES + rows, :])

    def run(h16, rows, put_up, after_conv):
        def project(c):
            for off in (0, D_FF):
                c0 = off + c * FFN_COLS
                put_up(c0, _dot(h16, up_weights(c0)))

        def activate(c):
            for j in range(slabs_per_chunk):
                ja = c * slabs_per_chunk + j
                a = conv_slab(ja, rows)
                b = conv_slab(D_FF // LANES + ja, rows)
                after_conv(ja)
                after_conv(D_FF // LANES + ja)
                act_scr[0:rows, ja * LANES:(ja + 1) * LANES] = (_silu(a) * b).astype(BF16)

        project(0)
        for c in range(n_chunks):
            if c + 1 < n_chunks:
                project(c + 1)
            activate(c)
        return _dot(act_scr[0:rows, :], w_down16[...].reshape(D_FF, D_MODEL))

    @pl.when(jnp.logical_and(i >= n_load, i < n_load + n_tiles))
    def _():
        @pl.when(i == n_load)
        def _():
            for j in range(n_slabs):
                up_scr[j, 0:SUBLANES, :] = f0_ref[:, j * LANES:(j + 1) * LANES]

        x = xp_ref[...]
        h16 = _norm_mod(x, g_ref[...], shp_ref[0:1, :], scp_ref[0:1, :]).astype(BF16)

        def put_up(c0, up):
            for j in range(slabs_per_chunk):
                up_scr[c0 // LANES + j, SUBLANES:SUBLANES + tm, :] = up[:, j * LANES:(j + 1) * LANES]
            fp_out_ref[:, c0:c0 + FFN_COLS] = up[tm - SUBLANES:, :]

        def keep_tail(j):
            up_scr[j, 0:SUBLANES, :] = up_scr[j, tm:tm + SUBLANES, :]

        out = x + gatep_ref[0:1, :] * run(h16, tm, put_up, keep_tail)
        if final_norm:
            out = _rmsnorm(out, gf_ref[...])
        op_ref[...] = out

    @pl.when(i == n_load + n_tiles)
    def _():
        x = xs_ref[...]
        n_streams = x.shape[0] // t
        seg = SUBLANES + t
        rows = n_streams * seg
        h16 = _norm_mod(x, g_ref[...], _stream_rows(shs_ref[...], t), _stream_rows(scs_ref[...], t)).astype(BF16)
        for j in range(n_slabs):
            cols = slice(j * LANES, (j + 1) * LANES)
            for b in range(n_streams):
                up_scr[j, b * seg + SUBLANES - 2:b * seg + SUBLANES - 1, :] = f1_ref[b:b + 1, cols]
                up_scr[j, b * seg + SUBLANES - 1:b * seg + SUBLANES, :] = f2_ref[b:b + 1, cols]

        def put_up(c0, up):
            for b in range(n_streams):
                for j in range(slabs_per_chunk):
                    up_scr[c0 // LANES + j, b * seg + SUBLANES:(b + 1) * seg, :] = \
                        up[b * t:(b + 1) * t, j * LANES:(j + 1) * LANES]
                fs_out_ref[b, :, c0:c0 + FFN_COLS] = up[(b + 1) * t - SUBLANES:(b + 1) * t, :]

        y = run(h16, rows, put_up, lambda j: None)
        y = jnp.concatenate([y[b * seg:b * seg + t, :] for b in range(n_streams)], axis=0)
        out = x + _stream_rows(gates_ref[...], t) * y
        if final_norm:
            out = _rmsnorm(out, gf_ref[...])
        os_ref[...] = out


def _ffn(xp, xs, mod, l, norm_g, w_up, cw, w_down, f0, f1, f2, gf, final_norm, t):
    seq, rows = xp.shape[0], xs.shape[0]
    n_streams = rows // t
    tm = ROW_TILE
    n_tiles = seq // tm
    n_load = 2 * D_FF // FFN_LOAD_COLS
    load_rows = D_FF // n_load
    assert n_load * FFN_LOAD_COLS == 2 * D_FF and load_rows % (2 * SUBLANES) == 0 and n_streams * (SUBLANES + t) <= tm

    def tile(i):
        return jnp.clip(i - n_load, 0, n_tiles - 1)

    return pl.pallas_call(
        functools.partial(_ffn_kernel, final_norm=final_norm, n_load=n_load, n_tiles=n_tiles, t=t),
        out_shape=(jax.ShapeDtypeStruct((seq, D_MODEL), F32), jax.ShapeDtypeStruct((rows, D_MODEL), F32),
                   jax.ShapeDtypeStruct((SUBLANES, 2 * D_FF), F32),
                   jax.ShapeDtypeStruct((n_streams, SUBLANES, 2 * D_FF), F32)),
        grid=(n_load + n_tiles + 1,),
        in_specs=[pl.BlockSpec((tm, D_MODEL), lambda i: (tile(i), 0)), _whole((rows, D_MODEL)), _layer((1, D_MODEL), l)]
        + [_mod_prompt(l, 3 + j, n_streams) for j in range(3)] + [_mod_streams(l, 3 + j, n_streams) for j in range(3)]
        + [pl.BlockSpec((None, D_MODEL, FFN_LOAD_COLS), lambda i: (l, 0, jnp.minimum(i, n_load - 1))),
           _layer((3, 2 * D_FF), l),
           pl.BlockSpec((None, load_rows, D_MODEL), lambda i: (l, jnp.minimum(i, n_load - 1), 0)),
           _whole((SUBLANES, 2 * D_FF)), _layer((n_streams, 2 * D_FF), l), _layer((n_streams, 2 * D_FF), l),
           _whole((1, D_MODEL))],
        out_specs=(pl.BlockSpec((tm, D_MODEL), lambda i: (tile(i), 0)), _whole_out((rows, D_MODEL)),
                   _whole_out((SUBLANES, 2 * D_FF)), _whole_out((n_streams, SUBLANES, 2 * D_FF))),
        scratch_shapes=[pltpu.VMEM((n_load, D_MODEL, FFN_LOAD_COLS), BF16), pltpu.VMEM((n_load, load_rows, D_MODEL), BF16),
                        pltpu.VMEM((2 * D_FF // LANES, SUBLANES + tm, LANES), F32), pltpu.VMEM((tm, D_FF), BF16)],
        compiler_params=_params(),
        name="ffn",
    )(xp, xs, norm_g, mod, mod, mod, mod, mod, mod, w_up, cw, w_down, f0, f1, f2, gf)


def _rotary_triplet(pos):
    half = DK_RET // 2
    inv = ROPE_BASE ** (-jnp.arange(half, dtype=F32) / half)
    ang = pos.astype(F32)[:, None] * inv[None, :]
    cos, sin = jnp.cos(ang), jnp.sin(ang)
    return jnp.stack([jnp.concatenate([cos] * 4, axis=-1), jnp.concatenate([sin] * 4, axis=-1),
                      jnp.concatenate([-sin, sin, -sin, sin], axis=-1)])


def _retention_tables(tb, n_streams=1):
    idx = np.arange(tb, dtype=np.float64)
    diff = idx[:, None] - idx[None, :]
    dec1 = np.where(diff[None] >= 0, np.exp(LOG_G[:, None, None] * np.maximum(diff, 0.0)[None]), 0.0)
    dec = np.zeros((H_RET, n_streams * tb, n_streams * tb))
    for b in range(n_streams):
        dec[:, b * tb:(b + 1) * tb, b * tb:(b + 1) * tb] = dec1
    xi = np.tile(np.repeat(np.exp(LOG_G[:, None] * (idx + 1)[None, :]).T, DK_RET, axis=1), (n_streams, 1))
    zeta_t = np.tile(np.repeat(np.exp(LOG_G[:, None] * (tb - 1 - idx)[None, :]), DK_RET, axis=0), (1, n_streams))
    gmat = np.zeros((HEAD_PAIRS, LANES, LANES))
    for h in range(H_RET):
        o = (h % 2) * DK_RET
        gmat[h // 2, o:o + DK_RET, o:o + DK_RET] = np.exp(LOG_G[h] * tb)
    return tuple(jnp.asarray(a, F32) for a in (dec, xi, zeta_t, gmat))


def _pair_state(s):
    lead = s.shape[:-3]
    s = s.reshape(lead + (HEAD_PAIRS, 2, DK_RET, DK_RET))
    z = jnp.zeros_like(s[..., 0, :, :])
    top = jnp.concatenate([s[..., 0, :, :], z], axis=-1)
    bot = jnp.concatenate([z, s[..., 1, :, :]], axis=-1)
    return jnp.concatenate([top, bot], axis=-2)


def _unpair_state(r):
    a = r[..., :DK_RET, :DK_RET]
    b = r[..., DK_RET:, DK_RET:]
    s = jnp.stack([a, b], axis=-3)
    return s.reshape(r.shape[:-3] + (H_RET, DK_RET, DK_RET))


def _band_bias(table):
    nq, nk = 2 * CHUNK, BAND2
    period = nq + nk
    j = np.arange(period)
    j = np.where(j < nk, j, j - period)
    idx = np.clip(HIST - j, -(CHUNK - 1), REL_CLIP) + (CHUNK - 1)
    one_period = table[:, idx].astype(F32)
    flat = jnp.tile(one_period, (1, nq + 1))[:, :nq * (period - 1)]
    return flat.reshape(-1, nq, period - 1)[:, :, :nk]


def _band_bias_t(raw):
    qq = np.arange(2 * CHUNK)[:, None]
    kk = np.arange(BAND2)[None, :]
    valid = np.where(qq < CHUNK, kk < BAND, kk >= CHUNK)
    b = jnp.where(valid, raw * LOG2E, NEG_INF).reshape(HEAD_PAIRS, 2, 2, CHUNK, BAND2)
    return jnp.transpose(b, (0, 4, 2, 1, 3)).reshape(HEAD_PAIRS, BAND2, 2 * LANES)


def _tail_rows(a, n):
    return a[..., a.shape[-2] - n:, :]


def kernel(x_prompt, x_sample, state_ret, state_sconv, state_pool, cache_k, cache_v, state_ffn, c_prompt, c_sample,
           norm_mix, norm_ffn, norm_final, w_ada, b_ada, w_in_even, w_out_even, ret_gn_gain, sconv_w, w_in_odd,
           w_out_odd, pool_w, pool_scale, rel_bias_table, ffn_w_up, ffn_conv, ffn_w_down):
    n_prompt, seq, _ = x_prompt.shape
    n_streams, t_s, _ = x_sample.shape
    assert n_prompt == 1 and n_streams % MOD_ROWS_PROMPT == 0
    rows_s = n_streams * t_s
    n_even, n_odd = (DEPTH + 1) // 2, DEPTH // 2

    c_all = jnp.concatenate([c_sample, c_prompt], axis=0)
    mod = _ada(jnp.pad(c_all, ((0, MOD_ROWS_PROMPT - 1), (0, 0))), w_ada, b_ada)

    bf = lambda w: w.astype(BF16)
    pool_w16 = bf(pool_w)
    norm_mix3, norm_ffn3 = norm_mix.reshape(DEPTH, 1, D_MODEL), norm_ffn.reshape(DEPTH, 1, D_MODEL)
    gain3, pool_scale3 = ret_gn_gain.reshape(n_even, 1, D_RET), pool_scale.reshape(n_odd, 1, D_POOL)
    norm_final2 = norm_final.reshape(1, D_MODEL)

    n_tiles = seq // ROW_TILE
    tabs_p = (_rotary_triplet(jnp.arange(n_tiles, dtype=jnp.int32) * ROW_TILE),
              _rotary_triplet(jnp.arange(ROW_TILE, dtype=jnp.int32))) + _retention_tables(RET_BLOCK)
    rot_s = _rotary_triplet(PAST_LEN + jnp.arange(t_s, dtype=jnp.int32))
    tabs_s = (jnp.tile(rot_s[0], (n_streams, 1)), jnp.tile(rot_s[2], (n_streams, 1))) \
        + _retention_tables(t_s, n_streams)

    cache_len = cache_k.shape[2]
    assert cache_len == HIST and t_s <= CHUNK
    bias_raw = [_band_bias(rel_bias_table[i]) for i in range(n_odd)]
    bias_p = [_band_bias_t(b) for b in bias_raw]
    bias_c = [b[:, :t_s, :cache_len].reshape(H_ATT * t_s, cache_len) for b in bias_raw]
    bias_n = [jnp.tile(b[:, :t_s, cache_len:cache_len + t_s], (1, 1, n_streams)).reshape(H_ATT * t_s, rows_s)
              for b in bias_raw]

    paired = _pair_state(state_ret)
    s_stack = jnp.transpose(paired, (0, 2, 1, 3, 4)).reshape(n_even, HEAD_PAIRS, n_streams * LANES, LANES)
    s_wide = jnp.transpose(paired, (0, 2, 3, 1, 4)).reshape(n_even, HEAD_PAIRS, LANES, n_streams * LANES)
    u1, u2 = state_sconv[:, :, 0, :], state_sconv[:, :, 1, :]
    f1, f2 = state_ffn[:, :, 0, :], state_ffn[:, :, 1, :]
    p0_s = jnp.pad(state_pool, ((0, 0), (0, 0), (POOL_BASE - POOL_BUF, 0), (0, 0)))
    kc16 = cache_k.reshape(n_odd, n_streams, cache_len, D_ATT).astype(BF16)
    vc16 = cache_v.reshape(n_odd, n_streams, cache_len, D_ATT).astype(BF16)

    xp = x_prompt.reshape(seq, D_MODEL)
    xs = x_sample.reshape(rows_s, D_MODEL)

    ret_p, ret_s, sconv_p, sconv_s, pool_p, pool_s = [], [], [], [], [], []
    k_p, k_s, v_p, v_s, ffn_p, ffn_s = [], [], [], [], [], []
    for l in range(DEPTH):
        i = l // 2
        if l % 2 == 0:
            xp, xs, r_new, u_new, s_new, u_all = _even(
                xp, xs, mod, l, norm_mix3, w_in_even, w_out_even, gain3, sconv_w, tabs_p, tabs_s,
                jnp.zeros((HEAD_PAIRS, LANES, LANES), F32), jnp.zeros((SUBLANES, D_SCONV), F32),
                s_stack, s_wide, u1, u2, t_s)
            ret_p.append(_unpair_state(r_new)[None])
            sconv_p.append(_tail_rows(u_new, 2)[None])
            s_new = jnp.transpose(s_new.reshape(HEAD_PAIRS, LANES, n_streams, LANES), (2, 0, 1, 3))
            ret_s.append(_unpair_state(s_new))
            sconv_s.append(_tail_rows(u_all.reshape(n_streams, t_s, D_SCONV), 2))
        else:
            xp, xs, p_new, k_new, v_new, ps_new, ks_new, vs_new = _odd(
                xp, xs, mod, l, norm_mix3, w_in_odd, w_out_odd, pool_w16, pool_scale3, bias_p[i],
                jnp.zeros((POOL_BASE, D_POOL), F32), bias_c[i], bias_n[i], p0_s, kc16, vc16, t_s, PAST_LEN)
            pool_p.append(_tail_rows(p_new, POOL_BUF)[None])
            k_p.append(k_new.reshape(1, HIST, H_ATT, DH_ATT))
            v_p.append(v_new.reshape(1, HIST, H_ATT, DH_ATT))
            p_new, k_new, v_new = ps_new, ks_new, vs_new
            pool_s.append(_tail_rows(p_new, POOL_BUF))
            k_s.append(k_new.reshape(n_streams, t_s, H_ATT, DH_ATT))
            v_s.append(v_new.reshape(n_streams, t_s, H_ATT, DH_ATT))
        last = l == DEPTH - 1
        xp, xs, f_new, fs_new = _ffn(xp, xs, mod, l, norm_ffn3, ffn_w_up, ffn_conv, ffn_w_down,
                                     jnp.zeros((SUBLANES, 2 * D_FF), F32), f1, f2, norm_final2, last, t_s)
        ffn_p.append(_tail_rows(f_new, 2)[None])
        ffn_s.append(_tail_rows(fs_new, 2))

    st = jnp.stack
    return (xp.reshape(1, seq, D_MODEL), xs.reshape(n_streams, t_s, D_MODEL),
            st(ret_p), st(ret_s), st(sconv_p), st(sconv_s), st(pool_p), st(pool_s),
            st(k_p), st(k_s), st(v_p), st(v_s), st(ffn_p), st(ffn_s))
```

```python
import functools

import numpy as np
import jax
import jax.numpy as jnp
from jax import lax
from jax.experimental import pallas as pl
from jax.experimental.pallas import tpu as pltpu

F32 = jnp.float32
BF16 = jnp.bfloat16

D_MODEL = 1024
DEPTH = 4
PAST_LEN = 4096
CHUNK = 64
H_RET = 8
DK_RET = 64
D_RET = H_RET * DK_RET
ROPE_BASE = 10000.0
D_SCONV = D_MODEL - D_RET
POOL_WINDOWS = (2, 4, 8, 16)
D_POOL = D_MODEL // 2
POOL_GROUP = D_POOL // len(POOL_WINDOWS)
POOL_BUF = max(POOL_WINDOWS) - 1
H_ATT = 8
DH_ATT = 64
D_ATT = H_ATT * DH_ATT
N_PREV_CHUNKS = 8
REL_CLIP = 256
D_FF = 2816
EPS = 1e-6
NEG_INF = -1e30
D_IN_EVEN = 4 * D_RET + 3 * D_SCONV
D_IN_ODD = D_POOL + 3 * D_ATT

LANES = 128
SUBLANES = 8
HEAD_PAIRS = H_RET // 2
ROW_TILE = 512
RET_BLOCK = 128
FFN_COLS = 256
BAND = (N_PREV_CHUNKS + 1) * CHUNK
BAND2 = BAND + CHUNK
LOG2E = 1.4426950408889634
VT_EXTRA = 16
HIST = N_PREV_CHUNKS * CHUNK
POOL_BASE = 2 * SUBLANES
VMEM_LIMIT = 56 * 1024 * 1024

LOG_G = np.log1p(-(2.0 ** (-5.0 - np.arange(H_RET, dtype=np.float64))))


def _params(n_axes=1):
    return pltpu.CompilerParams(dimension_semantics=("arbitrary",) * n_axes, vmem_limit_bytes=VMEM_LIMIT)


def _whole(shape):
    nd = len(shape)
    return pl.BlockSpec(shape, lambda i: (0,) * nd, pipeline_mode=pl.Buffered(1))


def _whole_out(shape):
    nd = len(shape)
    return pl.BlockSpec(shape, lambda i: (0,) * nd)


def _layer(shape, l):
    nd = len(shape)
    return pl.BlockSpec((None,) + tuple(shape), lambda i: (l,) + (0,) * nd, pipeline_mode=pl.Buffered(1))


MOD_ROWS_PROMPT = SUBLANES


def _mod_prompt(l, j, n_streams):
    return pl.BlockSpec((None, MOD_ROWS_PROMPT, D_MODEL), lambda i: (l, n_streams // MOD_ROWS_PROMPT, j),
                        pipeline_mode=pl.Buffered(1))


def _mod_streams(l, j, n_streams):
    return pl.BlockSpec((None, n_streams, D_MODEL), lambda i: (l, 0, j), pipeline_mode=pl.Buffered(1))


def _stream_rows(m, t):
    return jnp.concatenate([jnp.broadcast_to(m[b:b + 1, :], (t, m.shape[1])) for b in range(m.shape[0])], axis=0)


def _dot(a, b):
    return jnp.dot(a, b, preferred_element_type=F32)


def _dot_nt(a, b):
    return lax.dot_general(a, b, (((1,), (1,)), ((), ())), preferred_element_type=F32)


def _rmsnorm(x, g):
    return x * lax.rsqrt(jnp.mean(x * x, axis=-1, keepdims=True) + EPS) * g


def _norm_mod(x, g, shift, scale):
    return _rmsnorm(x, g) * (1.0 + scale) + shift


def _silu(x):
    return x * (1.0 / (1.0 + jnp.exp(-x)))


def _low_half(shape):
    return (lax.broadcasted_iota(jnp.int32, shape, len(shape) - 1) % LANES) < DK_RET


def _shift_rows(u, prev8, s):
    rolled = pltpu.roll(u, s, axis=0)
    prolled = pltpu.roll(prev8, s, axis=0)
    row = lax.broadcasted_iota(jnp.int32, prev8.shape, 0)
    first = jnp.where(row < s, prolled, rolled[0:SUBLANES])
    return jnp.concatenate([first, rolled[SUBLANES:]], axis=0)


def _shift_rows_streams(u, older, newer, s, t):
    row_in_stream = lax.broadcasted_iota(jnp.int32, u.shape, 0) % t
    rolled = pltpu.roll(u, s, axis=0)
    if s == 1:
        return jnp.where(row_in_stream == 0, _stream_rows(newer, t), rolled)
    return jnp.where(row_in_stream == 0, _stream_rows(older, t),
                     jnp.where(row_in_stream == 1, _stream_rows(newer, t), rolled))


def _ada_kernel(c_ref, w_ref, b_ref, o_ref):
    c = c_ref[...]
    o_ref[0] = _dot(_silu(c).astype(BF16), w_ref[0].astype(BF16)) + b_ref[0]


def _ada(c_all, w_ada, b_ada):
    rows = c_all.shape[0]
    tn = 1536
    return pl.pallas_call(
        _ada_kernel,
        out_shape=jax.ShapeDtypeStruct((DEPTH, rows, 6 * D_MODEL), F32),
        grid=(DEPTH, 6 * D_MODEL // tn),
        in_specs=[
            pl.BlockSpec((rows, D_MODEL), lambda l, j: (0, 0)),
            pl.BlockSpec((1, D_MODEL, tn), lambda l, j: (l, 0, j)),
            pl.BlockSpec((1, 1, tn), lambda l, j: (l, 0, j)),
        ],
        out_specs=pl.BlockSpec((1, rows, tn), lambda l, j: (l, 0, j)),
        compiler_params=_params(2),
        name="ada_mod",
    )(c_all, w_ada, b_ada.reshape(DEPTH, 1, 6 * D_MODEL))


def _rotary_pair(x, cos, sin_signed):
    lane = lax.broadcasted_iota(jnp.int32, x.shape, 1)
    first_half = (lane % DK_RET) < (DK_RET // 2)
    swapped = jnp.where(first_half, pltpu.roll(x, LANES - DK_RET // 2, axis=1),
                        pltpu.roll(x, DK_RET // 2, axis=1))
    return x * cos + swapped * sin_signed


def _retention_block(proj, cos, sin_signed, dec_ref, xi, zeta_t_ref, gain, cross_fn, update_fn):
    tb = proj.shape[0]
    low = _low_half((tb, LANES))
    inv_n = 1.0 / DK_RET
    pairs = range(HEAD_PAIRS)
    cols = [slice(p * LANES, (p + 1) * LANES) for p in pairs]
    q = [_rotary_pair(proj[:, cols[p]], cos, sin_signed) for p in pairs]
    k_t = [(_rotary_pair(proj[:, D_RET + p * LANES:D_RET + (p + 1) * LANES], cos, sin_signed)
            * (DK_RET ** -0.5)).T for p in pairs]
    v16 = [proj[:, 2 * D_RET + p * LANES:2 * D_RET + (p + 1) * LANES].astype(BF16) for p in pairs]
    k_t16 = [k_t[p].astype(BF16) for p in pairs]
    kz_t16 = [(k_t[p] * zeta_t_ref[cols[p], :]).astype(BF16) for p in pairs]
    q16 = [q[p].astype(BF16) for p in pairs]
    qe16 = [jnp.where(low, q[p], 0.0).astype(BF16) for p in pairs]
    qo16 = [jnp.where(low, 0.0, q[p]).astype(BF16) for p in pairs]
    s_e = [(_dot(qe16[p], k_t16[p]) * dec_ref[2 * p]).astype(BF16) for p in pairs]
    s_o = [(_dot(qo16[p], k_t16[p]) * dec_ref[2 * p + 1]).astype(BF16) for p in pairs]
    cross = [cross_fn(p, q16[p]) * xi[:, cols[p]] for p in pairs]
    o = [jnp.where(low, _dot(s_e[p], v16[p]), _dot(s_o[p], v16[p])) + cross[p] for p in pairs]
    for p in pairs:
        update_fn(p, kz_t16[p], v16[p])
    outs = []
    for p in pairs:
        s_lo = jnp.sum(jnp.where(low, o[p], 0.0), axis=-1, keepdims=True)
        s_hi = jnp.sum(jnp.where(low, 0.0, o[p]), axis=-1, keepdims=True)
        d = o[p] - jnp.where(low, s_lo, s_hi) * inv_n
        d2 = d * d
        v_lo = jnp.sum(jnp.where(low, d2, 0.0), axis=-1, keepdims=True)
        v_hi = jnp.sum(jnp.where(low, 0.0, d2), axis=-1, keepdims=True)
        on = d * lax.rsqrt(jnp.where(low, v_lo, v_hi) * inv_n + EPS)
        g = proj[:, 3 * D_RET + p * LANES:3 * D_RET + (p + 1) * LANES]
        outs.append(_silu(g) * (on * gain[:, cols[p]]))
    return jnp.concatenate(outs, axis=-1)


def _even_tail(x, proj, ret_out, conv_in_shift, cw_ref, gate, w_out16):
    gate_b = proj[:, 4 * D_RET:4 * D_RET + D_SCONV]
    u = proj[:, 4 * D_RET + D_SCONV:4 * D_RET + 2 * D_SCONV] * proj[:, 4 * D_RET + 2 * D_SCONV:]
    conv = cw_ref[0:1, :] * conv_in_shift(u, 2) + cw_ref[1:2, :] * conv_in_shift(u, 1) + cw_ref[2:3, :] * u
    mixed = jnp.concatenate([ret_out, gate_b * conv], axis=-1).astype(BF16)
    return x + gate * _dot(mixed, w_out16), u


EVEN_LOAD_COLS = 512
EVEN_OUT_LOAD_ROWS = 256


def _even_kernel(xp_ref, xs_ref, g_ref, shp_ref, scp_ref, gatep_ref, shs_ref, scs_ref, gates_ref,
                 w_in_ref, w_out_ref, gain_ref, cw_ref,
                 rot_tile_ref, rot_row_ref, dec_ref, xi_ref, zt_ref, gmat_ref, r0_ref, u0_ref,
                 cos_s_ref, sin_s_ref, dec_s_ref, xi_s_ref, zt_s_ref, gmat_s_ref, s_stack_ref, s_wide_ref, u1_ref, u2_ref,
                 op_ref, os_ref, r_out_ref, u_out_ref, s_out_ref, us_out_ref,
                 w_in16, w_out16, r_scr, u_scr, *, n_load, n_tiles, t):
    i = pl.program_id(0)
    n_out_load = w_out16.shape[0]

    @pl.when(i < n_load)
    def _():
        w_in16[i] = w_in_ref[...].astype(BF16)

    @pl.when(i < n_out_load)
    def _():
        w_out16[i] = w_out_ref[...].astype(BF16)

    def in_project(h16):
        return jnp.concatenate([_dot(h16, w_in16[c]) for c in range(n_load)], axis=-1)

    def out_weights():
        return w_out16[...].reshape(D_MODEL, D_MODEL)

    @pl.when(jnp.logical_and(i >= n_load, i < n_load + n_tiles))
    def _():
        tile = i - n_load
        cos_0, sin_0, ssin_0 = (rot_tile_ref[j, pl.ds(tile, 1), :] for j in range(3))
        cos = cos_0 * rot_row_ref[0] - sin_0 * rot_row_ref[1]
        sin_signed = ssin_0 * rot_row_ref[0] + cos_0 * rot_row_ref[2]

        n_slabs = D_SCONV // LANES

        @pl.when(tile == 0)
        def _():
            r_scr[...] = r0_ref[...]
            for j in range(n_slabs):
                u_scr[j, 0:SUBLANES, :] = u0_ref[:, j * LANES:(j + 1) * LANES]

        x = xp_ref[...]
        h = _norm_mod(x, g_ref[...], shp_ref[0:1, :], scp_ref[0:1, :]).astype(BF16)
        proj = in_project(h)
        tm = x.shape[0]
        tb = dec_ref.shape[1]
        r_i = lax.broadcasted_iota(jnp.int32, (LANES, LANES), 0) < DK_RET
        c_i = lax.broadcasted_iota(jnp.int32, (LANES, LANES), 1) < DK_RET
        blockdiag = r_i == c_i

        def cross_fn(p, q16):
            return _dot(q16, r_scr[p].astype(BF16))

        def update_fn(p, kz_t16, v16):
            r_scr[p] = r_scr[p] * gmat_ref[p] + jnp.where(blockdiag, _dot(kz_t16, v16), 0.0)

        rets = []
        for r in range(tm // tb):
            rows = slice(r * tb, (r + 1) * tb)
            rets.append(_retention_block(proj[rows, :], cos[rows, :], sin_signed[rows, :], dec_ref, xi_ref[...],
                                         zt_ref, gain_ref[...], cross_fn, update_fn))
        ret_out = jnp.concatenate(rets, axis=0)

        def delayed(u, s):
            if s == 2:
                for j in range(n_slabs):
                    u_scr[j, SUBLANES:SUBLANES + tm, :] = u[:, j * LANES:(j + 1) * LANES]
            return jnp.concatenate([u_scr[j, SUBLANES - s:SUBLANES - s + tm, :] for j in range(n_slabs)], axis=-1)

        out, u = _even_tail(x, proj, ret_out, delayed, cw_ref, gatep_ref[0:1, :], out_weights())
        for j in range(n_slabs):
            u_scr[j, 0:SUBLANES, :] = u_scr[j, tm:tm + SUBLANES, :]
        op_ref[...] = out
        r_out_ref[...] = r_scr[...]
        u_out_ref[...] = u[tm - SUBLANES:, :]

    @pl.when(i == n_load + n_tiles)
    def _():
        x = xs_ref[...]
        rows = x.shape[0]
        n_streams = rows // t
        wide = n_streams * LANES
        h = _norm_mod(x, g_ref[...], _stream_rows(shs_ref[...], t), _stream_rows(scs_ref[...], t)).astype(BF16)
        proj = in_project(h)
        own = (lax.broadcasted_iota(jnp.int32, (rows, wide), 0) // t
               == lax.broadcasted_iota(jnp.int32, (rows, wide), 1) // LANES)
        r_i = lax.broadcasted_iota(jnp.int32, (LANES, wide), 0) < DK_RET
        blockdiag = r_i == _low_half((LANES, wide))

        def expand(a16):
            tiled = jnp.concatenate([a16.astype(F32)] * n_streams, axis=-1)
            return jnp.where(own, tiled, 0.0).astype(BF16)

        def cross_fn(p, q16):
            return _dot(expand(q16), s_stack_ref[p].astype(BF16))

        def update_fn(p, kz_t16, v16):
            kv = _dot(kz_t16, expand(v16))
            decay = jnp.concatenate([gmat_s_ref[p]] * n_streams, axis=-1)
            s_out_ref[p] = s_wide_ref[p] * decay + jnp.where(blockdiag, kv, 0.0)

        ret_out = _retention_block(proj, cos_s_ref[...], sin_s_ref[...], dec_s_ref, xi_s_ref[...], zt_s_ref,
                                   gain_ref[...], cross_fn, update_fn)
        out, u = _even_tail(x, proj, ret_out, lambda u, s: _shift_rows_streams(u, u1_ref[...], u2_ref[...], s, t),
                            cw_ref, _stream_rows(gates_ref[...], t), out_weights())
        os_ref[...] = out
        us_out_ref[...] = u


def _even(xp, xs, mod, l, norm_g, w_in, w_out, gain, cw, tabs_p, tabs_s, r0, u0, s_stack, s_wide, u1, u2, t):
    seq, rows = xp.shape[0], xs.shape[0]
    n_streams = rows // t
    wide = n_streams * LANES
    tm, tb = ROW_TILE, RET_BLOCK
    n_tiles = seq // tm
    n_load = D_IN_EVEN // EVEN_LOAD_COLS
    n_out_load = D_MODEL // EVEN_OUT_LOAD_ROWS
    assert n_load * EVEN_LOAD_COLS == D_IN_EVEN and n_out_load <= n_load
    rot_tile, rot_row, dec, xi, zt, gmat = tabs_p
    cos_s, sin_s, dec_s, xi_s, zt_s, gmat_s = tabs_s
    li = l // 2
    state = (HEAD_PAIRS, LANES, LANES)

    def tile(i):
        return jnp.clip(i - n_load, 0, n_tiles - 1)

    return pl.pallas_call(
        functools.partial(_even_kernel, n_load=n_load, n_tiles=n_tiles, t=t),
        out_shape=(jax.ShapeDtypeStruct((seq, D_MODEL), F32), jax.ShapeDtypeStruct((rows, D_MODEL), F32),
                   jax.ShapeDtypeStruct(state, F32), jax.ShapeDtypeStruct((SUBLANES, D_SCONV), F32),
                   jax.ShapeDtypeStruct((HEAD_PAIRS, LANES, wide), F32), jax.ShapeDtypeStruct((rows, D_SCONV), F32)),
        grid=(n_load + n_tiles + 1,),
        in_specs=[pl.BlockSpec((tm, D_MODEL), lambda i: (tile(i), 0)), _whole((rows, D_MODEL)), _layer((1, D_MODEL), l)]
        + [_mod_prompt(l, j, n_streams) for j in range(3)] + [_mod_streams(l, j, n_streams) for j in range(3)]
        + [pl.BlockSpec((None, D_MODEL, EVEN_LOAD_COLS), lambda i: (li, 0, jnp.minimum(i, n_load - 1))),
           pl.BlockSpec((None, EVEN_OUT_LOAD_ROWS, D_MODEL), lambda i: (li, jnp.minimum(i, n_out_load - 1), 0)),
           _layer((1, D_RET), li), _layer((3, D_SCONV), li),
           _whole((3, n_tiles, LANES)), _whole((3, tm, LANES)),
           _whole((H_RET, tb, tb)), _whole((tb, D_RET)), _whole((D_RET, tb)),
           _whole(state), _whole(state), _whole((SUBLANES, D_SCONV)),
           _whole((rows, LANES)), _whole((rows, LANES)),
           _whole((H_RET, rows, rows)), _whole((rows, D_RET)), _whole((D_RET, rows)), _whole(state),
           _layer((HEAD_PAIRS, wide, LANES), li), _layer((HEAD_PAIRS, LANES, wide), li),
           _layer((n_streams, D_SCONV), li), _layer((n_streams, D_SCONV), li)],
        out_specs=(pl.BlockSpec((tm, D_MODEL), lambda i: (tile(i), 0)), _whole_out((rows, D_MODEL)),
                   _whole_out(state), _whole_out((SUBLANES, D_SCONV)),
                   _whole_out((HEAD_PAIRS, LANES, wide)), _whole_out((rows, D_SCONV))),
        scratch_shapes=[pltpu.VMEM((n_load, D_MODEL, EVEN_LOAD_COLS), BF16),
                        pltpu.VMEM((n_out_load, EVEN_OUT_LOAD_ROWS, D_MODEL), BF16),
                        pltpu.VMEM(state, F32), pltpu.VMEM((D_SCONV // LANES, SUBLANES + tm, LANES), F32)],
        compiler_params=_params(),
        name="even_mixer",
    )(xp, xs, norm_g, mod, mod, mod, mod, mod, mod, w_in, w_out, gain, cw,
      rot_tile, rot_row, dec, xi, zt, gmat, r0, u0, cos_s, sin_s, dec_s, xi_s, zt_s, gmat_s, s_stack, s_wide, u1, u2)


def _pool_put(hist_ref, row0, block):
    for gi in range(len(POOL_WINDOWS)):
        hist_ref[gi, row0:row0 + block.shape[0], :] = block[:, gi * POOL_GROUP:(gi + 1) * POOL_GROUP]


def _pool_get(hist_ref, row0, rows):
    return jnp.concatenate([hist_ref[gi, row0:row0 + rows, :] for gi in range(len(POOL_WINDOWS))], axis=-1)


def _pool(hist_ref, p, pos, pool_w_ref, scale):
    t = p.shape[0]
    outs = []
    for gi, w in enumerate(POOL_WINDOWS):
        cols = slice(gi * POOL_GROUP, (gi + 1) * POOL_GROUP)
        win = p[:, cols]
        for d in range(1, w):
            win = win + hist_ref[gi, POOL_BASE - d:POOL_BASE - d + t, :]
        inv_cnt = 1.0 / jnp.minimum(pos + 1, w).astype(F32)
        pooled = win * inv_cnt - p[:, cols]
        outs.append(_dot(pooled.astype(BF16), pool_w_ref[gi]) * scale[:, cols])
    return jnp.concatenate(outs, axis=-1)


def _attend_scores(kbs, q_as, q_bs, biases):
    low = _low_half((CHUNK, LANES))
    out = []
    for kb, q_a, q_b, bias in zip(kbs, q_as, q_bs, biases):
        qs = jnp.concatenate([jnp.where(low, q_a, 0.0), jnp.where(low, 0.0, q_a),
                              jnp.where(low, q_b, 0.0), jnp.where(low, 0.0, q_b)], axis=0)
        qbd = qs.T.astype(BF16)
        half = (kb.shape[0] // 2) // (2 * SUBLANES) * (2 * SUBLANES)
        out.append(jnp.concatenate([_dot(kb[:half], qbd), _dot(kb[half:], qbd)], axis=0) + bias)
    return out


def _attend_values(scores, vts):
    low = _low_half((CHUNK, LANES))
    n = len(scores)
    e16 = [jnp.exp2(s - jnp.max(s, axis=0, keepdims=True)).astype(BF16) for s in scores]
    o_t = [jnp.concatenate([_dot(vts[j][:DK_RET], e16[j]), _dot(vts[j][DK_RET:], e16[j])], axis=0) for j in range(n)]
    o_t = [o[:LANES] * (1.0 / o[LANES:LANES + 1]) for o in o_t]
    o_t = [o.T for o in o_t]
    return [(jnp.where(low, o[0:CHUNK], o[CHUNK:2 * CHUNK]), jnp.where(low, o[2 * CHUNK:3 * CHUNK], o[3 * CHUNK:]))
            for o in o_t]


ODD_LOAD_COLS = 512
SAMPLE_STREAMS_PER_STEP = 4


def _odd_kernel(xp_ref, xs_ref, g_ref, shp_ref, scp_ref, gatep_ref, shs_ref, scs_ref, gates_ref,
                w_in_ref, w_out_ref, pw_ref, ps_ref, bias_ref, p0_ref, bias_c_ref, bias_n_ref, p0s_ref, kc_ref, vc_ref,
                op_ref, os_ref, p_out_ref, k_out_ref, v_out_ref, ps_out_ref, ks_out_ref, vs_out_ref,
                w_in16, w_out16, pbuf, kbuf, vtbuf, q_scr, att_scr, proj_scr, mix_scr, pbuf_s,
                *, n_load, n_tiles, pos0, t):
    i = pl.program_id(0)
    tm = xp_ref.shape[0]
    rows = xs_ref.shape[0]
    per_step = kc_ref.shape[0]

    @pl.when(i < n_load)
    def _():
        w_in16[i] = w_in_ref[...].astype(BF16)
        w_out16[i] = w_out_ref[...].astype(BF16)

    def in_project(h16):
        return jnp.concatenate([_dot(h16, w_in16[c]) for c in range(n_load)], axis=-1)

    def out_weights():
        return w_out16[...].reshape(D_MODEL, D_MODEL)

    @pl.when(jnp.logical_and(i >= n_load, i < n_load + n_tiles))
    def _():
        tile = i - n_load

        @pl.when(tile == 0)
        def _():
            _pool_put(pbuf, 0, p0_ref[...])
            kbuf[0:HIST, :] = jnp.zeros((HIST, D_ATT), BF16)
            vtbuf[:, :, 0:HIST] = jnp.zeros((HEAD_PAIRS, LANES + VT_EXTRA, HIST), BF16)

        x = xp_ref[...]
        h = _norm_mod(x, g_ref[...], shp_ref[0:1, :], scp_ref[0:1, :]).astype(BF16)
        proj = in_project(h)
        p = proj[:, :D_POOL]
        q_scr[...] = proj[:, D_POOL:D_POOL + D_ATT] * (DH_ATT ** -0.5 * LOG2E)
        k = proj[:, D_POOL + D_ATT:D_POOL + 2 * D_ATT]
        v = proj[:, D_POOL + 2 * D_ATT:]
        _pool_put(pbuf, POOL_BASE, p)
        kbuf[HIST:HIST + tm, :] = k.astype(BF16)
        ones_row = jnp.where(lax.broadcasted_iota(jnp.int32, (VT_EXTRA, tm), 0) == 0, 1.0, 0.0).astype(BF16)
        for pr in range(HEAD_PAIRS):
            vtbuf[pr, 0:LANES, HIST:HIST + tm] = v[:, pr * LANES:(pr + 1) * LANES].T.astype(BF16)
            vtbuf[pr, LANES:LANES + VT_EXTRA, HIST:HIST + tm] = ones_row
        k_out_ref[...] = k[tm - HIST:, :]
        v_out_ref[...] = v[tm - HIST:, :]

        pos = tile * tm + lax.broadcasted_iota(jnp.int32, (tm, 1), 0)
        pool_out = _pool(pbuf, p, pos, pw_ref, ps_ref[...])

        def attend_tile(first_tile):
            pairs = range(HEAD_PAIRS)
            lanes = [slice(pr * LANES, (pr + 1) * LANES) for pr in pairs]
            n_blocks = tm // (2 * CHUNK)
            skip = [max(HIST - jb * 2 * CHUNK, 0) if first_tile else 0 for jb in range(n_blocks)]

            def scores(jb):
                r0 = jb * 2 * CHUNK
                return _attend_scores([kbuf[r0 + skip[jb]:r0 + BAND2, lanes[pr]] for pr in pairs],
                                      [q_scr[r0:r0 + CHUNK, lanes[pr]] for pr in pairs],
                                      [q_scr[r0 + CHUNK:r0 + 2 * CHUNK, lanes[pr]] for pr in pairs],
                                      [bias_ref[pr, skip[jb]:, :] for pr in pairs])

            s_next = scores(0)
            for jb in range(n_blocks):
                r0 = jb * 2 * CHUNK
                s_cur = s_next
                if jb + 1 < n_blocks:
                    s_next = scores(jb + 1)
                outs = _attend_values(s_cur, [vtbuf[pr, :, r0 + skip[jb]:r0 + BAND2] for pr in pairs])
                att_scr[r0:r0 + CHUNK, :] = jnp.concatenate([o[0] for o in outs], axis=-1)
                att_scr[r0 + CHUNK:r0 + 2 * CHUNK, :] = jnp.concatenate([o[1] for o in outs], axis=-1)

        pl.when(tile == 0)(functools.partial(attend_tile, True))
        pl.when(tile > 0)(functools.partial(attend_tile, False))

        kbuf[0:HIST, :] = kbuf[tm:tm + HIST, :]
        vtbuf[:, :, 0:HIST] = vtbuf[:, :, tm:tm + HIST]
        tail = _pool_get(pbuf, tm, POOL_BASE)
        _pool_put(pbuf, 0, tail)
        p_out_ref[...] = tail

        mixed = jnp.concatenate([pool_out, att_scr[...]], axis=-1).astype(BF16)
        op_ref[...] = x + gatep_ref[0:1, :] * _dot(mixed, out_weights())

    @pl.when(i >= n_load + n_tiles)
    def _():
        step = i - (n_load + n_tiles)

        @pl.when(step == 0)
        def _():
            h = _norm_mod(xs_ref[...], g_ref[...], _stream_rows(shs_ref[...], t),
                          _stream_rows(scs_ref[...], t)).astype(BF16)
            proj = in_project(h)
            proj_scr[...] = proj
            ks_out_ref[...] = proj[:, D_POOL + D_ATT:D_POOL + 2 * D_ATT]
            vs_out_ref[...] = proj[:, D_POOL + 2 * D_ATT:]

        kn = proj_scr[:, D_POOL + D_ATT:D_POOL + 2 * D_ATT].astype(BF16)
        vn = proj_scr[:, D_POOL + 2 * D_ATT:].astype(BF16)
        pos = pos0 + lax.broadcasted_iota(jnp.int32, (t, 1), 0)
        head_of_lane = lax.broadcasted_iota(jnp.int32, (t, D_ATT), 1) // DH_ATT
        stream_of_col = lax.broadcasted_iota(jnp.int32, (H_ATT * t, rows), 1) // t
        js = range(per_step)
        r0 = [pl.multiple_of((step * per_step + j) * t, t) for j in js]
        proj = [proj_scr[pl.ds(r0[j], t), :] for j in js]
        q_heads = []
        for j in js:
            q = proj[j][:, D_POOL:D_POOL + D_ATT] * (DH_ATT ** -0.5)
            q_heads.append(jnp.concatenate([jnp.where(head_of_lane == hh, q, 0.0) for hh in range(H_ATT)],
                                           axis=0).astype(BF16))
        s_c = [_dot_nt(q_heads[j], kc_ref[j]) + bias_c_ref[...] for j in js]
        s_n = [jnp.where(stream_of_col == step * per_step + j, _dot_nt(q_heads[j], kn) + bias_n_ref[...], NEG_INF)
               for j in js]
        pool_out = []
        for j in js:
            p = proj[j][:, :D_POOL]
            _pool_put(pbuf_s.at[j], 0, p0s_ref[j])
            _pool_put(pbuf_s.at[j], POOL_BASE, p)
            ps_out_ref[j] = p
            pool_out.append(_pool(pbuf_s.at[j], p, pos, pw_ref, ps_ref[...]))
        m = [jnp.maximum(jnp.max(s_c[j], axis=-1, keepdims=True), jnp.max(s_n[j], axis=-1, keepdims=True)) for j in js]
        e_c = [jnp.exp(s_c[j] - m[j]) for j in js]
        e_n = [jnp.exp(s_n[j] - m[j]) for j in js]
        inv_l = [1.0 / (jnp.sum(e_c[j], axis=-1, keepdims=True) + jnp.sum(e_n[j], axis=-1, keepdims=True)) for j in js]
        o_heads = [(_dot(e_c[j].astype(BF16), vc_ref[j]) + _dot(e_n[j].astype(BF16), vn)) * inv_l[j]
                   for j in js]
        for j in js:
            att = jnp.where(head_of_lane == 0, o_heads[j][0:t], 0.0)
            for hh in range(1, H_ATT):
                att = jnp.where(head_of_lane == hh, o_heads[j][hh * t:(hh + 1) * t], att)
            mix_scr[pl.ds(r0[j], t), :] = jnp.concatenate([pool_out[j], att], axis=-1)

        @pl.when(i == pl.num_programs(0) - 1)
        def _():
            os_ref[...] = xs_ref[...] + _stream_rows(gates_ref[...], t) * _dot(mix_scr[...].astype(BF16), out_weights())


def _odd(xp, xs, mod, l, norm_g, w_in, w_out, pw16, ps, bias_t, p0, bias_c, bias_n, p0s, kc16, vc16, t, pos0):
    seq, rows = xp.shape[0], xs.shape[0]
    n_streams = rows // t
    cache = kc16.shape[2]
    tm = ROW_TILE
    assert tm == HIST and seq % tm == 0
    n_tiles = seq // tm
    n_load = D_IN_ODD // ODD_LOAD_COLS
    out_rows = D_MODEL // n_load
    per_step = SAMPLE_STREAMS_PER_STEP
    n_steps = n_streams // per_step
    assert n_load * ODD_LOAD_COLS == D_IN_ODD and n_streams % per_step == 0
    li = l // 2

    def tile(i):
        return jnp.clip(i - n_load, 0, n_tiles - 1)

    def step(i):
        return jnp.clip(i - n_load - n_tiles, 0, n_steps - 1)

    def streams(shape):
        nd = len(shape)
        return pl.BlockSpec((None, per_step) + tuple(shape), lambda i: (li, step(i)) + (0,) * nd)

    return pl.pallas_call(
        functools.partial(_odd_kernel, n_load=n_load, n_tiles=n_tiles, pos0=pos0, t=t),
        out_shape=(jax.ShapeDtypeStruct((seq, D_MODEL), F32), jax.ShapeDtypeStruct((rows, D_MODEL), F32),
                   jax.ShapeDtypeStruct((POOL_BASE, D_POOL), F32),
                   jax.ShapeDtypeStruct((HIST, D_ATT), F32), jax.ShapeDtypeStruct((HIST, D_ATT), F32),
                   jax.ShapeDtypeStruct((n_streams, t, D_POOL), F32),
                   jax.ShapeDtypeStruct((rows, D_ATT), F32), jax.ShapeDtypeStruct((rows, D_ATT), F32)),
        grid=(n_load + n_tiles + n_steps,),
        in_specs=[pl.BlockSpec((tm, D_MODEL), lambda i: (tile(i), 0)), _whole((rows, D_MODEL)), _layer((1, D_MODEL), l)]
        + [_mod_prompt(l, j, n_streams) for j in range(3)] + [_mod_streams(l, j, n_streams) for j in range(3)]
        + [pl.BlockSpec((None, D_MODEL, ODD_LOAD_COLS), lambda i: (li, 0, jnp.minimum(i, n_load - 1))),
           pl.BlockSpec((None, out_rows, D_MODEL), lambda i: (li, jnp.minimum(i, n_load - 1), 0)),
           _layer((len(POOL_WINDOWS), POOL_GROUP, POOL_GROUP), li), _layer((1, D_POOL), li),
           _whole((HEAD_PAIRS, BAND2, 2 * LANES)), _whole((POOL_BASE, D_POOL)),
           _whole((H_ATT * t, cache)), _whole((H_ATT * t, rows)),
           streams((POOL_BASE, D_POOL)), streams((cache, D_ATT)), streams((cache, D_ATT))],
        out_specs=(pl.BlockSpec((tm, D_MODEL), lambda i: (tile(i), 0)), _whole_out((rows, D_MODEL)),
                   _whole_out((POOL_BASE, D_POOL)), _whole_out((HIST, D_ATT)), _whole_out((HIST, D_ATT)),
                   pl.BlockSpec((per_step, t, D_POOL), lambda i: (step(i), 0, 0)),
                   _whole_out((rows, D_ATT)), _whole_out((rows, D_ATT))),
        scratch_shapes=[pltpu.VMEM((n_load, D_MODEL, ODD_LOAD_COLS), BF16), pltpu.VMEM((n_load, out_rows, D_MODEL), BF16),
                        pltpu.VMEM((len(POOL_WINDOWS), POOL_BASE + tm, POOL_GROUP), F32),
                        pltpu.VMEM((HIST + tm, D_ATT), BF16),
                        pltpu.VMEM((HEAD_PAIRS, LANES + VT_EXTRA, HIST + tm), BF16),
                        pltpu.VMEM((tm, D_ATT), F32), pltpu.VMEM((tm, D_ATT), F32),
                        pltpu.VMEM((rows, D_IN_ODD), F32), pltpu.VMEM((rows, D_MODEL), F32),
                        pltpu.VMEM((per_step, len(POOL_WINDOWS), POOL_BASE + t, POOL_GROUP), F32)],
        compiler_params=_params(),
        name="odd_mixer",
    )(xp, xs, norm_g, mod, mod, mod, mod, mod, mod, w_in, w_out, pw16, ps, bias_t, p0, bias_c, bias_n, p0s, kc16, vc16)


FFN_LOAD_COLS = 512


def _ffn_kernel(xp_ref, xs_ref, g_ref, shp_ref, scp_ref, gatep_ref, shs_ref, scs_ref, gates_ref,
                w_up_ref, cw_ref, w_down_ref, f0_ref, f1_ref, f2_ref, gf_ref,
                op_ref, os_ref, fp_out_ref, fs_out_ref, w_up16, w_down16, up_scr, act_scr,
                *, final_norm, n_load, n_tiles, t):
    i = pl.program_id(0)
    tm = xp_ref.shape[0]
    n_chunks = D_FF // FFN_COLS
    slabs_per_chunk = FFN_COLS // LANES
    n_slabs = 2 * D_FF // LANES

    @pl.when(i < n_load)
    def _():
        w_up16[i] = w_up_ref[...].astype(BF16)
        w_down16[i] = w_down_ref[...].astype(BF16)

    def up_weights(c0):
        return w_up16[c0 // FFN_LOAD_COLS, :, c0 % FFN_LOAD_COLS:c0 % FFN_LOAD_COLS + FFN_COLS]

    def conv_slab(j, rows):
        cols = slice(j * LANES, (j + 1) * LANES)
        return (cw_ref[0:1, cols] * up_scr[j, SUBLANES - 2:SUBLANES - 2 + rows, :]
                + cw_ref[1:2, cols] * up_scr[j, SUBLANES - 1:SUBLANES - 1 + rows, :]
                + cw_ref[2:3, cols] * up_scr[j, SUBLANES:SUBLANES + rows, :])

    def run(h16, rows, put_up, after_conv):
        def project(c):
            for off in (0, D_FF):
                c0 = off + c * FFN_COLS
                put_up(c0, _dot(h16, up_weights(c0)))

        def activate(c):
            for j in range(slabs_per_chunk):
                ja = c * slabs_per_chunk + j
                a = conv_slab(ja, rows)
                b = conv_slab(D_FF // LANES + ja, rows)
                after_conv(ja)
                after_conv(D_FF // LANES + ja)
                act_scr[0:rows, ja * LANES:(ja + 1) * LANES] = (_silu(a) * b).astype(BF16)

        project(0)
        for c in range(n_chunks):
            if c + 1 < n_chunks:
                project(c + 1)
            activate(c)
        return _dot(act_scr[0:rows, :], w_down16[...].reshape(D_FF, D_MODEL))

    @pl.when(jnp.logical_and(i >= n_load, i < n_load + n_tiles))
    def _():
        @pl.when(i == n_load)
        def _():
            for j in range(n_slabs):
                up_scr[j, 0:SUBLANES, :] = f0_ref[:, j * LANES:(j + 1) * LANES]

        x = xp_ref[...]
        h16 = _norm_mod(x, g_ref[...], shp_ref[0:1, :], scp_ref[0:1, :]).astype(BF16)

        def put_up(c0, up):
            for j in range(slabs_per_chunk):
                up_scr[c0 // LANES + j, SUBLANES:SUBLANES + tm, :] = up[:, j * LANES:(j + 1) * LANES]
            fp_out_ref[:, c0:c0 + FFN_COLS] = up[tm - SUBLANES:, :]

        def keep_tail(j):
            up_scr[j, 0:SUBLANES, :] = up_scr[j, tm:tm + SUBLANES, :]

        out = x + gatep_ref[0:1, :] * run(h16, tm, put_up, keep_tail)
        if final_norm:
            out = _rmsnorm(out, gf_ref[...])
        op_ref[...] = out

    @pl.when(i == n_load + n_tiles)
    def _():
        x = xs_ref[...]
        n_streams = x.shape[0] // t
        seg = SUBLANES + t
        rows = n_streams * seg
        h16 = _norm_mod(x, g_ref[...], _stream_rows(shs_ref[...], t), _stream_rows(scs_ref[...], t)).astype(BF16)
        for j in range(n_slabs):
            cols = slice(j * LANES, (j + 1) * LANES)
            for b in range(n_streams):
                up_scr[j, b * seg + SUBLANES - 2:b * seg + SUBLANES - 1, :] = f1_ref[b:b + 1, cols]
                up_scr[j, b * seg + SUBLANES - 1:b * seg + SUBLANES, :] = f2_ref[b:b + 1, cols]

        def put_up(c0, up):
            for b in range(n_streams):
                for j in range(slabs_per_chunk):
                    up_scr[c0 // LANES + j, b * seg + SUBLANES:(b + 1) * seg, :] = \
                        up[b * t:(b + 1) * t, j * LANES:(j + 1) * LANES]
                fs_out_ref[b, :, c0:c0 + FFN_COLS] = up[(b + 1) * t - SUBLANES:(b + 1) * t, :]

        y = run(h16, rows, put_up, lambda j: None)
        y = jnp.concatenate([y[b * seg:b * seg + t, :] for b in range(n_streams)], axis=0)
        out = x + _stream_rows(gates_ref[...], t) * y
        if final_norm:
            out = _rmsnorm(out, gf_ref[...])
        os_ref[...] = out


def _ffn(xp, xs, mod, l, norm_g, w_up, cw, w_down, f0, f1, f2, gf, final_norm, t):
    seq, rows = xp.shape[0], xs.shape[0]
    n_streams = rows // t
    tm = ROW_TILE
    n_tiles = seq // tm
    n_load = 2 * D_FF // FFN_LOAD_COLS
    load_rows = D_FF // n_load
    assert n_load * FFN_LOAD_COLS == 2 * D_FF and load_rows % (2 * SUBLANES) == 0 and n_streams * (SUBLANES + t) <= tm

    def tile(i):
        return jnp.clip(i - n_load, 0, n_tiles - 1)

    return pl.pallas_call(
        functools.partial(_ffn_kernel, final_norm=final_norm, n_load=n_load, n_tiles=n_tiles, t=t),
        out_shape=(jax.ShapeDtypeStruct((seq, D_MODEL), F32), jax.ShapeDtypeStruct((rows, D_MODEL), F32),
                   jax.ShapeDtypeStruct((SUBLANES, 2 * D_FF), F32),
                   jax.ShapeDtypeStruct((n_streams, SUBLANES, 2 * D_FF), F32)),
        grid=(n_load + n_tiles + 1,),
        in_specs=[pl.BlockSpec((tm, D_MODEL), lambda i: (tile(i), 0)), _whole((rows, D_MODEL)), _layer((1, D_MODEL), l)]
        + [_mod_prompt(l, 3 + j, n_streams) for j in range(3)] + [_mod_streams(l, 3 + j, n_streams) for j in range(3)]
        + [pl.BlockSpec((None, D_MODEL, FFN_LOAD_COLS), lambda i: (l, 0, jnp.minimum(i, n_load - 1))),
           _layer((3, 2 * D_FF), l),
           pl.BlockSpec((None, load_rows, D_MODEL), lambda i: (l, jnp.minimum(i, n_load - 1), 0)),
           _whole((SUBLANES, 2 * D_FF)), _layer((n_streams, 2 * D_FF), l), _layer((n_streams, 2 * D_FF), l),
           _whole((1, D_MODEL))],
        out_specs=(pl.BlockSpec((tm, D_MODEL), lambda i: (tile(i), 0)), _whole_out((rows, D_MODEL)),
                   _whole_out((SUBLANES, 2 * D_FF)), _whole_out((n_streams, SUBLANES, 2 * D_FF))),
        scratch_shapes=[pltpu.VMEM((n_load, D_MODEL, FFN_LOAD_COLS), BF16), pltpu.VMEM((n_load, load_rows, D_MODEL), BF16),
                        pltpu.VMEM((2 * D_FF // LANES, SUBLANES + tm, LANES), F32), pltpu.VMEM((tm, D_FF), BF16)],
        compiler_params=_params(),
        name="ffn",
    )(xp, xs, norm_g, mod, mod, mod, mod, mod, mod, w_up, cw, w_down, f0, f1, f2, gf)


def _rotary_triplet(pos):
    half = DK_RET // 2
    inv = ROPE_BASE ** (-jnp.arange(half, dtype=F32) / half)
    ang = pos.astype(F32)[:, None] * inv[None, :]
    cos, sin = jnp.cos(ang), jnp.sin(ang)
    return jnp.stack([jnp.concatenate([cos] * 4, axis=-1), jnp.concatenate([sin] * 4, axis=-1),
                      jnp.concatenate([-sin, sin, -sin, sin], axis=-1)])


def _retention_tables(tb, n_streams=1):
    idx = np.arange(tb, dtype=np.float64)
    diff = idx[:, None] - idx[None, :]
    dec1 = np.where(diff[None] >= 0, np.exp(LOG_G[:, None, None] * np.maximum(diff, 0.0)[None]), 0.0)
    dec = np.zeros((H_RET, n_streams * tb, n_streams * tb))
    for b in range(n_streams):
        dec[:, b * tb:(b + 1) * tb, b * tb:(b + 1) * tb] = dec1
    xi = np.tile(np.repeat(np.exp(LOG_G[:, None] * (idx + 1)[None, :]).T, DK_RET, axis=1), (n_streams, 1))
    zeta_t = np.tile(np.repeat(np.exp(LOG_G[:, None] * (tb - 1 - idx)[None, :]), DK_RET, axis=0), (1, n_streams))
    gmat = np.zeros((HEAD_PAIRS, LANES, LANES))
    for h in range(H_RET):
        o = (h % 2) * DK_RET
        gmat[h // 2, o:o + DK_RET, o:o + DK_RET] = np.exp(LOG_G[h] * tb)
    return tuple(jnp.asarray(a, F32) for a in (dec, xi, zeta_t, gmat))


def _pair_state(s):
    lead = s.shape[:-3]
    s = s.reshape(lead + (HEAD_PAIRS, 2, DK_RET, DK_RET))
    z = jnp.zeros_like(s[..., 0, :, :])
    top = jnp.concatenate([s[..., 0, :, :], z], axis=-1)
    bot = jnp.concatenate([z, s[..., 1, :, :]], axis=-1)
    return jnp.concatenate([top, bot], axis=-2)


def _unpair_state(r):
    a = r[..., :DK_RET, :DK_RET]
    b = r[..., DK_RET:, DK_RET:]
    s = jnp.stack([a, b], axis=-3)
    return s.reshape(r.shape[:-3] + (H_RET, DK_RET, DK_RET))


def _band_bias(table):
    nq, nk = 2 * CHUNK, BAND2
    period = nq + nk
    j = np.arange(period)
    j = np.where(j < nk, j, j - period)
    idx = np.clip(HIST - j, -(CHUNK - 1), REL_CLIP) + (CHUNK - 1)
    one_period = table[:, idx].astype(F32)
    flat = jnp.tile(one_period, (1, nq + 1))[:, :nq * (period - 1)]
    return flat.reshape(-1, nq, period - 1)[:, :, :nk]


def _band_bias_t(raw):
    qq = np.arange(2 * CHUNK)[:, None]
    kk = np.arange(BAND2)[None, :]
    valid = np.where(qq < CHUNK, kk < BAND, kk >= CHUNK)
    b = jnp.where(valid, raw * LOG2E, NEG_INF).reshape(HEAD_PAIRS, 2, 2, CHUNK, BAND2)
    return jnp.transpose(b, (0, 4, 2, 1, 3)).reshape(HEAD_PAIRS, BAND2, 2 * LANES)


def _tail_rows(a, n):
    return a[..., a.shape[-2] - n:, :]


def kernel(x_prompt, x_sample, state_ret, state_sconv, state_pool, cache_k, cache_v, state_ffn, c_prompt, c_sample,
           norm_mix, norm_ffn, norm_final, w_ada, b_ada, w_in_even, w_out_even, ret_gn_gain, sconv_w, w_in_odd,
           w_out_odd, pool_w, pool_scale, rel_bias_table, ffn_w_up, ffn_conv, ffn_w_down):
    n_prompt, seq, _ = x_prompt.shape
    n_streams, t_s, _ = x_sample.shape
    assert n_prompt == 1 and n_streams % MOD_ROWS_PROMPT == 0
    rows_s = n_streams * t_s
    n_even, n_odd = (DEPTH + 1) // 2, DEPTH // 2

    c_all = jnp.concatenate([c_sample, c_prompt], axis=0)
    mod = _ada(jnp.pad(c_all, ((0, MOD_ROWS_PROMPT - 1), (0, 0))), w_ada, b_ada)

    bf = lambda w: w.astype(BF16)
    pool_w16 = bf(pool_w)
    norm_mix3, norm_ffn3 = norm_mix.reshape(DEPTH, 1, D_MODEL), norm_ffn.reshape(DEPTH, 1, D_MODEL)
    gain3, pool_scale3 = ret_gn_gain.reshape(n_even, 1, D_RET), pool_scale.reshape(n_odd, 1, D_POOL)
    norm_final2 = norm_final.reshape(1, D_MODEL)

    n_tiles = seq // ROW_TILE
    tabs_p = (_rotary_triplet(jnp.arange(n_tiles, dtype=jnp.int32) * ROW_TILE),
              _rotary_triplet(jnp.arange(ROW_TILE, dtype=jnp.int32))) + _retention_tables(RET_BLOCK)
    rot_s = _rotary_triplet(PAST_LEN + jnp.arange(t_s, dtype=jnp.int32))
    tabs_s = (jnp.tile(rot_s[0], (n_streams, 1)), jnp.tile(rot_s[2], (n_streams, 1))) \
        + _retention_tables(t_s, n_streams)

    cache_len = cache_k.shape[2]
    assert cache_len == HIST and t_s <= CHUNK
    bias_raw = [_band_bias(rel_bias_table[i]) for i in range(n_odd)]
    bias_p = [_band_bias_t(b) for b in bias_raw]
    bias_c = [b[:, :t_s, :cache_len].reshape(H_ATT * t_s, cache_len) for b in bias_raw]
    bias_n = [jnp.tile(b[:, :t_s, cache_len:cache_len + t_s], (1, 1, n_streams)).reshape(H_ATT * t_s, rows_s)
              for b in bias_raw]

    paired = _pair_state(state_ret)
    s_stack = jnp.transpose(paired, (0, 2, 1, 3, 4)).reshape(n_even, HEAD_PAIRS, n_streams * LANES, LANES)
    s_wide = jnp.transpose(paired, (0, 2, 3, 1, 4)).reshape(n_even, HEAD_PAIRS, LANES, n_streams * LANES)
    u1, u2 = state_sconv[:, :, 0, :], state_sconv[:, :, 1, :]
    f1, f2 = state_ffn[:, :, 0, :], state_ffn[:, :, 1, :]
    p0_s = jnp.pad(state_pool, ((0, 0), (0, 0), (POOL_BASE - POOL_BUF, 0), (0, 0)))
    kc16 = cache_k.reshape(n_odd, n_streams, cache_len, D_ATT).astype(BF16)
    vc16 = cache_v.reshape(n_odd, n_streams, cache_len, D_ATT).astype(BF16)

    xp = x_prompt.reshape(seq, D_MODEL)
    xs = x_sample.reshape(rows_s, D_MODEL)

    ret_p, ret_s, sconv_p, sconv_s, pool_p, pool_s = [], [], [], [], [], []
    k_p, k_s, v_p, v_s, ffn_p, ffn_s = [], [], [], [], [], []
    for l in range(DEPTH):
        i = l // 2
        if l % 2 == 0:
            xp, xs, r_new, u_new, s_new, u_all = _even(
                xp, xs, mod, l, norm_mix3, w_in_even, w_out_even, gain3, sconv_w, tabs_p, tabs_s,
                jnp.zeros((HEAD_PAIRS, LANES, LANES), F32), jnp.zeros((SUBLANES, D_SCONV), F32),
                s_stack, s_wide, u1, u2, t_s)
            ret_p.append(_unpair_state(r_new)[None])
            sconv_p.append(_tail_rows(u_new, 2)[None])
            s_new = jnp.transpose(s_new.reshape(HEAD_PAIRS, LANES, n_streams, LANES), (2, 0, 1, 3))
            ret_s.append(_unpair_state(s_new))
            sconv_s.append(_tail_rows(u_all.reshape(n_streams, t_s, D_SCONV), 2))
        else:
            xp, xs, p_new, k_new, v_new, ps_new, ks_new, vs_new = _odd(
                xp, xs, mod, l, norm_mix3, w_in_odd, w_out_odd, pool_w16, pool_scale3, bias_p[i],
                jnp.zeros((POOL_BASE, D_POOL), F32), bias_c[i], bias_n[i], p0_s, kc16, vc16, t_s, PAST_LEN)
            pool_p.append(_tail_rows(p_new, POOL_BUF)[None])
            k_p.append(k_new.reshape(1, HIST, H_ATT, DH_ATT))
            v_p.append(v_new.reshape(1, HIST, H_ATT, DH_ATT))
            p_new, k_new, v_new = ps_new, ks_new, vs_new
            pool_s.append(_tail_rows(p_new, POOL_BUF))
            k_s.append(k_new.reshape(n_streams, t_s, H_ATT, DH_ATT))
            v_s.append(v_new.reshape(n_streams, t_s, H_ATT, DH_ATT))
        last = l == DEPTH - 1
        xp, xs, f_new, fs_new = _ffn(xp, xs, mod, l, norm_ffn3, ffn_w_up, ffn_conv, ffn_w_down,
                                     jnp.zeros((SUBLANES, 2 * D_FF), F32), f1, f2, norm_final2, last, t_s)
        ffn_p.append(_tail_rows(f_new, 2)[None])
        ffn_s.append(_tail_rows(fs_new, 2))

    st = jnp.stack
    return (xp.reshape(1, seq, D_MODEL), xs.reshape(n_streams, t_s, D_MODEL),
            st(ret_p), st(ret_s), st(sconv_p), st(sconv_s), st(pool_p), st(pool_s),
            st(k_p), st(k_s), st(v_p), st(v_s), st(ffn_p), st(ffn_s))
```

```python
import functools

import numpy as np
import jax
import jax.numpy as jnp
from jax import lax
from jax.experimental import pallas as pl
from jax.experimental.pallas import tpu as pltpu

F32 = jnp.float32
BF16 = jnp.bfloat16

D_MODEL = 1024
DEPTH = 4
PAST_LEN = 4096
CHUNK = 64
H_RET = 8
DK_RET = 64
D_RET = H_RET * DK_RET
ROPE_BASE = 10000.0
D_SCONV = D_MODEL - D_RET
POOL_WINDOWS = (2, 4, 8, 16)
D_POOL = D_MODEL // 2
POOL_GROUP = D_POOL // len(POOL_WINDOWS)
POOL_BUF = max(POOL_WINDOWS) - 1
H_ATT = 8
DH_ATT = 64
D_ATT = H_ATT * DH_ATT
N_PREV_CHUNKS = 8
REL_CLIP = 256
D_FF = 2816
EPS = 1e-6
NEG_INF = -1e30
D_IN_EVEN = 4 * D_RET + 3 * D_SCONV
D_IN_ODD = D_POOL + 3 * D_ATT

LANES = 128
SUBLANES = 8
HEAD_PAIRS = H_RET // 2
ROW_TILE = 512
RET_BLOCK = 128
FFN_COLS = 256
BAND = (N_PREV_CHUNKS + 1) * CHUNK
BAND2 = BAND + CHUNK
LOG2E = 1.4426950408889634
VT_EXTRA = 16
HIST = N_PREV_CHUNKS * CHUNK
POOL_BASE = 2 * SUBLANES
VMEM_LIMIT = 56 * 1024 * 1024

LOG_G = np.log1p(-(2.0 ** (-5.0 - np.arange(H_RET, dtype=np.float64))))


def _params(n_axes=1):
    return pltpu.CompilerParams(dimension_semantics=("arbitrary",) * n_axes, vmem_limit_bytes=VMEM_LIMIT)


def _whole(shape):
    nd = len(shape)
    return pl.BlockSpec(shape, lambda i: (0,) * nd, pipeline_mode=pl.Buffered(1))


def _whole_out(shape):
    nd = len(shape)
    return pl.BlockSpec(shape, lambda i: (0,) * nd)


def _layer(shape, l):
    nd = len(shape)
    return pl.BlockSpec((None,) + tuple(shape), lambda i: (l,) + (0,) * nd, pipeline_mode=pl.Buffered(1))


MOD_ROWS_PROMPT = SUBLANES


def _mod_prompt(l, j, n_streams):
    return pl.BlockSpec((None, MOD_ROWS_PROMPT, D_MODEL), lambda i: (l, n_streams // MOD_ROWS_PROMPT, j),
                        pipeline_mode=pl.Buffered(1))


def _mod_streams(l, j, n_streams):
    return pl.BlockSpec((None, n_streams, D_MODEL), lambda i: (l, 0, j), pipeline_mode=pl.Buffered(1))


def _stream_rows(m, t):
    return jnp.concatenate([jnp.broadcast_to(m[b:b + 1, :], (t, m.shape[1])) for b in range(m.shape[0])], axis=0)


def _dot(a, b):
    return jnp.dot(a, b, preferred_element_type=F32)


def _dot_nt(a, b):
    return lax.dot_general(a, b, (((1,), (1,)), ((), ())), preferred_element_type=F32)


def _rmsnorm(x, g):
    return x * lax.rsqrt(jnp.mean(x * x, axis=-1, keepdims=True) + EPS) * g


def _norm_mod(x, g, shift, scale):
    return _rmsnorm(x, g) * (1.0 + scale) + shift


def _silu(x):
    return x * (1.0 / (1.0 + jnp.exp(-x)))


def _low_half(shape):
    return (lax.broadcasted_iota(jnp.int32, shape, len(shape) - 1) % LANES) < DK_RET


def _shift_rows(u, prev8, s):
    rolled = pltpu.roll(u, s, axis=0)
    prolled = pltpu.roll(prev8, s, axis=0)
    row = lax.broadcasted_iota(jnp.int32, prev8.shape, 0)
    first = jnp.where(row < s, prolled, rolled[0:SUBLANES])
    return jnp.concatenate([first, rolled[SUBLANES:]], axis=0)


def _shift_rows_streams(u, older, newer, s, t):
    row_in_stream = lax.broadcasted_iota(jnp.int32, u.shape, 0) % t
    rolled = pltpu.roll(u, s, axis=0)
    if s == 1:
        return jnp.where(row_in_stream == 0, _stream_rows(newer, t), rolled)
    return jnp.where(row_in_stream == 0, _stream_rows(older, t),
                     jnp.where(row_in_stream == 1, _stream_rows(newer, t), rolled))


def _ada_kernel(c_ref, w_ref, b_ref, o_ref):
    c = c_ref[...]
    o_ref[0] = _dot(_silu(c).astype(BF16), w_ref[0].astype(BF16)) + b_ref[0]


def _ada(c_all, w_ada, b_ada):
    rows = c_all.shape[0]
    tn = 1536
    return pl.pallas_call(
        _ada_kernel,
        out_shape=jax.ShapeDtypeStruct((DEPTH, rows, 6 * D_MODEL), F32),
        grid=(DEPTH, 6 * D_MODEL // tn),
        in_specs=[
            pl.BlockSpec((rows, D_MODEL), lambda l, j: (0, 0)),
            pl.BlockSpec((1, D_MODEL, tn), lambda l, j: (l, 0, j)),
            pl.BlockSpec((1, 1, tn), lambda l, j: (l, 0, j)),
        ],
        out_specs=pl.BlockSpec((1, rows, tn), lambda l, j: (l, 0, j)),
        compiler_params=_params(2),
        name="ada_mod",
    )(c_all, w_ada, b_ada.reshape(DEPTH, 1, 6 * D_MODEL))


def _rotary_pair(x, cos, sin_signed):
    lane = lax.broadcasted_iota(jnp.int32, x.shape, 1)
    first_half = (lane % DK_RET) < (DK_RET // 2)
    swapped = jnp.where(first_half, pltpu.roll(x, LANES - DK_RET // 2, axis=1),
                        pltpu.roll(x, DK_RET // 2, axis=1))
    return x * cos + swapped * sin_signed


def _retention_block(proj, cos, sin_signed, dec_ref, xi, zeta_t_ref, gain, cross_fn, update_fn):
    tb = proj.shape[0]
    low = _low_half((tb, LANES))
    inv_n = 1.0 / DK_RET
    pairs = range(HEAD_PAIRS)
    cols = [slice(p * LANES, (p + 1) * LANES) for p in pairs]
    q = [_rotary_pair(proj[:, cols[p]], cos, sin_signed) for p in pairs]
    k_t = [(_rotary_pair(proj[:, D_RET + p * LANES:D_RET + (p + 1) * LANES], cos, sin_signed)
            * (DK_RET ** -0.5)).T for p in pairs]
    v16 = [proj[:, 2 * D_RET + p * LANES:2 * D_RET + (p + 1) * LANES].astype(BF16) for p in pairs]
    k_t16 = [k_t[p].astype(BF16) for p in pairs]
    kz_t16 = [(k_t[p] * zeta_t_ref[cols[p], :]).astype(BF16) for p in pairs]
    q16 = [q[p].astype(BF16) for p in pairs]
    qe16 = [jnp.where(low, q[p], 0.0).astype(BF16) for p in pairs]
    qo16 = [jnp.where(low, 0.0, q[p]).astype(BF16) for p in pairs]
    s_e = [(_dot(qe16[p], k_t16[p]) * dec_ref[2 * p]).astype(BF16) for p in pairs]
    s_o = [(_dot(qo16[p], k_t16[p]) * dec_ref[2 * p + 1]).astype(BF16) for p in pairs]
    cross = [cross_fn(p, q16[p]) * xi[:, cols[p]] for p in pairs]
    o = [jnp.where(low, _dot(s_e[p], v16[p]), _dot(s_o[p], v16[p])) + cross[p] for p in pairs]
    for p in pairs:
        update_fn(p, kz_t16[p], v16[p])
    outs = []
    for p in pairs:
        s_lo = jnp.sum(jnp.where(low, o[p], 0.0), axis=-1, keepdims=True)
        s_hi = jnp.sum(jnp.where(low, 0.0, o[p]), axis=-1, keepdims=True)
        d = o[p] - jnp.where(low, s_lo, s_hi) * inv_n
        d2 = d * d
        v_lo = jnp.sum(jnp.where(low, d2, 0.0), axis=-1, keepdims=True)
        v_hi = jnp.sum(jnp.where(low, 0.0, d2), axis=-1, keepdims=True)
        on = d * lax.rsqrt(jnp.where(low, v_lo, v_hi) * inv_n + EPS)
        g = proj[:, 3 * D_RET + p * LANES:3 * D_RET + (p + 1) * LANES]
        outs.append(_silu(g) * (on * gain[:, cols[p]]))
    return jnp.concatenate(outs, axis=-1)


def _even_tail(x, proj, ret_out, conv_in_shift, cw_ref, gate, w_out16):
    gate_b = proj[:, 4 * D_RET:4 * D_RET + D_SCONV]
    u = proj[:, 4 * D_RET + D_SCONV:4 * D_RET + 2 * D_SCONV] * proj[:, 4 * D_RET + 2 * D_SCONV:]
    conv = cw_ref[0:1, :] * conv_in_shift(u, 2) + cw_ref[1:2, :] * conv_in_shift(u, 1) + cw_ref[2:3, :] * u
    mixed = jnp.concatenate([ret_out, gate_b * conv], axis=-1).astype(BF16)
    return x + gate * _dot(mixed, w_out16), u


EVEN_LOAD_COLS = 512
EVEN_OUT_LOAD_ROWS = 256


def _even_kernel(xp_ref, xs_ref, g_ref, shp_ref, scp_ref, gatep_ref, shs_ref, scs_ref, gates_ref,
                 w_in_ref, w_out_ref, gain_ref, cw_ref,
                 rot_tile_ref, rot_row_ref, dec_ref, xi_ref, zt_ref, gmat_ref, r0_ref, u0_ref,
                 cos_s_ref, sin_s_ref, dec_s_ref, xi_s_ref, zt_s_ref, gmat_s_ref, s_stack_ref, s_wide_ref, u1_ref, u2_ref,
                 op_ref, os_ref, r_out_ref, u_out_ref, s_out_ref, us_out_ref,
                 w_in16, w_out16, r_scr, u_scr, *, n_load, n_tiles, t):
    i = pl.program_id(0)
    n_out_load = w_out16.shape[0]

    @pl.when(i < n_load)
    def _():
        w_in16[i] = w_in_ref[...].astype(BF16)

    @pl.when(i < n_out_load)
    def _():
        w_out16[i] = w_out_ref[...].astype(BF16)

    def in_project(h16):
        return jnp.concatenate([_dot(h16, w_in16[c]) for c in range(n_load)], axis=-1)

    def out_weights():
        return w_out16[...].reshape(D_MODEL, D_MODEL)

    @pl.when(jnp.logical_and(i >= n_load, i < n_load + n_tiles))
    def _():
        tile = i - n_load
        cos_0, sin_0, ssin_0 = (rot_tile_ref[j, pl.ds(tile, 1), :] for j in range(3))
        cos = cos_0 * rot_row_ref[0] - sin_0 * rot_row_ref[1]
        sin_signed = ssin_0 * rot_row_ref[0] + cos_0 * rot_row_ref[2]

        n_slabs = D_SCONV // LANES

        @pl.when(tile == 0)
        def _():
            r_scr[...] = r0_ref[...]
            for j in range(n_slabs):
                u_scr[j, 0:SUBLANES, :] = u0_ref[:, j * LANES:(j + 1) * LANES]

        x = xp_ref[...]
        h = _norm_mod(x, g_ref[...], shp_ref[0:1, :], scp_ref[0:1, :]).astype(BF16)
        proj = in_project(h)
        tm = x.shape[0]
        tb = dec_ref.shape[1]
        r_i = lax.broadcasted_iota(jnp.int32, (LANES, LANES), 0) < DK_RET
        c_i = lax.broadcasted_iota(jnp.int32, (LANES, LANES), 1) < DK_RET
        blockdiag = r_i == c_i

        def cross_fn(p, q16):
            return _dot(q16, r_scr[p].astype(BF16))

        def update_fn(p, kz_t16, v16):
            r_scr[p] = r_scr[p] * gmat_ref[p] + jnp.where(blockdiag, _dot(kz_t16, v16), 0.0)

        rets = []
        for r in range(tm // tb):
            rows = slice(r * tb, (r + 1) * tb)
            rets.append(_retention_block(proj[rows, :], cos[rows, :], sin_signed[rows, :], dec_ref, xi_ref[...],
                                         zt_ref, gain_ref[...], cross_fn, update_fn))
        ret_out = jnp.concatenate(rets, axis=0)

        def delayed(u, s):
            if s == 2:
                for j in range(n_slabs):
                    u_scr[j, SUBLANES:SUBLANES + tm, :] = u[:, j * LANES:(j + 1) * LANES]
            return jnp.concatenate([u_scr[j, SUBLANES - s:SUBLANES - s + tm, :] for j in range(n_slabs)], axis=-1)

        out, u = _even_tail(x, proj, ret_out, delayed, cw_ref, gatep_ref[0:1, :], out_weights())
        for j in range(n_slabs):
            u_scr[j, 0:SUBLANES, :] = u_scr[j, tm:tm + SUBLANES, :]
        op_ref[...] = out
        r_out_ref[...] = r_scr[...]
        u_out_ref[...] = u[tm - SUBLANES:, :]

    @pl.when(i == n_load + n_tiles)
    def _():
        x = xs_ref[...]
        rows = x.shape[0]
        n_streams = rows // t
        wide = n_streams * LANES
        h = _norm_mod(x, g_ref[...], _stream_rows(shs_ref[...], t), _stream_rows(scs_ref[...], t)).astype(BF16)
        proj = in_project(h)
        own = (lax.broadcasted_iota(jnp.int32, (rows, wide), 0) // t
               == lax.broadcasted_iota(jnp.int32, (rows, wide), 1) // LANES)
        r_i = lax.broadcasted_iota(jnp.int32, (LANES, wide), 0) < DK_RET
        blockdiag = r_i == _low_half((LANES, wide))

        def expand(a16):
            tiled = jnp.concatenate([a16.astype(F32)] * n_streams, axis=-1)
            return jnp.where(own, tiled, 0.0).astype(BF16)

        def cross_fn(p, q16):
            return _dot(expand(q16), s_stack_ref[p].astype(BF16))

        def update_fn(p, kz_t16, v16):
            kv = _dot(kz_t16, expand(v16))
            decay = jnp.concatenate([gmat_s_ref[p]] * n_streams, axis=-1)
            s_out_ref[p] = s_wide_ref[p] * decay + jnp.where(blockdiag, kv, 0.0)

        ret_out = _retention_block(proj, cos_s_ref[...], sin_s_ref[...], dec_s_ref, xi_s_ref[...], zt_s_ref,
                                   gain_ref[...], cross_fn, update_fn)
        out, u = _even_tail(x, proj, ret_out, lambda u, s: _shift_rows_streams(u, u1_ref[...], u2_ref[...], s, t),
                            cw_ref, _stream_rows(gates_ref[...], t), out_weights())
        os_ref[...] = out
        us_out_ref[...] = u


def _even(xp, xs, mod, l, norm_g, w_in, w_out, gain, cw, tabs_p, tabs_s, r0, u0, s_stack, s_wide, u1, u2, t):
    seq, rows = xp.shape[0], xs.shape[0]
    n_streams = rows // t
    wide = n_streams * LANES
    tm, tb = ROW_TILE, RET_BLOCK
    n_tiles = seq // tm
    n_load = D_IN_EVEN // EVEN_LOAD_COLS
    n_out_load = D_MODEL // EVEN_OUT_LOAD_ROWS
    assert n_load * EVEN_LOAD_COLS == D_IN_EVEN and n_out_load <= n_load
    rot_tile, rot_row, dec, xi, zt, gmat = tabs_p
    cos_s, sin_s, dec_s, xi_s, zt_s, gmat_s = tabs_s
    li = l // 2
    state = (HEAD_PAIRS, LANES, LANES)

    def tile(i):
        return jnp.clip(i - n_load, 0, n_tiles - 1)

    return pl.pallas_call(
        functools.partial(_even_kernel, n_load=n_load, n_tiles=n_tiles, t=t),
        out_shape=(jax.ShapeDtypeStruct((seq, D_MODEL), F32), jax.ShapeDtypeStruct((rows, D_MODEL), F32),
                   jax.ShapeDtypeStruct(state, F32), jax.ShapeDtypeStruct((SUBLANES, D_SCONV), F32),
                   jax.ShapeDtypeStruct((HEAD_PAIRS, LANES, wide), F32), jax.ShapeDtypeStruct((rows, D_SCONV), F32)),
        grid=(n_load + n_tiles + 1,),
        in_specs=[pl.BlockSpec((tm, D_MODEL), lambda i: (tile(i), 0)), _whole((rows, D_MODEL)), _layer((1, D_MODEL), l)]
        + [_mod_prompt(l, j, n_streams) for j in range(3)] + [_mod_streams(l, j, n_streams) for j in range(3)]
        + [pl.BlockSpec((None, D_MODEL, EVEN_LOAD_COLS), lambda i: (li, 0, jnp.minimum(i, n_load - 1))),
           pl.BlockSpec((None, EVEN_OUT_LOAD_ROWS, D_MODEL), lambda i: (li, jnp.minimum(i, n_out_load - 1), 0)),
           _layer((1, D_RET), li), _layer((3, D_SCONV), li),
           _whole((3, n_tiles, LANES)), _whole((3, tm, LANES)),
           _whole((H_RET, tb, tb)), _whole((tb, D_RET)), _whole((D_RET, tb)),
           _whole(state), _whole(state), _whole((SUBLANES, D_SCONV)),
           _whole((rows, LANES)), _whole((rows, LANES)),
           _whole((H_RET, rows, rows)), _whole((rows, D_RET)), _whole((D_RET, rows)), _whole(state),
           _layer((HEAD_PAIRS, wide, LANES), li), _layer((HEAD_PAIRS, LANES, wide), li),
           _layer((n_streams, D_SCONV), li), _layer((n_streams, D_SCONV), li)],
        out_specs=(pl.BlockSpec((tm, D_MODEL), lambda i: (tile(i), 0)), _whole_out((rows, D_MODEL)),
                   _whole_out(state), _whole_out((SUBLANES, D_SCONV)),
                   _whole_out((HEAD_PAIRS, LANES, wide)), _whole_out((rows, D_SCONV))),
        scratch_shapes=[pltpu.VMEM((n_load, D_MODEL, EVEN_LOAD_COLS), BF16),
                        pltpu.VMEM((n_out_load, EVEN_OUT_LOAD_ROWS, D_MODEL), BF16),
                        pltpu.VMEM(state, F32), pltpu.VMEM((D_SCONV // LANES, SUBLANES + tm, LANES), F32)],
        compiler_params=_params(),
        name="even_mixer",
    )(xp, xs, norm_g, mod, mod, mod, mod, mod, mod, w_in, w_out, gain, cw,
      rot_tile, rot_row, dec, xi, zt, gmat, r0, u0, cos_s, sin_s, dec_s, xi_s, zt_s, gmat_s, s_stack, s_wide, u1, u2)


def _pool_put(hist_ref, row0, block):
    for gi in range(len(POOL_WINDOWS)):
        hist_ref[gi, row0:row0 + block.shape[0], :] = block[:, gi * POOL_GROUP:(gi + 1) * POOL_GROUP]


def _pool_get(hist_ref, row0, rows):
    return jnp.concatenate([hist_ref[gi, row0:row0 + rows, :] for gi in range(len(POOL_WINDOWS))], axis=-1)


def _pool(hist_ref, p, pos, pool_w_ref, scale):
    t = p.shape[0]
    outs = []
    for gi, w in enumerate(POOL_WINDOWS):
        cols = slice(gi * POOL_GROUP, (gi + 1) * POOL_GROUP)
        win = p[:, cols]
        for d in range(1, w):
            win = win + hist_ref[gi, POOL_BASE - d:POOL_BASE - d + t, :]
        inv_cnt = 1.0 / jnp.minimum(pos + 1, w).astype(F32)
        pooled = win * inv_cnt - p[:, cols]
        outs.append(_dot(pooled.astype(BF16), pool_w_ref[gi]) * scale[:, cols])
    return jnp.concatenate(outs, axis=-1)


def _attend_scores(kbs, q_as, q_bs, biases):
    low = _low_half((CHUNK, LANES))
    out = []
    for kb, q_a, q_b, bias in zip(kbs, q_as, q_bs, biases):
        qs = jnp.concatenate([jnp.where(low, q_a, 0.0), jnp.where(low, 0.0, q_a),
                              jnp.where(low, q_b, 0.0), jnp.where(low, 0.0, q_b)], axis=0)
        qbd = qs.T.astype(BF16)
        half = (kb.shape[0] // 2) // (2 * SUBLANES) * (2 * SUBLANES)
        out.append(jnp.concatenate([_dot(kb[:half], qbd), _dot(kb[half:], qbd)], axis=0) + bias)
    return out


def _attend_values(scores, vts):
    low = _low_half((CHUNK, LANES))
    n = len(scores)
    e16 = [jnp.exp2(s - jnp.max(s, axis=0, keepdims=True)).astype(BF16) for s in scores]
    o_t = [jnp.concatenate([_dot(vts[j][:DK_RET], e16[j]), _dot(vts[j][DK_RET:], e16[j])], axis=0) for j in range(n)]
    o_t = [o[:LANES] * (1.0 / o[LANES:LANES + 1]) for o in o_t]
    o_t = [o.T for o in o_t]
    return [(jnp.where(low, o[0:CHUNK], o[CHUNK:2 * CHUNK]), jnp.where(low, o[2 * CHUNK:3 * CHUNK], o[3 * CHUNK:]))
            for o in o_t]


ODD_LOAD_COLS = 512
SAMPLE_STREAMS_PER_STEP = 4


def _odd_kernel(xp_ref, xs_ref, g_ref, shp_ref, scp_ref, gatep_ref, shs_ref, scs_ref, gates_ref,
                w_in_ref, w_out_ref, pw_ref, ps_ref, bias_ref, p0_ref, bias_c_ref, bias_n_ref, p0s_ref, kc_ref, vc_ref,
                op_ref, os_ref, p_out_ref, k_out_ref, v_out_ref, ps_out_ref, ks_out_ref, vs_out_ref,
                w_in16, w_out16, pbuf, kbuf, vtbuf, q_scr, att_scr, proj_scr, mix_scr, pbuf_s,
                *, n_load, n_tiles, pos0, t):
    i = pl.program_id(0)
    tm = xp_ref.shape[0]
    rows = xs_ref.shape[0]
    per_step = kc_ref.shape[0]

    @pl.when(i < n_load)
    def _():
        w_in16[i] = w_in_ref[...].astype(BF16)
        w_out16[i] = w_out_ref[...].astype(BF16)

    def in_project(h16):
        return jnp.concatenate([_dot(h16, w_in16[c]) for c in range(n_load)], axis=-1)

    def out_weights():
        return w_out16[...].reshape(D_MODEL, D_MODEL)

    @pl.when(jnp.logical_and(i >= n_load, i < n_load + n_tiles))
    def _():
        tile = i - n_load

        @pl.when(tile == 0)
        def _():
            _pool_put(pbuf, 0, p0_ref[...])
            kbuf[0:HIST, :] = jnp.zeros((HIST, D_ATT), BF16)
            vtbuf[:, :, 0:HIST] = jnp.zeros((HEAD_PAIRS, LANES + VT_EXTRA, HIST), BF16)

        x = xp_ref[...]
        h = _norm_mod(x, g_ref[...], shp_ref[0:1, :], scp_ref[0:1, :]).astype(BF16)
        proj = in_project(h)
        p = proj[:, :D_POOL]
        q_scr[...] = proj[:, D_POOL:D_POOL + D_ATT] * (DH_ATT ** -0.5 * LOG2E)
        k = proj[:, D_POOL + D_ATT:D_POOL + 2 * D_ATT]
        v = proj[:, D_POOL + 2 * D_ATT:]
        _pool_put(pbuf, POOL_BASE, p)
        kbuf[HIST:HIST + tm, :] = k.astype(BF16)
        ones_row = jnp.where(lax.broadcasted_iota(jnp.int32, (VT_EXTRA, tm), 0) == 0, 1.0, 0.0).astype(BF16)
        for pr in range(HEAD_PAIRS):
            vtbuf[pr, 0:LANES, HIST:HIST + tm] = v[:, pr * LANES:(pr + 1) * LANES].T.astype(BF16)
            vtbuf[pr, LANES:LANES + VT_EXTRA, HIST:HIST + tm] = ones_row
        k_out_ref[...] = k[tm - HIST:, :]
        v_out_ref[...] = v[tm - HIST:, :]

        pos = tile * tm + lax.broadcasted_iota(jnp.int32, (tm, 1), 0)
        pool_out = _pool(pbuf, p, pos, pw_ref, ps_ref[...])

        def attend_tile(first_tile):
            pairs = range(HEAD_PAIRS)
            lanes = [slice(pr * LANES, (pr + 1) * LANES) for pr in pairs]
            n_blocks = tm // (2 * CHUNK)
            skip = [max(HIST - jb * 2 * CHUNK, 0) if first_tile else 0 for jb in range(n_blocks)]

            def scores(jb):
                r0 = jb * 2 * CHUNK
                return _attend_scores([kbuf[r0 + skip[jb]:r0 + BAND2, lanes[pr]] for pr in pairs],
                                      [q_scr[r0:r0 + CHUNK, lanes[pr]] for pr in pairs],
                                      [q_scr[r0 + CHUNK:r0 + 2 * CHUNK, lanes[pr]] for pr in pairs],
                                      [bias_ref[pr, skip[jb]:, :] for pr in pairs])

            s_next = scores(0)
            for jb in range(n_blocks):
                r0 = jb * 2 * CHUNK
                s_cur = s_next
                if jb + 1 < n_blocks:
                    s_next = scores(jb + 1)
                outs = _attend_values(s_cur, [vtbuf[pr, :, r0 + skip[jb]:r0 + BAND2] for pr in pairs])
                att_scr[r0:r0 + CHUNK, :] = jnp.concatenate([o[0] for o in outs], axis=-1)
                att_scr[r0 + CHUNK:r0 + 2 * CHUNK, :] = jnp.concatenate([o[1] for o in outs], axis=-1)

        pl.when(tile == 0)(functools.partial(attend_tile, True))
        pl.when(tile > 0)(functools.partial(attend_tile, False))

        kbuf[0:HIST, :] = kbuf[tm:tm + HIST, :]
        vtbuf[:, :, 0:HIST] = vtbuf[:, :, tm:tm + HIST]
        tail = _pool_get(pbuf, tm, POOL_BASE)
        _pool_put(pbuf, 0, tail)
        p_out_ref[...] = tail

        mixed = jnp.concatenate([pool_out, att_scr[...]], axis=-1).astype(BF16)
        op_ref[...] = x + gatep_ref[0:1, :] * _dot(mixed, out_weights())

    @pl.when(i >= n_load + n_tiles)
    def _():
        step = i - (n_load + n_tiles)

        @pl.when(step == 0)
        def _():
            h = _norm_mod(xs_ref[...], g_ref[...], _stream_rows(shs_ref[...], t),
                          _stream_rows(scs_ref[...], t)).astype(BF16)
            proj = in_project(h)
            proj_scr[...] = proj
            ks_out_ref[...] = proj[:, D_POOL + D_ATT:D_POOL + 2 * D_ATT]
            vs_out_ref[...] = proj[:, D_POOL + 2 * D_ATT:]

        kn = proj_scr[:, D_POOL + D_ATT:D_POOL + 2 * D_ATT].astype(BF16)
        vn = proj_scr[:, D_POOL + 2 * D_ATT:].astype(BF16)
        pos = pos0 + lax.broadcasted_iota(jnp.int32, (t, 1), 0)
        head_of_lane = lax.broadcasted_iota(jnp.int32, (t, D_ATT), 1) // DH_ATT
        stream_of_col = lax.broadcasted_iota(jnp.int32, (H_ATT * t, rows), 1) // t
        js = range(per_step)
        r0 = [pl.multiple_of((step * per_step + j) * t, t) for j in js]
        proj = [proj_scr[pl.ds(r0[j], t), :] for j in js]
        q_heads = []
        for j in js:
            q = proj[j][:, D_POOL:D_POOL + D_ATT] * (DH_ATT ** -0.5)
            q_heads.append(jnp.concatenate([jnp.where(head_of_lane == hh, q, 0.0) for hh in range(H_ATT)],
                                           axis=0).astype(BF16))
        s_c = [_dot_nt(q_heads[j], kc_ref[j]) + bias_c_ref[...] for j in js]
        s_n = [jnp.where(stream_of_col == step * per_step + j, _dot_nt(q_heads[j], kn) + bias_n_ref[...], NEG_INF)
               for j in js]
        pool_out = []
        for j in js:
            p = proj[j][:, :D_POOL]
            _pool_put(pbuf_s.at[j], 0, p0s_ref[j])
            _pool_put(pbuf_s.at[j], POOL_BASE, p)
            ps_out_ref[j] = p
            pool_out.append(_pool(pbuf_s.at[j], p, pos, pw_ref, ps_ref[...]))
        m = [jnp.maximum(jnp.max(s_c[j], axis=-1, keepdims=True), jnp.max(s_n[j], axis=-1, keepdims=True)) for j in js]
        e_c = [jnp.exp(s_c[j] - m[j]) for j in js]
        e_n = [jnp.exp(s_n[j] - m[j]) for j in js]
        inv_l = [1.0 / (jnp.sum(e_c[j], axis=-1, keepdims=True) + jnp.sum(e_n[j], axis=-1, keepdims=True)) for j in js]
        o_heads = [(_dot(e_c[j].astype(BF16), vc_ref[j]) + _dot(e_n[j].astype(BF16), vn)) * inv_l[j]
                   for j in js]
        for j in js:
            att = jnp.where(head_of_lane == 0, o_heads[j][0:t], 0.0)
            for hh in range(1, H_ATT):
                att = jnp.where(head_of_lane == hh, o_heads[j][hh * t:(hh + 1) * t], att)
            mix_scr[pl.ds(r0[j], t), :] = jnp.concatenate([pool_out[j], att], axis=-1)

        @pl.when(i == pl.num_programs(0) - 1)
        def _():
            os_ref[...] = xs_ref[...] + _stream_rows(gates_ref[...], t) * _dot(mix_scr[...].astype(BF16), out_weights())


def _odd(xp, xs, mod, l, norm_g, w_in, w_out, pw16, ps, bias_t, p0, bias_c, bias_n, p0s, kc16, vc16, t, pos0):
    seq, rows = xp.shape[0], xs.shape[0]
    n_streams = rows // t
    cache = kc16.shape[2]
    tm = ROW_TILE
    assert tm == HIST and seq % tm == 0
    n_tiles = seq // tm
    n_load = D_IN_ODD // ODD_LOAD_COLS
    out_rows = D_MODEL // n_load
    per_step = SAMPLE_STREAMS_PER_STEP
    n_steps = n_streams // per_step
    assert n_load * ODD_LOAD_COLS == D_IN_ODD and n_streams % per_step == 0
    li = l // 2

    def tile(i):
        return jnp.clip(i - n_load, 0, n_tiles - 1)

    def step(i):
        return jnp.clip(i - n_load - n_tiles, 0, n_steps - 1)

    def streams(shape):
        nd = len(shape)
        return pl.BlockSpec((None, per_step) + tuple(shape), lambda i: (li, step(i)) + (0,) * nd)

    return pl.pallas_call(
        functools.partial(_odd_kernel, n_load=n_load, n_tiles=n_tiles, pos0=pos0, t=t),
        out_shape=(jax.ShapeDtypeStruct((seq, D_MODEL), F32), jax.ShapeDtypeStruct((rows, D_MODEL), F32),
                   jax.ShapeDtypeStruct((POOL_BASE, D_POOL), F32),
                   jax.ShapeDtypeStruct((HIST, D_ATT), F32), jax.ShapeDtypeStruct((HIST, D_ATT), F32),
                   jax.ShapeDtypeStruct((n_streams, t, D_POOL), F32),
                   jax.ShapeDtypeStruct((rows, D_ATT), F32), jax.ShapeDtypeStruct((rows, D_ATT), F32)),
        grid=(n_load + n_tiles + n_steps,),
        in_specs=[pl.BlockSpec((tm, D_MODEL), lambda i: (tile(i), 0)), _whole((rows, D_MODEL)), _layer((1, D_MODEL), l)]
        + [_mod_prompt(l, j, n_streams) for j in range(3)] + [_mod_streams(l, j, n_streams) for j in range(3)]
        + [pl.BlockSpec((None, D_MODEL, ODD_LOAD_COLS), lambda i: (li, 0, jnp.minimum(i, n_load - 1))),
           pl.BlockSpec((None, out_rows, D_MODEL), lambda i: (li, jnp.minimum(i, n_load - 1), 0)),
           _layer((len(POOL_WINDOWS), POOL_GROUP, POOL_GROUP), li), _layer((1, D_POOL), li),
           _whole((HEAD_PAIRS, BAND2, 2 * LANES)), _whole((POOL_BASE, D_POOL)),
           _whole((H_ATT * t, cache)), _whole((H_ATT * t, rows)),
           streams((POOL_BASE, D_POOL)), streams((cache, D_ATT)), streams((cache, D_ATT))],
        out_specs=(pl.BlockSpec((tm, D_MODEL), lambda i: (tile(i), 0)), _whole_out((rows, D_MODEL)),
                   _whole_out((POOL_BASE, D_POOL)), _whole_out((HIST, D_ATT)), _whole_out((HIST, D_ATT)),
                   pl.BlockSpec((per_step, t, D_POOL), lambda i: (step(i), 0, 0)),
                   _whole_out((rows, D_ATT)), _whole_out((rows, D_ATT))),
        scratch_shapes=[pltpu.VMEM((n_load, D_MODEL, ODD_LOAD_COLS), BF16), pltpu.VMEM((n_load, out_rows, D_MODEL), BF16),
                        pltpu.VMEM((len(POOL_WINDOWS), POOL_BASE + tm, POOL_GROUP), F32),
                        pltpu.VMEM((HIST + tm, D_ATT), BF16),
                        pltpu.VMEM((HEAD_PAIRS, LANES + VT_EXTRA, HIST + tm), BF16),
                        pltpu.VMEM((tm, D_ATT), F32), pltpu.VMEM((tm, D_ATT), F32),
                        pltpu.VMEM((rows, D_IN_ODD), F32), pltpu.VMEM((rows, D_MODEL), F32),
                        pltpu.VMEM((per_step, len(POOL_WINDOWS), POOL_BASE + t, POOL_GROUP), F32)],
        compiler_params=_params(),
        name="odd_mixer",
    )(xp, xs, norm_g, mod, mod, mod, mod, mod, mod, w_in, w_out, pw16, ps, bias_t, p0, bias_c, bias_n, p0s, kc16, vc16)


FFN_LOAD_STEPS = 16


def _ffn_kernel(xp_ref, xs_ref, g_ref, shp_ref, scp_ref, gatep_ref, shs_ref, scs_ref, gates_ref,
                w_up_ref, cw_ref, w_down_ref, f0_ref, f1_ref, f2_ref, gf_ref,
                op_ref, os_ref, fp_out_ref, fs_out_ref, w_up16, w_down16, up_scr, act_scr,
                *, final_norm, n_load, n_tiles, t):
    i = pl.program_id(0)
    tm = xp_ref.shape[0]
    n_chunks = D_FF // FFN_COLS
    slabs_per_chunk = FFN_COLS // LANES
    n_slabs = 2 * D_FF // LANES

    @pl.when(i < n_load)
    def _():
        up_rows, down_rows = w_up_ref.shape[0], w_down_ref.shape[0]
        w_up16[pl.ds(pl.multiple_of(i * up_rows, up_rows), up_rows), :] = w_up_ref[...].astype(BF16)
        w_down16[pl.ds(pl.multiple_of(i * down_rows, down_rows), down_rows), :] = w_down_ref[...].astype(BF16)

    def up_weights(c0):
        return w_up16[:, c0:c0 + FFN_COLS]

    def conv_slab(j, rows):
        cols = slice(j * LANES, (j + 1) * LANES)
        return (cw_ref[0:1, cols] * up_scr[j, SUBLANES - 2:SUBLANES - 2 + rows, :]
                + cw_ref[1:2, cols] * up_scr[j, SUBLANES - 1:SUBLANES - 1 + rows, :]
                + cw_ref[2:3, cols] * up_scr[j, SUBLANES:SUBLANES + rows, :])

    def run(h16, rows, put_up, after_conv):
        def project(c):
            for off in (0, D_FF):
                c0 = off + c * FFN_COLS
                put_up(c0, _dot(h16, up_weights(c0)))

        def activate(c):
            for j in range(slabs_per_chunk):
                ja = c * slabs_per_chunk + j
                a = conv_slab(ja, rows)
                b = conv_slab(D_FF // LANES + ja, rows)
                after_conv(ja)
                after_conv(D_FF // LANES + ja)
                act_scr[0:rows, ja * LANES:(ja + 1) * LANES] = (_silu(a) * b).astype(BF16)

        project(0)
        for c in range(n_chunks):
            if c + 1 < n_chunks:
                project(c + 1)
            activate(c)
        return _dot(act_scr[0:rows, :], w_down16[...])

    @pl.when(jnp.logical_and(i >= n_load, i < n_load + n_tiles))
    def _():
        @pl.when(i == n_load)
        def _():
            for j in range(n_slabs):
                up_scr[j, 0:SUBLANES, :] = f0_ref[:, j * LANES:(j + 1) * LANES]

        x = xp_ref[...]
        h16 = _norm_mod(x, g_ref[...], shp_ref[0:1, :], scp_ref[0:1, :]).astype(BF16)

        def put_up(c0, up):
            for j in range(slabs_per_chunk):
                up_scr[c0 // LANES + j, SUBLANES:SUBLANES + tm, :] = up[:, j * LANES:(j + 1) * LANES]
            fp_out_ref[:, c0:c0 + FFN_COLS] = up[tm - SUBLANES:, :]

        def keep_tail(j):
            up_scr[j, 0:SUBLANES, :] = up_scr[j, tm:tm + SUBLANES, :]

        out = x + gatep_ref[0:1, :] * run(h16, tm, put_up, keep_tail)
        if final_norm:
            out = _rmsnorm(out, gf_ref[...])
        op_ref[...] = out

    @pl.when(i == n_load + n_tiles)
    def _():
        x = xs_ref[...]
        n_streams = x.shape[0] // t
        seg = SUBLANES + t
        rows = n_streams * seg
        h16 = _norm_mod(x, g_ref[...], _stream_rows(shs_ref[...], t), _stream_rows(scs_ref[...], t)).astype(BF16)
        for j in range(n_slabs):
            cols = slice(j * LANES, (j + 1) * LANES)
            for b in range(n_streams):
                up_scr[j, b * seg + SUBLANES - 2:b * seg + SUBLANES - 1, :] = f1_ref[b:b + 1, cols]
                up_scr[j, b * seg + SUBLANES - 1:b * seg + SUBLANES, :] = f2_ref[b:b + 1, cols]

        def put_up(c0, up):
            for b in range(n_streams):
                for j in range(slabs_per_chunk):
                    up_scr[c0 // LANES + j, b * seg + SUBLANES:(b + 1) * seg, :] = \
                        up[b * t:(b + 1) * t, j * LANES:(j + 1) * LANES]
                fs_out_ref[b, :, c0:c0 + FFN_COLS] = up[(b + 1) * t - SUBLANES:(b + 1) * t, :]

        y = run(h16, rows, put_up, lambda j: None)
        y = jnp.concatenate([y[b * seg:b * seg + t, :] for b in range(n_streams)], axis=0)
        out = x + _stream_rows(gates_ref[...], t) * y
        if final_norm:
            out = _rmsnorm(out, gf_ref[...])
        os_ref[...] = out


def _ffn(xp, xs, mod, l, norm_g, w_up, cw, w_down, f0, f1, f2, gf, final_norm, t):
    seq, rows = xp.shape[0], xs.shape[0]
    n_streams = rows // t
    tm = ROW_TILE
    n_tiles = seq // tm
    n_load = FFN_LOAD_STEPS
    up_rows, down_rows = D_MODEL // n_load, D_FF // n_load
    assert up_rows * n_load == D_MODEL and down_rows * n_load == D_FF
    assert up_rows % (2 * SUBLANES) == 0 and down_rows % (2 * SUBLANES) == 0 and n_streams * (SUBLANES + t) <= tm

    def tile(i):
        return jnp.clip(i - n_load, 0, n_tiles - 1)

    return pl.pallas_call(
        functools.partial(_ffn_kernel, final_norm=final_norm, n_load=n_load, n_tiles=n_tiles, t=t),
        out_shape=(jax.ShapeDtypeStruct((seq, D_MODEL), F32), jax.ShapeDtypeStruct((rows, D_MODEL), F32),
                   jax.ShapeDtypeStruct((SUBLANES, 2 * D_FF), F32),
                   jax.ShapeDtypeStruct((n_streams, SUBLANES, 2 * D_FF), F32)),
        grid=(n_load + n_tiles + 1,),
        in_specs=[pl.BlockSpec((tm, D_MODEL), lambda i: (tile(i), 0)), _whole((rows, D_MODEL)), _layer((1, D_MODEL), l)]
        + [_mod_prompt(l, 3 + j, n_streams) for j in range(3)] + [_mod_streams(l, 3 + j, n_streams) for j in range(3)]
        + [pl.BlockSpec((None, up_rows, 2 * D_FF), lambda i: (l, jnp.minimum(i, n_load - 1), 0)),
           _layer((3, 2 * D_FF), l),
           pl.BlockSpec((None, down_rows, D_MODEL), lambda i: (l, jnp.minimum(i, n_load - 1), 0)),
           _whole((SUBLANES, 2 * D_FF)), _layer((n_streams, 2 * D_FF), l), _layer((n_streams, 2 * D_FF), l),
           _whole((1, D_MODEL))],
        out_specs=(pl.BlockSpec((tm, D_MODEL), lambda i: (tile(i), 0)), _whole_out((rows, D_MODEL)),
                   _whole_out((SUBLANES, 2 * D_FF)), _whole_out((n_streams, SUBLANES, 2 * D_FF))),
        scratch_shapes=[pltpu.VMEM((D_MODEL, 2 * D_FF), BF16), pltpu.VMEM((D_FF, D_MODEL), BF16),
                        pltpu.VMEM((2 * D_FF // LANES, SUBLANES + tm, LANES), F32), pltpu.VMEM((tm, D_FF), BF16)],
        compiler_params=_params(),
        name="ffn",
    )(xp, xs, norm_g, mod, mod, mod, mod, mod, mod, w_up, cw, w_down, f0, f1, f2, gf)


def _rotary_triplet(pos):
    half = DK_RET // 2
    inv = ROPE_BASE ** (-jnp.arange(half, dtype=F32) / half)
    ang = pos.astype(F32)[:, None] * inv[None, :]
    cos, sin = jnp.cos(ang), jnp.sin(ang)
    return jnp.stack([jnp.concatenate([cos] * 4, axis=-1), jnp.concatenate([sin] * 4, axis=-1),
                      jnp.concatenate([-sin, sin, -sin, sin], axis=-1)])


def _retention_tables(tb, n_streams=1):
    idx = np.arange(tb, dtype=np.float64)
    diff = idx[:, None] - idx[None, :]
    dec1 = np.where(diff[None] >= 0, np.exp(LOG_G[:, None, None] * np.maximum(diff, 0.0)[None]), 0.0)
    dec = np.zeros((H_RET, n_streams * tb, n_streams * tb))
    for b in range(n_streams):
        dec[:, b * tb:(b + 1) * tb, b * tb:(b + 1) * tb] = dec1
    xi = np.tile(np.repeat(np.exp(LOG_G[:, None] * (idx + 1)[None, :]).T, DK_RET, axis=1), (n_streams, 1))
    zeta_t = np.tile(np.repeat(np.exp(LOG_G[:, None] * (tb - 1 - idx)[None, :]), DK_RET, axis=0), (1, n_streams))
    gmat = np.zeros((HEAD_PAIRS, LANES, LANES))
    for h in range(H_RET):
        o = (h % 2) * DK_RET
        gmat[h // 2, o:o + DK_RET, o:o + DK_RET] = np.exp(LOG_G[h] * tb)
    return tuple(jnp.asarray(a, F32) for a in (dec, xi, zeta_t, gmat))


def _pair_state(s):
    lead = s.shape[:-3]
    s = s.reshape(lead + (HEAD_PAIRS, 2, DK_RET, DK_RET))
    z = jnp.zeros_like(s[..., 0, :, :])
    top = jnp.concatenate([s[..., 0, :, :], z], axis=-1)
    bot = jnp.concatenate([z, s[..., 1, :, :]], axis=-1)
    return jnp.concatenate([top, bot], axis=-2)


def _unpair_state(r):
    a = r[..., :DK_RET, :DK_RET]
    b = r[..., DK_RET:, DK_RET:]
    s = jnp.stack([a, b], axis=-3)
    return s.reshape(r.shape[:-3] + (H_RET, DK_RET, DK_RET))


def _band_bias(table):
    nq, nk = 2 * CHUNK, BAND2
    period = nq + nk
    j = np.arange(period)
    j = np.where(j < nk, j, j - period)
    idx = np.clip(HIST - j, -(CHUNK - 1), REL_CLIP) + (CHUNK - 1)
    one_period = table[:, idx].astype(F32)
    flat = jnp.tile(one_period, (1, nq + 1))[:, :nq * (period - 1)]
    return flat.reshape(-1, nq, period - 1)[:, :, :nk]


def _band_bias_t(raw):
    qq = np.arange(2 * CHUNK)[:, None]
    kk = np.arange(BAND2)[None, :]
    valid = np.where(qq < CHUNK, kk < BAND, kk >= CHUNK)
    b = jnp.where(valid, raw * LOG2E, NEG_INF).reshape(HEAD_PAIRS, 2, 2, CHUNK, BAND2)
    return jnp.transpose(b, (0, 4, 2, 1, 3)).reshape(HEAD_PAIRS, BAND2, 2 * LANES)


def _tail_rows(a, n):
    return a[..., a.shape[-2] - n:, :]


def kernel(x_prompt, x_sample, state_ret, state_sconv, state_pool, cache_k, cache_v, state_ffn, c_prompt, c_sample,
           norm_mix, norm_ffn, norm_final, w_ada, b_ada, w_in_even, w_out_even, ret_gn_gain, sconv_w, w_in_odd,
           w_out_odd, pool_w, pool_scale, rel_bias_table, ffn_w_up, ffn_conv, ffn_w_down):
    n_prompt, seq, _ = x_prompt.shape
    n_streams, t_s, _ = x_sample.shape
    assert n_prompt == 1 and n_streams % MOD_ROWS_PROMPT == 0
    rows_s = n_streams * t_s
    n_even, n_odd = (DEPTH + 1) // 2, DEPTH // 2

    c_all = jnp.concatenate([c_sample, c_prompt], axis=0)
    mod = _ada(jnp.pad(c_all, ((0, MOD_ROWS_PROMPT - 1), (0, 0))), w_ada, b_ada)

    bf = lambda w: w.astype(BF16)
    pool_w16 = bf(pool_w)
    norm_mix3, norm_ffn3 = norm_mix.reshape(DEPTH, 1, D_MODEL), norm_ffn.reshape(DEPTH, 1, D_MODEL)
    gain3, pool_scale3 = ret_gn_gain.reshape(n_even, 1, D_RET), pool_scale.reshape(n_odd, 1, D_POOL)
    norm_final2 = norm_final.reshape(1, D_MODEL)

    n_tiles = seq // ROW_TILE
    tabs_p = (_rotary_triplet(jnp.arange(n_tiles, dtype=jnp.int32) * ROW_TILE),
              _rotary_triplet(jnp.arange(ROW_TILE, dtype=jnp.int32))) + _retention_tables(RET_BLOCK)
    rot_s = _rotary_triplet(PAST_LEN + jnp.arange(t_s, dtype=jnp.int32))
    tabs_s = (jnp.tile(rot_s[0], (n_streams, 1)), jnp.tile(rot_s[2], (n_streams, 1))) \
        + _retention_tables(t_s, n_streams)

    cache_len = cache_k.shape[2]
    assert cache_len == HIST and t_s <= CHUNK
    bias_raw = [_band_bias(rel_bias_table[i]) for i in range(n_odd)]
    bias_p = [_band_bias_t(b) for b in bias_raw]
    bias_c = [b[:, :t_s, :cache_len].reshape(H_ATT * t_s, cache_len) for b in bias_raw]
    bias_n = [jnp.tile(b[:, :t_s, cache_len:cache_len + t_s], (1, 1, n_streams)).reshape(H_ATT * t_s, rows_s)
              for b in bias_raw]

    paired = _pair_state(state_ret)
    s_stack = jnp.transpose(paired, (0, 2, 1, 3, 4)).reshape(n_even, HEAD_PAIRS, n_streams * LANES, LANES)
    s_wide = jnp.transpose(paired, (0, 2, 3, 1, 4)).reshape(n_even, HEAD_PAIRS, LANES, n_streams * LANES)
    u1, u2 = state_sconv[:, :, 0, :], state_sconv[:, :, 1, :]
    f1, f2 = state_ffn[:, :, 0, :], state_ffn[:, :, 1, :]
    p0_s = jnp.pad(state_pool, ((0, 0), (0, 0), (POOL_BASE - POOL_BUF, 0), (0, 0)))
    kc16 = cache_k.reshape(n_odd, n_streams, cache_len, D_ATT).astype(BF16)
    vc16 = cache_v.reshape(n_odd, n_streams, cache_len, D_ATT).astype(BF16)

    xp = x_prompt.reshape(seq, D_MODEL)
    xs = x_sample.reshape(rows_s, D_MODEL)

    ret_p, ret_s, sconv_p, sconv_s, pool_p, pool_s = [], [], [], [], [], []
    k_p, k_s, v_p, v_s, ffn_p, ffn_s = [], [], [], [], [], []
    for l in range(DEPTH):
        i = l // 2
        if l % 2 == 0:
            xp, xs, r_new, u_new, s_new, u_all = _even(
                xp, xs, mod, l, norm_mix3, w_in_even, w_out_even, gain3, sconv_w, tabs_p, tabs_s,
                jnp.zeros((HEAD_PAIRS, LANES, LANES), F32), jnp.zeros((SUBLANES, D_SCONV), F32),
                s_stack, s_wide, u1, u2, t_s)
            ret_p.append(_unpair_state(r_new)[None])
            sconv_p.append(_tail_rows(u_new, 2)[None])
            s_new = jnp.transpose(s_new.reshape(HEAD_PAIRS, LANES, n_streams, LANES), (2, 0, 1, 3))
            ret_s.append(_unpair_state(s_new))
            sconv_s.append(_tail_rows(u_all.reshape(n_streams, t_s, D_SCONV), 2))
        else:
            xp, xs, p_new, k_new, v_new, ps_new, ks_new, vs_new = _odd(
                xp, xs, mod, l, norm_mix3, w_in_odd, w_out_odd, pool_w16, pool_scale3, bias_p[i],
                jnp.zeros((POOL_BASE, D_POOL), F32), bias_c[i], bias_n[i], p0_s, kc16, vc16, t_s, PAST_LEN)
            pool_p.append(_tail_rows(p_new, POOL_BUF)[None])
            k_p.append(k_new.reshape(1, HIST, H_ATT, DH_ATT))
            v_p.append(v_new.reshape(1, HIST, H_ATT, DH_ATT))
            p_new, k_new, v_new = ps_new, ks_new, vs_new
            pool_s.append(_tail_rows(p_new, POOL_BUF))
            k_s.append(k_new.reshape(n_streams, t_s, H_ATT, DH_ATT))
            v_s.append(v_new.reshape(n_streams, t_s, H_ATT, DH_ATT))
        last = l == DEPTH - 1
        xp, xs, f_new, fs_new = _ffn(xp, xs, mod, l, norm_ffn3, ffn_w_up, ffn_conv, ffn_w_down,
                                     jnp.zeros((SUBLANES, 2 * D_FF), F32), f1, f2, norm_final2, last, t_s)
        ffn_p.append(_tail_rows(f_new, 2)[None])
        ffn_s.append(_tail_rows(fs_new, 2))

    st = jnp.stack
    return (xp.reshape(1, seq, D_MODEL), xs.reshape(n_streams, t_s, D_MODEL),
            st(ret_p), st(ret_s), st(sconv_p), st(sconv_s), st(pool_p), st(pool_s),
            st(k_p), st(k_s), st(v_p), st(v_s), st(ffn_p), st(ffn_s))
```

```python
import functools

import numpy as np
import jax
import jax.numpy as jnp
from jax import lax
from jax.experimental import pallas as pl
from jax.experimental.pallas import tpu as pltpu

F32 = jnp.float32
BF16 = jnp.bfloat16

D_MODEL = 1024
DEPTH = 4
PAST_LEN = 4096
CHUNK = 64
H_RET = 8
DK_RET = 64
D_RET = H_RET * DK_RET
ROPE_BASE = 10000.0
D_SCONV = D_MODEL - D_RET
POOL_WINDOWS = (2, 4, 8, 16)
D_POOL = D_MODEL // 2
POOL_GROUP = D_POOL // len(POOL_WINDOWS)
POOL_BUF = max(POOL_WINDOWS) - 1
H_ATT = 8
DH_ATT = 64
D_ATT = H_ATT * DH_ATT
N_PREV_CHUNKS = 8
REL_CLIP = 256
D_FF = 2816
EPS = 1e-6
NEG_INF = -1e30
D_IN_EVEN = 4 * D_RET + 3 * D_SCONV
D_IN_ODD = D_POOL + 3 * D_ATT

LANES = 128
SUBLANES = 8
HEAD_PAIRS = H_RET // 2
ROW_TILE = 512
RET_BLOCK = 128
FFN_COLS = 256
BAND = (N_PREV_CHUNKS + 1) * CHUNK
BAND2 = BAND + CHUNK
LOG2E = 1.4426950408889634
VT_EXTRA = 16
HIST = N_PREV_CHUNKS * CHUNK
POOL_BASE = 2 * SUBLANES
VMEM_LIMIT = 56 * 1024 * 1024

LOG_G = np.log1p(-(2.0 ** (-5.0 - np.arange(H_RET, dtype=np.float64))))


def _params(n_axes=1):
    return pltpu.CompilerParams(dimension_semantics=("arbitrary",) * n_axes, vmem_limit_bytes=VMEM_LIMIT)


def _whole(shape):
    nd = len(shape)
    return pl.BlockSpec(shape, lambda i: (0,) * nd, pipeline_mode=pl.Buffered(1))


def _whole_out(shape):
    nd = len(shape)
    return pl.BlockSpec(shape, lambda i: (0,) * nd)


def _layer(shape, l):
    nd = len(shape)
    return pl.BlockSpec((None,) + tuple(shape), lambda i: (l,) + (0,) * nd, pipeline_mode=pl.Buffered(1))


MOD_ROWS_PROMPT = SUBLANES


def _mod_prompt(l, j, n_streams):
    return pl.BlockSpec((None, MOD_ROWS_PROMPT, D_MODEL), lambda i: (l, n_streams // MOD_ROWS_PROMPT, j),
                        pipeline_mode=pl.Buffered(1))


def _mod_streams(l, j, n_streams):
    return pl.BlockSpec((None, n_streams, D_MODEL), lambda i: (l, 0, j), pipeline_mode=pl.Buffered(1))


def _stream_rows(m, t):
    return jnp.concatenate([jnp.broadcast_to(m[b:b + 1, :], (t, m.shape[1])) for b in range(m.shape[0])], axis=0)


def _dot(a, b):
    return jnp.dot(a, b, preferred_element_type=F32)


def _dot_nt(a, b):
    return lax.dot_general(a, b, (((1,), (1,)), ((), ())), preferred_element_type=F32)


def _rmsnorm(x, g):
    return x * lax.rsqrt(jnp.mean(x * x, axis=-1, keepdims=True) + EPS) * g


def _norm_mod(x, g, shift, scale):
    return _rmsnorm(x, g) * (1.0 + scale) + shift


def _silu(x):
    return x * (1.0 / (1.0 + jnp.exp(-x)))


def _low_half(shape):
    return (lax.broadcasted_iota(jnp.int32, shape, len(shape) - 1) % LANES) < DK_RET


def _shift_rows(u, prev8, s):
    rolled = pltpu.roll(u, s, axis=0)
    prolled = pltpu.roll(prev8, s, axis=0)
    row = lax.broadcasted_iota(jnp.int32, prev8.shape, 0)
    first = jnp.where(row < s, prolled, rolled[0:SUBLANES])
    return jnp.concatenate([first, rolled[SUBLANES:]], axis=0)


def _shift_rows_streams(u, older, newer, s, t):
    row_in_stream = lax.broadcasted_iota(jnp.int32, u.shape, 0) % t
    rolled = pltpu.roll(u, s, axis=0)
    if s == 1:
        return jnp.where(row_in_stream == 0, _stream_rows(newer, t), rolled)
    return jnp.where(row_in_stream == 0, _stream_rows(older, t),
                     jnp.where(row_in_stream == 1, _stream_rows(newer, t), rolled))


def _ada_kernel(c_ref, w_ref, b_ref, o_ref):
    c = c_ref[...]
    o_ref[0] = _dot(_silu(c).astype(BF16), w_ref[0].astype(BF16)) + b_ref[0]


def _ada(c_all, w_ada, b_ada):
    rows = c_all.shape[0]
    tn = 3072
    return pl.pallas_call(
        _ada_kernel,
        out_shape=jax.ShapeDtypeStruct((DEPTH, rows, 6 * D_MODEL), F32),
        grid=(DEPTH, 6 * D_MODEL // tn),
        in_specs=[
            pl.BlockSpec((rows, D_MODEL), lambda l, j: (0, 0)),
            pl.BlockSpec((1, D_MODEL, tn), lambda l, j: (l, 0, j)),
            pl.BlockSpec((1, 1, tn), lambda l, j: (l, 0, j)),
        ],
        out_specs=pl.BlockSpec((1, rows, tn), lambda l, j: (l, 0, j)),
        compiler_params=_params(2),
        name="ada_mod",
    )(c_all, w_ada, b_ada.reshape(DEPTH, 1, 6 * D_MODEL))


def _rotary_pair(x, cos, sin_signed):
    lane = lax.broadcasted_iota(jnp.int32, x.shape, 1)
    first_half = (lane % DK_RET) < (DK_RET // 2)
    swapped = jnp.where(first_half, pltpu.roll(x, LANES - DK_RET // 2, axis=1),
                        pltpu.roll(x, DK_RET // 2, axis=1))
    return x * cos + swapped * sin_signed


def _retention_block(proj, cos, sin_signed, dec_ref, xi, zeta_t_ref, gain, cross_fn, update_fn):
    tb = proj.shape[0]
    low = _low_half((tb, LANES))
    inv_n = 1.0 / DK_RET
    pairs = range(HEAD_PAIRS)
    cols = [slice(p * LANES, (p + 1) * LANES) for p in pairs]
    q = [_rotary_pair(proj[:, cols[p]], cos, sin_signed) for p in pairs]
    k_t = [(_rotary_pair(proj[:, D_RET + p * LANES:D_RET + (p + 1) * LANES], cos, sin_signed)
            * (DK_RET ** -0.5)).T for p in pairs]
    v16 = [proj[:, 2 * D_RET + p * LANES:2 * D_RET + (p + 1) * LANES].astype(BF16) for p in pairs]
    k_t16 = [k_t[p].astype(BF16) for p in pairs]
    kz_t16 = [(k_t[p] * zeta_t_ref[cols[p], :]).astype(BF16) for p in pairs]
    q16 = [q[p].astype(BF16) for p in pairs]
    qe16 = [jnp.where(low, q[p], 0.0).astype(BF16) for p in pairs]
    qo16 = [jnp.where(low, 0.0, q[p]).astype(BF16) for p in pairs]
    s_e = [(_dot(qe16[p], k_t16[p]) * dec_ref[2 * p]).astype(BF16) for p in pairs]
    s_o = [(_dot(qo16[p], k_t16[p]) * dec_ref[2 * p + 1]).astype(BF16) for p in pairs]
    cross = [cross_fn(p, q16[p]) * xi[:, cols[p]] for p in pairs]
    o = [jnp.where(low, _dot(s_e[p], v16[p]), _dot(s_o[p], v16[p])) + cross[p] for p in pairs]
    for p in pairs:
        update_fn(p, kz_t16[p], v16[p])
    outs = []
    for p in pairs:
        s_lo = jnp.sum(jnp.where(low, o[p], 0.0), axis=-1, keepdims=True)
        s_hi = jnp.sum(jnp.where(low, 0.0, o[p]), axis=-1, keepdims=True)
        d = o[p] - jnp.where(low, s_lo, s_hi) * inv_n
        d2 = d * d
        v_lo = jnp.sum(jnp.where(low, d2, 0.0), axis=-1, keepdims=True)
        v_hi = jnp.sum(jnp.where(low, 0.0, d2), axis=-1, keepdims=True)
        on = d * lax.rsqrt(jnp.where(low, v_lo, v_hi) * inv_n + EPS)
        g = proj[:, 3 * D_RET + p * LANES:3 * D_RET + (p + 1) * LANES]
        outs.append(_silu(g) * (on * gain[:, cols[p]]))
    return jnp.concatenate(outs, axis=-1)


def _even_tail(x, proj, ret_out, conv_in_shift, cw_ref, gate, w_out16):
    gate_b = proj[:, 4 * D_RET:4 * D_RET + D_SCONV]
    u = proj[:, 4 * D_RET + D_SCONV:4 * D_RET + 2 * D_SCONV] * proj[:, 4 * D_RET + 2 * D_SCONV:]
    conv = cw_ref[0:1, :] * conv_in_shift(u, 2) + cw_ref[1:2, :] * conv_in_shift(u, 1) + cw_ref[2:3, :] * u
    mixed = jnp.concatenate([ret_out, gate_b * conv], axis=-1).astype(BF16)
    return x + gate * _dot(mixed, w_out16), u


EVEN_LOAD_COLS = 512
EVEN_OUT_LOAD_ROWS = 256


def _even_kernel(xp_ref, xs_ref, g_ref, shp_ref, scp_ref, gatep_ref, shs_ref, scs_ref, gates_ref,
                 w_in_ref, w_out_ref, gain_ref, cw_ref,
                 rot_tile_ref, rot_row_ref, dec_ref, xi_ref, zt_ref, gmat_ref, r0_ref, u0_ref,
                 cos_s_ref, sin_s_ref, dec_s_ref, xi_s_ref, zt_s_ref, gmat_s_ref, s_stack_ref, s_wide_ref, u1_ref, u2_ref,
                 op_ref, os_ref, r_out_ref, u_out_ref, s_out_ref, us_out_ref,
                 w_in16, w_out16, r_scr, u_scr, *, n_load, n_tiles, t):
    i = pl.program_id(0)
    n_out_load = w_out16.shape[0]

    @pl.when(i < n_load)
    def _():
        w_in16[i] = w_in_ref[...].astype(BF16)

    @pl.when(i < n_out_load)
    def _():
        w_out16[i] = w_out_ref[...].astype(BF16)

    def in_project(h16):
        return jnp.concatenate([_dot(h16, w_in16[c]) for c in range(n_load)], axis=-1)

    def out_weights():
        return w_out16[...].reshape(D_MODEL, D_MODEL)

    @pl.when(jnp.logical_and(i >= n_load, i < n_load + n_tiles))
    def _():
        tile = i - n_load
        cos_0, sin_0, ssin_0 = (rot_tile_ref[j, pl.ds(tile, 1), :] for j in range(3))
        cos = cos_0 * rot_row_ref[0] - sin_0 * rot_row_ref[1]
        sin_signed = ssin_0 * rot_row_ref[0] + cos_0 * rot_row_ref[2]

        n_slabs = D_SCONV // LANES

        @pl.when(tile == 0)
        def _():
            r_scr[...] = r0_ref[...]
            for j in range(n_slabs):
                u_scr[j, 0:SUBLANES, :] = u0_ref[:, j * LANES:(j + 1) * LANES]

        x = xp_ref[...]
        h = _norm_mod(x, g_ref[...], shp_ref[0:1, :], scp_ref[0:1, :]).astype(BF16)
        proj = in_project(h)
        tm = x.shape[0]
        tb = dec_ref.shape[1]
        r_i = lax.broadcasted_iota(jnp.int32, (LANES, LANES), 0) < DK_RET
        c_i = lax.broadcasted_iota(jnp.int32, (LANES, LANES), 1) < DK_RET
        blockdiag = r_i == c_i

        def cross_fn(p, q16):
            return _dot(q16, r_scr[p].astype(BF16))

        def update_fn(p, kz_t16, v16):
            r_scr[p] = r_scr[p] * gmat_ref[p] + jnp.where(blockdiag, _dot(kz_t16, v16), 0.0)

        rets = []
        for r in range(tm // tb):
            rows = slice(r * tb, (r + 1) * tb)
            rets.append(_retention_block(proj[rows, :], cos[rows, :], sin_signed[rows, :], dec_ref, xi_ref[...],
                                         zt_ref, gain_ref[...], cross_fn, update_fn))
        ret_out = jnp.concatenate(rets, axis=0)

        def delayed(u, s):
            if s == 2:
                for j in range(n_slabs):
                    u_scr[j, SUBLANES:SUBLANES + tm, :] = u[:, j * LANES:(j + 1) * LANES]
            return jnp.concatenate([u_scr[j, SUBLANES - s:SUBLANES - s + tm, :] for j in range(n_slabs)], axis=-1)

        out, u = _even_tail(x, proj, ret_out, delayed, cw_ref, gatep_ref[0:1, :], out_weights())
        for j in range(n_slabs):
            u_scr[j, 0:SUBLANES, :] = u_scr[j, tm:tm + SUBLANES, :]
        op_ref[...] = out
        r_out_ref[...] = r_scr[...]
        u_out_ref[...] = u[tm - SUBLANES:, :]

    @pl.when(i == n_load + n_tiles)
    def _():
        x = xs_ref[...]
        rows = x.shape[0]
        n_streams = rows // t
        wide = n_streams * LANES
        h = _norm_mod(x, g_ref[...], _stream_rows(shs_ref[...], t), _stream_rows(scs_ref[...], t)).astype(BF16)
        proj = in_project(h)
        own = (lax.broadcasted_iota(jnp.int32, (rows, wide), 0) // t
               == lax.broadcasted_iota(jnp.int32, (rows, wide), 1) // LANES)
        r_i = lax.broadcasted_iota(jnp.int32, (LANES, wide), 0) < DK_RET
        blockdiag = r_i == _low_half((LANES, wide))

        def expand(a16):
            tiled = jnp.concatenate([a16.astype(F32)] * n_streams, axis=-1)
            return jnp.where(own, tiled, 0.0).astype(BF16)

        def cross_fn(p, q16):
            return _dot(expand(q16), s_stack_ref[p].astype(BF16))

        def update_fn(p, kz_t16, v16):
            kv = _dot(kz_t16, expand(v16))
            decay = jnp.concatenate([gmat_s_ref[p]] * n_streams, axis=-1)
            s_out_ref[p] = s_wide_ref[p] * decay + jnp.where(blockdiag, kv, 0.0)

        ret_out = _retention_block(proj, cos_s_ref[...], sin_s_ref[...], dec_s_ref, xi_s_ref[...], zt_s_ref,
                                   gain_ref[...], cross_fn, update_fn)
        out, u = _even_tail(x, proj, ret_out, lambda u, s: _shift_rows_streams(u, u1_ref[...], u2_ref[...], s, t),
                            cw_ref, _stream_rows(gates_ref[...], t), out_weights())
        os_ref[...] = out
        us_out_ref[...] = u


def _even(xp, xs, mod, l, norm_g, w_in, w_out, gain, cw, tabs_p, tabs_s, r0, u0, s_stack, s_wide, u1, u2, t):
    seq, rows = xp.shape[0], xs.shape[0]
    n_streams = rows // t
    wide = n_streams * LANES
    tm, tb = ROW_TILE, RET_BLOCK
    n_tiles = seq // tm
    n_load = D_IN_EVEN // EVEN_LOAD_COLS
    n_out_load = D_MODEL // EVEN_OUT_LOAD_ROWS
    assert n_load * EVEN_LOAD_COLS == D_IN_EVEN and n_out_load <= n_load
    rot_tile, rot_row, dec, xi, zt, gmat = tabs_p
    cos_s, sin_s, dec_s, xi_s, zt_s, gmat_s = tabs_s
    li = l // 2
    state = (HEAD_PAIRS, LANES, LANES)

    def tile(i):
        return jnp.clip(i - n_load, 0, n_tiles - 1)

    return pl.pallas_call(
        functools.partial(_even_kernel, n_load=n_load, n_tiles=n_tiles, t=t),
        out_shape=(jax.ShapeDtypeStruct((seq, D_MODEL), F32), jax.ShapeDtypeStruct((rows, D_MODEL), F32),
                   jax.ShapeDtypeStruct(state, F32), jax.ShapeDtypeStruct((SUBLANES, D_SCONV), F32),
                   jax.ShapeDtypeStruct((HEAD_PAIRS, LANES, wide), F32), jax.ShapeDtypeStruct((rows, D_SCONV), F32)),
        grid=(n_load + n_tiles + 1,),
        in_specs=[pl.BlockSpec((tm, D_MODEL), lambda i: (tile(i), 0)), _whole((rows, D_MODEL)), _layer((1, D_MODEL), l)]
        + [_mod_prompt(l, j, n_streams) for j in range(3)] + [_mod_streams(l, j, n_streams) for j in range(3)]
        + [pl.BlockSpec((None, D_MODEL, EVEN_LOAD_COLS), lambda i: (li, 0, jnp.minimum(i, n_load - 1))),
           pl.BlockSpec((None, EVEN_OUT_LOAD_ROWS, D_MODEL), lambda i: (li, jnp.minimum(i, n_out_load - 1), 0)),
           _layer((1, D_RET), li), _layer((3, D_SCONV), li),
           _whole((3, n_tiles, LANES)), _whole((3, tm, LANES)),
           _whole((H_RET, tb, tb)), _whole((tb, D_RET)), _whole((D_RET, tb)),
           _whole(state), _whole(state), _whole((SUBLANES, D_SCONV)),
           _whole((rows, LANES)), _whole((rows, LANES)),
           _whole((H_RET, rows, rows)), _whole((rows, D_RET)), _whole((D_RET, rows)), _whole(state),
           _layer((HEAD_PAIRS, wide, LANES), li), _layer((HEAD_PAIRS, LANES, wide), li),
           _layer((n_streams, D_SCONV), li), _layer((n_streams, D_SCONV), li)],
        out_specs=(pl.BlockSpec((tm, D_MODEL), lambda i: (tile(i), 0)), _whole_out((rows, D_MODEL)),
                   _whole_out(state), _whole_out((SUBLANES, D_SCONV)),
                   _whole_out((HEAD_PAIRS, LANES, wide)), _whole_out((rows, D_SCONV))),
        scratch_shapes=[pltpu.VMEM((n_load, D_MODEL, EVEN_LOAD_COLS), BF16),
                        pltpu.VMEM((n_out_load, EVEN_OUT_LOAD_ROWS, D_MODEL), BF16),
                        pltpu.VMEM(state, F32), pltpu.VMEM((D_SCONV // LANES, SUBLANES + tm, LANES), F32)],
        compiler_params=_params(),
        name="even_mixer",
    )(xp, xs, norm_g, mod, mod, mod, mod, mod, mod, w_in, w_out, gain, cw,
      rot_tile, rot_row, dec, xi, zt, gmat, r0, u0, cos_s, sin_s, dec_s, xi_s, zt_s, gmat_s, s_stack, s_wide, u1, u2)


def _pool_put(hist_ref, row0, block):
    for gi in range(len(POOL_WINDOWS)):
        hist_ref[gi, row0:row0 + block.shape[0], :] = block[:, gi * POOL_GROUP:(gi + 1) * POOL_GROUP]


def _pool_get(hist_ref, row0, rows):
    return jnp.concatenate([hist_ref[gi, row0:row0 + rows, :] for gi in range(len(POOL_WINDOWS))], axis=-1)


def _pool(hist_ref, p, pos, pool_w_ref, scale):
    t = p.shape[0]
    outs = []
    for gi, w in enumerate(POOL_WINDOWS):
        cols = slice(gi * POOL_GROUP, (gi + 1) * POOL_GROUP)
        win = p[:, cols]
        for d in range(1, w):
            win = win + hist_ref[gi, POOL_BASE - d:POOL_BASE - d + t, :]
        inv_cnt = 1.0 / jnp.minimum(pos + 1, w).astype(F32)
        pooled = win * inv_cnt - p[:, cols]
        outs.append(_dot(pooled.astype(BF16), pool_w_ref[gi]) * scale[:, cols])
    return jnp.concatenate(outs, axis=-1)


def _attend_scores(kbs, q_as, q_bs, biases):
    low = _low_half((CHUNK, LANES))
    out = []
    for kb, q_a, q_b, bias in zip(kbs, q_as, q_bs, biases):
        qs = jnp.concatenate([jnp.where(low, q_a, 0.0), jnp.where(low, 0.0, q_a),
                              jnp.where(low, q_b, 0.0), jnp.where(low, 0.0, q_b)], axis=0)
        qbd = qs.T.astype(BF16)
        half = (kb.shape[0] // 2) // (2 * SUBLANES) * (2 * SUBLANES)
        out.append(jnp.concatenate([_dot(kb[:half], qbd), _dot(kb[half:], qbd)], axis=0) + bias)
    return out


def _attend_values(scores, vts):
    low = _low_half((CHUNK, LANES))
    n = len(scores)
    e16 = [jnp.exp2(s - jnp.max(s, axis=0, keepdims=True)).astype(BF16) for s in scores]
    o_t = [jnp.concatenate([_dot(vts[j][:DK_RET], e16[j]), _dot(vts[j][DK_RET:], e16[j])], axis=0) for j in range(n)]
    o_t = [o[:LANES] * (1.0 / o[LANES:LANES + 1]) for o in o_t]
    o_t = [o.T for o in o_t]
    return [(jnp.where(low, o[0:CHUNK], o[CHUNK:2 * CHUNK]), jnp.where(low, o[2 * CHUNK:3 * CHUNK], o[3 * CHUNK:]))
            for o in o_t]


ODD_LOAD_COLS = 512
SAMPLE_STREAMS_PER_STEP = 4


def _odd_kernel(xp_ref, xs_ref, g_ref, shp_ref, scp_ref, gatep_ref, shs_ref, scs_ref, gates_ref,
                w_in_ref, w_out_ref, pw_ref, ps_ref, bias_ref, p0_ref, bias_c_ref, bias_n_ref, p0s_ref, kc_ref, vc_ref,
                op_ref, os_ref, p_out_ref, k_out_ref, v_out_ref, ps_out_ref, ks_out_ref, vs_out_ref,
                w_in16, w_out16, pbuf, kbuf, vtbuf, q_scr, att_scr, proj_scr, mix_scr, pbuf_s,
                *, n_load, n_tiles, pos0, t):
    i = pl.program_id(0)
    tm = xp_ref.shape[0]
    rows = xs_ref.shape[0]
    per_step = kc_ref.shape[0]

    @pl.when(i < n_load)
    def _():
        w_in16[i] = w_in_ref[...].astype(BF16)
        w_out16[i] = w_out_ref[...].astype(BF16)

    def in_project(h16):
        return jnp.concatenate([_dot(h16, w_in16[c]) for c in range(n_load)], axis=-1)

    def out_weights():
        return w_out16[...].reshape(D_MODEL, D_MODEL)

    @pl.when(jnp.logical_and(i >= n_load, i < n_load + n_tiles))
    def _():
        tile = i - n_load

        @pl.when(tile == 0)
        def _():
            _pool_put(pbuf, 0, p0_ref[...])
            kbuf[0:HIST, :] = jnp.zeros((HIST, D_ATT), BF16)
            vtbuf[:, :, 0:HIST] = jnp.zeros((HEAD_PAIRS, LANES + VT_EXTRA, HIST), BF16)

        x = xp_ref[...]
        h = _norm_mod(x, g_ref[...], shp_ref[0:1, :], scp_ref[0:1, :]).astype(BF16)
        proj = in_project(h)
        p = proj[:, :D_POOL]
        q_scr[...] = proj[:, D_POOL:D_POOL + D_ATT] * (DH_ATT ** -0.5 * LOG2E)
        k = proj[:, D_POOL + D_ATT:D_POOL + 2 * D_ATT]
        v = proj[:, D_POOL + 2 * D_ATT:]
        _pool_put(pbuf, POOL_BASE, p)
        kbuf[HIST:HIST + tm, :] = k.astype(BF16)
        ones_row = jnp.where(lax.broadcasted_iota(jnp.int32, (VT_EXTRA, tm), 0) == 0, 1.0, 0.0).astype(BF16)
        for pr in range(HEAD_PAIRS):
            vtbuf[pr, 0:LANES, HIST:HIST + tm] = v[:, pr * LANES:(pr + 1) * LANES].T.astype(BF16)
            vtbuf[pr, LANES:LANES + VT_EXTRA, HIST:HIST + tm] = ones_row
        k_out_ref[...] = k[tm - HIST:, :]
        v_out_ref[...] = v[tm - HIST:, :]

        pos = tile * tm + lax.broadcasted_iota(jnp.int32, (tm, 1), 0)
        pool_out = _pool(pbuf, p, pos, pw_ref, ps_ref[...])

        def attend_tile(first_tile):
            pairs = range(HEAD_PAIRS)
            lanes = [slice(pr * LANES, (pr + 1) * LANES) for pr in pairs]
            n_blocks = tm // (2 * CHUNK)
            skip = [max(HIST - jb * 2 * CHUNK, 0) if first_tile else 0 for jb in range(n_blocks)]

            def scores(jb):
                r0 = jb * 2 * CHUNK
                return _attend_scores([kbuf[r0 + skip[jb]:r0 + BAND2, lanes[pr]] for pr in pairs],
                                      [q_scr[r0:r0 + CHUNK, lanes[pr]] for pr in pairs],
                                      [q_scr[r0 + CHUNK:r0 + 2 * CHUNK, lanes[pr]] for pr in pairs],
                                      [bias_ref[pr, skip[jb]:, :] for pr in pairs])

            s_next = scores(0)
            for jb in range(n_blocks):
                r0 = jb * 2 * CHUNK
                s_cur = s_next
                if jb + 1 < n_blocks:
                    s_next = scores(jb + 1)
                outs = _attend_values(s_cur, [vtbuf[pr, :, r0 + skip[jb]:r0 + BAND2] for pr in pairs])
                att_scr[r0:r0 + CHUNK, :] = jnp.concatenate([o[0] for o in outs], axis=-1)
                att_scr[r0 + CHUNK:r0 + 2 * CHUNK, :] = jnp.concatenate([o[1] for o in outs], axis=-1)

        pl.when(tile == 0)(functools.partial(attend_tile, True))
        pl.when(tile > 0)(functools.partial(attend_tile, False))

        kbuf[0:HIST, :] = kbuf[tm:tm + HIST, :]
        vtbuf[:, :, 0:HIST] = vtbuf[:, :, tm:tm + HIST]
        tail = _pool_get(pbuf, tm, POOL_BASE)
        _pool_put(pbuf, 0, tail)
        p_out_ref[...] = tail

        mixed = jnp.concatenate([pool_out, att_scr[...]], axis=-1).astype(BF16)
        op_ref[...] = x + gatep_ref[0:1, :] * _dot(mixed, out_weights())

    @pl.when(i >= n_load + n_tiles)
    def _():
        step = i - (n_load + n_tiles)

        @pl.when(step == 0)
        def _():
            h = _norm_mod(xs_ref[...], g_ref[...], _stream_rows(shs_ref[...], t),
                          _stream_rows(scs_ref[...], t)).astype(BF16)
            proj = in_project(h)
            proj_scr[...] = proj
            ks_out_ref[...] = proj[:, D_POOL + D_ATT:D_POOL + 2 * D_ATT]
            vs_out_ref[...] = proj[:, D_POOL + 2 * D_ATT:]

        kn = proj_scr[:, D_POOL + D_ATT:D_POOL + 2 * D_ATT].astype(BF16)
        vn = proj_scr[:, D_POOL + 2 * D_ATT:].astype(BF16)
        pos = pos0 + lax.broadcasted_iota(jnp.int32, (t, 1), 0)
        head_of_lane = lax.broadcasted_iota(jnp.int32, (t, D_ATT), 1) // DH_ATT
        stream_of_col = lax.broadcasted_iota(jnp.int32, (H_ATT * t, rows), 1) // t
        js = range(per_step)
        r0 = [pl.multiple_of((step * per_step + j) * t, t) for j in js]
        proj = [proj_scr[pl.ds(r0[j], t), :] for j in js]
        q_heads = []
        for j in js:
            q = proj[j][:, D_POOL:D_POOL + D_ATT] * (DH_ATT ** -0.5)
            q_heads.append(jnp.concatenate([jnp.where(head_of_lane == hh, q, 0.0) for hh in range(H_ATT)],
                                           axis=0).astype(BF16))
        s_c = [_dot_nt(q_heads[j], kc_ref[j]) + bias_c_ref[...] for j in js]
        s_n = [jnp.where(stream_of_col == step * per_step + j, _dot_nt(q_heads[j], kn) + bias_n_ref[...], NEG_INF)
               for j in js]
        pool_out = []
        for j in js:
            p = proj[j][:, :D_POOL]
            _pool_put(pbuf_s.at[j], 0, p0s_ref[j])
            _pool_put(pbuf_s.at[j], POOL_BASE, p)
            ps_out_ref[j] = p
            pool_out.append(_pool(pbuf_s.at[j], p, pos, pw_ref, ps_ref[...]))
        m = [jnp.maximum(jnp.max(s_c[j], axis=-1, keepdims=True), jnp.max(s_n[j], axis=-1, keepdims=True)) for j in js]
        e_c = [jnp.exp(s_c[j] - m[j]) for j in js]
        e_n = [jnp.exp(s_n[j] - m[j]) for j in js]
        inv_l = [1.0 / (jnp.sum(e_c[j], axis=-1, keepdims=True) + jnp.sum(e_n[j], axis=-1, keepdims=True)) for j in js]
        o_heads = [(_dot(e_c[j].astype(BF16), vc_ref[j]) + _dot(e_n[j].astype(BF16), vn)) * inv_l[j]
                   for j in js]
        for j in js:
            att = jnp.where(head_of_lane == 0, o_heads[j][0:t], 0.0)
            for hh in range(1, H_ATT):
                att = jnp.where(head_of_lane == hh, o_heads[j][hh * t:(hh + 1) * t], att)
            mix_scr[pl.ds(r0[j], t), :] = jnp.concatenate([pool_out[j], att], axis=-1)

        @pl.when(i == pl.num_programs(0) - 1)
        def _():
            os_ref[...] = xs_ref[...] + _stream_rows(gates_ref[...], t) * _dot(mix_scr[...].astype(BF16), out_weights())


def _odd(xp, xs, mod, l, norm_g, w_in, w_out, pw16, ps, bias_t, p0, bias_c, bias_n, p0s, kc16, vc16, t, pos0):
    seq, rows = xp.shape[0], xs.shape[0]
    n_streams = rows // t
    cache = kc16.shape[2]
    tm = ROW_TILE
    assert tm == HIST and seq % tm == 0
    n_tiles = seq // tm
    n_load = D_IN_ODD // ODD_LOAD_COLS
    out_rows = D_MODEL // n_load
    per_step = SAMPLE_STREAMS_PER_STEP
    n_steps = n_streams // per_step
    assert n_load * ODD_LOAD_COLS == D_IN_ODD and n_streams % per_step == 0
    li = l // 2

    def tile(i):
        return jnp.clip(i - n_load, 0, n_tiles - 1)

    def step(i):
        return jnp.clip(i - n_load - n_tiles, 0, n_steps - 1)

    def streams(shape):
        nd = len(shape)
        return pl.BlockSpec((None, per_step) + tuple(shape), lambda i: (li, step(i)) + (0,) * nd)

    return pl.pallas_call(
        functools.partial(_odd_kernel, n_load=n_load, n_tiles=n_tiles, pos0=pos0, t=t),
        out_shape=(jax.ShapeDtypeStruct((seq, D_MODEL), F32), jax.ShapeDtypeStruct((rows, D_MODEL), F32),
                   jax.ShapeDtypeStruct((POOL_BASE, D_POOL), F32),
                   jax.ShapeDtypeStruct((HIST, D_ATT), F32), jax.ShapeDtypeStruct((HIST, D_ATT), F32),
                   jax.ShapeDtypeStruct((n_streams, t, D_POOL), F32),
                   jax.ShapeDtypeStruct((rows, D_ATT), F32), jax.ShapeDtypeStruct((rows, D_ATT), F32)),
        grid=(n_load + n_tiles + n_steps,),
        in_specs=[pl.BlockSpec((tm, D_MODEL), lambda i: (tile(i), 0)), _whole((rows, D_MODEL)), _layer((1, D_MODEL), l)]
        + [_mod_prompt(l, j, n_streams) for j in range(3)] + [_mod_streams(l, j, n_streams) for j in range(3)]
        + [pl.BlockSpec((None, D_MODEL, ODD_LOAD_COLS), lambda i: (li, 0, jnp.minimum(i, n_load - 1))),
           pl.BlockSpec((None, out_rows, D_MODEL), lambda i: (li, jnp.minimum(i, n_load - 1), 0)),
           _layer((len(POOL_WINDOWS), POOL_GROUP, POOL_GROUP), li), _layer((1, D_POOL), li),
           _whole((HEAD_PAIRS, BAND2, 2 * LANES)), _whole((POOL_BASE, D_POOL)),
           _whole((H_ATT * t, cache)), _whole((H_ATT * t, rows)),
           streams((POOL_BASE, D_POOL)), streams((cache, D_ATT)), streams((cache, D_ATT))],
        out_specs=(pl.BlockSpec((tm, D_MODEL), lambda i: (tile(i), 0)), _whole_out((rows, D_MODEL)),
                   _whole_out((POOL_BASE, D_POOL)), _whole_out((HIST, D_ATT)), _whole_out((HIST, D_ATT)),
                   pl.BlockSpec((per_step, t, D_POOL), lambda i: (step(i), 0, 0)),
                   _whole_out((rows, D_ATT)), _whole_out((rows, D_ATT))),
        scratch_shapes=[pltpu.VMEM((n_load, D_MODEL, ODD_LOAD_COLS), BF16), pltpu.VMEM((n_load, out_rows, D_MODEL), BF16),
                        pltpu.VMEM((len(POOL_WINDOWS), POOL_BASE + tm, POOL_GROUP), F32),
                        pltpu.VMEM((HIST + tm, D_ATT), BF16),
                        pltpu.VMEM((HEAD_PAIRS, LANES + VT_EXTRA, HIST + tm), BF16),
                        pltpu.VMEM((tm, D_ATT), F32), pltpu.VMEM((tm, D_ATT), F32),
                        pltpu.VMEM((rows, D_IN_ODD), F32), pltpu.VMEM((rows, D_MODEL), F32),
                        pltpu.VMEM((per_step, len(POOL_WINDOWS), POOL_BASE + t, POOL_GROUP), F32)],
        compiler_params=_params(),
        name="odd_mixer",
    )(xp, xs, norm_g, mod, mod, mod, mod, mod, mod, w_in, w_out, pw16, ps, bias_t, p0, bias_c, bias_n, p0s, kc16, vc16)


FFN_LOAD_STEPS = 16


def _ffn_kernel(xp_ref, xs_ref, g_ref, shp_ref, scp_ref, gatep_ref, shs_ref, scs_ref, gates_ref,
                w_up_ref, cw_ref, w_down_ref, f0_ref, f1_ref, f2_ref, gf_ref,
                op_ref, os_ref, fp_out_ref, fs_out_ref, w_up16, w_down16, up_scr, act_scr,
                *, final_norm, n_load, n_tiles, t):
    i = pl.program_id(0)
    tm = xp_ref.shape[0]
    n_chunks = D_FF // FFN_COLS
    slabs_per_chunk = FFN_COLS // LANES
    n_slabs = 2 * D_FF // LANES

    @pl.when(i < n_load)
    def _():
        up_rows, down_rows = w_up_ref.shape[0], w_down_ref.shape[0]
        w_up16[pl.ds(pl.multiple_of(i * up_rows, up_rows), up_rows), :] = w_up_ref[...].astype(BF16)
        w_down16[pl.ds(pl.multiple_of(i * down_rows, down_rows), down_rows), :] = w_down_ref[...].astype(BF16)

    def up_weights(c0):
        return w_up16[:, c0:c0 + FFN_COLS]

    def conv_slab(j, rows):
        cols = slice(j * LANES, (j + 1) * LANES)
        return (cw_ref[0:1, cols] * up_scr[j, SUBLANES - 2:SUBLANES - 2 + rows, :]
                + cw_ref[1:2, cols] * up_scr[j, SUBLANES - 1:SUBLANES - 1 + rows, :]
                + cw_ref[2:3, cols] * up_scr[j, SUBLANES:SUBLANES + rows, :])

    def run(h16, rows, put_up, after_conv):
        def project(c):
            for off in (0, D_FF):
                c0 = off + c * FFN_COLS
                put_up(c0, _dot(h16, up_weights(c0)))

        def activate(c):
            for j in range(slabs_per_chunk):
                ja = c * slabs_per_chunk + j
                a = conv_slab(ja, rows)
                b = conv_slab(D_FF // LANES + ja, rows)
                after_conv(ja)
                after_conv(D_FF // LANES + ja)
                act_scr[0:rows, ja * LANES:(ja + 1) * LANES] = (_silu(a) * b).astype(BF16)

        project(0)
        for c in range(n_chunks):
            if c + 1 < n_chunks:
                project(c + 1)
            activate(c)
        return _dot(act_scr[0:rows, :], w_down16[...])

    @pl.when(jnp.logical_and(i >= n_load, i < n_load + n_tiles))
    def _():
        @pl.when(i == n_load)
        def _():
            for j in range(n_slabs):
                up_scr[j, 0:SUBLANES, :] = f0_ref[:, j * LANES:(j + 1) * LANES]

        x = xp_ref[...]
        h16 = _norm_mod(x, g_ref[...], shp_ref[0:1, :], scp_ref[0:1, :]).astype(BF16)

        def put_up(c0, up):
            for j in range(slabs_per_chunk):
                up_scr[c0 // LANES + j, SUBLANES:SUBLANES + tm, :] = up[:, j * LANES:(j + 1) * LANES]
            fp_out_ref[:, c0:c0 + FFN_COLS] = up[tm - SUBLANES:, :]

        def keep_tail(j):
            up_scr[j, 0:SUBLANES, :] = up_scr[j, tm:tm + SUBLANES, :]

        out = x + gatep_ref[0:1, :] * run(h16, tm, put_up, keep_tail)
        if final_norm:
            out = _rmsnorm(out, gf_ref[...])
        op_ref[...] = out

    @pl.when(i == n_load + n_tiles)
    def _():
        x = xs_ref[...]
        n_streams = x.shape[0] // t
        seg = SUBLANES + t
        rows = n_streams * seg
        h16 = _norm_mod(x, g_ref[...], _stream_rows(shs_ref[...], t), _stream_rows(scs_ref[...], t)).astype(BF16)
        for j in range(n_slabs):
            cols = slice(j * LANES, (j + 1) * LANES)
            for b in range(n_streams):
                up_scr[j, b * seg + SUBLANES - 2:b * seg + SUBLANES - 1, :] = f1_ref[b:b + 1, cols]
                up_scr[j, b * seg + SUBLANES - 1:b * seg + SUBLANES, :] = f2_ref[b:b + 1, cols]

        def put_up(c0, up):
            for b in range(n_streams):
                for j in range(slabs_per_chunk):
                    up_scr[c0 // LANES + j, b * seg + SUBLANES:(b + 1) * seg, :] = \
                        up[b * t:(b + 1) * t, j * LANES:(j + 1) * LANES]
                fs_out_ref[b, :, c0:c0 + FFN_COLS] = up[(b + 1) * t - SUBLANES:(b + 1) * t, :]

        y = run(h16, rows, put_up, lambda j: None)
        y = jnp.concatenate([y[b * seg:b * seg + t, :] for b in range(n_streams)], axis=0)
        out = x + _stream_rows(gates_ref[...], t) * y
        if final_norm:
            out = _rmsnorm(out, gf_ref[...])
        os_ref[...] = out


def _ffn(xp, xs, mod, l, norm_g, w_up, cw, w_down, f0, f1, f2, gf, final_norm, t):
    seq, rows = xp.shape[0], xs.shape[0]
    n_streams = rows // t
    tm = ROW_TILE
    n_tiles = seq // tm
    n_load = FFN_LOAD_STEPS
    up_rows, down_rows = D_MODEL // n_load, D_FF // n_load
    assert up_rows * n_load == D_MODEL and down_rows * n_load == D_FF
    assert up_rows % (2 * SUBLANES) == 0 and down_rows % (2 * SUBLANES) == 0 and n_streams * (SUBLANES + t) <= tm

    def tile(i):
        return jnp.clip(i - n_load, 0, n_tiles - 1)

    return pl.pallas_call(
        functools.partial(_ffn_kernel, final_norm=final_norm, n_load=n_load, n_tiles=n_tiles, t=t),
        out_shape=(jax.ShapeDtypeStruct((seq, D_MODEL), F32), jax.ShapeDtypeStruct((rows, D_MODEL), F32),
                   jax.ShapeDtypeStruct((SUBLANES, 2 * D_FF), F32),
                   jax.ShapeDtypeStruct((n_streams, SUBLANES, 2 * D_FF), F32)),
        grid=(n_load + n_tiles + 1,),
        in_specs=[pl.BlockSpec((tm, D_MODEL), lambda i: (tile(i), 0)), _whole((rows, D_MODEL)), _layer((1, D_MODEL), l)]
        + [_mod_prompt(l, 3 + j, n_streams) for j in range(3)] + [_mod_streams(l, 3 + j, n_streams) for j in range(3)]
        + [pl.BlockSpec((None, up_rows, 2 * D_FF), lambda i: (l, jnp.minimum(i, n_load - 1), 0)),
           _layer((3, 2 * D_FF), l),
           pl.BlockSpec((None, down_rows, D_MODEL), lambda i: (l, jnp.minimum(i, n_load - 1), 0)),
           _whole((SUBLANES, 2 * D_FF)), _layer((n_streams, 2 * D_FF), l), _layer((n_streams, 2 * D_FF), l),
           _whole((1, D_MODEL))],
        out_specs=(pl.BlockSpec((tm, D_MODEL), lambda i: (tile(i), 0)), _whole_out((rows, D_MODEL)),
                   _whole_out((SUBLANES, 2 * D_FF)), _whole_out((n_streams, SUBLANES, 2 * D_FF))),
        scratch_shapes=[pltpu.VMEM((D_MODEL, 2 * D_FF), BF16), pltpu.VMEM((D_FF, D_MODEL), BF16),
                        pltpu.VMEM((2 * D_FF // LANES, SUBLANES + tm, LANES), F32), pltpu.VMEM((tm, D_FF), BF16)],
        compiler_params=_params(),
        name="ffn",
    )(xp, xs, norm_g, mod, mod, mod, mod, mod, mod, w_up, cw, w_down, f0, f1, f2, gf)


def _rotary_triplet(pos):
    half = DK_RET // 2
    inv = ROPE_BASE ** (-jnp.arange(half, dtype=F32) / half)
    ang = pos.astype(F32)[:, None] * inv[None, :]
    cos, sin = jnp.cos(ang), jnp.sin(ang)
    return jnp.stack([jnp.concatenate([cos] * 4, axis=-1), jnp.concatenate([sin] * 4, axis=-1),
                      jnp.concatenate([-sin, sin, -sin, sin], axis=-1)])


def _retention_tables(tb, n_streams=1):
    idx = np.arange(tb, dtype=np.float64)
    diff = idx[:, None] - idx[None, :]
    dec1 = np.where(diff[None] >= 0, np.exp(LOG_G[:, None, None] * np.maximum(diff, 0.0)[None]), 0.0)
    dec = np.zeros((H_RET, n_streams * tb, n_streams * tb))
    for b in range(n_streams):
        dec[:, b * tb:(b + 1) * tb, b * tb:(b + 1) * tb] = dec1
    xi = np.tile(np.repeat(np.exp(LOG_G[:, None] * (idx + 1)[None, :]).T, DK_RET, axis=1), (n_streams, 1))
    zeta_t = np.tile(np.repeat(np.exp(LOG_G[:, None] * (tb - 1 - idx)[None, :]), DK_RET, axis=0), (1, n_streams))
    gmat = np.zeros((HEAD_PAIRS, LANES, LANES))
    for h in range(H_RET):
        o = (h % 2) * DK_RET
        gmat[h // 2, o:o + DK_RET, o:o + DK_RET] = np.exp(LOG_G[h] * tb)
    return tuple(jnp.asarray(a, F32) for a in (dec, xi, zeta_t, gmat))


def _pair_state(s):
    lead = s.shape[:-3]
    s = s.reshape(lead + (HEAD_PAIRS, 2, DK_RET, DK_RET))
    z = jnp.zeros_like(s[..., 0, :, :])
    top = jnp.concatenate([s[..., 0, :, :], z], axis=-1)
    bot = jnp.concatenate([z, s[..., 1, :, :]], axis=-1)
    return jnp.concatenate([top, bot], axis=-2)


def _unpair_state(r):
    a = r[..., :DK_RET, :DK_RET]
    b = r[..., DK_RET:, DK_RET:]
    s = jnp.stack([a, b], axis=-3)
    return s.reshape(r.shape[:-3] + (H_RET, DK_RET, DK_RET))


def _band_bias(table):
    nq, nk = 2 * CHUNK, BAND2
    period = nq + nk
    j = np.arange(period)
    j = np.where(j < nk, j, j - period)
    idx = np.clip(HIST - j, -(CHUNK - 1), REL_CLIP) + (CHUNK - 1)
    one_period = table[:, idx].astype(F32)
    flat = jnp.tile(one_period, (1, nq + 1))[:, :nq * (period - 1)]
    return flat.reshape(-1, nq, period - 1)[:, :, :nk]


def _band_bias_t(raw):
    qq = np.arange(2 * CHUNK)[:, None]
    kk = np.arange(BAND2)[None, :]
    valid = np.where(qq < CHUNK, kk < BAND, kk >= CHUNK)
    b = jnp.where(valid, raw * LOG2E, NEG_INF).reshape(HEAD_PAIRS, 2, 2, CHUNK, BAND2)
    return jnp.transpose(b, (0, 4, 2, 1, 3)).reshape(HEAD_PAIRS, BAND2, 2 * LANES)


def _tail_rows(a, n):
    return a[..., a.shape[-2] - n:, :]


def kernel(x_prompt, x_sample, state_ret, state_sconv, state_pool, cache_k, cache_v, state_ffn, c_prompt, c_sample,
           norm_mix, norm_ffn, norm_final, w_ada, b_ada, w_in_even, w_out_even, ret_gn_gain, sconv_w, w_in_odd,
           w_out_odd, pool_w, pool_scale, rel_bias_table, ffn_w_up, ffn_conv, ffn_w_down):
    n_prompt, seq, _ = x_prompt.shape
    n_streams, t_s, _ = x_sample.shape
    assert n_prompt == 1 and n_streams % MOD_ROWS_PROMPT == 0
    rows_s = n_streams * t_s
    n_even, n_odd = (DEPTH + 1) // 2, DEPTH // 2

    c_all = jnp.concatenate([c_sample, c_prompt], axis=0)
    mod = _ada(jnp.pad(c_all, ((0, MOD_ROWS_PROMPT - 1), (0, 0))), w_ada, b_ada)

    bf = lambda w: w.astype(BF16)
    pool_w16 = bf(pool_w)
    norm_mix3, norm_ffn3 = norm_mix.reshape(DEPTH, 1, D_MODEL), norm_ffn.reshape(DEPTH, 1, D_MODEL)
    gain3, pool_scale3 = ret_gn_gain.reshape(n_even, 1, D_RET), pool_scale.reshape(n_odd, 1, D_POOL)
    norm_final2 = norm_final.reshape(1, D_MODEL)

    n_tiles = seq // ROW_TILE
    tabs_p = (_rotary_triplet(jnp.arange(n_tiles, dtype=jnp.int32) * ROW_TILE),
              _rotary_triplet(jnp.arange(ROW_TILE, dtype=jnp.int32))) + _retention_tables(RET_BLOCK)
    rot_s = _rotary_triplet(PAST_LEN + jnp.arange(t_s, dtype=jnp.int32))
    tabs_s = (jnp.tile(rot_s[0], (n_streams, 1)), jnp.tile(rot_s[2], (n_streams, 1))) \
        + _retention_tables(t_s, n_streams)

    cache_len = cache_k.shape[2]
    assert cache_len == HIST and t_s <= CHUNK
    bias_raw = [_band_bias(rel_bias_table[i]) for i in range(n_odd)]
    bias_p = [_band_bias_t(b) for b in bias_raw]
    bias_c = [b[:, :t_s, :cache_len].reshape(H_ATT * t_s, cache_len) for b in bias_raw]
    bias_n = [jnp.tile(b[:, :t_s, cache_len:cache_len + t_s], (1, 1, n_streams)).reshape(H_ATT * t_s, rows_s)
              for b in bias_raw]

    paired = _pair_state(state_ret)
    s_stack = jnp.transpose(paired, (0, 2, 1, 3, 4)).reshape(n_even, HEAD_PAIRS, n_streams * LANES, LANES)
    s_wide = jnp.transpose(paired, (0, 2, 3, 1, 4)).reshape(n_even, HEAD_PAIRS, LANES, n_streams * LANES)
    u1, u2 = state_sconv[:, :, 0, :], state_sconv[:, :, 1, :]
    f1, f2 = state_ffn[:, :, 0, :], state_ffn[:, :, 1, :]
    p0_s = jnp.pad(state_pool, ((0, 0), (0, 0), (POOL_BASE - POOL_BUF, 0), (0, 0)))
    kc16 = cache_k.reshape(n_odd, n_streams, cache_len, D_ATT).astype(BF16)
    vc16 = cache_v.reshape(n_odd, n_streams, cache_len, D_ATT).astype(BF16)

    xp = x_prompt.reshape(seq, D_MODEL)
    xs = x_sample.reshape(rows_s, D_MODEL)

    ret_p, ret_s, sconv_p, sconv_s, pool_p, pool_s = [], [], [], [], [], []
    k_p, k_s, v_p, v_s, ffn_p, ffn_s = [], [], [], [], [], []
    for l in range(DEPTH):
        i = l // 2
        if l % 2 == 0:
            xp, xs, r_new, u_new, s_new, u_all = _even(
                xp, xs, mod, l, norm_mix3, w_in_even, w_out_even, gain3, sconv_w, tabs_p, tabs_s,
                jnp.zeros((HEAD_PAIRS, LANES, LANES), F32), jnp.zeros((SUBLANES, D_SCONV), F32),
                s_stack, s_wide, u1, u2, t_s)
            ret_p.append(_unpair_state(r_new)[None])
            sconv_p.append(_tail_rows(u_new, 2)[None])
            s_new = jnp.transpose(s_new.reshape(HEAD_PAIRS, LANES, n_streams, LANES), (2, 0, 1, 3))
            ret_s.append(_unpair_state(s_new))
            sconv_s.append(_tail_rows(u_all.reshape(n_streams, t_s, D_SCONV), 2))
        else:
            xp, xs, p_new, k_new, v_new, ps_new, ks_new, vs_new = _odd(
                xp, xs, mod, l, norm_mix3, w_in_odd, w_out_odd, pool_w16, pool_scale3, bias_p[i],
                jnp.zeros((POOL_BASE, D_POOL), F32), bias_c[i], bias_n[i], p0_s, kc16, vc16, t_s, PAST_LEN)
            pool_p.append(_tail_rows(p_new, POOL_BUF)[None])
            k_p.append(k_new.reshape(1, HIST, H_ATT, DH_ATT))
            v_p.append(v_new.reshape(1, HIST, H_ATT, DH_ATT))
            p_new, k_new, v_new = ps_new, ks_new, vs_new
            pool_s.append(_tail_rows(p_new, POOL_BUF))
            k_s.append(k_new.reshape(n_streams, t_s, H_ATT, DH_ATT))
            v_s.append(v_new.reshape(n_streams, t_s, H_ATT, DH_ATT))
        last = l == DEPTH - 1
        xp, xs, f_new, fs_new = _ffn(xp, xs, mod, l, norm_ffn3, ffn_w_up, ffn_conv, ffn_w_down,
                                     jnp.zeros((SUBLANES, 2 * D_FF), F32), f1, f2, norm_final2, last, t_s)
        ffn_p.append(_tail_rows(f_new, 2)[None])
        ffn_s.append(_tail_rows(fs_new, 2))

    st = jnp.stack
    return (xp.reshape(1, seq, D_MODEL), xs.reshape(n_streams, t_s, D_MODEL),
            st(ret_p), st(ret_s), st(sconv_p), st(sconv_s), st(pool_p), st(pool_s),
            st(k_p), st(k_s), st(v_p), st(v_s), st(ffn_p), st(ffn_s))
```

```python
import functools

import numpy as np
import jax
import jax.numpy as jnp
from jax import lax
from jax.experimental import pallas as pl
from jax.experimental.pallas import tpu as pltpu

F32 = jnp.float32
BF16 = jnp.bfloat16

D_MODEL = 1024
DEPTH = 4
PAST_LEN = 4096
CHUNK = 64
H_RET = 8
DK_RET = 64
D_RET = H_RET * DK_RET
ROPE_BASE = 10000.0
D_SCONV = D_MODEL - D_RET
POOL_WINDOWS = (2, 4, 8, 16)
D_POOL = D_MODEL // 2
POOL_GROUP = D_POOL // len(POOL_WINDOWS)
POOL_BUF = max(POOL_WINDOWS) - 1
H_ATT = 8
DH_ATT = 64
D_ATT = H_ATT * DH_ATT
N_PREV_CHUNKS = 8
REL_CLIP = 256
D_FF = 2816
EPS = 1e-6
NEG_INF = -1e30
D_IN_EVEN = 4 * D_RET + 3 * D_SCONV
D_IN_ODD = D_POOL + 3 * D_ATT

LANES = 128
SUBLANES = 8
HEAD_PAIRS = H_RET // 2
ROW_TILE = 512
RET_BLOCK = 128
FFN_COLS = 256
BAND = (N_PREV_CHUNKS + 1) * CHUNK
BAND2 = BAND + CHUNK
LOG2E = 1.4426950408889634
VT_EXTRA = 16
HIST = N_PREV_CHUNKS * CHUNK
POOL_BASE = 2 * SUBLANES
VMEM_LIMIT = 56 * 1024 * 1024

LOG_G = np.log1p(-(2.0 ** (-5.0 - np.arange(H_RET, dtype=np.float64))))


def _params(n_axes=1):
    return pltpu.CompilerParams(dimension_semantics=("arbitrary",) * n_axes, vmem_limit_bytes=VMEM_LIMIT)


def _whole(shape):
    nd = len(shape)
    return pl.BlockSpec(shape, lambda i: (0,) * nd, pipeline_mode=pl.Buffered(1))


def _whole_out(shape):
    nd = len(shape)
    return pl.BlockSpec(shape, lambda i: (0,) * nd)


def _layer(shape, l):
    nd = len(shape)
    return pl.BlockSpec((None,) + tuple(shape), lambda i: (l,) + (0,) * nd, pipeline_mode=pl.Buffered(1))


MOD_ROWS_PROMPT = SUBLANES


def _mod_prompt(l, j, n_streams):
    return pl.BlockSpec((None, MOD_ROWS_PROMPT, D_MODEL), lambda i: (l, n_streams // MOD_ROWS_PROMPT, j),
                        pipeline_mode=pl.Buffered(1))


def _mod_streams(l, j, n_streams):
    return pl.BlockSpec((None, n_streams, D_MODEL), lambda i: (l, 0, j), pipeline_mode=pl.Buffered(1))


def _stream_rows(m, t):
    return jnp.concatenate([jnp.broadcast_to(m[b:b + 1, :], (t, m.shape[1])) for b in range(m.shape[0])], axis=0)


def _dot(a, b):
    return jnp.dot(a, b, preferred_element_type=F32)


def _dot_nt(a, b):
    return lax.dot_general(a, b, (((1,), (1,)), ((), ())), preferred_element_type=F32)


def _rmsnorm(x, g):
    return x * lax.rsqrt(jnp.mean(x * x, axis=-1, keepdims=True) + EPS) * g


def _norm_mod(x, g, shift, scale):
    return _rmsnorm(x, g) * (1.0 + scale) + shift


def _silu(x):
    return x * (1.0 / (1.0 + jnp.exp(-x)))


def _low_half(shape):
    return (lax.broadcasted_iota(jnp.int32, shape, len(shape) - 1) % LANES) < DK_RET


def _shift_rows(u, prev8, s):
    rolled = pltpu.roll(u, s, axis=0)
    prolled = pltpu.roll(prev8, s, axis=0)
    row = lax.broadcasted_iota(jnp.int32, prev8.shape, 0)
    first = jnp.where(row < s, prolled, rolled[0:SUBLANES])
    return jnp.concatenate([first, rolled[SUBLANES:]], axis=0)


def _shift_rows_streams(u, older, newer, s, t):
    row_in_stream = lax.broadcasted_iota(jnp.int32, u.shape, 0) % t
    rolled = pltpu.roll(u, s, axis=0)
    if s == 1:
        return jnp.where(row_in_stream == 0, _stream_rows(newer, t), rolled)
    return jnp.where(row_in_stream == 0, _stream_rows(older, t),
                     jnp.where(row_in_stream == 1, _stream_rows(newer, t), rolled))


def _ada_kernel(c_ref, w_ref, b_ref, o_ref):
    c = c_ref[...]
    o_ref[0] = _dot(_silu(c).astype(BF16), w_ref[0].astype(BF16)) + b_ref[0]


def _ada(c_all, w_ada, b_ada):
    rows = c_all.shape[0]
    tn = 1536
    return pl.pallas_call(
        _ada_kernel,
        out_shape=jax.ShapeDtypeStruct((DEPTH, rows, 6 * D_MODEL), F32),
        grid=(DEPTH, 6 * D_MODEL // tn),
        in_specs=[
            pl.BlockSpec((rows, D_MODEL), lambda l, j: (0, 0)),
            pl.BlockSpec((1, D_MODEL, tn), lambda l, j: (l, 0, j)),
            pl.BlockSpec((1, 1, tn), lambda l, j: (l, 0, j)),
        ],
        out_specs=pl.BlockSpec((1, rows, tn), lambda l, j: (l, 0, j)),
        compiler_params=_params(2),
        name="ada_mod",
    )(c_all, w_ada, b_ada.reshape(DEPTH, 1, 6 * D_MODEL))


def _rotary_pair(x, cos, sin_signed):
    lane = lax.broadcasted_iota(jnp.int32, x.shape, 1)
    first_half = (lane % DK_RET) < (DK_RET // 2)
    swapped = jnp.where(first_half, pltpu.roll(x, LANES - DK_RET // 2, axis=1),
                        pltpu.roll(x, DK_RET // 2, axis=1))
    return x * cos + swapped * sin_signed


def _retention_block(proj, cos, sin_signed, dec_ref, xi, zeta_t_ref, gain, cross_fn, update_fn):
    tb = proj.shape[0]
    low = _low_half((tb, LANES))
    inv_n = 1.0 / DK_RET
    pairs = range(HEAD_PAIRS)
    cols = [slice(p * LANES, (p + 1) * LANES) for p in pairs]
    q = [_rotary_pair(proj[:, cols[p]], cos, sin_signed) for p in pairs]
    k_t = [(_rotary_pair(proj[:, D_RET + p * LANES:D_RET + (p + 1) * LANES], cos, sin_signed)
            * (DK_RET ** -0.5)).T for p in pairs]
    v16 = [proj[:, 2 * D_RET + p * LANES:2 * D_RET + (p + 1) * LANES].astype(BF16) for p in pairs]
    k_t16 = [k_t[p].astype(BF16) for p in pairs]
    kz_t16 = [(k_t[p] * zeta_t_ref[cols[p], :]).astype(BF16) for p in pairs]
    q16 = [q[p].astype(BF16) for p in pairs]
    qe16 = [jnp.where(low, q[p], 0.0).astype(BF16) for p in pairs]
    qo16 = [jnp.where(low, 0.0, q[p]).astype(BF16) for p in pairs]
    s_e = [(_dot(qe16[p], k_t16[p]) * dec_ref[2 * p]).astype(BF16) for p in pairs]
    s_o = [(_dot(qo16[p], k_t16[p]) * dec_ref[2 * p + 1]).astype(BF16) for p in pairs]
    cross = [cross_fn(p, q16[p]) * xi[:, cols[p]] for p in pairs]
    o = [jnp.where(low, _dot(s_e[p], v16[p]), _dot(s_o[p], v16[p])) + cross[p] for p in pairs]
    for p in pairs:
        update_fn(p, kz_t16[p], v16[p])
    outs = []
    for p in pairs:
        s_lo = jnp.sum(jnp.where(low, o[p], 0.0), axis=-1, keepdims=True)
        s_hi = jnp.sum(jnp.where(low, 0.0, o[p]), axis=-1, keepdims=True)
        d = o[p] - jnp.where(low, s_lo, s_hi) * inv_n
        d2 = d * d
        v_lo = jnp.sum(jnp.where(low, d2, 0.0), axis=-1, keepdims=True)
        v_hi = jnp.sum(jnp.where(low, 0.0, d2), axis=-1, keepdims=True)
        on = d * lax.rsqrt(jnp.where(low, v_lo, v_hi) * inv_n + EPS)
        g = proj[:, 3 * D_RET + p * LANES:3 * D_RET + (p + 1) * LANES]
        outs.append(_silu(g) * (on * gain[:, cols[p]]))
    return jnp.concatenate(outs, axis=-1)


def _even_tail(x, proj, ret_out, conv_in_shift, cw_ref, gate, w_out16):
    gate_b = proj[:, 4 * D_RET:4 * D_RET + D_SCONV]
    u = proj[:, 4 * D_RET + D_SCONV:4 * D_RET + 2 * D_SCONV] * proj[:, 4 * D_RET + 2 * D_SCONV:]
    conv = cw_ref[0:1, :] * conv_in_shift(u, 2) + cw_ref[1:2, :] * conv_in_shift(u, 1) + cw_ref[2:3, :] * u
    mixed = jnp.concatenate([ret_out, gate_b * conv], axis=-1).astype(BF16)
    return x + gate * _dot(mixed, w_out16), u


EVEN_LOAD_COLS = 512
EVEN_OUT_LOAD_ROWS = 256


def _even_kernel(xp_ref, xs_ref, g_ref, shp_ref, scp_ref, gatep_ref, shs_ref, scs_ref, gates_ref,
                 w_in_ref, w_out_ref, gain_ref, cw_ref,
                 rot_tile_ref, rot_row_ref, dec_ref, xi_ref, zt_ref, gmat_ref, r0_ref, u0_ref,
                 cos_s_ref, sin_s_ref, dec_s_ref, xi_s_ref, zt_s_ref, gmat_s_ref, s_stack_ref, s_wide_ref, u1_ref, u2_ref,
                 op_ref, os_ref, r_out_ref, u_out_ref, s_out_ref, us_out_ref,
                 w_in16, w_out16, r_scr, u_scr, *, n_load, n_tiles, t):
    i = pl.program_id(0)
    n_out_load = w_out16.shape[0]

    @pl.when(i < n_load)
    def _():
        w_in16[i] = w_in_ref[...].astype(BF16)

    @pl.when(i < n_out_load)
    def _():
        w_out16[i] = w_out_ref[...].astype(BF16)

    def in_project(h16):
        return jnp.concatenate([_dot(h16, w_in16[c]) for c in range(n_load)], axis=-1)

    def out_weights():
        return w_out16[...].reshape(D_MODEL, D_MODEL)

    @pl.when(jnp.logical_and(i >= n_load, i < n_load + n_tiles))
    def _():
        tile = i - n_load
        cos_0, sin_0, ssin_0 = (rot_tile_ref[j, pl.ds(tile, 1), :] for j in range(3))
        cos = cos_0 * rot_row_ref[0] - sin_0 * rot_row_ref[1]
        sin_signed = ssin_0 * rot_row_ref[0] + cos_0 * rot_row_ref[2]

        n_slabs = D_SCONV // LANES

        @pl.when(tile == 0)
        def _():
            r_scr[...] = r0_ref[...]
            for j in range(n_slabs):
                u_scr[j, 0:SUBLANES, :] = u0_ref[:, j * LANES:(j + 1) * LANES]

        x = xp_ref[...]
        h = _norm_mod(x, g_ref[...], shp_ref[0:1, :], scp_ref[0:1, :]).astype(BF16)
        proj = in_project(h)
        tm = x.shape[0]
        tb = dec_ref.shape[1]
        r_i = lax.broadcasted_iota(jnp.int32, (LANES, LANES), 0) < DK_RET
        c_i = lax.broadcasted_iota(jnp.int32, (LANES, LANES), 1) < DK_RET
        blockdiag = r_i == c_i

        def cross_fn(p, q16):
            return _dot(q16, r_scr[p].astype(BF16))

        def update_fn(p, kz_t16, v16):
            r_scr[p] = r_scr[p] * gmat_ref[p] + jnp.where(blockdiag, _dot(kz_t16, v16), 0.0)

        rets = []
        for r in range(tm // tb):
            rows = slice(r * tb, (r + 1) * tb)
            rets.append(_retention_block(proj[rows, :], cos[rows, :], sin_signed[rows, :], dec_ref, xi_ref[...],
                                         zt_ref, gain_ref[...], cross_fn, update_fn))
        ret_out = jnp.concatenate(rets, axis=0)

        def delayed(u, s):
            if s == 2:
                for j in range(n_slabs):
                    u_scr[j, SUBLANES:SUBLANES + tm, :] = u[:, j * LANES:(j + 1) * LANES]
            return jnp.concatenate([u_scr[j, SUBLANES - s:SUBLANES - s + tm, :] for j in range(n_slabs)], axis=-1)

        out, u = _even_tail(x, proj, ret_out, delayed, cw_ref, gatep_ref[0:1, :], out_weights())
        for j in range(n_slabs):
            u_scr[j, 0:SUBLANES, :] = u_scr[j, tm:tm + SUBLANES, :]
        op_ref[...] = out
        r_out_ref[...] = r_scr[...]
        u_out_ref[...] = u[tm - SUBLANES:, :]

    @pl.when(i == n_load + n_tiles)
    def _():
        x = xs_ref[...]
        rows = x.shape[0]
        n_streams = rows // t
        wide = n_streams * LANES
        h = _norm_mod(x, g_ref[...], _stream_rows(shs_ref[...], t), _stream_rows(scs_ref[...], t)).astype(BF16)
        proj = in_project(h)
        own = (lax.broadcasted_iota(jnp.int32, (rows, wide), 0) // t
               == lax.broadcasted_iota(jnp.int32, (rows, wide), 1) // LANES)
        r_i = lax.broadcasted_iota(jnp.int32, (LANES, wide), 0) < DK_RET
        blockdiag = r_i == _low_half((LANES, wide))

        def expand(a16):
            tiled = jnp.concatenate([a16.astype(F32)] * n_streams, axis=-1)
            return jnp.where(own, tiled, 0.0).astype(BF16)

        def cross_fn(p, q16):
            return _dot(expand(q16), s_stack_ref[p].astype(BF16))

        def update_fn(p, kz_t16, v16):
            kv = _dot(kz_t16, expand(v16))
            decay = jnp.concatenate([gmat_s_ref[p]] * n_streams, axis=-1)
            s_out_ref[p] = s_wide_ref[p] * decay + jnp.where(blockdiag, kv, 0.0)

        ret_out = _retention_block(proj, cos_s_ref[...], sin_s_ref[...], dec_s_ref, xi_s_ref[...], zt_s_ref,
                                   gain_ref[...], cross_fn, update_fn)
        out, u = _even_tail(x, proj, ret_out, lambda u, s: _shift_rows_streams(u, u1_ref[...], u2_ref[...], s, t),
                            cw_ref, _stream_rows(gates_ref[...], t), out_weights())
        os_ref[...] = out
        us_out_ref[...] = u


def _even(xp, xs, mod, l, norm_g, w_in, w_out, gain, cw, tabs_p, tabs_s, r0, u0, s_stack, s_wide, u1, u2, t):
    seq, rows = xp.shape[0], xs.shape[0]
    n_streams = rows // t
    wide = n_streams * LANES
    tm, tb = ROW_TILE, RET_BLOCK
    n_tiles = seq // tm
    n_load = D_IN_EVEN // EVEN_LOAD_COLS
    n_out_load = D_MODEL // EVEN_OUT_LOAD_ROWS
    assert n_load * EVEN_LOAD_COLS == D_IN_EVEN and n_out_load <= n_load
    rot_tile, rot_row, dec, xi, zt, gmat = tabs_p
    cos_s, sin_s, dec_s, xi_s, zt_s, gmat_s = tabs_s
    li = l // 2
    state = (HEAD_PAIRS, LANES, LANES)

    def tile(i):
        return jnp.clip(i - n_load, 0, n_tiles - 1)

    return pl.pallas_call(
        functools.partial(_even_kernel, n_load=n_load, n_tiles=n_tiles, t=t),
        out_shape=(jax.ShapeDtypeStruct((seq, D_MODEL), F32), jax.ShapeDtypeStruct((rows, D_MODEL), F32),
                   jax.ShapeDtypeStruct(state, F32), jax.ShapeDtypeStruct((SUBLANES, D_SCONV), F32),
                   jax.ShapeDtypeStruct((HEAD_PAIRS, LANES, wide), F32), jax.ShapeDtypeStruct((rows, D_SCONV), F32)),
        grid=(n_load + n_tiles + 1,),
        in_specs=[pl.BlockSpec((tm, D_MODEL), lambda i: (tile(i), 0)), _whole((rows, D_MODEL)), _layer((1, D_MODEL), l)]
        + [_mod_prompt(l, j, n_streams) for j in range(3)] + [_mod_streams(l, j, n_streams) for j in range(3)]
        + [pl.BlockSpec((None, D_MODEL, EVEN_LOAD_COLS), lambda i: (li, 0, jnp.minimum(i, n_load - 1))),
           pl.BlockSpec((None, EVEN_OUT_LOAD_ROWS, D_MODEL), lambda i: (li, jnp.minimum(i, n_out_load - 1), 0)),
           _layer((1, D_RET), li), _layer((3, D_SCONV), li),
           _whole((3, n_tiles, LANES)), _whole((3, tm, LANES)),
           _whole((H_RET, tb, tb)), _whole((tb, D_RET)), _whole((D_RET, tb)),
           _whole(state), _whole(state), _whole((SUBLANES, D_SCONV)),
           _whole((rows, LANES)), _whole((rows, LANES)),
           _whole((H_RET, rows, rows)), _whole((rows, D_RET)), _whole((D_RET, rows)), _whole(state),
           _layer((HEAD_PAIRS, wide, LANES), li), _layer((HEAD_PAIRS, LANES, wide), li),
           _layer((n_streams, D_SCONV), li), _layer((n_streams, D_SCONV), li)],
        out_specs=(pl.BlockSpec((tm, D_MODEL), lambda i: (tile(i), 0)), _whole_out((rows, D_MODEL)),
                   _whole_out(state), _whole_out((SUBLANES, D_SCONV)),
                   _whole_out((HEAD_PAIRS, LANES, wide)), _whole_out((rows, D_SCONV))),
        scratch_shapes=[pltpu.VMEM((n_load, D_MODEL, EVEN_LOAD_COLS), BF16),
                        pltpu.VMEM((n_out_load, EVEN_OUT_LOAD_ROWS, D_MODEL), BF16),
                        pltpu.VMEM(state, F32), pltpu.VMEM((D_SCONV // LANES, SUBLANES + tm, LANES), F32)],
        compiler_params=_params(),
        name="even_mixer",
    )(xp, xs, norm_g, mod, mod, mod, mod, mod, mod, w_in, w_out, gain, cw,
      rot_tile, rot_row, dec, xi, zt, gmat, r0, u0, cos_s, sin_s, dec_s, xi_s, zt_s, gmat_s, s_stack, s_wide, u1, u2)


def _pool_put(hist_ref, row0, block):
    for gi in range(len(POOL_WINDOWS)):
        hist_ref[gi, row0:row0 + block.shape[0], :] = block[:, gi * POOL_GROUP:(gi + 1) * POOL_GROUP]


def _pool_get(hist_ref, row0, rows):
    return jnp.concatenate([hist_ref[gi, row0:row0 + rows, :] for gi in range(len(POOL_WINDOWS))], axis=-1)


def _pool(hist_ref, p, pos, pool_w_ref, scale):
    t = p.shape[0]
    outs = []
    for gi, w in enumerate(POOL_WINDOWS):
        cols = slice(gi * POOL_GROUP, (gi + 1) * POOL_GROUP)
        win = p[:, cols]
        for d in range(1, w):
            win = win + hist_ref[gi, POOL_BASE - d:POOL_BASE - d + t, :]
        inv_cnt = 1.0 / jnp.minimum(pos + 1, w).astype(F32)
        pooled = win * inv_cnt - p[:, cols]
        outs.append(_dot(pooled.astype(BF16), pool_w_ref[gi]) * scale[:, cols])
    return jnp.concatenate(outs, axis=-1)


def _attend_scores(kbs, q_as, q_bs, biases):
    low = _low_half((CHUNK, LANES))
    out = []
    for kb, q_a, q_b, bias in zip(kbs, q_as, q_bs, biases):
        qs = jnp.concatenate([jnp.where(low, q_a, 0.0), jnp.where(low, 0.0, q_a),
                              jnp.where(low, q_b, 0.0), jnp.where(low, 0.0, q_b)], axis=0)
        qbd = qs.T.astype(BF16)
        half = (kb.shape[0] // 2) // (2 * SUBLANES) * (2 * SUBLANES)
        out.append(jnp.concatenate([_dot(kb[:half], qbd), _dot(kb[half:], qbd)], axis=0) + bias)
    return out


def _attend_values(scores, vts):
    low = _low_half((CHUNK, LANES))
    n = len(scores)
    e16 = [jnp.exp2(s - jnp.max(s, axis=0, keepdims=True)).astype(BF16) for s in scores]
    o_t = [jnp.concatenate([_dot(vts[j][:DK_RET], e16[j]), _dot(vts[j][DK_RET:], e16[j])], axis=0) for j in range(n)]
    o_t = [o[:LANES] * (1.0 / o[LANES:LANES + 1]) for o in o_t]
    o_t = [o.T for o in o_t]
    return [(jnp.where(low, o[0:CHUNK], o[CHUNK:2 * CHUNK]), jnp.where(low, o[2 * CHUNK:3 * CHUNK], o[3 * CHUNK:]))
            for o in o_t]


ODD_LOAD_COLS = 512
SAMPLE_STREAMS_PER_STEP = 2


def _odd_kernel(xp_ref, xs_ref, g_ref, shp_ref, scp_ref, gatep_ref, shs_ref, scs_ref, gates_ref,
                w_in_ref, w_out_ref, pw_ref, ps_ref, bias_ref, p0_ref, bias_c_ref, bias_n_ref, p0s_ref, kc_ref, vc_ref,
                op_ref, os_ref, p_out_ref, k_out_ref, v_out_ref, ps_out_ref, ks_out_ref, vs_out_ref,
                w_in16, w_out16, pbuf, kbuf, vtbuf, q_scr, att_scr, proj_scr, mix_scr, pbuf_s,
                *, n_load, n_tiles, pos0, t):
    i = pl.program_id(0)
    tm = xp_ref.shape[0]
    rows = xs_ref.shape[0]
    per_step = kc_ref.shape[0]

    @pl.when(i < n_load)
    def _():
        w_in16[i] = w_in_ref[...].astype(BF16)
        w_out16[i] = w_out_ref[...].astype(BF16)

    def in_project(h16):
        return jnp.concatenate([_dot(h16, w_in16[c]) for c in range(n_load)], axis=-1)

    def out_weights():
        return w_out16[...].reshape(D_MODEL, D_MODEL)

    @pl.when(jnp.logical_and(i >= n_load, i < n_load + n_tiles))
    def _():
        tile = i - n_load

        @pl.when(tile == 0)
        def _():
            _pool_put(pbuf, 0, p0_ref[...])
            kbuf[0:HIST, :] = jnp.zeros((HIST, D_ATT), BF16)
            vtbuf[:, :, 0:HIST] = jnp.zeros((HEAD_PAIRS, LANES + VT_EXTRA, HIST), BF16)

        x = xp_ref[...]
        h = _norm_mod(x, g_ref[...], shp_ref[0:1, :], scp_ref[0:1, :]).astype(BF16)
        proj = in_project(h)
        p = proj[:, :D_POOL]
        q_scr[...] = proj[:, D_POOL:D_POOL + D_ATT] * (DH_ATT ** -0.5 * LOG2E)
        k = proj[:, D_POOL + D_ATT:D_POOL + 2 * D_ATT]
        v = proj[:, D_POOL + 2 * D_ATT:]
        _pool_put(pbuf, POOL_BASE, p)
        kbuf[HIST:HIST + tm, :] = k.astype(BF16)
        ones_row = jnp.where(lax.broadcasted_iota(jnp.int32, (VT_EXTRA, tm), 0) == 0, 1.0, 0.0).astype(BF16)
        for pr in range(HEAD_PAIRS):
            vtbuf[pr, 0:LANES, HIST:HIST + tm] = v[:, pr * LANES:(pr + 1) * LANES].T.astype(BF16)
            vtbuf[pr, LANES:LANES + VT_EXTRA, HIST:HIST + tm] = ones_row
        k_out_ref[...] = k[tm - HIST:, :]
        v_out_ref[...] = v[tm - HIST:, :]

        pos = tile * tm + lax.broadcasted_iota(jnp.int32, (tm, 1), 0)
        pool_out = _pool(pbuf, p, pos, pw_ref, ps_ref[...])

        def attend_tile(first_tile):
            pairs = range(HEAD_PAIRS)
            lanes = [slice(pr * LANES, (pr + 1) * LANES) for pr in pairs]
            n_blocks = tm // (2 * CHUNK)
            skip = [max(HIST - jb * 2 * CHUNK, 0) if first_tile else 0 for jb in range(n_blocks)]

            def scores(jb):
                r0 = jb * 2 * CHUNK
                return _attend_scores([kbuf[r0 + skip[jb]:r0 + BAND2, lanes[pr]] for pr in pairs],
                                      [q_scr[r0:r0 + CHUNK, lanes[pr]] for pr in pairs],
                                      [q_scr[r0 + CHUNK:r0 + 2 * CHUNK, lanes[pr]] for pr in pairs],
                                      [bias_ref[pr, skip[jb]:, :] for pr in pairs])

            s_next = scores(0)
            for jb in range(n_blocks):
                r0 = jb * 2 * CHUNK
                s_cur = s_next
                if jb + 1 < n_blocks:
                    s_next = scores(jb + 1)
                outs = _attend_values(s_cur, [vtbuf[pr, :, r0 + skip[jb]:r0 + BAND2] for pr in pairs])
                att_scr[r0:r0 + CHUNK, :] = jnp.concatenate([o[0] for o in outs], axis=-1)
                att_scr[r0 + CHUNK:r0 + 2 * CHUNK, :] = jnp.concatenate([o[1] for o in outs], axis=-1)

        pl.when(tile == 0)(functools.partial(attend_tile, True))
        pl.when(tile > 0)(functools.partial(attend_tile, False))

        kbuf[0:HIST, :] = kbuf[tm:tm + HIST, :]
        vtbuf[:, :, 0:HIST] = vtbuf[:, :, tm:tm + HIST]
        tail = _pool_get(pbuf, tm, POOL_BASE)
        _pool_put(pbuf, 0, tail)
        p_out_ref[...] = tail

        mixed = jnp.concatenate([pool_out, att_scr[...]], axis=-1).astype(BF16)
        op_ref[...] = x + gatep_ref[0:1, :] * _dot(mixed, out_weights())

    @pl.when(i >= n_load + n_tiles)
    def _():
        step = i - (n_load + n_tiles)

        @pl.when(step == 0)
        def _():
            h = _norm_mod(xs_ref[...], g_ref[...], _stream_rows(shs_ref[...], t),
                          _stream_rows(scs_ref[...], t)).astype(BF16)
            proj = in_project(h)
            proj_scr[...] = proj
            ks_out_ref[...] = proj[:, D_POOL + D_ATT:D_POOL + 2 * D_ATT]
            vs_out_ref[...] = proj[:, D_POOL + 2 * D_ATT:]

        kn = proj_scr[:, D_POOL + D_ATT:D_POOL + 2 * D_ATT].astype(BF16)
        vn = proj_scr[:, D_POOL + 2 * D_ATT:].astype(BF16)
        pos = pos0 + lax.broadcasted_iota(jnp.int32, (t, 1), 0)
        head_of_lane = lax.broadcasted_iota(jnp.int32, (t, D_ATT), 1) // DH_ATT
        stream_of_col = lax.broadcasted_iota(jnp.int32, (H_ATT * t, rows), 1) // t
        js = range(per_step)
        r0 = [pl.multiple_of((step * per_step + j) * t, t) for j in js]
        proj = [proj_scr[pl.ds(r0[j], t), :] for j in js]
        q_heads = []
        for j in js:
            q = proj[j][:, D_POOL:D_POOL + D_ATT] * (DH_ATT ** -0.5)
            q_heads.append(jnp.concatenate([jnp.where(head_of_lane == hh, q, 0.0) for hh in range(H_ATT)],
                                           axis=0).astype(BF16))
        s_c = [_dot_nt(q_heads[j], kc_ref[j].astype(BF16)) + bias_c_ref[...] for j in js]
        s_n = [jnp.where(stream_of_col == step * per_step + j, _dot_nt(q_heads[j], kn) + bias_n_ref[...], NEG_INF)
               for j in js]
        pool_out = []
        for j in js:
            p = proj[j][:, :D_POOL]
            _pool_put(pbuf_s.at[j], 0, p0s_ref[j])
            _pool_put(pbuf_s.at[j], POOL_BASE, p)
            ps_out_ref[j] = p
            pool_out.append(_pool(pbuf_s.at[j], p, pos, pw_ref, ps_ref[...]))
        m = [jnp.maximum(jnp.max(s_c[j], axis=-1, keepdims=True), jnp.max(s_n[j], axis=-1, keepdims=True)) for j in js]
        e_c = [jnp.exp(s_c[j] - m[j]) for j in js]
        e_n = [jnp.exp(s_n[j] - m[j]) for j in js]
        inv_l = [1.0 / (jnp.sum(e_c[j], axis=-1, keepdims=True) + jnp.sum(e_n[j], axis=-1, keepdims=True)) for j in js]
        o_heads = [(_dot(e_c[j].astype(BF16), vc_ref[j].astype(BF16)) + _dot(e_n[j].astype(BF16), vn)) * inv_l[j]
                   for j in js]
        for j in js:
            att = jnp.where(head_of_lane == 0, o_heads[j][0:t], 0.0)
            for hh in range(1, H_ATT):
                att = jnp.where(head_of_lane == hh, o_heads[j][hh * t:(hh + 1) * t], att)
            mix_scr[pl.ds(r0[j], t), :] = jnp.concatenate([pool_out[j], att], axis=-1)

        @pl.when(i == pl.num_programs(0) - 1)
        def _():
            os_ref[...] = xs_ref[...] + _stream_rows(gates_ref[...], t) * _dot(mix_scr[...].astype(BF16), out_weights())


def _odd(xp, xs, mod, l, norm_g, w_in, w_out, pw16, ps, bias_t, p0, bias_c, bias_n, p0s, kc16, vc16, t, pos0):
    seq, rows = xp.shape[0], xs.shape[0]
    n_streams = rows // t
    cache = kc16.shape[2]
    tm = ROW_TILE
    assert tm == HIST and seq % tm == 0
    n_tiles = seq // tm
    n_load = D_IN_ODD // ODD_LOAD_COLS
    out_rows = D_MODEL // n_load
    per_step = SAMPLE_STREAMS_PER_STEP
    n_steps = n_streams // per_step
    assert n_load * ODD_LOAD_COLS == D_IN_ODD and n_streams % per_step == 0
    li = l // 2

    def tile(i):
        return jnp.clip(i - n_load, 0, n_tiles - 1)

    def step(i):
        return jnp.clip(i - n_load - n_tiles, 0, n_steps - 1)

    def streams(shape):
        nd = len(shape)
        return pl.BlockSpec((None, per_step) + tuple(shape), lambda i: (li, step(i)) + (0,) * nd)

    return pl.pallas_call(
        functools.partial(_odd_kernel, n_load=n_load, n_tiles=n_tiles, pos0=pos0, t=t),
        out_shape=(jax.ShapeDtypeStruct((seq, D_MODEL), F32), jax.ShapeDtypeStruct((rows, D_MODEL), F32),
                   jax.ShapeDtypeStruct((POOL_BASE, D_POOL), F32),
                   jax.ShapeDtypeStruct((HIST, D_ATT), F32), jax.ShapeDtypeStruct((HIST, D_ATT), F32),
                   jax.ShapeDtypeStruct((n_streams, t, D_POOL), F32),
                   jax.ShapeDtypeStruct((rows, D_ATT), F32), jax.ShapeDtypeStruct((rows, D_ATT), F32)),
        grid=(n_load + n_tiles + n_steps,),
        in_specs=[pl.BlockSpec((tm, D_MODEL), lambda i: (tile(i), 0)), _whole((rows, D_MODEL)), _layer((1, D_MODEL), l)]
        + [_mod_prompt(l, j, n_streams) for j in range(3)] + [_mod_streams(l, j, n_streams) for j in range(3)]
        + [pl.BlockSpec((None, D_MODEL, ODD_LOAD_COLS), lambda i: (li, 0, jnp.minimum(i, n_load - 1))),
           pl.BlockSpec((None, out_rows, D_MODEL), lambda i: (li, jnp.minimum(i, n_load - 1), 0)),
           _layer((len(POOL_WINDOWS), POOL_GROUP, POOL_GROUP), li), _layer((1, D_POOL), li),
           _whole((HEAD_PAIRS, BAND2, 2 * LANES)), _whole((POOL_BASE, D_POOL)),
           _whole((H_ATT * t, cache)), _whole((H_ATT * t, rows)),
           streams((POOL_BASE, D_POOL)), streams((cache, D_ATT)), streams((cache, D_ATT))],
        out_specs=(pl.BlockSpec((tm, D_MODEL), lambda i: (tile(i), 0)), _whole_out((rows, D_MODEL)),
                   _whole_out((POOL_BASE, D_POOL)), _whole_out((HIST, D_ATT)), _whole_out((HIST, D_ATT)),
                   pl.BlockSpec((per_step, t, D_POOL), lambda i: (step(i), 0, 0)),
                   _whole_out((rows, D_ATT)), _whole_out((rows, D_ATT))),
        scratch_shapes=[pltpu.VMEM((n_load, D_MODEL, ODD_LOAD_COLS), BF16), pltpu.VMEM((n_load, out_rows, D_MODEL), BF16),
                        pltpu.VMEM((len(POOL_WINDOWS), POOL_BASE + tm, POOL_GROUP), F32),
                        pltpu.VMEM((HIST + tm, D_ATT), BF16),
                        pltpu.VMEM((HEAD_PAIRS, LANES + VT_EXTRA, HIST + tm), BF16),
                        pltpu.VMEM((tm, D_ATT), F32), pltpu.VMEM((tm, D_ATT), F32),
                        pltpu.VMEM((rows, D_IN_ODD), F32), pltpu.VMEM((rows, D_MODEL), F32),
                        pltpu.VMEM((per_step, len(POOL_WINDOWS), POOL_BASE + t, POOL_GROUP), F32)],
        compiler_params=_params(),
        name="odd_mixer",
    )(xp, xs, norm_g, mod, mod, mod, mod, mod, mod, w_in, w_out, pw16, ps, bias_t, p0, bias_c, bias_n, p0s, kc16, vc16)


FFN_LOAD_STEPS = 16


def _ffn_kernel(xp_ref, xs_ref, g_ref, shp_ref, scp_ref, gatep_ref, shs_ref, scs_ref, gates_ref,
                w_up_ref, cw_ref, w_down_ref, f0_ref, f1_ref, f2_ref, gf_ref,
                op_ref, os_ref, fp_out_ref, fs_out_ref, w_up16, w_down16, up_scr, act_scr,
                *, final_norm, n_load, n_tiles, t):
    i = pl.program_id(0)
    tm = xp_ref.shape[0]
    n_chunks = D_FF // FFN_COLS
    slabs_per_chunk = FFN_COLS // LANES
    n_slabs = 2 * D_FF // LANES

    @pl.when(i < n_load)
    def _():
        up_rows, down_rows = w_up_ref.shape[0], w_down_ref.shape[0]
        w_up16[pl.ds(pl.multiple_of(i * up_rows, up_rows), up_rows), :] = w_up_ref[...].astype(BF16)
        w_down16[pl.ds(pl.multiple_of(i * down_rows, down_rows), down_rows), :] = w_down_ref[...].astype(BF16)

    def up_weights(c0):
        return w_up16[:, c0:c0 + FFN_COLS]

    def conv_slab(j, rows):
        cols = slice(j * LANES, (j + 1) * LANES)
        return (cw_ref[0:1, cols] * up_scr[j, SUBLANES - 2:SUBLANES - 2 + rows, :]
                + cw_ref[1:2, cols] * up_scr[j, SUBLANES - 1:SUBLANES - 1 + rows, :]
                + cw_ref[2:3, cols] * up_scr[j, SUBLANES:SUBLANES + rows, :])

    def run(h16, rows, put_up, after_conv):
        def project(c):
            for off in (0, D_FF):
                c0 = off + c * FFN_COLS
                put_up(c0, _dot(h16, up_weights(c0)))

        def activate(c):
            for j in range(slabs_per_chunk):
                ja = c * slabs_per_chunk + j
                a = conv_slab(ja, rows)
                b = conv_slab(D_FF // LANES + ja, rows)
                after_conv(ja)
                after_conv(D_FF // LANES + ja)
                act_scr[0:rows, ja * LANES:(ja + 1) * LANES] = (_silu(a) * b).astype(BF16)

        project(0)
        for c in range(n_chunks):
            if c + 1 < n_chunks:
                project(c + 1)
            activate(c)
        return _dot(act_scr[0:rows, :], w_down16[...])

    @pl.when(jnp.logical_and(i >= n_load, i < n_load + n_tiles))
    def _():
        @pl.when(i == n_load)
        def _():
            for j in range(n_slabs):
                up_scr[j, 0:SUBLANES, :] = f0_ref[:, j * LANES:(j + 1) * LANES]

        x = xp_ref[...]
        h16 = _norm_mod(x, g_ref[...], shp_ref[0:1, :], scp_ref[0:1, :]).astype(BF16)

        def put_up(c0, up):
            for j in range(slabs_per_chunk):
                up_scr[c0 // LANES + j, SUBLANES:SUBLANES + tm, :] = up[:, j * LANES:(j + 1) * LANES]
            fp_out_ref[:, c0:c0 + FFN_COLS] = up[tm - SUBLANES:, :]

        def keep_tail(j):
            up_scr[j, 0:SUBLANES, :] = up_scr[j, tm:tm + SUBLANES, :]

        out = x + gatep_ref[0:1, :] * run(h16, tm, put_up, keep_tail)
        if final_norm:
            out = _rmsnorm(out, gf_ref[...])
        op_ref[...] = out

    @pl.when(i == n_load + n_tiles)
    def _():
        x = xs_ref[...]
        n_streams = x.shape[0] // t
        seg = SUBLANES + t
        rows = n_streams * seg
        h16 = _norm_mod(x, g_ref[...], _stream_rows(shs_ref[...], t), _stream_rows(scs_ref[...], t)).astype(BF16)
        for j in range(n_slabs):
            cols = slice(j * LANES, (j + 1) * LANES)
            for b in range(n_streams):
                up_scr[j, b * seg + SUBLANES - 2:b * seg + SUBLANES - 1, :] = f1_ref[b:b + 1, cols]
                up_scr[j, b * seg + SUBLANES - 1:b * seg + SUBLANES, :] = f2_ref[b:b + 1, cols]

        def put_up(c0, up):
            for b in range(n_streams):
                for j in range(slabs_per_chunk):
                    up_scr[c0 // LANES + j, b * seg + SUBLANES:(b + 1) * seg, :] = \
                        up[b * t:(b + 1) * t, j * LANES:(j + 1) * LANES]
                fs_out_ref[b, :, c0:c0 + FFN_COLS] = up[(b + 1) * t - SUBLANES:(b + 1) * t, :]

        y = run(h16, rows, put_up, lambda j: None)
        y = jnp.concatenate([y[b * seg:b * seg + t, :] for b in range(n_streams)], axis=0)
        out = x + _stream_rows(gates_ref[...], t) * y
        if final_norm:
            out = _rmsnorm(out, gf_ref[...])
        os_ref[...] = out


def _ffn(xp, xs, mod, l, norm_g, w_up, cw, w_down, f0, f1, f2, gf, final_norm, t):
    seq, rows = xp.shape[0], xs.shape[0]
    n_streams = rows // t
    tm = ROW_TILE
    n_tiles = seq // tm
    n_load = FFN_LOAD_STEPS
    up_rows, down_rows = D_MODEL // n_load, D_FF // n_load
    assert up_rows * n_load == D_MODEL and down_rows * n_load == D_FF
    assert up_rows % (2 * SUBLANES) == 0 and down_rows % (2 * SUBLANES) == 0 and n_streams * (SUBLANES + t) <= tm

    def tile(i):
        return jnp.clip(i - n_load, 0, n_tiles - 1)

    return pl.pallas_call(
        functools.partial(_ffn_kernel, final_norm=final_norm, n_load=n_load, n_tiles=n_tiles, t=t),
        out_shape=(jax.ShapeDtypeStruct((seq, D_MODEL), F32), jax.ShapeDtypeStruct((rows, D_MODEL), F32),
                   jax.ShapeDtypeStruct((SUBLANES, 2 * D_FF), F32),
                   jax.ShapeDtypeStruct((n_streams, SUBLANES, 2 * D_FF), F32)),
        grid=(n_load + n_tiles + 1,),
        in_specs=[pl.BlockSpec((tm, D_MODEL), lambda i: (tile(i), 0)), _whole((rows, D_MODEL)), _layer((1, D_MODEL), l)]
        + [_mod_prompt(l, 3 + j, n_streams) for j in range(3)] + [_mod_streams(l, 3 + j, n_streams) for j in range(3)]
        + [pl.BlockSpec((None, up_rows, 2 * D_FF), lambda i: (l, jnp.minimum(i, n_load - 1), 0)),
           _layer((3, 2 * D_FF), l),
           pl.BlockSpec((None, down_rows, D_MODEL), lambda i: (l, jnp.minimum(i, n_load - 1), 0)),
           _whole((SUBLANES, 2 * D_FF)), _layer((n_streams, 2 * D_FF), l), _layer((n_streams, 2 * D_FF), l),
           _whole((1, D_MODEL))],
        out_specs=(pl.BlockSpec((tm, D_MODEL), lambda i: (tile(i), 0)), _whole_out((rows, D_MODEL)),
                   _whole_out((SUBLANES, 2 * D_FF)), _whole_out((n_streams, SUBLANES, 2 * D_FF))),
        scratch_shapes=[pltpu.VMEM((D_MODEL, 2 * D_FF), BF16), pltpu.VMEM((D_FF, D_MODEL), BF16),
                        pltpu.VMEM((2 * D_FF // LANES, SUBLANES + tm, LANES), F32), pltpu.VMEM((tm, D_FF), BF16)],
        compiler_params=_params(),
        name="ffn",
    )(xp, xs, norm_g, mod, mod, mod, mod, mod, mod, w_up, cw, w_down, f0, f1, f2, gf)


def _rotary_triplet(pos):
    half = DK_RET // 2
    inv = ROPE_BASE ** (-jnp.arange(half, dtype=F32) / half)
    ang = pos.astype(F32)[:, None] * inv[None, :]
    cos, sin = jnp.cos(ang), jnp.sin(ang)
    return jnp.stack([jnp.concatenate([cos] * 4, axis=-1), jnp.concatenate([sin] * 4, axis=-1),
                      jnp.concatenate([-sin, sin, -sin, sin], axis=-1)])


def _retention_tables(tb, n_streams=1):
    idx = np.arange(tb, dtype=np.float64)
    diff = idx[:, None] - idx[None, :]
    dec1 = np.where(diff[None] >= 0, np.exp(LOG_G[:, None, None] * np.maximum(diff, 0.0)[None]), 0.0)
    dec = np.zeros((H_RET, n_streams * tb, n_streams * tb))
    for b in range(n_streams):
        dec[:, b * tb:(b + 1) * tb, b * tb:(b + 1) * tb] = dec1
    xi = np.tile(np.repeat(np.exp(LOG_G[:, None] * (idx + 1)[None, :]).T, DK_RET, axis=1), (n_streams, 1))
    zeta_t = np.tile(np.repeat(np.exp(LOG_G[:, None] * (tb - 1 - idx)[None, :]), DK_RET, axis=0), (1, n_streams))
    gmat = np.zeros((HEAD_PAIRS, LANES, LANES))
    for h in range(H_RET):
        o = (h % 2) * DK_RET
        gmat[h // 2, o:o + DK_RET, o:o + DK_RET] = np.exp(LOG_G[h] * tb)
    return tuple(jnp.asarray(a, F32) for a in (dec, xi, zeta_t, gmat))


def _pair_state(s):
    lead = s.shape[:-3]
    s = s.reshape(lead + (HEAD_PAIRS, 2, DK_RET, DK_RET))
    z = jnp.zeros_like(s[..., 0, :, :])
    top = jnp.concatenate([s[..., 0, :, :], z], axis=-1)
    bot = jnp.concatenate([z, s[..., 1, :, :]], axis=-1)
    return jnp.concatenate([top, bot], axis=-2)


def _unpair_state(r):
    a = r[..., :DK_RET, :DK_RET]
    b = r[..., DK_RET:, DK_RET:]
    s = jnp.stack([a, b], axis=-3)
    return s.reshape(r.shape[:-3] + (H_RET, DK_RET, DK_RET))


def _band_bias(table):
    nq, nk = 2 * CHUNK, BAND2
    period = nq + nk
    j = np.arange(period)
    j = np.where(j < nk, j, j - period)
    idx = np.clip(HIST - j, -(CHUNK - 1), REL_CLIP) + (CHUNK - 1)
    one_period = table[:, idx].astype(F32)
    flat = jnp.tile(one_period, (1, nq + 1))[:, :nq * (period - 1)]
    return flat.reshape(-1, nq, period - 1)[:, :, :nk]


def _band_bias_t(raw):
    qq = np.arange(2 * CHUNK)[:, None]
    kk = np.arange(BAND2)[None, :]
    valid = np.where(qq < CHUNK, kk < BAND, kk >= CHUNK)
    b = jnp.where(valid, raw * LOG2E, NEG_INF).reshape(HEAD_PAIRS, 2, 2, CHUNK, BAND2)
    return jnp.transpose(b, (0, 4, 2, 1, 3)).reshape(HEAD_PAIRS, BAND2, 2 * LANES)


def _tail_rows(a, n):
    return a[..., a.shape[-2] - n:, :]


def kernel(x_prompt, x_sample, state_ret, state_sconv, state_pool, cache_k, cache_v, state_ffn, c_prompt, c_sample,
           norm_mix, norm_ffn, norm_final, w_ada, b_ada, w_in_even, w_out_even, ret_gn_gain, sconv_w, w_in_odd,
           w_out_odd, pool_w, pool_scale, rel_bias_table, ffn_w_up, ffn_conv, ffn_w_down):
    n_prompt, seq, _ = x_prompt.shape
    n_streams, t_s, _ = x_sample.shape
    assert n_prompt == 1 and n_streams % MOD_ROWS_PROMPT == 0
    rows_s = n_streams * t_s
    n_even, n_odd = (DEPTH + 1) // 2, DEPTH // 2

    c_all = jnp.concatenate([c_sample, c_prompt], axis=0)
    mod = _ada(jnp.pad(c_all, ((0, MOD_ROWS_PROMPT - 1), (0, 0))), w_ada, b_ada)

    bf = lambda w: w.astype(BF16)
    pool_w16 = bf(pool_w)
    norm_mix3, norm_ffn3 = norm_mix.reshape(DEPTH, 1, D_MODEL), norm_ffn.reshape(DEPTH, 1, D_MODEL)
    gain3, pool_scale3 = ret_gn_gain.reshape(n_even, 1, D_RET), pool_scale.reshape(n_odd, 1, D_POOL)
    norm_final2 = norm_final.reshape(1, D_MODEL)

    n_tiles = seq // ROW_TILE
    tabs_p = (_rotary_triplet(jnp.arange(n_tiles, dtype=jnp.int32) * ROW_TILE),
              _rotary_triplet(jnp.arange(ROW_TILE, dtype=jnp.int32))) + _retention_tables(RET_BLOCK)
    rot_s = _rotary_triplet(PAST_LEN + jnp.arange(t_s, dtype=jnp.int32))
    tabs_s = (jnp.tile(rot_s[0], (n_streams, 1)), jnp.tile(rot_s[2], (n_streams, 1))) \
        + _retention_tables(t_s, n_streams)

    cache_len = cache_k.shape[2]
    assert cache_len == HIST and t_s <= CHUNK
    bias_raw = [_band_bias(rel_bias_table[i]) for i in range(n_odd)]
    bias_p = [_band_bias_t(b) for b in bias_raw]
    bias_c = [b[:, :t_s, :cache_len].reshape(H_ATT * t_s, cache_len) for b in bias_raw]
    bias_n = [jnp.tile(b[:, :t_s, cache_len:cache_len + t_s], (1, 1, n_streams)).reshape(H_ATT * t_s, rows_s)
              for b in bias_raw]

    paired = _pair_state(state_ret)
    s_stack = jnp.transpose(paired, (0, 2, 1, 3, 4)).reshape(n_even, HEAD_PAIRS, n_streams * LANES, LANES)
    s_wide = jnp.transpose(paired, (0, 2, 3, 1, 4)).reshape(n_even, HEAD_PAIRS, LANES, n_streams * LANES)
    u1, u2 = state_sconv[:, :, 0, :], state_sconv[:, :, 1, :]
    f1, f2 = state_ffn[:, :, 0, :], state_ffn[:, :, 1, :]
    p0_s = jnp.pad(state_pool, ((0, 0), (0, 0), (POOL_BASE - POOL_BUF, 0), (0, 0)))
    kc16 = cache_k.reshape(n_odd, n_streams, cache_len, D_ATT)
    vc16 = cache_v.reshape(n_odd, n_streams, cache_len, D_ATT)

    xp = x_prompt.reshape(seq, D_MODEL)
    xs = x_sample.reshape(rows_s, D_MODEL)

    ret_p, ret_s, sconv_p, sconv_s, pool_p, pool_s = [], [], [], [], [], []
    k_p, k_s, v_p, v_s, ffn_p, ffn_s = [], [], [], [], [], []
    for l in range(DEPTH):
        i = l // 2
        if l % 2 == 0:
            xp, xs, r_new, u_new, s_new, u_all = _even(
                xp, xs, mod, l, norm_mix3, w_in_even, w_out_even, gain3, sconv_w, tabs_p, tabs_s,
                jnp.zeros((HEAD_PAIRS, LANES, LANES), F32), jnp.zeros((SUBLANES, D_SCONV), F32),
                s_stack, s_wide, u1, u2, t_s)
            ret_p.append(_unpair_state(r_new)[None])
            sconv_p.append(_tail_rows(u_new, 2)[None])
            s_new = jnp.transpose(s_new.reshape(HEAD_PAIRS, LANES, n_streams, LANES), (2, 0, 1, 3))
            ret_s.append(_unpair_state(s_new))
            sconv_s.append(_tail_rows(u_all.reshape(n_streams, t_s, D_SCONV), 2))
        else:
            xp, xs, p_new, k_new, v_new, ps_new, ks_new, vs_new = _odd(
                xp, xs, mod, l, norm_mix3, w_in_odd, w_out_odd, pool_w16, pool_scale3, bias_p[i],
                jnp.zeros((POOL_BASE, D_POOL), F32), bias_c[i], bias_n[i], p0_s, kc16, vc16, t_s, PAST_LEN)
            pool_p.append(_tail_rows(p_new, POOL_BUF)[None])
            k_p.append(k_new.reshape(1, HIST, H_ATT, DH_ATT))
            v_p.append(v_new.reshape(1, HIST, H_ATT, DH_ATT))
            p_new, k_new, v_new = ps_new, ks_new, vs_new
            pool_s.append(_tail_rows(p_new, POOL_BUF))
            k_s.append(k_new.reshape(n_streams, t_s, H_ATT, DH_ATT))
            v_s.append(v_new.reshape(n_streams, t_s, H_ATT, DH_ATT))
        last = l == DEPTH - 1
        xp, xs, f_new, fs_new = _ffn(xp, xs, mod, l, norm_ffn3, ffn_w_up, ffn_conv, ffn_w_down,
                                     jnp.zeros((SUBLANES, 2 * D_FF), F32), f1, f2, norm_final2, last, t_s)
        ffn_p.append(_tail_rows(f_new, 2)[None])
        ffn_s.append(_tail_rows(fs_new, 2))

    st = jnp.stack
    return (xp.reshape(1, seq, D_MODEL), xs.reshape(n_streams, t_s, D_MODEL),
            st(ret_p), st(ret_s), st(sconv_p), st(sconv_s), st(pool_p), st(pool_s),
            st(k_p), st(k_s), st(v_p), st(v_s), st(ffn_p), st(ffn_s))
```
